```python
import math
import jax, jax.numpy as jnp
from jax import lax
import numpy as np

D_MODEL = 2048
BATCH = 8
SEQ = 8192
DEPTH = 2

N_HEADS = 8
HEAD_DIM = 128
ATTN_W = N_HEADS * HEAD_DIM
CONV_CH = D_MODEL // 2
CONV_K = 31
FFN_HIDDEN = ((8 * D_MODEL // 3 + 255) // 256) * 256
PLE_DIM = 256
Q_BLOCK = 128
EPS = 1e-6
IN_COLS = 2 * CONV_CH + 3 * ATTN_W + N_HEADS + 2 * D_MODEL
NEG_INF = -1e30

kernel_name = "hybrid_conformer_fox_gated_trunk"


def rmsnorm(x, g):
    xf = x.astype(jnp.float32)
    y = xf * lax.rsqrt(jnp.mean(xf * xf, axis=-1, keepdims=True) + EPS)
    return (y * g.astype(jnp.float32)).astype(x.dtype)


def layernorm(x, g, b):
    xf = x.astype(jnp.float32)
    mu = jnp.mean(xf, axis=-1, keepdims=True)
    var = jnp.mean(jnp.square(xf - mu), axis=-1, keepdims=True)
    y = (xf - mu) * lax.rsqrt(var + EPS)
    return (y * g.astype(jnp.float32) + b.astype(jnp.float32)).astype(x.dtype)


def conformer_conv(glu_in, conv_w, conv_b, ln_g, ln_b, w_conv_out):
    a, gate = jnp.split(glu_in, 2, axis=-1)
    u = a * jax.nn.sigmoid(gate)
    u = lax.conv_general_dilated(
        u, conv_w[:, None, :].astype(u.dtype),
        window_strides=(1,), padding=[(CONV_K - 1, 0)],
        dimension_numbers=("NWC", "WIO", "NWC"),
        feature_group_count=CONV_CH) + conv_b
    u = jax.nn.silu(layernorm(u, ln_g, ln_b))
    return u @ w_conv_out


def forgetting_attention(q, k, v, log_f):
    b, s, h, d = q.shape
    nb = s // Q_BLOCK
    scale = 1.0 / math.sqrt(d)
    c = jnp.cumsum(log_f, axis=1).transpose(0, 2, 1)
    q_blocks = q.reshape(b, nb, Q_BLOCK, h, d).transpose(1, 0, 2, 3, 4)
    c_blocks = c.reshape(b, h, nb, Q_BLOCK).transpose(2, 0, 1, 3)
    starts = jnp.arange(nb, dtype=jnp.int32) * Q_BLOCK
    k_pos = jnp.arange(s, dtype=jnp.int32)

    def one_block(args):
        qb, cb, start = args
        sc = jnp.einsum("bqhd,bkhd->bhqk", qb, k,
                        preferred_element_type=jnp.float32) * scale
        sc = sc + cb[..., :, None] - c[:, :, None, :]
        q_pos = start + jnp.arange(Q_BLOCK, dtype=jnp.int32)
        mask = k_pos[None, :] <= q_pos[:, None]
        sc = jnp.where(mask, sc, NEG_INF)
        pr = jax.nn.softmax(sc, axis=-1)
        return jnp.einsum("bhqk,bkhd->bqhd", pr.astype(v.dtype), v)

    out = lax.map(one_block, (q_blocks, c_blocks, starts))
    return out.transpose(1, 0, 2, 3, 4).reshape(b, s, h * d)


def _fwd_setup_inputs(seed: int = 0) -> dict:
    key = jax.random.key(seed)
    ks = jax.random.split(key, 20)
    f32 = jnp.float32

    def w(k, shape, fan_in):
        return jax.random.normal(k, shape, f32) * (fan_in ** -0.5)

    def gain(k, shape):
        return 1.0 + 0.02 * jax.random.normal(k, shape, f32)

    return {
        "x": jax.random.normal(ks[0], (BATCH, SEQ, D_MODEL), f32),
        "p": jax.random.normal(ks[1], (DEPTH, BATCH, SEQ, PLE_DIM), f32),
        "norm_mix_g": gain(ks[2], (DEPTH, D_MODEL)),
        "w_in": w(ks[3], (DEPTH, D_MODEL, IN_COLS), D_MODEL),
        "b_forget": 2.0 + 0.1 * jax.random.normal(ks[4], (DEPTH, N_HEADS), f32),
        "conv_w": w(ks[5], (DEPTH, CONV_K, CONV_CH), CONV_K),
        "conv_b": 0.02 * jax.random.normal(ks[6], (DEPTH, CONV_CH), f32),
        "conv_ln_g": gain(ks[7], (DEPTH, CONV_CH)),
        "conv_ln_b": 0.02 * jax.random.normal(ks[8], (DEPTH, CONV_CH), f32),
        "w_conv_out": w(ks[9], (DEPTH, CONV_CH, D_MODEL), CONV_CH),
        "w_attn_out": w(ks[10], (DEPTH, ATTN_W, D_MODEL), ATTN_W),
        "w_out": w(ks[11], (DEPTH, D_MODEL, D_MODEL), D_MODEL),
        "norm_ffn_g": gain(ks[12], (DEPTH, D_MODEL)),
        "w_gate_up": w(ks[13], (DEPTH, D_MODEL, 2 * FFN_HIDDEN), D_MODEL),
        "w_down": w(ks[14], (DEPTH, FFN_HIDDEN, D_MODEL), FFN_HIDDEN),
        "norm_ple_g": gain(ks[15], (DEPTH, D_MODEL)),
        "w_ple_gate": w(ks[16], (DEPTH, D_MODEL, D_MODEL), D_MODEL),
        "w_ple_proj": w(ks[17], (DEPTH, PLE_DIM, D_MODEL), PLE_DIM),
        "final_g": gain(ks[18], (D_MODEL,)),
    }


def _fwd_reference(x, p, norm_mix_g, w_in, b_forget, conv_w, conv_b, conv_ln_g, conv_ln_b,
              w_conv_out, w_attn_out, w_out, norm_ffn_g, w_gate_up, w_down,
              norm_ple_g, w_ple_gate, w_ple_proj, final_g):
    b, s, _ = x.shape
    split_pts = np.cumsum([2 * CONV_CH, ATTN_W, ATTN_W, ATTN_W, N_HEADS, D_MODEL]).tolist()
    for i in range(DEPTH):
        h = rmsnorm(x, norm_mix_g[i])
        proj = h @ w_in[i]
        glu_in, q, k, v, f_logit, g_conv, g_attn = jnp.split(proj, split_pts, axis=-1)

        y_conv = conformer_conv(glu_in, conv_w[i], conv_b[i], conv_ln_g[i],
                                conv_ln_b[i], w_conv_out[i])

        q = q.reshape(b, s, N_HEADS, HEAD_DIM)
        k = k.reshape(b, s, N_HEADS, HEAD_DIM)
        v = v.reshape(b, s, N_HEADS, HEAD_DIM)
        log_f = jax.nn.log_sigmoid((f_logit + b_forget[i]).astype(jnp.float32))
        y_attn = forgetting_attention(q, k, v, log_f) @ w_attn_out[i]

        merged = jax.nn.sigmoid(g_conv) * y_conv + jax.nn.sigmoid(g_attn) * y_attn
        x = x + merged @ w_out[i]

        hf = rmsnorm(x, norm_ffn_g[i])
        gate, up = jnp.split(hf @ w_gate_up[i], 2, axis=-1)
        x = x + (jax.nn.silu(gate) * up) @ w_down[i]

        hp = rmsnorm(x, norm_ple_g[i])
        x = x + jax.nn.sigmoid(hp @ w_ple_gate[i]) * (p[i] @ w_ple_proj[i])
    return rmsnorm(x, final_g)


import jax as _jax
import jax.numpy as _jnp

TWIN_FORMAT = 'train_step'
FWD_PARAMS = ['x', 'p', 'norm_mix_g', 'w_in', 'b_forget', 'conv_w', 'conv_b', 'conv_ln_g', 'conv_ln_b', 'w_conv_out', 'w_attn_out', 'w_out', 'norm_ffn_g', 'w_gate_up', 'w_down', 'norm_ple_g', 'w_ple_gate', 'w_ple_proj', 'final_g']
TWIN_WEIGHTS = ['norm_mix_g', 'w_in', 'b_forget', 'conv_w', 'conv_b', 'conv_ln_g', 'conv_ln_b', 'w_conv_out', 'w_attn_out', 'w_out', 'norm_ffn_g', 'w_gate_up', 'w_down', 'norm_ple_g', 'w_ple_gate', 'w_ple_proj', 'final_g']
TWIN_DIFF_INPUT = 'x'
TWIN_INPUTS = ['x', 'p', 'norm_mix_g', 'w_in', 'b_forget', 'conv_w', 'conv_b', 'conv_ln_g', 'conv_ln_b', 'w_conv_out', 'w_attn_out', 'w_out', 'norm_ffn_g', 'w_gate_up', 'w_down', 'norm_ple_g', 'w_ple_gate', 'w_ple_proj', 'final_g', 'loss_target', 'm_norm_mix_g', 'm_w_in', 'm_b_forget', 'm_conv_w', 'm_conv_b', 'm_conv_ln_g', 'm_conv_ln_b', 'm_w_conv_out', 'm_w_attn_out', 'm_w_out', 'm_norm_ffn_g', 'm_w_gate_up', 'm_w_down', 'm_norm_ple_g', 'm_w_ple_gate', 'm_w_ple_proj', 'm_final_g', 'v_norm_mix_g', 'v_w_in', 'v_b_forget', 'v_conv_w', 'v_conv_b', 'v_conv_ln_g', 'v_conv_ln_b', 'v_w_conv_out', 'v_w_attn_out', 'v_w_out', 'v_norm_ffn_g', 'v_w_gate_up', 'v_w_down', 'v_norm_ple_g', 'v_w_ple_gate', 'v_w_ple_proj', 'v_final_g']
TWIN_OUTPUTS = ['loss', 'grad_x', 'grad_norm_mix_g', 'grad_w_in', 'grad_b_forget', 'grad_conv_w', 'grad_conv_b', 'grad_conv_ln_g', 'grad_conv_ln_b', 'grad_w_conv_out', 'grad_w_attn_out', 'grad_w_out', 'grad_norm_ffn_g', 'grad_w_gate_up', 'grad_w_down', 'grad_norm_ple_g', 'grad_w_ple_gate', 'grad_w_ple_proj', 'grad_final_g', 'delta_norm_mix_g', 'delta_w_in', 'delta_b_forget', 'delta_conv_w', 'delta_conv_b', 'delta_conv_ln_g', 'delta_conv_ln_b', 'delta_w_conv_out', 'delta_w_attn_out', 'delta_w_out', 'delta_norm_ffn_g', 'delta_w_gate_up', 'delta_w_down', 'delta_norm_ple_g', 'delta_w_ple_gate', 'delta_w_ple_proj', 'delta_final_g', 'new_m_norm_mix_g', 'new_m_w_in', 'new_m_b_forget', 'new_m_conv_w', 'new_m_conv_b', 'new_m_conv_ln_g', 'new_m_conv_ln_b', 'new_m_w_conv_out', 'new_m_w_attn_out', 'new_m_w_out', 'new_m_norm_ffn_g', 'new_m_w_gate_up', 'new_m_w_down', 'new_m_norm_ple_g', 'new_m_w_ple_gate', 'new_m_w_ple_proj', 'new_m_final_g', 'new_v_norm_mix_g', 'new_v_w_in', 'new_v_b_forget', 'new_v_conv_w', 'new_v_conv_b', 'new_v_conv_ln_g', 'new_v_conv_ln_b', 'new_v_w_conv_out', 'new_v_w_attn_out', 'new_v_w_out', 'new_v_norm_ffn_g', 'new_v_w_gate_up', 'new_v_w_down', 'new_v_norm_ple_g', 'new_v_w_ple_gate', 'new_v_w_ple_proj', 'new_v_final_g']
TWIN_LEAF_KINDS = {'loss': 'loss', 'grad_x': 'grad_x', 'grad_norm_mix_g': 'grad_w', 'grad_w_in': 'grad_w', 'grad_b_forget': 'grad_w', 'grad_conv_w': 'grad_w', 'grad_conv_b': 'grad_w', 'grad_conv_ln_g': 'grad_w', 'grad_conv_ln_b': 'grad_w', 'grad_w_conv_out': 'grad_w', 'grad_w_attn_out': 'grad_w', 'grad_w_out': 'grad_w', 'grad_norm_ffn_g': 'grad_w', 'grad_w_gate_up': 'grad_w', 'grad_w_down': 'grad_w', 'grad_norm_ple_g': 'grad_w', 'grad_w_ple_gate': 'grad_w', 'grad_w_ple_proj': 'grad_w', 'grad_final_g': 'grad_w', 'delta_norm_mix_g': 'delta_w', 'delta_w_in': 'delta_w', 'delta_b_forget': 'delta_w', 'delta_conv_w': 'delta_w', 'delta_conv_b': 'delta_w', 'delta_conv_ln_g': 'delta_w', 'delta_conv_ln_b': 'delta_w', 'delta_w_conv_out': 'delta_w', 'delta_w_attn_out': 'delta_w', 'delta_w_out': 'delta_w', 'delta_norm_ffn_g': 'delta_w', 'delta_w_gate_up': 'delta_w', 'delta_w_down': 'delta_w', 'delta_norm_ple_g': 'delta_w', 'delta_w_ple_gate': 'delta_w', 'delta_w_ple_proj': 'delta_w', 'delta_final_g': 'delta_w', 'new_m_norm_mix_g': 'new_m', 'new_m_w_in': 'new_m', 'new_m_b_forget': 'new_m', 'new_m_conv_w': 'new_m', 'new_m_conv_b': 'new_m', 'new_m_conv_ln_g': 'new_m', 'new_m_conv_ln_b': 'new_m', 'new_m_w_conv_out': 'new_m', 'new_m_w_attn_out': 'new_m', 'new_m_w_out': 'new_m', 'new_m_norm_ffn_g': 'new_m', 'new_m_w_gate_up': 'new_m', 'new_m_w_down': 'new_m', 'new_m_norm_ple_g': 'new_m', 'new_m_w_ple_gate': 'new_m', 'new_m_w_ple_proj': 'new_m', 'new_m_final_g': 'new_m', 'new_v_norm_mix_g': 'new_v', 'new_v_w_in': 'new_v', 'new_v_b_forget': 'new_v', 'new_v_conv_w': 'new_v', 'new_v_conv_b': 'new_v', 'new_v_conv_ln_g': 'new_v', 'new_v_conv_ln_b': 'new_v', 'new_v_w_conv_out': 'new_v', 'new_v_w_attn_out': 'new_v', 'new_v_w_out': 'new_v', 'new_v_norm_ffn_g': 'new_v', 'new_v_w_gate_up': 'new_v', 'new_v_w_down': 'new_v', 'new_v_norm_ple_g': 'new_v', 'new_v_w_ple_gate': 'new_v', 'new_v_w_ple_proj': 'new_v', 'new_v_final_g': 'new_v'}


def _forward(args):
    return _fwd_reference(*[args[k] for k in FWD_PARAMS])


def _output_shape():
    def fwd():
        inp = _fwd_setup_inputs(0)
        return _fwd_reference(*[inp[k] for k in FWD_PARAMS])
    out = _jax.eval_shape(fwd)
    return out.shape, out.dtype

N_MICROBATCH = 1
ADAM_LR = 0.001
ADAM_B1 = 0.9
ADAM_B2 = 0.999
ADAM_EPS = 1e-08
ADAM_WD = 0.01
ADAM_STEP = 10
PER_EXAMPLE_BATCH_AXIS = {'x': 0, 'p': 1, 'loss_target': 0}
SHARED_INPUTS = []
_WEIGHT_DTYPES = {'norm_mix_g': _jnp.float32, 'w_in': _jnp.float32, 'b_forget': _jnp.float32, 'conv_w': _jnp.float32, 'conv_b': _jnp.float32, 'conv_ln_g': _jnp.float32, 'conv_ln_b': _jnp.float32, 'w_conv_out': _jnp.float32, 'w_attn_out': _jnp.float32, 'w_out': _jnp.float32, 'norm_ffn_g': _jnp.float32, 'w_gate_up': _jnp.float32, 'w_down': _jnp.float32, 'norm_ple_g': _jnp.float32, 'w_ple_gate': _jnp.float32, 'w_ple_proj': _jnp.float32, 'final_g': _jnp.float32}
MOMENT_SCALE = {'norm_mix_g': 5.643883e-02, 'w_in': 2.716499e-02, 'b_forget': 2.104614e-01, 'conv_w': 5.288945e-02, 'conv_b': 1.135819e-01, 'conv_ln_g': 6.386986e-02, 'conv_ln_b': 5.631681e-02, 'w_conv_out': 3.672223e-02, 'w_attn_out': 2.575217e-02, 'w_out': 4.442565e-02, 'norm_ffn_g': 7.734848e-02, 'w_gate_up': 3.279609e-02, 'w_down': 5.358494e-02, 'norm_ple_g': 1.899183e-02, 'w_ple_gate': 1.888168e-02, 'w_ple_proj': 4.811107e-02, 'final_g': 3.199570e+01}


def _to_microbatches(a, axis):
    t = _jnp.moveaxis(a, axis, 0)
    t = t.reshape((N_MICROBATCH, t.shape[0] // N_MICROBATCH) + t.shape[1:])
    return _jnp.moveaxis(t, 1, axis + 1)


def setup_inputs(seed: int = 0) -> dict:
    inp = _fwd_setup_inputs(seed)
    key = _jax.random.fold_in(_jax.random.key(seed), 7919)
    shape, _ = _output_shape()
    out = dict(inp)
    out["loss_target"] = _jax.random.normal(_jax.random.fold_in(key, 0), shape, _jnp.float32)
    for i, name in enumerate(TWIN_WEIGHTS):
        w = inp[name].astype(_jnp.float32)
        if MOMENT_SCALE is None:
            s = _jnp.sqrt(_jnp.mean(_jnp.square(w)) + 1e-30)
        else:
            s = MOMENT_SCALE[name]
        km, kv = _jax.random.split(_jax.random.fold_in(key, i + 1))
        out[name] = w
        out["m_" + name] = s * _jax.random.normal(km, w.shape, _jnp.float32)
        out["v_" + name] = (s * s) * _jax.random.uniform(kv, w.shape, _jnp.float32, 0.5, 1.5)
    if N_MICROBATCH > 1:
        for name, axis in PER_EXAMPLE_BATCH_AXIS.items():
            out[name] = _to_microbatches(out[name], axis)
    return {'x': out['x'], 'p': out['p'], 'norm_mix_g': out['norm_mix_g'], 'w_in': out['w_in'], 'b_forget': out['b_forget'], 'conv_w': out['conv_w'], 'conv_b': out['conv_b'], 'conv_ln_g': out['conv_ln_g'], 'conv_ln_b': out['conv_ln_b'], 'w_conv_out': out['w_conv_out'], 'w_attn_out': out['w_attn_out'], 'w_out': out['w_out'], 'norm_ffn_g': out['norm_ffn_g'], 'w_gate_up': out['w_gate_up'], 'w_down': out['w_down'], 'norm_ple_g': out['norm_ple_g'], 'w_ple_gate': out['w_ple_gate'], 'w_ple_proj': out['w_ple_proj'], 'final_g': out['final_g'], 'loss_target': out['loss_target'], 'm_norm_mix_g': out['m_norm_mix_g'], 'm_w_in': out['m_w_in'], 'm_b_forget': out['m_b_forget'], 'm_conv_w': out['m_conv_w'], 'm_conv_b': out['m_conv_b'], 'm_conv_ln_g': out['m_conv_ln_g'], 'm_conv_ln_b': out['m_conv_ln_b'], 'm_w_conv_out': out['m_w_conv_out'], 'm_w_attn_out': out['m_w_attn_out'], 'm_w_out': out['m_w_out'], 'm_norm_ffn_g': out['m_norm_ffn_g'], 'm_w_gate_up': out['m_w_gate_up'], 'm_w_down': out['m_w_down'], 'm_norm_ple_g': out['m_norm_ple_g'], 'm_w_ple_gate': out['m_w_ple_gate'], 'm_w_ple_proj': out['m_w_ple_proj'], 'm_final_g': out['m_final_g'], 'v_norm_mix_g': out['v_norm_mix_g'], 'v_w_in': out['v_w_in'], 'v_b_forget': out['v_b_forget'], 'v_conv_w': out['v_conv_w'], 'v_conv_b': out['v_conv_b'], 'v_conv_ln_g': out['v_conv_ln_g'], 'v_conv_ln_b': out['v_conv_ln_b'], 'v_w_conv_out': out['v_w_conv_out'], 'v_w_attn_out': out['v_w_attn_out'], 'v_w_out': out['v_w_out'], 'v_norm_ffn_g': out['v_norm_ffn_g'], 'v_w_gate_up': out['v_w_gate_up'], 'v_w_down': out['v_w_down'], 'v_norm_ple_g': out['v_norm_ple_g'], 'v_w_ple_gate': out['v_w_ple_gate'], 'v_w_ple_proj': out['v_w_ple_proj'], 'v_final_g': out['v_final_g']}


def _loss(weights, diff, rest, loss_target):
    with _jax.named_scope("forward"):
        args = {**rest, TWIN_DIFF_INPUT: diff, **{k: w.astype(_WEIGHT_DTYPES[k]) for k, w in weights.items()}}
        y = _forward(args)
    with _jax.named_scope("loss_head"):
        err = _jnp.square(y.astype(_jnp.float32) - loss_target)
        return 0.5 * _jnp.sum(_jnp.mean(err, axis=-1)) if err.ndim else 0.5 * err


def _adamw(w, g, m, v):
    m = ADAM_B1 * m + (1.0 - ADAM_B1) * g
    v = ADAM_B2 * v + (1.0 - ADAM_B2) * _jnp.square(g)
    m_hat = m / (1.0 - ADAM_B1 ** ADAM_STEP)
    v_hat = v / (1.0 - ADAM_B2 ** ADAM_STEP)
    delta = -ADAM_LR * (m_hat / (_jnp.sqrt(v_hat) + ADAM_EPS) + ADAM_WD * w)
    return delta, m, v


def reference(x, p, norm_mix_g, w_in, b_forget, conv_w, conv_b, conv_ln_g, conv_ln_b, w_conv_out, w_attn_out, w_out, norm_ffn_g, w_gate_up, w_down, norm_ple_g, w_ple_gate, w_ple_proj, final_g, loss_target, m_norm_mix_g, m_w_in, m_b_forget, m_conv_w, m_conv_b, m_conv_ln_g, m_conv_ln_b, m_w_conv_out, m_w_attn_out, m_w_out, m_norm_ffn_g, m_w_gate_up, m_w_down, m_norm_ple_g, m_w_ple_gate, m_w_ple_proj, m_final_g, v_norm_mix_g, v_w_in, v_b_forget, v_conv_w, v_conv_b, v_conv_ln_g, v_conv_ln_b, v_w_conv_out, v_w_attn_out, v_w_out, v_norm_ffn_g, v_w_gate_up, v_w_down, v_norm_ple_g, v_w_ple_gate, v_w_ple_proj, v_final_g):
    given = dict(x=x, p=p, norm_mix_g=norm_mix_g, w_in=w_in, b_forget=b_forget, conv_w=conv_w, conv_b=conv_b, conv_ln_g=conv_ln_g, conv_ln_b=conv_ln_b, w_conv_out=w_conv_out, w_attn_out=w_attn_out, w_out=w_out, norm_ffn_g=norm_ffn_g, w_gate_up=w_gate_up, w_down=w_down, norm_ple_g=norm_ple_g, w_ple_gate=w_ple_gate, w_ple_proj=w_ple_proj, final_g=final_g, loss_target=loss_target, m_norm_mix_g=m_norm_mix_g, m_w_in=m_w_in, m_b_forget=m_b_forget, m_conv_w=m_conv_w, m_conv_b=m_conv_b, m_conv_ln_g=m_conv_ln_g, m_conv_ln_b=m_conv_ln_b, m_w_conv_out=m_w_conv_out, m_w_attn_out=m_w_attn_out, m_w_out=m_w_out, m_norm_ffn_g=m_norm_ffn_g, m_w_gate_up=m_w_gate_up, m_w_down=m_w_down, m_norm_ple_g=m_norm_ple_g, m_w_ple_gate=m_w_ple_gate, m_w_ple_proj=m_w_ple_proj, m_final_g=m_final_g, v_norm_mix_g=v_norm_mix_g, v_w_in=v_w_in, v_b_forget=v_b_forget, v_conv_w=v_conv_w, v_conv_b=v_conv_b, v_conv_ln_g=v_conv_ln_g, v_conv_ln_b=v_conv_ln_b, v_w_conv_out=v_w_conv_out, v_w_attn_out=v_w_attn_out, v_w_out=v_w_out, v_norm_ffn_g=v_norm_ffn_g, v_w_gate_up=v_w_gate_up, v_w_down=v_w_down, v_norm_ple_g=v_norm_ple_g, v_w_ple_gate=v_w_ple_gate, v_w_ple_proj=v_w_ple_proj, v_final_g=v_final_g)
    weights = {n: given[n] for n in TWIN_WEIGHTS}
    shared = {n: given[n] for n in SHARED_INPUTS}
    per_example = {n: given[n] for n in ['x', 'p']}
    grad_fn = _jax.value_and_grad(_loss, argnums=(0, 1))

    def one_microbatch(ex, loss_target):
        ex = dict(ex)
        diff = ex.pop(TWIN_DIFF_INPUT)
        return grad_fn(weights, diff, {**shared, **ex}, loss_target)

    if N_MICROBATCH == 1:
        loss, (grad_w, grad_x) = one_microbatch(per_example, given["loss_target"])
    else:
        def body(carry, xs):
            loss_sum, grad_sum = carry
            l_k, (gw_k, gx_k) = one_microbatch(xs[0], xs[1])
            with _jax.named_scope("update"):
                return (loss_sum + l_k, _jax.tree.map(_jnp.add, grad_sum, gw_k)), gx_k

        init = (_jnp.zeros((), _jnp.float32), _jax.tree.map(_jnp.zeros_like, weights))
        (loss, grad_w), grad_x = _jax.lax.scan(body, init, (per_example, given["loss_target"]))
    with _jax.named_scope("update"):
        delta_w, new_m, new_v = {}, {}, {}
        for n in TWIN_WEIGHTS:
            delta_w[n], new_m[n], new_v[n] = _adamw(weights[n], grad_w[n], given["m_" + n], given["v_" + n])
    return (loss, grad_x, *[grad_w[n] for n in TWIN_WEIGHTS], *[delta_w[n] for n in TWIN_WEIGHTS],
            *[new_m[n] for n in TWIN_WEIGHTS], *[new_v[n] for n in TWIN_WEIGHTS])
```

```python
import functools
import math

import jax
import jax.numpy as jnp
from jax import lax
from jax.experimental import pallas as pl
from jax.experimental.pallas import tpu as pltpu

F32 = jnp.float32
BF16 = jnp.bfloat16
MESH = pl.DeviceIdType.MESH
ANY = pl.BlockSpec(memory_space=pl.ANY)

N_DEV = 8
N_HEADS = 8
HEAD_DIM = 128
CONV_K = 31
HALO = 32
EPS = 1e-6
NEG_INF = -1e30
ADAM_LR = 0.001
ADAM_B1 = 0.9
ADAM_B2 = 0.999
ADAM_EPS = 1e-08
ADAM_WD = 0.01
ADAM_STEP = 10

V7X_VMEM_BYTES = 64 * 1024 * 1024
VMEM_LIMIT = V7X_VMEM_BYTES * 3 // 4
LANES = 128
ROW_TILE = 256
ATTN_TILE = 512
CUM_TILE = 256


def _params(*sem):
    return pltpu.CompilerParams(dimension_semantics=sem, vmem_limit_bytes=VMEM_LIMIT)


def _tile(dim, pref, mult=LANES):
    t = min(pref, dim)
    t -= t % mult
    while t >= mult:
        if dim % t == 0:
            return t
        t -= mult
    return dim


def _sigmoid(x):
    return 1.0 / (1.0 + jnp.exp(-x))


def _place():
    x, y, c = lax.axis_index("x"), lax.axis_index("y"), lax.axis_index("c")
    chips = [(1 - x, y), (x, 1 - y), (1 - x, 1 - y)]
    return x, y, c, chips


def _all_gather(shards, name):
    n = len(shards)

    def body(*refs):
        ins, outs = refs[:n], refs[n:2 * n]
        send_sems, recv_sems, local_sems = refs[2 * n:]
        x, y, c, chips = _place()
        me, sibling = (x, y, c), (x, y, 1 - c)

        def idx(px, py, pc):
            return 4 * px + 2 * py + pc

        def copy(t, k, block, to, src=None):
            rows = outs[t].at[idx(*block)]
            return pltpu.make_async_remote_copy(
                src_ref=rows if src is None else src, dst_ref=rows,
                send_sem=send_sems.at[t, k], recv_sem=recv_sems.at[t, k],
                device_id=to, device_id_type=MESH)

        local = [pltpu.make_async_copy(ins[t], outs[t].at[idx(*me)], local_sems.at[t]) for t in range(n)]
        for cp in local:
            cp.start()
        first = []
        for t in range(n):
            first.append(copy(t, 0, me, sibling, src=ins[t]))
            first += [copy(t, 1 + j, me, (*chip, c), src=ins[t]) for j, chip in enumerate(chips)]
        for cp in first:
            cp.start()
        passed = []
        for j, chip in enumerate(chips):
            for t in range(n):
                copy(t, 1 + j, (*chip, c), me).wait_recv()
                cp = copy(t, 4 + j, (*chip, c), sibling)
                cp.start()
                passed.append(cp)
        for t in range(n):
            copy(t, 0, sibling, me).wait_recv()
        for j, chip in enumerate(chips):
            for t in range(n):
                copy(t, 4 + j, (*chip, 1 - c), me).wait_recv()
        for cp in first + passed:
            cp.wait_send()
        for cp in local:
            cp.wait()

    return pl.pallas_call(
        body, name=name,
        out_shape=[jax.ShapeDtypeStruct((N_DEV,) + s.shape, s.dtype) for s in shards],
        in_specs=[ANY] * n, out_specs=[ANY] * n,
        scratch_shapes=[pltpu.SemaphoreType.DMA((n, 7)), pltpu.SemaphoreType.DMA((n, 7)),
                        pltpu.SemaphoreType.DMA((n,))],
    )(*shards)


def _pair_exchange(grads, name):
    n = len(grads)

    def body(*refs):
        ins, mine, theirs = refs[:n], refs[n:2 * n], refs[2 * n:3 * n]
        send_sems, recv_sems, local_sems = refs[3 * n:]
        x, y, c, _ = _place()
        copies = []
        for t in range(n):
            for q in range(4):
                keep = pltpu.make_async_copy(ins[t].at[2 * q + c], mine[t].at[q], local_sems.at[t, q])
                give = pltpu.make_async_remote_copy(
                    src_ref=ins[t].at[2 * q + 1 - c], dst_ref=theirs[t].at[q],
                    send_sem=send_sems.at[t, q], recv_sem=recv_sems.at[t, q],
                    device_id=(x, y, 1 - c), device_id_type=MESH)
                keep.start()
                give.start()
                copies += [keep, give]
        for cp in copies:
            cp.wait()

    shapes = [jax.ShapeDtypeStruct((4,) + g.shape[1:], g.dtype) for g in grads]
    outs = pl.pallas_call(
        body, name=name, out_shape=shapes + shapes,
        in_specs=[ANY] * n, out_specs=[ANY] * (2 * n),
        scratch_shapes=[pltpu.SemaphoreType.DMA((n, 4)), pltpu.SemaphoreType.DMA((n, 4)),
                        pltpu.SemaphoreType.DMA((n, 4))],
    )(*grads)
    return outs[:n], outs[n:]


def _chip_exchange(parts, name):
    n = len(parts)

    def body(*refs):
        ins, outs = refs[:n], refs[n:2 * n]
        send_sems, recv_sems, local_sems = refs[2 * n:]
        x, y, c, chips = _place()
        copies = []
        for t in range(n):
            own = pltpu.make_async_copy(ins[t].at[2 * x + y], outs[t].at[0], local_sems.at[t])
            own.start()
            copies.append(own)
            for j, (px, py) in enumerate(chips):
                cp = pltpu.make_async_remote_copy(
                    src_ref=ins[t].at[2 * px + py], dst_ref=outs[t].at[1 + j],
                    send_sem=send_sems.at[t, j], recv_sem=recv_sems.at[t, j],
                    device_id=(px, py, c), device_id_type=MESH)
                cp.start()
                copies.append(cp)
        for cp in copies:
            cp.wait()

    return pl.pallas_call(
        body, name=name, out_shape=[jax.ShapeDtypeStruct(p.shape, p.dtype) for p in parts],
        in_specs=[ANY] * n, out_specs=[ANY] * n,
        scratch_shapes=[pltpu.SemaphoreType.DMA((n, 3)), pltpu.SemaphoreType.DMA((n, 3)),
                        pltpu.SemaphoreType.DMA((n,))],
    )(*parts)


_DOT_DIMS = {"nn": (((1,), (0,)), ((), ())), "nt": (((1,), (1,)), ((), ())), "tn": (((0,), (0,)), ((), ()))}


def _mm(a, b, mode, out_dtype, name, residual=None, tm=1024, tn=1024, tk=512):
    if mode == "tn":
        (k_dim, m_dim), n_dim = a.shape, b.shape[1]
    else:
        (m_dim, k_dim), n_dim = a.shape, (b.shape[1] if mode == "nn" else b.shape[0])
    tm, tn, tk = _tile(m_dim, tm), _tile(n_dim, tn), _tile(k_dim, tk)
    nk = k_dim // tk

    def body(*refs):
        if residual is None:
            a_ref, b_ref, o_ref, acc = refs
        else:
            a_ref, b_ref, r_ref, o_ref, acc = refs
        k = pl.program_id(2)

        @pl.when(k == 0)
        def _():
            acc[...] = jnp.zeros_like(acc)

        acc[...] += lax.dot_general(a_ref[...].astype(BF16), b_ref[...].astype(BF16), _DOT_DIMS[mode],
                                    preferred_element_type=F32)

        @pl.when(k == nk - 1)
        def _():
            r = acc[...]
            if residual is not None:
                r = r + r_ref[...].astype(F32)
            o_ref[...] = r.astype(o_ref.dtype)

    a_spec = pl.BlockSpec((tk, tm), lambda i, j, k: (k, i)) if mode == "tn" else pl.BlockSpec((tm, tk), lambda i, j, k: (i, k))
    b_spec = pl.BlockSpec((tn, tk), lambda i, j, k: (j, k)) if mode == "nt" else pl.BlockSpec((tk, tn), lambda i, j, k: (k, j))
    o_spec = pl.BlockSpec((tm, tn), lambda i, j, k: (i, j))
    ins, specs = [a, b], [a_spec, b_spec]
    if residual is not None:
        ins.append(residual)
        specs.append(o_spec)
    return pl.pallas_call(
        body, name=name, grid=(m_dim // tm, n_dim // tn, nk),
        out_shape=jax.ShapeDtypeStruct((m_dim, n_dim), out_dtype),
        in_specs=specs, out_specs=o_spec,
        scratch_shapes=[pltpu.VMEM((tm, tn), F32)],
        compiler_params=_params("parallel", "parallel", "arbitrary"),
    )(*ins)


def _rows(body, name, row_ins, full_ins, row_outs, acc_outs, ts=ROW_TILE):
    s = row_ins[0].shape[0]
    ts = _tile(s, ts, 16)
    n_ri, n_fi, n_ro = len(row_ins), len(full_ins), len(row_outs)

    def kernel(*refs):
        i = pl.program_id(0)
        ri, fi = refs[:n_ri], refs[n_ri:n_ri + n_fi]
        ro, ao = refs[n_ri + n_fi:n_ri + n_fi + n_ro], refs[n_ri + n_fi + n_ro:]

        if ao:
            @pl.when(i == 0)
            def _():
                for r in ao:
                    r[...] = jnp.zeros_like(r)

        body(i, ri, fi, ro, ao)

    def row_spec(cols):
        return pl.BlockSpec((ts, cols), lambda i: (i, 0))

    def full_spec(shape):
        return pl.BlockSpec(shape, lambda i: (0,) * len(shape))

    return pl.pallas_call(
        kernel, name=name, grid=(s // ts,),
        out_shape=[jax.ShapeDtypeStruct((s, c), d) for c, d in row_outs]
        + [jax.ShapeDtypeStruct(sh, d) for sh, d in acc_outs],
        in_specs=[row_spec(a.shape[1]) for a in row_ins] + [full_spec(a.shape) for a in full_ins],
        out_specs=[row_spec(c) for c, _ in row_outs] + [full_spec(sh) for sh, _ in acc_outs],
        compiler_params=_params("arbitrary"),
    )(*row_ins, *full_ins)


def _rms_stats(x):
    return lax.rsqrt(jnp.mean(x * x, axis=-1, keepdims=True) + EPS)


def _rms_fwd(x, g, name):
    def body(i, ri, fi, ro, ao):
        xv = ri[0][...]
        ro[0][...] = (xv * _rms_stats(xv) * fi[0][...]).astype(BF16)

    return _rows(body, name, [x], [g], [(x.shape[1], BF16)], [])[0]


def _rms_bwd_rows(xv, gv, dh):
    r = _rms_stats(xv)
    xhat = xv * r
    dxhat = dh * gv
    dx = r * (dxhat - xhat * jnp.mean(dxhat * xhat, axis=-1, keepdims=True))
    return dx, jnp.sum(dh * xhat, axis=0, keepdims=True)


def _rms_bwd(x, g, dh, dres, name):
    d = x.shape[1]

    def body(i, ri, fi, ro, ao):
        dx, dg = _rms_bwd_rows(ri[0][...], fi[0][...], ri[1][...])
        dx = dx + ri[2][...]
        ro[0][...] = dx
        ro[1][...] = dx.astype(BF16)
        ao[0][...] += dg

    return _rows(body, name, [x, dh, dres], [g], [(d, F32), (d, BF16)], [((1, d), F32)])


def _loss_head(x, g, target, name):
    d = x.shape[1]

    def body(i, ri, fi, ro, ao):
        xv, gv = ri[0][...], fi[0][...]
        y = xv * _rms_stats(xv) * gv
        err = y - ri[1][...]
        ao[1][...] += 0.5 * jnp.sum(jnp.mean(err * err, axis=-1, keepdims=True))
        dx, dg = _rms_bwd_rows(xv, gv, err * (1.0 / d))
        ro[0][...] = dx
        ao[0][...] += dg

    return _rows(body, name, [x, target], [g], [(d, F32)], [((1, d), F32), ((8, LANES), F32)])


def _merge_fwd(gg, y_conv, y_attn, name):
    d = y_conv.shape[1]

    def body(i, ri, fi, ro, ao):
        gv = ri[0][...]
        m = _sigmoid(gv[:, :d]) * ri[1][...] + _sigmoid(gv[:, d:]) * ri[2][...]
        ro[0][...] = m.astype(BF16)

    return _rows(body, name, [gg, y_conv, y_attn], [], [(d, BF16)], [])[0]


def _merge_bwd(gg, y_conv, y_attn, dm, name):
    d = y_conv.shape[1]

    def body(i, ri, fi, ro, ao):
        gv, yc, ya, dmv = ri[0][...], ri[1][...], ri[2][...], ri[3][...]
        sc, sa = _sigmoid(gv[:, :d]), _sigmoid(gv[:, d:])
        ro[0][...] = (dmv * sc).astype(BF16)
        ro[1][...] = (dmv * sa).astype(BF16)
        ro[2][:, :d] = (dmv * yc * sc * (1.0 - sc)).astype(BF16)
        ro[2][:, d:] = (dmv * ya * sa * (1.0 - sa)).astype(BF16)

    return _rows(body, name, [gg, y_conv, y_attn, dm], [], [(d, BF16), (d, BF16), (2 * d, BF16)], [])


def _swiglu_fwd(gu, name):
    f = gu.shape[1] // 2

    def body(i, ri, fi, ro, ao):
        v = ri[0][...]
        gate, up = v[:, :f], v[:, f:]
        ro[0][...] = (gate * _sigmoid(gate) * up).astype(BF16)

    return _rows(body, name, [gu], [], [(f, BF16)], [], ts=128)[0]


def _swiglu_bwd(gu, dact, name):
    f = gu.shape[1] // 2

    def body(i, ri, fi, ro, ao):
        v, dv = ri[0][...], ri[1][...]
        gate, up = v[:, :f], v[:, f:]
        sg = _sigmoid(gate)
        ro[0][:, :f] = (dv * up * sg * (1.0 + gate * (1.0 - sg))).astype(BF16)
        ro[0][:, f:] = (dv * gate * sg).astype(BF16)

    return _rows(body, name, [gu, dact], [], [(2 * f, BF16)], [], ts=128)[0]


def _ple_fwd(x, z, pp, name):
    d = x.shape[1]

    def body(i, ri, fi, ro, ao):
        ro[0][...] = ri[0][...] + _sigmoid(ri[1][...]) * ri[2][...]

    return _rows(body, name, [x, z, pp], [], [(d, F32)], [])[0]


def _ple_bwd(dx, z, pp, name):
    d = dx.shape[1]

    def body(i, ri, fi, ro, ao):
        dv, sz = ri[0][...], _sigmoid(ri[1][...])
        ro[0][...] = (dv * ri[2][...] * sz * (1.0 - sz)).astype(BF16)
        ro[1][...] = (dv * sz).astype(BF16)

    return _rows(body, name, [dx, z, pp], [], [(d, BF16), (d, BF16)], [])


def _triangle(n, lower):
    r = lax.broadcasted_iota(jnp.int32, (n, n), 0)
    c = lax.broadcasted_iota(jnp.int32, (n, n), 1)
    return jnp.where((c <= r) if lower else (c >= r), 1.0, 0.0).astype(F32)


def _forget_fwd(f, bias, name):
    s = f.shape[0]
    tb = _tile(s, CUM_TILE, 8)

    def body(f_ref, b_ref, c_ref, carry):
        @pl.when(pl.program_id(0) == 0)
        def _():
            carry[...] = jnp.zeros_like(carry)

        xv = f_ref[...] + b_ref[...]
        logf = jnp.minimum(xv, 0.0) - jnp.log(1.0 + jnp.exp(-jnp.abs(xv)))
        cv = jnp.dot(_triangle(tb, True), logf, precision=lax.Precision.HIGHEST, preferred_element_type=F32) + carry[...]
        c_ref[...] = cv
        carry[...] = cv[tb - 1:tb, :]

    return pl.pallas_call(
        body, name=name, grid=(s // tb,), out_shape=jax.ShapeDtypeStruct((s, LANES), F32),
        in_specs=[pl.BlockSpec((tb, LANES), lambda i: (i, 0)), pl.BlockSpec((1, LANES), lambda i: (0, 0))],
        out_specs=pl.BlockSpec((tb, LANES), lambda i: (i, 0)),
        scratch_shapes=[pltpu.VMEM((1, LANES), F32)],
        compiler_params=_params("arbitrary"),
    )(f, bias)


def _forget_bwd(f, bias, dc, name):
    s = f.shape[0]
    tb = _tile(s, CUM_TILE, 16)
    nb = s // tb

    def body(f_ref, b_ref, dc_ref, df_ref, db_ref, carry):
        @pl.when(pl.program_id(0) == 0)
        def _():
            carry[...] = jnp.zeros_like(carry)
            db_ref[...] = jnp.zeros_like(db_ref)

        dlog = jnp.dot(_triangle(tb, False), dc_ref[...], precision=lax.Precision.HIGHEST,
                       preferred_element_type=F32) + carry[...]
        carry[...] = dlog[0:1, :]
        dlogit = dlog * (1.0 - _sigmoid(f_ref[...] + b_ref[...]))
        df_ref[...] = dlogit.astype(BF16)
        db_ref[...] += jnp.sum(dlogit, axis=0, keepdims=True)

    back = lambda i: (nb - 1 - i, 0)
    return pl.pallas_call(
        body, name=name, grid=(nb,),
        out_shape=[jax.ShapeDtypeStruct((s, LANES), BF16), jax.ShapeDtypeStruct((1, LANES), F32)],
        in_specs=[pl.BlockSpec((tb, LANES), back), pl.BlockSpec((1, LANES), lambda i: (0, 0)),
                  pl.BlockSpec((tb, LANES), back)],
        out_specs=[pl.BlockSpec((tb, LANES), back), pl.BlockSpec((1, LANES), lambda i: (0, 0))],
        scratch_shapes=[pltpu.VMEM((1, LANES), F32)],
        compiler_params=_params("arbitrary"),
    )(f, bias, dc)


def _glu(block, ch):
    return block[:, :ch] * _sigmoid(block[:, ch:])


def _layernorm_rows(u1, g, b):
    mu = jnp.mean(u1, axis=-1, keepdims=True)
    cen = u1 - mu
    rstd = lax.rsqrt(jnp.mean(cen * cen, axis=-1, keepdims=True) + EPS)
    xhat = cen * rstd
    return xhat, rstd, xhat * g + b


def _conv_fwd(ag, conv_w, conv_b, ln_g, ln_b, name):
    s, ch = ag.shape[0], ag.shape[1] // 2
    ts = _tile(s, ROW_TILE, HALO)
    per = ts // HALO

    def body(cur_ref, halo_ref, w_ref, cb_ref, g_ref, b_ref, u1_ref, u3_ref, cat):
        i = pl.program_id(0)
        cat[HALO:, :] = _glu(cur_ref[...], ch)
        cat[:HALO, :] = jnp.where(i == 0, 0.0, _glu(halo_ref[...], ch))
        full = cat[...]
        acc = jnp.zeros((ts, ch), F32) + cb_ref[...]
        for j in range(CONV_K):
            shifted = full if j == CONV_K - 1 else pltpu.roll(full, CONV_K - 1 - j, 0)
            acc = acc + w_ref[j:j + 1, :] * shifted[HALO:, :]
        u1_ref[...] = acc
        _, _, u2 = _layernorm_rows(acc, g_ref[...], b_ref[...])
        u3_ref[...] = (u2 * _sigmoid(u2)).astype(BF16)

    whole = lambda shape: pl.BlockSpec(shape, lambda i: (0, 0))
    return pl.pallas_call(
        body, name=name, grid=(s // ts,),
        out_shape=[jax.ShapeDtypeStruct((s, ch), F32), jax.ShapeDtypeStruct((s, ch), BF16)],
        in_specs=[pl.BlockSpec((ts, 2 * ch), lambda i: (i, 0)),
                  pl.BlockSpec((HALO, 2 * ch), lambda i: (jnp.maximum(i * per - 1, 0), 0)),
                  whole(conv_w.shape), whole((1, ch)), whole((1, ch)), whole((1, ch))],
        out_specs=[pl.BlockSpec((ts, ch), lambda i: (i, 0)), pl.BlockSpec((ts, ch), lambda i: (i, 0))],
        scratch_shapes=[pltpu.VMEM((HALO + ts, ch), F32)],
        compiler_params=_params("arbitrary"),
    )(ag, ag, conv_w, conv_b, ln_g, ln_b)


def _conv_norm_bwd(u1, du3, ln_g, ln_b, name):
    ch = u1.shape[1]

    def body(i, ri, fi, ro, ao):
        g = fi[0][...]
        xhat, rstd, u2 = _layernorm_rows(ri[0][...], g, fi[1][...])
        sg = _sigmoid(u2)
        du2 = ri[1][...] * sg * (1.0 + u2 * (1.0 - sg))
        dxhat = du2 * g
        du1 = rstd * (dxhat - jnp.mean(dxhat, axis=-1, keepdims=True)
                      - xhat * jnp.mean(dxhat * xhat, axis=-1, keepdims=True))
        ro[0][...] = du1
        ao[0][...] += jnp.sum(du2 * xhat, axis=0, keepdims=True)
        ao[1][...] += jnp.sum(du2, axis=0, keepdims=True)
        ao[2][...] += jnp.sum(du1, axis=0, keepdims=True)

    return _rows(body, name, [u1, du3], [ln_g, ln_b], [(ch, F32)], [((1, ch), F32)] * 3)


def _conv_bwd(ag, du1, conv_w, name):
    s, ch = du1.shape
    ts = _tile(s, ROW_TILE, HALO)
    per = ts // HALO
    last = s // HALO - 1
    nb = s // ts

    def body(cur_ref, halo_ref, d_ref, dnext_ref, w_ref, dag_ref, dw_ref, cat, dcat):
        i = pl.program_id(0)

        @pl.when(i == 0)
        def _():
            dw_ref[...] = jnp.zeros_like(dw_ref)

        cur = cur_ref[...]
        cat[HALO:, :] = _glu(cur, ch)
        cat[:HALO, :] = jnp.where(i == 0, 0.0, _glu(halo_ref[...], ch))
        dcur = d_ref[...]
        dcat[:ts, :] = dcur
        dcat[ts:, :] = jnp.where(i == nb - 1, 0.0, dnext_ref[...])
        full, dfull = cat[...], dcat[...]
        du0 = jnp.zeros((ts, ch), F32)
        for j in range(CONV_K):
            back = CONV_K - 1 - j
            dshift = dfull if back == 0 else pltpu.roll(dfull, ts + HALO - back, 0)
            du0 = du0 + w_ref[j:j + 1, :] * dshift[:ts, :]
            ushift = full if back == 0 else pltpu.roll(full, back, 0)
            dw_ref[j:j + 1, :] += jnp.sum(dcur * ushift[HALO:, :], axis=0, keepdims=True)
        a, sg = cur[:, :ch], _sigmoid(cur[:, ch:])
        dag_ref[:, :ch] = (du0 * sg).astype(BF16)
        dag_ref[:, ch:] = (du0 * a * sg * (1.0 - sg)).astype(BF16)

    return pl.pallas_call(
        body, name=name, grid=(nb,),
        out_shape=[jax.ShapeDtypeStruct((s, 2 * ch), BF16), jax.ShapeDtypeStruct(conv_w.shape, F32)],
        in_specs=[pl.BlockSpec((ts, 2 * ch), lambda i: (i, 0)),
                  pl.BlockSpec((HALO, 2 * ch), lambda i: (jnp.maximum(i * per - 1, 0), 0)),
                  pl.BlockSpec((ts, ch), lambda i: (i, 0)),
                  pl.BlockSpec((HALO, ch), lambda i: (jnp.minimum((i + 1) * per, last), 0)),
                  pl.BlockSpec(conv_w.shape, lambda i: (0, 0))],
        out_specs=[pl.BlockSpec((ts, 2 * ch), lambda i: (i, 0)), pl.BlockSpec(conv_w.shape, lambda i: (0, 0))],
        scratch_shapes=[pltpu.VMEM((HALO + ts, ch), F32), pltpu.VMEM((ts + HALO, ch), F32)],
        compiler_params=_params("arbitrary"),
    )(ag, ag, du1, du1, conv_w)


_SCALE = 1.0 / math.sqrt(HEAD_DIM)


def _scores(q, k, c_col, c_row, i, j, t):
    s = lax.dot_general(q, k, _DOT_DIMS["nt"], preferred_element_type=F32) * _SCALE + (c_col - c_row)
    q_pos = i * t + lax.broadcasted_iota(jnp.int32, (t, t), 0)
    k_pos = j * t + lax.broadcasted_iota(jnp.int32, (t, t), 1)
    return s, k_pos <= q_pos


def _attn_fwd(qkv, c_col, c_row, name):
    s = qkv.shape[0]
    t = _tile(s, ATTN_TILE)
    nt = s // t
    h_, d = N_HEADS, HEAD_DIM

    def body(q_ref, k_ref, v_ref, cc_ref, cr_ref, o_ref, lse_ref, m_sc, l_sc, acc):
        i, j = pl.program_id(1), pl.program_id(2)

        @pl.when(j == 0)
        def _():
            m_sc[...] = jnp.full_like(m_sc, NEG_INF)
            l_sc[...] = jnp.zeros_like(l_sc)
            acc[...] = jnp.zeros_like(acc)

        @pl.when(j <= i)
        def _():
            sc, mask = _scores(q_ref[...], k_ref[...], cc_ref[0], cr_ref[0], i, j, t)
            sc = jnp.where(mask, sc, NEG_INF)
            m_new = jnp.maximum(m_sc[...], jnp.max(sc, axis=-1, keepdims=True))
            alpha = jnp.exp(m_sc[...] - m_new)
            p = jnp.exp(sc - m_new)
            l_sc[...] = alpha * l_sc[...] + jnp.sum(p, axis=-1, keepdims=True)
            acc[...] = alpha * acc[...] + jnp.dot(p.astype(BF16), v_ref[...], preferred_element_type=F32)
            m_sc[...] = m_new

        @pl.when(j == nt - 1)
        def _():
            o_ref[...] = (acc[...] / l_sc[...]).astype(BF16)
            lse_ref[0] = m_sc[...] + jnp.log(l_sc[...])

    return pl.pallas_call(
        body, name=name, grid=(h_, nt, nt),
        out_shape=[jax.ShapeDtypeStruct((s, h_ * d), BF16), jax.ShapeDtypeStruct((h_, s, 1), F32)],
        in_specs=[pl.BlockSpec((t, d), lambda h, i, j: (i, h)),
                  pl.BlockSpec((t, d), lambda h, i, j: (jnp.minimum(j, i), h_ + h)),
                  pl.BlockSpec((t, d), lambda h, i, j: (jnp.minimum(j, i), 2 * h_ + h)),
                  pl.BlockSpec((1, t, 1), lambda h, i, j: (h, i, 0)),
                  pl.BlockSpec((1, 1, t), lambda h, i, j: (h, 0, jnp.minimum(j, i)))],
        out_specs=[pl.BlockSpec((t, d), lambda h, i, j: (i, h)), pl.BlockSpec((1, t, 1), lambda h, i, j: (h, i, 0))],
        scratch_shapes=[pltpu.VMEM((t, 1), F32), pltpu.VMEM((t, 1), F32), pltpu.VMEM((t, d), F32)],
        compiler_params=_params("parallel", "parallel", "arbitrary"),
    )(qkv, qkv, qkv, c_col, c_row)


def _attn_delta(o, do, name):
    s = o.shape[0]
    t = _tile(s, ATTN_TILE)
    h_, d = N_HEADS, HEAD_DIM

    def body(o_ref, do_ref, out_ref):
        out_ref[0] = jnp.sum(o_ref[...].astype(F32) * do_ref[...], axis=-1, keepdims=True)

    return pl.pallas_call(
        body, name=name, grid=(h_, s // t), out_shape=jax.ShapeDtypeStruct((h_, s, 1), F32),
        in_specs=[pl.BlockSpec((t, d), lambda h, i: (i, h)), pl.BlockSpec((t, d), lambda h, i: (i, h))],
        out_specs=pl.BlockSpec((1, t, 1), lambda h, i: (h, i, 0)),
        compiler_params=_params("parallel", "parallel"),
    )(o, do)


def _softmax_grad(q, k, v, do, cc, cr, lse, delta, i, j, t):
    sc, mask = _scores(q, k, cc, cr, i, j, t)
    p = jnp.where(mask, jnp.exp(sc - lse), 0.0)
    dp = lax.dot_general(do, v, _DOT_DIMS["nt"], preferred_element_type=F32)
    return p, p * (dp - delta)


def _attn_bwd_kv(qkv, do, c_col, c_row, lse, delta, name):
    s = qkv.shape[0]
    t = _tile(s, ATTN_TILE)
    nt = s // t
    h_, d = N_HEADS, HEAD_DIM

    def body(q_ref, k_ref, v_ref, do_ref, cc_ref, cr_ref, lse_ref, dl_ref, dk_ref, dv_ref, dc_ref, dk_acc, dv_acc, dc_acc):
        j, i = pl.program_id(1), pl.program_id(2)

        @pl.when(i == 0)
        def _():
            dk_acc[...] = jnp.zeros_like(dk_acc)
            dv_acc[...] = jnp.zeros_like(dv_acc)
            dc_acc[...] = jnp.zeros_like(dc_acc)

        @pl.when(i >= j)
        def _():
            q, dov = q_ref[...], do_ref[...].astype(BF16)
            p, ds = _softmax_grad(q, k_ref[...], v_ref[...], dov, cc_ref[0], cr_ref[0], lse_ref[0], dl_ref[0], i, j, t)
            dv_acc[...] += lax.dot_general(p.astype(BF16), dov, _DOT_DIMS["tn"], preferred_element_type=F32)
            dk_acc[...] += lax.dot_general(ds.astype(BF16), q, _DOT_DIMS["tn"], preferred_element_type=F32)
            dc_acc[...] -= jnp.sum(ds, axis=0, keepdims=True)

        @pl.when(i == nt - 1)
        def _():
            dk_ref[...] = (dk_acc[...] * _SCALE).astype(BF16)
            dv_ref[...] = dv_acc[...].astype(BF16)
            dc_ref[0] = dc_acc[...]

    qi = lambda h, j, i: (jnp.maximum(i, j), h)
    col = lambda h, j, i: (h, jnp.maximum(i, j), 0)
    return pl.pallas_call(
        body, name=name, grid=(h_, nt, nt),
        out_shape=[jax.ShapeDtypeStruct((s, h_ * d), BF16), jax.ShapeDtypeStruct((s, h_ * d), BF16),
                   jax.ShapeDtypeStruct((h_, 1, s), F32)],
        in_specs=[pl.BlockSpec((t, d), qi),
                  pl.BlockSpec((t, d), lambda h, j, i: (j, h_ + h)),
                  pl.BlockSpec((t, d), lambda h, j, i: (j, 2 * h_ + h)),
                  pl.BlockSpec((t, d), qi),
                  pl.BlockSpec((1, t, 1), col),
                  pl.BlockSpec((1, 1, t), lambda h, j, i: (h, 0, j)),
                  pl.BlockSpec((1, t, 1), col), pl.BlockSpec((1, t, 1), col)],
        out_specs=[pl.BlockSpec((t, d), lambda h, j, i: (j, h)), pl.BlockSpec((t, d), lambda h, j, i: (j, h)),
                   pl.BlockSpec((1, 1, t), lambda h, j, i: (h, 0, j))],
        scratch_shapes=[pltpu.VMEM((t, d), F32), pltpu.VMEM((t, d), F32), pltpu.VMEM((1, t), F32)],
        compiler_params=_params("parallel", "parallel", "arbitrary"),
    )(qkv, qkv, qkv, do, c_col, c_row, lse, delta)


def _attn_bwd_q(qkv, do, c_col, c_row, lse, delta, name):
    s = qkv.shape[0]
    t = _tile(s, ATTN_TILE)
    nt = s // t
    h_, d = N_HEADS, HEAD_DIM

    def body(q_ref, k_ref, v_ref, do_ref, cc_ref, cr_ref, lse_ref, dl_ref, dq_ref, dc_ref, dq_acc, dc_acc):
        i, j = pl.program_id(1), pl.program_id(2)

        @pl.when(j == 0)
        def _():
            dq_acc[...] = jnp.zeros_like(dq_acc)
            dc_acc[...] = jnp.zeros_like(dc_acc)

        @pl.when(j <= i)
        def _():
            k = k_ref[...]
            _, ds = _softmax_grad(q_ref[...], k, v_ref[...], do_ref[...].astype(BF16), cc_ref[0], cr_ref[0],
                                  lse_ref[0], dl_ref[0], i, j, t)
            dq_acc[...] += jnp.dot(ds.astype(BF16), k, preferred_element_type=F32)
            dc_acc[...] += jnp.sum(ds, axis=-1, keepdims=True)

        @pl.when(j == nt - 1)
        def _():
            dq_ref[...] = (dq_acc[...] * _SCALE).astype(BF16)
            dc_ref[0] = dc_acc[...]

    kj = lambda h, i, j: jnp.minimum(j, i)
    col = lambda h, i, j: (h, i, 0)
    return pl.pallas_call(
        body, name=name, grid=(h_, nt, nt),
        out_shape=[jax.ShapeDtypeStruct((s, h_ * d), BF16), jax.ShapeDtypeStruct((h_, s, 1), F32)],
        in_specs=[pl.BlockSpec((t, d), lambda h, i, j: (i, h)),
                  pl.BlockSpec((t, d), lambda h, i, j: (kj(h, i, j), h_ + h)),
                  pl.BlockSpec((t, d), lambda h, i, j: (kj(h, i, j), 2 * h_ + h)),
                  pl.BlockSpec((t, d), lambda h, i, j: (i, h)),
                  pl.BlockSpec((1, t, 1), col),
                  pl.BlockSpec((1, 1, t), lambda h, i, j: (h, 0, kj(h, i, j))),
                  pl.BlockSpec((1, t, 1), col), pl.BlockSpec((1, t, 1), col)],
        out_specs=[pl.BlockSpec((t, d), lambda h, i, j: (i, h)), pl.BlockSpec((1, t, 1), col)],
        scratch_shapes=[pltpu.VMEM((t, d), F32), pltpu.VMEM((t, 1), F32)],
        compiler_params=_params("parallel", "parallel", "arbitrary"),
    )(qkv, qkv, qkv, do, c_col, c_row, lse, delta)


def _pair_sum(mine, theirs, name):
    _, l, r, c = mine.shape
    tr = _tile(r, 256, 16)

    def body(a_ref, b_ref, o_ref):
        o_ref[...] = (a_ref[...].astype(F32) + b_ref[...].astype(F32)).astype(BF16)

    spec = pl.BlockSpec((1, 1, tr, c), lambda q, k, i: (q, k, i, 0))
    return pl.pallas_call(
        body, name=name, grid=(4, l, r // tr), out_shape=jax.ShapeDtypeStruct(mine.shape, BF16),
        in_specs=[spec, spec], out_specs=spec, compiler_params=_params("parallel", "parallel", "parallel"),
    )(mine, theirs)


def _adamw(g, w, m, v):
    m = ADAM_B1 * m + (1.0 - ADAM_B1) * g
    v = ADAM_B2 * v + (1.0 - ADAM_B2) * (g * g)
    m_hat = m / (1.0 - ADAM_B1 ** ADAM_STEP)
    v_hat = v / (1.0 - ADAM_B2 ** ADAM_STEP)
    return -ADAM_LR * (m_hat / (jnp.sqrt(v_hat) + ADAM_EPS) + ADAM_WD * w), m, v


def _sum_adamw(parts, w, m, v, name):
    n_parts, l, r, c = parts.shape
    tr = _tile(r, 128, 16)

    def body(p_ref, w_ref, m_ref, v_ref, g_ref, d_ref, nm_ref, nv_ref):
        g = p_ref[0].astype(F32)
        for k in range(1, n_parts):
            g = g + p_ref[k].astype(F32)
        g_ref[...] = g
        d_ref[...], nm_ref[...], nv_ref[...] = _adamw(g, w_ref[...], m_ref[...], v_ref[...])

    spec = pl.BlockSpec((1, tr, c), lambda k, i: (k, i, 0))
    return pl.pallas_call(
        body, name=name, grid=(l, r // tr), out_shape=[jax.ShapeDtypeStruct(w.shape, F32)] * 4,
        in_specs=[pl.BlockSpec((n_parts, 1, tr, c), lambda k, i: (0, k, i, 0)), spec, spec, spec],
        out_specs=[spec] * 4, compiler_params=_params("parallel", "parallel"),
    )(parts, w, m, v)


def _unshard_cols(gathered):
    return jnp.transpose(gathered, (1, 0, 2)).reshape(gathered.shape[1], -1)


def _shard_cols(full):
    k = full.shape[0]
    return jnp.transpose(full.reshape(k, N_DEV, -1), (1, 0, 2))


def _layer_weights(gathered, layer, d_model):
    w = {name: arr[:, layer] for name, arr in gathered.items()}
    hd = N_HEADS * HEAD_DIM
    w_in = _unshard_cols(w["w_in"])
    f0 = d_model + 3 * hd
    out = {
        "w_ag": w_in[:, :d_model], "w_qkv": w_in[:, d_model:f0],
        "w_f": jnp.pad(w_in[:, f0:f0 + N_HEADS], ((0, 0), (0, LANES - N_HEADS))),
        "w_gg": w_in[:, f0 + N_HEADS:],
        "w_conv_out": _unshard_cols(w["w_conv_out"]), "w_attn_out": _unshard_cols(w["w_attn_out"]),
        "w_out": w["w_out"].reshape(-1, d_model), "w_gate_up": _unshard_cols(w["w_gate_up"]),
        "w_down": w["w_down"].reshape(-1, d_model), "w_ple_gate": w["w_ple_gate"].reshape(-1, d_model),
        "w_ple_proj": _unshard_cols(w["w_ple_proj"]),
        "conv_w": jnp.transpose(w["conv_w"], (1, 0, 2)).reshape(CONV_K, -1),
    }
    return out


def _pad_lanes(vec):
    return jnp.pad(vec, (0, LANES - vec.shape[0]))[None, :]


def _layer_fwd(x, p, w, small, tag):
    h = _rms_fwd(x, small["norm_mix_g"], f"rms_mix_{tag}")
    ag = _mm(h, w["w_ag"], "nn", F32, f"proj_glu_{tag}")
    qkv = _mm(h, w["w_qkv"], "nn", BF16, f"proj_qkv_{tag}")
    gg = _mm(h, w["w_gg"], "nn", F32, f"proj_gates_{tag}")
    f = _mm(h, w["w_f"], "nn", F32, f"proj_forget_{tag}")
    u1, u3 = _conv_fwd(ag, w["conv_w"], small["conv_b"], small["conv_ln_g"], small["conv_ln_b"], f"conv_{tag}")
    y_conv = _mm(u3, w["w_conv_out"], "nn", F32, f"conv_out_{tag}")
    c = _forget_fwd(f, small["b_forget"], f"forget_{tag}")
    c_heads = jnp.transpose(c[:, :N_HEADS])
    c_col, c_row = c_heads[:, :, None], c_heads[:, None, :]
    o, lse = _attn_fwd(qkv, c_col, c_row, f"attn_{tag}")
    y_attn = _mm(o, w["w_attn_out"], "nn", F32, f"attn_out_{tag}")
    merged = _merge_fwd(gg, y_conv, y_attn, f"merge_{tag}")
    x1 = _mm(merged, w["w_out"], "nn", F32, f"mix_out_{tag}", residual=x)
    hf = _rms_fwd(x1, small["norm_ffn_g"], f"rms_ffn_{tag}")
    gu = _mm(hf, w["w_gate_up"], "nn", F32, f"ffn_up_{tag}")
    act = _swiglu_fwd(gu, f"swiglu_{tag}")
    x2 = _mm(act, w["w_down"], "nn", F32, f"ffn_down_{tag}", residual=x1)
    hp = _rms_fwd(x2, small["norm_ple_g"], f"rms_ple_{tag}")
    z = _mm(hp, w["w_ple_gate"], "nn", F32, f"ple_gate_{tag}")
    pp = _mm(p, w["w_ple_proj"], "nn", F32, f"ple_proj_{tag}")
    x3 = _ple_fwd(x2, z, pp, f"ple_{tag}")
    saved = dict(x=x, p=p, h=h, ag=ag, qkv=qkv, gg=gg, f=f, u1=u1, u3=u3, y_conv=y_conv, c_col=c_col, c_row=c_row,
                 o=o, lse=lse, y_attn=y_attn, merged=merged, x1=x1, hf=hf, gu=gu, act=act, x2=x2, hp=hp, z=z, pp=pp)
    return x3, saved


def _layer_bwd(dx3, sv, w, small, tag):
    d_model = dx3.shape[1]
    gw, gs = {}, {}
    dz, dpp = _ple_bwd(dx3, sv["z"], sv["pp"], f"ple_bwd_{tag}")
    gw["w_ple_proj"] = _mm(sv["p"], dpp, "tn", F32, f"ple_proj_dw_{tag}")
    gw["w_ple_gate"] = _mm(sv["hp"], dz, "tn", F32, f"ple_gate_dw_{tag}")
    dhp = _mm(dz, w["w_ple_gate"], "nt", F32, f"ple_gate_dx_{tag}")
    dx2, dx2_b, gs["norm_ple_g"] = _rms_bwd(sv["x2"], small["norm_ple_g"], dhp, dx3, f"rms_ple_bwd_{tag}")
    gw["w_down"] = _mm(sv["act"], dx2_b, "tn", F32, f"ffn_down_dw_{tag}")
    dact = _mm(dx2_b, w["w_down"], "nt", F32, f"ffn_down_dx_{tag}")
    dgu = _swiglu_bwd(sv["gu"], dact, f"swiglu_bwd_{tag}")
    gw["w_gate_up"] = _mm(sv["hf"], dgu, "tn", F32, f"ffn_up_dw_{tag}")
    dhf = _mm(dgu, w["w_gate_up"], "nt", F32, f"ffn_up_dx_{tag}")
    dx1, dx1_b, gs["norm_ffn_g"] = _rms_bwd(sv["x1"], small["norm_ffn_g"], dhf, dx2, f"rms_ffn_bwd_{tag}")
    gw["w_out"] = _mm(sv["merged"], dx1_b, "tn", F32, f"mix_out_dw_{tag}")
    dm = _mm(dx1_b, w["w_out"], "nt", F32, f"mix_out_dx_{tag}")
    dyc, dya, dgg = _merge_bwd(sv["gg"], sv["y_conv"], sv["y_attn"], dm, f"merge_bwd_{tag}")
    gw["w_attn_out"] = _mm(sv["o"], dya, "tn", F32, f"attn_out_dw_{tag}")
    do = _mm(dya, w["w_attn_out"], "nt", F32, f"attn_out_dx_{tag}")
    delta = _attn_delta(sv["o"], do, f"attn_delta_{tag}")
    dk, dv, dc_row = _attn_bwd_kv(sv["qkv"], do, sv["c_col"], sv["c_row"], sv["lse"], delta, f"attn_bwd_kv_{tag}")
    dq, dc_col = _attn_bwd_q(sv["qkv"], do, sv["c_col"], sv["c_row"], sv["lse"], delta, f"attn_bwd_q_{tag}")
    dc = jnp.pad(jnp.transpose(dc_row[:, 0, :] + dc_col[:, :, 0]), ((0, 0), (0, LANES - N_HEADS)))
    df, db = _forget_bwd(sv["f"], small["b_forget"], dc, f"forget_bwd_{tag}")
    gs["b_forget"] = db[0, :N_HEADS]
    gw["w_conv_out"] = _mm(sv["u3"], dyc, "tn", F32, f"conv_out_dw_{tag}")
    du3 = _mm(dyc, w["w_conv_out"], "nt", F32, f"conv_out_dx_{tag}")
    du1, gs["conv_ln_g"], gs["conv_ln_b"], gs["conv_b"] = _conv_norm_bwd(
        sv["u1"], du3, small["conv_ln_g"], small["conv_ln_b"], f"conv_norm_bwd_{tag}")
    dag, gs["conv_w"] = _conv_bwd(sv["ag"], du1, w["conv_w"], f"conv_bwd_{tag}")
    dqkv = jnp.concatenate([dq, dk, dv], axis=1)
    g_ag = _mm(sv["h"], dag, "tn", F32, f"proj_glu_dw_{tag}")
    g_qkv = _mm(sv["h"], dqkv, "tn", F32, f"proj_qkv_dw_{tag}")
    g_f = _mm(sv["h"], df, "tn", F32, f"proj_forget_dw_{tag}")
    g_gg = _mm(sv["h"], dgg, "tn", F32, f"proj_gates_dw_{tag}")
    gw["w_in"] = jnp.concatenate([g_ag, g_qkv, g_f[:, :N_HEADS], g_gg], axis=1)
    dh = _mm(df, w["w_f"], "nt", F32, f"proj_forget_dx_{tag}")
    dh = _mm(dag, w["w_ag"], "nt", F32, f"proj_glu_dx_{tag}", residual=dh)
    dh = _mm(dqkv, w["w_qkv"], "nt", F32, f"proj_qkv_dx_{tag}", residual=dh)
    dh = _mm(dgg, w["w_gg"], "nt", F32, f"proj_gates_dx_{tag}", residual=dh)
    dx, _, gs["norm_mix_g"] = _rms_bwd(sv["x"], small["norm_mix_g"], dh, dx1, f"rms_mix_bwd_{tag}")
    for k in ("norm_ple_g", "norm_ffn_g", "norm_mix_g", "conv_ln_g", "conv_ln_b", "conv_b"):
        gs[k] = gs[k][0]
    return dx, gw, gs


_COL_SHARDED = ("w_in", "w_conv_out", "w_attn_out", "w_gate_up", "w_ple_proj")
_ROW_SHARDED = ("w_out", "w_down", "w_ple_gate")
_BIG = _COL_SHARDED + _ROW_SHARDED
_SMALL = ("norm_mix_g", "b_forget", "conv_b", "conv_ln_g", "conv_ln_b", "norm_ffn_g", "norm_ple_g")
_ORDER = ("norm_mix_g", "w_in", "b_forget", "conv_w", "conv_b", "conv_ln_g", "conv_ln_b", "w_conv_out", "w_attn_out",
          "w_out", "norm_ffn_g", "w_gate_up", "w_down", "norm_ple_g", "w_ple_gate", "w_ple_proj", "final_g")


def _pack(vectors, rows):
    flat = jnp.concatenate([v.reshape(-1) for v in vectors])
    return jnp.pad(flat, (0, rows * LANES - flat.shape[0])).reshape(1, rows, LANES)


def _unpack(packed, like):
    flat, out, at = packed.reshape(-1), [], 0
    for v in like:
        out.append(flat[at:at + v.size].reshape(v.shape))
        at += v.size
    return out


def kernel(x, p, norm_mix_g, w_in, b_forget, conv_w, conv_b, conv_ln_g, conv_ln_b, w_conv_out, w_attn_out, w_out, norm_ffn_g, w_gate_up, w_down, norm_ple_g, w_ple_gate, w_ple_proj, final_g, loss_target, m_norm_mix_g, m_w_in, m_b_forget, m_conv_w, m_conv_b, m_conv_ln_g, m_conv_ln_b, m_w_conv_out, m_w_attn_out, m_w_out, m_norm_ffn_g, m_w_gate_up, m_w_down, m_norm_ple_g, m_w_ple_gate, m_w_ple_proj, m_final_g, v_norm_mix_g, v_w_in, v_b_forget, v_conv_w, v_conv_b, v_conv_ln_g, v_conv_ln_b, v_w_conv_out, v_w_attn_out, v_w_out, v_norm_ffn_g, v_w_gate_up, v_w_down, v_norm_ple_g, v_w_ple_gate, v_w_ple_proj, v_final_g):
    weights = dict(norm_mix_g=norm_mix_g, w_in=w_in, b_forget=b_forget, conv_w=conv_w, conv_b=conv_b, conv_ln_g=conv_ln_g, conv_ln_b=conv_ln_b, w_conv_out=w_conv_out, w_attn_out=w_attn_out, w_out=w_out, norm_ffn_g=norm_ffn_g, w_gate_up=w_gate_up, w_down=w_down, norm_ple_g=norm_ple_g, w_ple_gate=w_ple_gate, w_ple_proj=w_ple_proj, final_g=final_g)
    mom1 = dict(norm_mix_g=m_norm_mix_g, w_in=m_w_in, b_forget=m_b_forget, conv_w=m_conv_w, conv_b=m_conv_b, conv_ln_g=m_conv_ln_g, conv_ln_b=m_conv_ln_b, w_conv_out=m_w_conv_out, w_attn_out=m_w_attn_out, w_out=m_w_out, norm_ffn_g=m_norm_ffn_g, w_gate_up=m_w_gate_up, w_down=m_w_down, norm_ple_g=m_norm_ple_g, w_ple_gate=m_w_ple_gate, w_ple_proj=m_w_ple_proj, final_g=m_final_g)
    mom2 = dict(norm_mix_g=v_norm_mix_g, w_in=v_w_in, b_forget=v_b_forget, conv_w=v_conv_w, conv_b=v_conv_b, conv_ln_g=v_conv_ln_g, conv_ln_b=v_conv_ln_b, w_conv_out=v_w_conv_out, w_attn_out=v_w_attn_out, w_out=v_w_out, norm_ffn_g=v_norm_ffn_g, w_gate_up=v_w_gate_up, w_down=v_w_down, norm_ple_g=v_norm_ple_g, w_ple_gate=v_w_ple_gate, w_ple_proj=v_w_ple_proj, final_g=v_final_g)
    depth = w_in.shape[0]
    d_model = x.shape[-1]
    xi, yi, ci = lax.axis_index("x"), lax.axis_index("y"), lax.axis_index("c")
    me = 4 * xi + 2 * yi + ci

    names = list(_BIG) + ["conv_w"]
    shards = [weights[k].astype(BF16) for k in _BIG] + [conv_w]
    gathered = dict(zip(names, _all_gather(shards, "gather_weights")))

    act = x[0]
    layers = []
    for l in range(depth):
        w = _layer_weights(gathered, l, d_model)
        small = {k: weights[k][l][None, :] for k in _SMALL if k != "b_forget"}
        small["b_forget"] = _pad_lanes(b_forget[l])
        act, saved = _layer_fwd(act, p[l, 0], w, small, f"l{l}")
        layers.append((w, small, saved))
    dact, g_final, loss_part = _loss_head(act, final_g[None, :], loss_target[0], "loss_head")
    loss = lax.psum(loss_part[0, 0], ("x", "y", "c"))

    big_grads = {k: [None] * depth for k in _BIG}
    small_grads = {k: [None] * depth for k in _SMALL + ("conv_w",)}
    for l in reversed(range(depth)):
        w, small, saved = layers[l]
        dact, gw, gs = _layer_bwd(dact, saved, w, small, f"l{l}")
        for k in _BIG:
            big_grads[k][l] = gw[k]
        for k in small_grads:
            small_grads[k][l] = gs[k]
    grad_x = dact[None]

    def to_blocks(k):
        per_layer = [_shard_cols(g) if k in _COL_SHARDED else g.reshape((N_DEV, -1) + g.shape[1:])
                     for g in big_grads[k]]
        return jnp.stack(per_layer, axis=1).astype(BF16)

    blocks = [to_blocks(k) for k in _BIG]
    mine, theirs = _pair_exchange(blocks, "grads_pair_exchange")
    chip_parts = [_pair_sum(a, b, f"grads_pair_sum_{k}") for k, a, b in zip(_BIG, mine, theirs)]
    arrived = _chip_exchange(chip_parts, "grads_chip_exchange")
    out = {}
    for k, parts in zip(_BIG, arrived):
        out[k] = _sum_adamw(parts, weights[k], mom1[k], mom2[k], f"adamw_{k}")

    small_names = list(_SMALL) + ["conv_w", "final_g"]
    small_list = [jnp.stack(small_grads[k]) for k in _SMALL + ("conv_w",)] + [g_final[0]]
    n_small = sum(v.size for v in small_list)
    rows = -(-n_small // (8 * LANES)) * 8
    all_parts = _all_gather([_pack(small_list, rows)], "gather_small_grads")[0]
    full_conv_w = jnp.transpose(gathered["conv_w"], (1, 2, 0, 3)).reshape(depth, CONV_K, -1)
    full_conv_m = jnp.zeros_like(full_conv_w)
    ch_shard = conv_w.shape[-1]
    at = (0, 0, me * ch_shard)
    small_w = [weights[k] for k in _SMALL] + [full_conv_w, final_g]
    small_m = [mom1[k] for k in _SMALL] + [lax.dynamic_update_slice(full_conv_m, mom1["conv_w"], at), mom1["final_g"]]
    small_v = [mom2[k] for k in _SMALL] + [lax.dynamic_update_slice(full_conv_m + 1.0, mom2["conv_w"], at), mom2["final_g"]]
    res = _sum_adamw(all_parts, _pack(small_w, rows), _pack(small_m, rows), _pack(small_v, rows), "adamw_small")
    unpacked = [_unpack(r, small_list) for r in res]
    for idx, k in enumerate(small_names):
        vals = [u[idx] for u in unpacked]
        if k == "conv_w":
            vals = [lax.dynamic_slice(v, at, conv_w.shape) for v in vals]
        out[k] = vals

    return (loss, grad_x, *[out[k][0] for k in _ORDER], *[out[k][1] for k in _ORDER],
            *[out[k][2] for k in _ORDER], *[out[k][3] for k in _ORDER])
```

```python
import functools
import math

import jax
import jax.numpy as jnp
from jax import lax
from jax.experimental import pallas as pl
from jax.experimental.pallas import tpu as pltpu

F32 = jnp.float32
BF16 = jnp.bfloat16
MESH = pl.DeviceIdType.MESH
ANY = pl.BlockSpec(memory_space=pl.ANY)

N_DEV = 8
N_HEADS = 8
HEAD_DIM = 128
CONV_K = 31
HALO = 32
EPS = 1e-6
NEG_INF = -1e30
ADAM_LR = 0.001
ADAM_B1 = 0.9
ADAM_B2 = 0.999
ADAM_EPS = 1e-08
ADAM_WD = 0.01
ADAM_STEP = 10

V7X_VMEM_BYTES = 64 * 1024 * 1024
VMEM_LIMIT = V7X_VMEM_BYTES * 3 // 4
LANES = 128
ROW_TILE = 256
ATTN_TILE = 512
CUM_TILE = 256


def _params(*sem):
    return pltpu.CompilerParams(dimension_semantics=sem, vmem_limit_bytes=VMEM_LIMIT)


def _tile(dim, pref, mult=LANES):
    t = min(pref, dim)
    t -= t % mult
    while t >= mult:
        if dim % t == 0:
            return t
        t -= mult
    return dim


def _sigmoid(x):
    return 1.0 / (1.0 + jnp.exp(-x))


def _place():
    x, y, c = lax.axis_index("x"), lax.axis_index("y"), lax.axis_index("c")
    chips = [(1 - x, y), (x, 1 - y), (1 - x, 1 - y)]
    return x, y, c, chips


def _all_gather(shards, name):
    n = len(shards)

    def body(*refs):
        ins, outs = refs[:n], refs[n:2 * n]
        send_sems, recv_sems, local_sems = refs[2 * n:]
        x, y, c, chips = _place()
        me, sibling = (x, y, c), (x, y, 1 - c)

        def idx(px, py, pc):
            return 4 * px + 2 * py + pc

        def copy(t, k, block, to, src=None):
            rows = outs[t].at[idx(*block)]
            return pltpu.make_async_remote_copy(
                src_ref=rows if src is None else src, dst_ref=rows,
                send_sem=send_sems.at[t, k], recv_sem=recv_sems.at[t, k],
                device_id=to, device_id_type=MESH)

        local = [pltpu.make_async_copy(ins[t], outs[t].at[idx(*me)], local_sems.at[t]) for t in range(n)]
        for cp in local:
            cp.start()
        first = []
        for t in range(n):
            first.append(copy(t, 0, me, sibling, src=ins[t]))
            first += [copy(t, 1 + j, me, (*chip, c), src=ins[t]) for j, chip in enumerate(chips)]
        for cp in first:
            cp.start()
        passed = []
        for j, chip in enumerate(chips):
            for t in range(n):
                copy(t, 1 + j, (*chip, c), me).wait_recv()
                cp = copy(t, 4 + j, (*chip, c), sibling)
                cp.start()
                passed.append(cp)
        for t in range(n):
            copy(t, 0, sibling, me).wait_recv()
        for j, chip in enumerate(chips):
            for t in range(n):
                copy(t, 4 + j, (*chip, 1 - c), me).wait_recv()
        for cp in first + passed:
            cp.wait_send()
        for cp in local:
            cp.wait()

    return pl.pallas_call(
        body, name=name,
        out_shape=[jax.ShapeDtypeStruct((N_DEV,) + s.shape, s.dtype) for s in shards],
        in_specs=[ANY] * n, out_specs=[ANY] * n,
        scratch_shapes=[pltpu.SemaphoreType.DMA((n, 7)), pltpu.SemaphoreType.DMA((n, 7)),
                        pltpu.SemaphoreType.DMA((n,))],
    )(*shards)


def _pair_exchange(grads, name):
    n = len(grads)

    def body(*refs):
        ins, theirs = refs[:n], refs[n:2 * n]
        send_sems, recv_sems = refs[2 * n:]
        x, y, c, _ = _place()
        copies = []
        for t in range(n):
            for q in range(4):
                give = pltpu.make_async_remote_copy(
                    src_ref=ins[t].at[2 * q + 1 - c], dst_ref=theirs[t].at[q],
                    send_sem=send_sems.at[t, q], recv_sem=recv_sems.at[t, q],
                    device_id=(x, y, 1 - c), device_id_type=MESH)
                give.start()
                copies.append(give)
        for cp in copies:
            cp.wait()

    return pl.pallas_call(
        body, name=name, out_shape=[jax.ShapeDtypeStruct((4,) + g.shape[1:], g.dtype) for g in grads],
        in_specs=[ANY] * n, out_specs=[ANY] * n,
        scratch_shapes=[pltpu.SemaphoreType.DMA((n, 4)), pltpu.SemaphoreType.DMA((n, 4))],
    )(*grads)


def _chip_exchange(parts, name):
    n = len(parts)

    def body(*refs):
        ins, outs = refs[:n], refs[n:2 * n]
        send_sems, recv_sems, local_sems = refs[2 * n:]
        x, y, c, chips = _place()
        copies = []
        for t in range(n):
            own = pltpu.make_async_copy(ins[t].at[2 * x + y], outs[t].at[0], local_sems.at[t])
            own.start()
            copies.append(own)
            for j, (px, py) in enumerate(chips):
                cp = pltpu.make_async_remote_copy(
                    src_ref=ins[t].at[2 * px + py], dst_ref=outs[t].at[1 + j],
                    send_sem=send_sems.at[t, j], recv_sem=recv_sems.at[t, j],
                    device_id=(px, py, c), device_id_type=MESH)
                cp.start()
                copies.append(cp)
        for cp in copies:
            cp.wait()

    return pl.pallas_call(
        body, name=name, out_shape=[jax.ShapeDtypeStruct(p.shape, p.dtype) for p in parts],
        in_specs=[ANY] * n, out_specs=[ANY] * n,
        scratch_shapes=[pltpu.SemaphoreType.DMA((n, 3)), pltpu.SemaphoreType.DMA((n, 3)),
                        pltpu.SemaphoreType.DMA((n,))],
    )(*parts)


_DOT_DIMS = {"nn": (((1,), (0,)), ((), ())), "nt": (((1,), (1,)), ((), ())), "tn": (((0,), (0,)), ((), ()))}


def _mm(a, b, mode, out_dtype, name, residual=None, tm=1024, tn=1024, tk=2048):
    if mode == "tn":
        (k_dim, m_dim), n_dim = a.shape, b.shape[1]
    else:
        (m_dim, k_dim), n_dim = a.shape, (b.shape[1] if mode == "nn" else b.shape[0])
    tm, tn, tk = _tile(m_dim, tm), _tile(n_dim, tn), _tile(k_dim, tk)
    nk = k_dim // tk

    def body(*refs):
        a_ref, b_ref = refs[:2]
        r_ref = refs[2] if residual is not None else None
        o_ref = refs[2 if residual is None else 3]

        def finish(r):
            if r_ref is not None:
                r = r + r_ref[...].astype(F32)
            o_ref[...] = r.astype(o_ref.dtype)

        part = lax.dot_general(a_ref[...].astype(BF16), b_ref[...].astype(BF16), _DOT_DIMS[mode],
                               preferred_element_type=F32)
        if nk == 1:
            finish(part)
            return
        acc, k = refs[-1], pl.program_id(2)

        @pl.when(k == 0)
        def _():
            acc[...] = part

        @pl.when((k > 0) & (k < nk - 1))
        def _():
            acc[...] += part

        @pl.when(k == nk - 1)
        def _():
            finish(acc[...] + part)

    a_spec = pl.BlockSpec((tk, tm), lambda i, j, k: (k, i)) if mode == "tn" else pl.BlockSpec((tm, tk), lambda i, j, k: (i, k))
    b_spec = pl.BlockSpec((tn, tk), lambda i, j, k: (j, k)) if mode == "nt" else pl.BlockSpec((tk, tn), lambda i, j, k: (k, j))
    o_spec = pl.BlockSpec((tm, tn), lambda i, j, k: (i, j))
    ins, specs = [a, b], [a_spec, b_spec]
    if residual is not None:
        ins.append(residual)
        specs.append(o_spec)
    return pl.pallas_call(
        body, name=name, grid=(m_dim // tm, n_dim // tn, nk),
        out_shape=jax.ShapeDtypeStruct((m_dim, n_dim), out_dtype),
        in_specs=specs, out_specs=o_spec,
        scratch_shapes=[pltpu.VMEM((tm, tn), F32)] if nk > 1 else [],
        compiler_params=_params("parallel", "parallel", "arbitrary"),
    )(*ins)


def _rows(body, name, row_ins, full_ins, row_outs, acc_outs, ts=ROW_TILE):
    s = row_ins[0].shape[0]
    ts = _tile(s, ts, 16)
    n_ri, n_fi, n_ro = len(row_ins), len(full_ins), len(row_outs)

    def kernel(*refs):
        i = pl.program_id(0)
        ri, fi = refs[:n_ri], refs[n_ri:n_ri + n_fi]
        ro, ao = refs[n_ri + n_fi:n_ri + n_fi + n_ro], refs[n_ri + n_fi + n_ro:]

        if ao:
            @pl.when(i == 0)
            def _():
                for r in ao:
                    r[...] = jnp.zeros_like(r)

        body(i, ri, fi, ro, ao)

    def row_spec(cols):
        return pl.BlockSpec((ts, cols), lambda i: (i, 0))

    def full_spec(shape):
        return pl.BlockSpec(shape, lambda i: (0,) * len(shape))

    return pl.pallas_call(
        kernel, name=name, grid=(s // ts,),
        out_shape=[jax.ShapeDtypeStruct((s, c), d) for c, d in row_outs]
        + [jax.ShapeDtypeStruct(sh, d) for sh, d in acc_outs],
        in_specs=[row_spec(a.shape[1]) for a in row_ins] + [full_spec(a.shape) for a in full_ins],
        out_specs=[row_spec(c) for c, _ in row_outs] + [full_spec(sh) for sh, _ in acc_outs],
        compiler_params=_params("arbitrary"),
    )(*row_ins, *full_ins)


def _rms_stats(x):
    return lax.rsqrt(jnp.mean(x * x, axis=-1, keepdims=True) + EPS)


def _rms_fwd(x, g, name):
    def body(i, ri, fi, ro, ao):
        xv = ri[0][...]
        ro[0][...] = (xv * _rms_stats(xv) * fi[0][...]).astype(BF16)

    return _rows(body, name, [x], [g], [(x.shape[1], BF16)], [])[0]


def _rms_bwd_rows(xv, gv, dh):
    r = _rms_stats(xv)
    xhat = xv * r
    dxhat = dh * gv
    dx = r * (dxhat - xhat * jnp.mean(dxhat * xhat, axis=-1, keepdims=True))
    return dx, jnp.sum(dh * xhat, axis=0, keepdims=True)


def _rms_bwd(x, g, dh, dres, name):
    d = x.shape[1]

    def body(i, ri, fi, ro, ao):
        dx, dg = _rms_bwd_rows(ri[0][...], fi[0][...], ri[1][...])
        dx = dx + ri[2][...]
        ro[0][...] = dx
        ro[1][...] = dx.astype(BF16)
        ao[0][...] += dg

    return _rows(body, name, [x, dh, dres], [g], [(d, F32), (d, BF16)], [((1, d), F32)])


def _loss_head(x, g, target, name):
    d = x.shape[1]

    def body(i, ri, fi, ro, ao):
        xv, gv = ri[0][...], fi[0][...]
        y = xv * _rms_stats(xv) * gv
        err = y - ri[1][...]
        ao[1][...] += 0.5 * jnp.sum(jnp.mean(err * err, axis=-1, keepdims=True))
        dx, dg = _rms_bwd_rows(xv, gv, err * (1.0 / d))
        ro[0][...] = dx
        ao[0][...] += dg

    return _rows(body, name, [x, target], [g], [(d, F32)], [((1, d), F32), ((8, LANES), F32)])


def _merge_fwd(gg, y_conv, y_attn, name):
    d = y_conv.shape[1]

    def body(i, ri, fi, ro, ao):
        gv = ri[0][...]
        m = _sigmoid(gv[:, :d]) * ri[1][...] + _sigmoid(gv[:, d:]) * ri[2][...]
        ro[0][...] = m.astype(BF16)

    return _rows(body, name, [gg, y_conv, y_attn], [], [(d, BF16)], [])[0]


def _merge_bwd(gg, y_conv, y_attn, dm, name):
    d = y_conv.shape[1]

    def body(i, ri, fi, ro, ao):
        gv, yc, ya, dmv = ri[0][...], ri[1][...], ri[2][...], ri[3][...]
        sc, sa = _sigmoid(gv[:, :d]), _sigmoid(gv[:, d:])
        ro[0][...] = (dmv * sc).astype(BF16)
        ro[1][...] = (dmv * sa).astype(BF16)
        ro[2][:, :d] = (dmv * yc * sc * (1.0 - sc)).astype(BF16)
        ro[2][:, d:] = (dmv * ya * sa * (1.0 - sa)).astype(BF16)

    return _rows(body, name, [gg, y_conv, y_attn, dm], [], [(d, BF16), (d, BF16), (2 * d, BF16)], [])


def _swiglu_fwd(gu, name):
    f = gu.shape[1] // 2

    def body(i, ri, fi, ro, ao):
        v = ri[0][...]
        gate, up = v[:, :f], v[:, f:]
        ro[0][...] = (gate * _sigmoid(gate) * up).astype(BF16)

    return _rows(body, name, [gu], [], [(f, BF16)], [], ts=128)[0]


def _swiglu_bwd(gu, dact, name):
    f = gu.shape[1] // 2

    def body(i, ri, fi, ro, ao):
        v, dv = ri[0][...], ri[1][...]
        gate, up = v[:, :f], v[:, f:]
        sg = _sigmoid(gate)
        ro[0][:, :f] = (dv * up * sg * (1.0 + gate * (1.0 - sg))).astype(BF16)
        ro[0][:, f:] = (dv * gate * sg).astype(BF16)

    return _rows(body, name, [gu, dact], [], [(2 * f, BF16)], [], ts=128)[0]


def _ple_fwd(x, z, pp, name):
    d = x.shape[1]

    def body(i, ri, fi, ro, ao):
        ro[0][...] = ri[0][...] + _sigmoid(ri[1][...]) * ri[2][...]

    return _rows(body, name, [x, z, pp], [], [(d, F32)], [])[0]


def _ple_bwd(dx, z, pp, name):
    d = dx.shape[1]

    def body(i, ri, fi, ro, ao):
        dv, sz = ri[0][...], _sigmoid(ri[1][...])
        ro[0][...] = (dv * ri[2][...] * sz * (1.0 - sz)).astype(BF16)
        ro[1][...] = (dv * sz).astype(BF16)

    return _rows(body, name, [dx, z, pp], [], [(d, BF16), (d, BF16)], [])


def _triangle(n, lower):
    r = lax.broadcasted_iota(jnp.int32, (n, n), 0)
    c = lax.broadcasted_iota(jnp.int32, (n, n), 1)
    return jnp.where((c <= r) if lower else (c >= r), 1.0, 0.0).astype(F32)


def _forget_fwd(f, bias, name):
    s = f.shape[0]
    tb = _tile(s, CUM_TILE, 8)

    def body(f_ref, b_ref, c_ref, carry):
        @pl.when(pl.program_id(0) == 0)
        def _():
            carry[...] = jnp.zeros_like(carry)

        xv = f_ref[...] + b_ref[...]
        logf = jnp.minimum(xv, 0.0) - jnp.log(1.0 + jnp.exp(-jnp.abs(xv)))
        cv = jnp.dot(_triangle(tb, True), logf, precision=lax.Precision.HIGHEST, preferred_element_type=F32) + carry[...]
        c_ref[...] = cv
        carry[...] = cv[tb - 1:tb, :]

    return pl.pallas_call(
        body, name=name, grid=(s // tb,), out_shape=jax.ShapeDtypeStruct((s, LANES), F32),
        in_specs=[pl.BlockSpec((tb, LANES), lambda i: (i, 0)), pl.BlockSpec((1, LANES), lambda i: (0, 0))],
        out_specs=pl.BlockSpec((tb, LANES), lambda i: (i, 0)),
        scratch_shapes=[pltpu.VMEM((1, LANES), F32)],
        compiler_params=_params("arbitrary"),
    )(f, bias)


def _forget_bwd(f, bias, dc, name):
    s = f.shape[0]
    tb = _tile(s, CUM_TILE, 16)
    nb = s // tb

    def body(f_ref, b_ref, dc_ref, df_ref, db_ref, carry):
        @pl.when(pl.program_id(0) == 0)
        def _():
            carry[...] = jnp.zeros_like(carry)
            db_ref[...] = jnp.zeros_like(db_ref)

        dlog = jnp.dot(_triangle(tb, False), dc_ref[...], precision=lax.Precision.HIGHEST,
                       preferred_element_type=F32) + carry[...]
        carry[...] = dlog[0:1, :]
        dlogit = dlog * (1.0 - _sigmoid(f_ref[...] + b_ref[...]))
        df_ref[...] = dlogit.astype(BF16)
        db_ref[...] += jnp.sum(dlogit, axis=0, keepdims=True)

    back = lambda i: (nb - 1 - i, 0)
    return pl.pallas_call(
        body, name=name, grid=(nb,),
        out_shape=[jax.ShapeDtypeStruct((s, LANES), BF16), jax.ShapeDtypeStruct((1, LANES), F32)],
        in_specs=[pl.BlockSpec((tb, LANES), back), pl.BlockSpec((1, LANES), lambda i: (0, 0)),
                  pl.BlockSpec((tb, LANES), back)],
        out_specs=[pl.BlockSpec((tb, LANES), back), pl.BlockSpec((1, LANES), lambda i: (0, 0))],
        scratch_shapes=[pltpu.VMEM((1, LANES), F32)],
        compiler_params=_params("arbitrary"),
    )(f, bias, dc)


def _glu(block, ch):
    return block[:, :ch] * _sigmoid(block[:, ch:])


def _layernorm_rows(u1, g, b):
    mu = jnp.mean(u1, axis=-1, keepdims=True)
    cen = u1 - mu
    rstd = lax.rsqrt(jnp.mean(cen * cen, axis=-1, keepdims=True) + EPS)
    xhat = cen * rstd
    return xhat, rstd, xhat * g + b


def _conv_fwd(ag, conv_w, conv_b, ln_g, ln_b, name):
    s, ch = ag.shape[0], ag.shape[1] // 2
    ts = _tile(s, ROW_TILE, HALO)
    per = ts // HALO

    def body(cur_ref, halo_ref, w_ref, cb_ref, g_ref, b_ref, u1_ref, u3_ref, cat):
        i = pl.program_id(0)
        cat[HALO:, :] = _glu(cur_ref[...], ch)
        cat[:HALO, :] = jnp.where(i == 0, 0.0, _glu(halo_ref[...], ch))
        full = cat[...]
        acc = jnp.zeros((ts, ch), F32) + cb_ref[...]
        for j in range(CONV_K):
            shifted = full if j == CONV_K - 1 else pltpu.roll(full, CONV_K - 1 - j, 0)
            acc = acc + w_ref[j:j + 1, :] * shifted[HALO:, :]
        u1_ref[...] = acc
        _, _, u2 = _layernorm_rows(acc, g_ref[...], b_ref[...])
        u3_ref[...] = (u2 * _sigmoid(u2)).astype(BF16)

    whole = lambda shape: pl.BlockSpec(shape, lambda i: (0, 0))
    return pl.pallas_call(
        body, name=name, grid=(s // ts,),
        out_shape=[jax.ShapeDtypeStruct((s, ch), F32), jax.ShapeDtypeStruct((s, ch), BF16)],
        in_specs=[pl.BlockSpec((ts, 2 * ch), lambda i: (i, 0)),
                  pl.BlockSpec((HALO, 2 * ch), lambda i: (jnp.maximum(i * per - 1, 0), 0)),
                  whole(conv_w.shape), whole((1, ch)), whole((1, ch)), whole((1, ch))],
        out_specs=[pl.BlockSpec((ts, ch), lambda i: (i, 0)), pl.BlockSpec((ts, ch), lambda i: (i, 0))],
        scratch_shapes=[pltpu.VMEM((HALO + ts, ch), F32)],
        compiler_params=_params("arbitrary"),
    )(ag, ag, conv_w, conv_b, ln_g, ln_b)


def _conv_norm_bwd(u1, du3, ln_g, ln_b, name):
    ch = u1.shape[1]

    def body(i, ri, fi, ro, ao):
        g = fi[0][...]
        xhat, rstd, u2 = _layernorm_rows(ri[0][...], g, fi[1][...])
        sg = _sigmoid(u2)
        du2 = ri[1][...] * sg * (1.0 + u2 * (1.0 - sg))
        dxhat = du2 * g
        du1 = rstd * (dxhat - jnp.mean(dxhat, axis=-1, keepdims=True)
                      - xhat * jnp.mean(dxhat * xhat, axis=-1, keepdims=True))
        ro[0][...] = du1
        ao[0][...] += jnp.sum(du2 * xhat, axis=0, keepdims=True)
        ao[1][...] += jnp.sum(du2, axis=0, keepdims=True)
        ao[2][...] += jnp.sum(du1, axis=0, keepdims=True)

    return _rows(body, name, [u1, du3], [ln_g, ln_b], [(ch, F32)], [((1, ch), F32)] * 3)


def _conv_bwd(ag, du1, conv_w, name):
    s, ch = du1.shape
    ts = _tile(s, ROW_TILE, HALO)
    per = ts // HALO
    last = s // HALO - 1
    nb = s // ts

    def body(cur_ref, halo_ref, d_ref, dnext_ref, w_ref, dag_ref, dw_ref, cat, dcat):
        i = pl.program_id(0)

        @pl.when(i == 0)
        def _():
            dw_ref[...] = jnp.zeros_like(dw_ref)

        cur = cur_ref[...]
        cat[HALO:, :] = _glu(cur, ch)
        cat[:HALO, :] = jnp.where(i == 0, 0.0, _glu(halo_ref[...], ch))
        dcur = d_ref[...]
        dcat[:ts, :] = dcur
        dcat[ts:, :] = jnp.where(i == nb - 1, 0.0, dnext_ref[...])
        full, dfull = cat[...], dcat[...]
        du0 = jnp.zeros((ts, ch), F32)
        for j in range(CONV_K):
            back = CONV_K - 1 - j
            dshift = dfull if back == 0 else pltpu.roll(dfull, ts + HALO - back, 0)
            du0 = du0 + w_ref[j:j + 1, :] * dshift[:ts, :]
            ushift = full if back == 0 else pltpu.roll(full, back, 0)
            dw_ref[j:j + 1, :] += jnp.sum(dcur * ushift[HALO:, :], axis=0, keepdims=True)
        a, sg = cur[:, :ch], _sigmoid(cur[:, ch:])
        dag_ref[:, :ch] = (du0 * sg).astype(BF16)
        dag_ref[:, ch:] = (du0 * a * sg * (1.0 - sg)).astype(BF16)

    return pl.pallas_call(
        body, name=name, grid=(nb,),
        out_shape=[jax.ShapeDtypeStruct((s, 2 * ch), BF16), jax.ShapeDtypeStruct(conv_w.shape, F32)],
        in_specs=[pl.BlockSpec((ts, 2 * ch), lambda i: (i, 0)),
                  pl.BlockSpec((HALO, 2 * ch), lambda i: (jnp.maximum(i * per - 1, 0), 0)),
                  pl.BlockSpec((ts, ch), lambda i: (i, 0)),
                  pl.BlockSpec((HALO, ch), lambda i: (jnp.minimum((i + 1) * per, last), 0)),
                  pl.BlockSpec(conv_w.shape, lambda i: (0, 0))],
        out_specs=[pl.BlockSpec((ts, 2 * ch), lambda i: (i, 0)), pl.BlockSpec(conv_w.shape, lambda i: (0, 0))],
        scratch_shapes=[pltpu.VMEM((HALO + ts, ch), F32), pltpu.VMEM((ts + HALO, ch), F32)],
        compiler_params=_params("arbitrary"),
    )(ag, ag, du1, du1, conv_w)


_SCALE = 1.0 / math.sqrt(HEAD_DIM)


STAT_LANES = 8


def _causal(rows, cols, row0):
    r = row0 + lax.broadcasted_iota(jnp.int32, (rows, cols), 0)
    return lax.broadcasted_iota(jnp.int32, (rows, cols), 1) <= r


def _attn_fwd(qkv, c_col, c_row, name):
    s = qkv.shape[0]
    t = _tile(s, ATTN_TILE)
    nt, half = s // t, t // 2
    h_, d = N_HEADS, HEAD_DIM

    def body(q_ref, k_ref, v_ref, cc_ref, cr_ref, o_ref, lse_ref, m_sc, l_sc, acc):
        i = pl.program_id(1)
        m_sc[...] = jnp.full_like(m_sc, NEG_INF)
        l_sc[...] = jnp.zeros_like(l_sc)
        acc[...] = jnp.zeros_like(acc)

        def block(j, diagonal):
            at = pl.multiple_of(j * t, t)
            k, v = k_ref[pl.ds(at, t), :], v_ref[pl.ds(at, t), :]
            cr = cr_ref[0, pl.ds(j, 1), :]
            for part in range(2):
                rows = pl.ds(part * half, half)
                sc = lax.dot_general(q_ref[rows, :], k, _DOT_DIMS["nt"], preferred_element_type=F32) * _SCALE
                sc = sc + (cc_ref[0, rows, :] - cr)
                if diagonal:
                    sc = jnp.where(_causal(half, t, part * half), sc, NEG_INF)
                m_old = m_sc[rows, :]
                m_new = jnp.maximum(m_old, jnp.max(sc, axis=-1, keepdims=True))
                alpha = jnp.exp(m_old - m_new)
                p = jnp.exp(sc - m_new)
                l_sc[rows, :] = alpha * l_sc[rows, :] + jnp.sum(p, axis=-1, keepdims=True)
                acc[rows, :] = alpha * acc[rows, :] + jnp.dot(p.astype(BF16), v, preferred_element_type=F32)
                m_sc[rows, :] = m_new

        def below(j, carry):
            block(j, False)
            return carry

        lax.fori_loop(0, i, below, 0)
        block(i, True)
        o_ref[...] = (acc[...] / l_sc[...]).astype(BF16)
        lse_ref[0] = m_sc[...] + jnp.log(l_sc[...])

    return pl.pallas_call(
        body, name=name, grid=(h_, nt),
        out_shape=[jax.ShapeDtypeStruct((s, h_ * d), BF16), jax.ShapeDtypeStruct((h_, s, 1), F32)],
        in_specs=[pl.BlockSpec((t, d), lambda h, i: (i, h)),
                  pl.BlockSpec((s, d), lambda h, i: (0, h_ + h)),
                  pl.BlockSpec((s, d), lambda h, i: (0, 2 * h_ + h)),
                  pl.BlockSpec((1, t, 1), lambda h, i: (h, i, 0)),
                  pl.BlockSpec((1, nt, t), lambda h, i: (h, 0, 0))],
        out_specs=[pl.BlockSpec((t, d), lambda h, i: (i, h)), pl.BlockSpec((1, t, 1), lambda h, i: (h, i, 0))],
        scratch_shapes=[pltpu.VMEM((t, 1), F32), pltpu.VMEM((t, 1), F32), pltpu.VMEM((t, d), F32)],
        compiler_params=_params("parallel", "arbitrary"),
    )(qkv, qkv, qkv, c_col, c_row.reshape(h_, nt, t))


def _attn_stats(o, do, c_col, lse, name):
    s = o.shape[0]
    t = _tile(s, ATTN_TILE)
    h_, d = N_HEADS, HEAD_DIM

    def body(o_ref, do_ref, cc_ref, lse_ref, st_ref, dob_ref):
        dov = do_ref[...]
        st_ref[...] = jnp.zeros_like(st_ref)
        st_ref[0, :, 0:1] = cc_ref[0] - lse_ref[0]
        st_ref[0, :, 1:2] = jnp.sum(o_ref[...].astype(F32) * dov, axis=-1, keepdims=True)
        dob_ref[...] = dov.astype(BF16)

    col = pl.BlockSpec((1, t, 1), lambda h, i: (h, i, 0))
    blk = pl.BlockSpec((t, d), lambda h, i: (i, h))
    return pl.pallas_call(
        body, name=name, grid=(h_, s // t),
        out_shape=[jax.ShapeDtypeStruct((h_, s, STAT_LANES), F32), jax.ShapeDtypeStruct(o.shape, BF16)],
        in_specs=[blk, blk, col, col],
        out_specs=[pl.BlockSpec((1, t, STAT_LANES), lambda h, i: (h, i, 0)), blk],
        compiler_params=_params("parallel", "parallel"),
    )(o, do, c_col, lse)


def _attn_bwd(qkv, do, stats, c_row, name):
    s = qkv.shape[0]
    t = _tile(s, ATTN_TILE)
    nt = s // t
    h_, d = N_HEADS, HEAD_DIM

    def body(q_ref, do_ref, k_ref, v_ref, st_ref, cr_ref, dq_ref, dk_ref, dv_ref, dcr_ref, dcc_ref,
             dq_acc, dk_acc, dv_acc, dcr_acc):
        j = pl.program_id(1)

        @pl.when(j == 0)
        def _():
            dq_acc[...] = jnp.zeros_like(dq_acc)
            dcc_ref[...] = jnp.zeros_like(dcc_ref)

        dk_acc[...] = jnp.zeros_like(dk_acc)
        dv_acc[...] = jnp.zeros_like(dv_acc)
        dcr_acc[...] = jnp.zeros_like(dcr_acc)
        k, v, cr = k_ref[...], v_ref[...], cr_ref[0]

        def block(i, diagonal):
            rows = pl.ds(pl.multiple_of(i * t, t), t)
            q, dov, st = q_ref[rows, :], do_ref[rows, :], st_ref[0, rows, :]
            sc = lax.dot_general(q, k, _DOT_DIMS["nt"], preferred_element_type=F32) * _SCALE + (st[:, 0:1] - cr)
            p = jnp.exp(sc)
            if diagonal:
                p = jnp.where(_causal(t, t, 0), p, 0.0)
            dp = lax.dot_general(dov, v, _DOT_DIMS["nt"], preferred_element_type=F32)
            ds = p * (dp - st[:, 1:2])
            ds_b = ds.astype(BF16)
            dv_acc[...] += lax.dot_general(p.astype(BF16), dov, _DOT_DIMS["tn"], preferred_element_type=F32)
            dk_acc[...] += lax.dot_general(ds_b, q, _DOT_DIMS["tn"], preferred_element_type=F32)
            dq_acc[rows, :] += jnp.dot(ds_b, k, preferred_element_type=F32)
            dcr_acc[...] -= jnp.sum(ds, axis=0, keepdims=True)
            dcc_ref[0, rows, :] += jnp.sum(ds, axis=-1, keepdims=True)

        def above(i, carry):
            block(i, False)
            return carry

        block(j, True)
        lax.fori_loop(j + 1, nt, above, 0)
        dk_ref[...] = (dk_acc[...] * _SCALE).astype(BF16)
        dv_ref[...] = dv_acc[...].astype(BF16)
        dcr_ref[0] = dcr_acc[...]

        @pl.when(j == nt - 1)
        def _():
            dq_ref[...] = (dq_acc[...] * _SCALE).astype(BF16)

    head = pl.BlockSpec((s, d), lambda h, j: (0, h))
    return pl.pallas_call(
        body, name=name, grid=(h_, nt),
        out_shape=[jax.ShapeDtypeStruct((s, h_ * d), BF16)] * 3
        + [jax.ShapeDtypeStruct((h_, 1, s), F32), jax.ShapeDtypeStruct((h_, s, 1), F32)],
        in_specs=[head, head,
                  pl.BlockSpec((t, d), lambda h, j: (j, h_ + h)),
                  pl.BlockSpec((t, d), lambda h, j: (j, 2 * h_ + h)),
                  pl.BlockSpec((1, s, STAT_LANES), lambda h, j: (h, 0, 0)),
                  pl.BlockSpec((1, 1, t), lambda h, j: (h, 0, j))],
        out_specs=[head, pl.BlockSpec((t, d), lambda h, j: (j, h)), pl.BlockSpec((t, d), lambda h, j: (j, h)),
                   pl.BlockSpec((1, 1, t), lambda h, j: (h, 0, j)), pl.BlockSpec((1, s, 1), lambda h, j: (h, 0, 0))],
        scratch_shapes=[pltpu.VMEM((s, d), F32), pltpu.VMEM((t, d), F32), pltpu.VMEM((t, d), F32),
                        pltpu.VMEM((1, t), F32)],
        compiler_params=_params("parallel", "arbitrary"),
    )(qkv, do, qkv, qkv, stats, c_row)


def _pair_sum(grads, theirs, core, name):
    _, l, r, c = theirs.shape
    tr = _tile(r, 256, 16)

    def body(core_ref, a_ref, b_ref, o_ref):
        o_ref[...] = (a_ref[...].astype(F32) + b_ref[...].astype(F32)).astype(BF16)

    spec = pl.BlockSpec((1, 1, tr, c), lambda q, k, i, core_ref: (q, k, i, 0))
    mine = pl.BlockSpec((1, 1, tr, c), lambda q, k, i, core_ref: (2 * q + core_ref[0], k, i, 0))
    return pl.pallas_call(
        body, name=name, out_shape=jax.ShapeDtypeStruct(theirs.shape, BF16),
        grid_spec=pltpu.PrefetchScalarGridSpec(num_scalar_prefetch=1, grid=(4, l, r // tr),
                                               in_specs=[mine, spec], out_specs=spec),
        compiler_params=_params("parallel", "parallel", "parallel"),
    )(core, grads, theirs)


def _adamw(g, w, m, v):
    m = ADAM_B1 * m + (1.0 - ADAM_B1) * g
    v = ADAM_B2 * v + (1.0 - ADAM_B2) * (g * g)
    m_hat = m / (1.0 - ADAM_B1 ** ADAM_STEP)
    v_hat = v / (1.0 - ADAM_B2 ** ADAM_STEP)
    return -ADAM_LR * (m_hat / (jnp.sqrt(v_hat) + ADAM_EPS) + ADAM_WD * w), m, v


def _sum_adamw(parts, w, m, v, name):
    n_parts, l, r, c = parts.shape
    tr = _tile(r, 128, 16)

    def body(p_ref, w_ref, m_ref, v_ref, g_ref, d_ref, nm_ref, nv_ref):
        g = p_ref[0].astype(F32)
        for k in range(1, n_parts):
            g = g + p_ref[k].astype(F32)
        g_ref[...] = g
        d_ref[...], nm_ref[...], nv_ref[...] = _adamw(g, w_ref[...], m_ref[...], v_ref[...])

    spec = pl.BlockSpec((1, tr, c), lambda k, i: (k, i, 0))
    return pl.pallas_call(
        body, name=name, grid=(l, r // tr), out_shape=[jax.ShapeDtypeStruct(w.shape, F32)] * 4,
        in_specs=[pl.BlockSpec((n_parts, 1, tr, c), lambda k, i: (0, k, i, 0)), spec, spec, spec],
        out_specs=[spec] * 4, compiler_params=_params("parallel", "parallel"),
    )(parts, w, m, v)


def _unshard_cols(gathered):
    return jnp.transpose(gathered, (1, 0, 2)).reshape(gathered.shape[1], -1)


def _shard_cols(full):
    k = full.shape[0]
    return jnp.transpose(full.reshape(k, N_DEV, -1), (1, 0, 2))


def _layer_weights(gathered, layer, d_model):
    w = {name: arr[:, layer] for name, arr in gathered.items()}
    hd = N_HEADS * HEAD_DIM
    w_in = _unshard_cols(w["w_in"])
    f0 = d_model + 3 * hd
    out = {
        "w_ag": w_in[:, :d_model], "w_qkv": w_in[:, d_model:f0],
        "w_f": jnp.pad(w_in[:, f0:f0 + N_HEADS], ((0, 0), (0, LANES - N_HEADS))),
        "w_gg": w_in[:, f0 + N_HEADS:],
        "w_conv_out": _unshard_cols(w["w_conv_out"]), "w_attn_out": _unshard_cols(w["w_attn_out"]),
        "w_out": w["w_out"].reshape(-1, d_model), "w_gate_up": _unshard_cols(w["w_gate_up"]),
        "w_down": w["w_down"].reshape(-1, d_model), "w_ple_gate": w["w_ple_gate"].reshape(-1, d_model),
        "w_ple_proj": _unshard_cols(w["w_ple_proj"]),
        "conv_w": jnp.transpose(w["conv_w"], (1, 0, 2)).reshape(CONV_K, -1),
    }
    return out


def _pad_lanes(vec):
    return jnp.pad(vec, (0, LANES - vec.shape[0]))[None, :]


def _layer_fwd(x, p, w, small, tag):
    h = _rms_fwd(x, small["norm_mix_g"], f"rms_mix_{tag}")
    ag = _mm(h, w["w_ag"], "nn", F32, f"proj_glu_{tag}")
    qkv = _mm(h, w["w_qkv"], "nn", BF16, f"proj_qkv_{tag}")
    gg = _mm(h, w["w_gg"], "nn", F32, f"proj_gates_{tag}")
    f = _mm(h, w["w_f"], "nn", F32, f"proj_forget_{tag}")
    u1, u3 = _conv_fwd(ag, w["conv_w"], small["conv_b"], small["conv_ln_g"], small["conv_ln_b"], f"conv_{tag}")
    y_conv = _mm(u3, w["w_conv_out"], "nn", F32, f"conv_out_{tag}")
    c = _forget_fwd(f, small["b_forget"], f"forget_{tag}")
    c_heads = jnp.transpose(c[:, :N_HEADS])
    c_col, c_row = c_heads[:, :, None], c_heads[:, None, :]
    o, lse = _attn_fwd(qkv, c_col, c_row, f"attn_{tag}")
    y_attn = _mm(o, w["w_attn_out"], "nn", F32, f"attn_out_{tag}")
    merged = _merge_fwd(gg, y_conv, y_attn, f"merge_{tag}")
    x1 = _mm(merged, w["w_out"], "nn", F32, f"mix_out_{tag}", residual=x)
    hf = _rms_fwd(x1, small["norm_ffn_g"], f"rms_ffn_{tag}")
    gu = _mm(hf, w["w_gate_up"], "nn", F32, f"ffn_up_{tag}")
    act = _swiglu_fwd(gu, f"swiglu_{tag}")
    x2 = _mm(act, w["w_down"], "nn", F32, f"ffn_down_{tag}", residual=x1)
    hp = _rms_fwd(x2, small["norm_ple_g"], f"rms_ple_{tag}")
    z = _mm(hp, w["w_ple_gate"], "nn", F32, f"ple_gate_{tag}")
    pp = _mm(p, w["w_ple_proj"], "nn", F32, f"ple_proj_{tag}")
    x3 = _ple_fwd(x2, z, pp, f"ple_{tag}")
    saved = dict(x=x, p=p, h=h, ag=ag, qkv=qkv, gg=gg, f=f, u1=u1, u3=u3, y_conv=y_conv, c_col=c_col, c_row=c_row,
                 o=o, lse=lse, y_attn=y_attn, merged=merged, x1=x1, hf=hf, gu=gu, act=act, x2=x2, hp=hp, z=z, pp=pp)
    return x3, saved


def _layer_bwd(dx3, sv, w, small, tag):
    d_model = dx3.shape[1]
    gw, gs = {}, {}
    dz, dpp = _ple_bwd(dx3, sv["z"], sv["pp"], f"ple_bwd_{tag}")
    gw["w_ple_proj"] = _mm(sv["p"], dpp, "tn", F32, f"ple_proj_dw_{tag}")
    gw["w_ple_gate"] = _mm(sv["hp"], dz, "tn", F32, f"ple_gate_dw_{tag}")
    dhp = _mm(dz, w["w_ple_gate"], "nt", F32, f"ple_gate_dx_{tag}")
    dx2, dx2_b, gs["norm_ple_g"] = _rms_bwd(sv["x2"], small["norm_ple_g"], dhp, dx3, f"rms_ple_bwd_{tag}")
    gw["w_down"] = _mm(sv["act"], dx2_b, "tn", F32, f"ffn_down_dw_{tag}")
    dact = _mm(dx2_b, w["w_down"], "nt", F32, f"ffn_down_dx_{tag}")
    dgu = _swiglu_bwd(sv["gu"], dact, f"swiglu_bwd_{tag}")
    gw["w_gate_up"] = _mm(sv["hf"], dgu, "tn", F32, f"ffn_up_dw_{tag}")
    dhf = _mm(dgu, w["w_gate_up"], "nt", F32, f"ffn_up_dx_{tag}")
    dx1, dx1_b, gs["norm_ffn_g"] = _rms_bwd(sv["x1"], small["norm_ffn_g"], dhf, dx2, f"rms_ffn_bwd_{tag}")
    gw["w_out"] = _mm(sv["merged"], dx1_b, "tn", F32, f"mix_out_dw_{tag}")
    dm = _mm(dx1_b, w["w_out"], "nt", F32, f"mix_out_dx_{tag}")
    dyc, dya, dgg = _merge_bwd(sv["gg"], sv["y_conv"], sv["y_attn"], dm, f"merge_bwd_{tag}")
    gw["w_attn_out"] = _mm(sv["o"], dya, "tn", F32, f"attn_out_dw_{tag}")
    do = _mm(dya, w["w_attn_out"], "nt", F32, f"attn_out_dx_{tag}")
    stats, do_b = _attn_stats(sv["o"], do, sv["c_col"], sv["lse"], f"attn_stats_{tag}")
    dq, dk, dv, dc_row, dc_col = _attn_bwd(sv["qkv"], do_b, stats, sv["c_row"], f"attn_bwd_{tag}")
    dc = jnp.pad(jnp.transpose(dc_row[:, 0, :] + dc_col[:, :, 0]), ((0, 0), (0, LANES - N_HEADS)))
    df, db = _forget_bwd(sv["f"], small["b_forget"], dc, f"forget_bwd_{tag}")
    gs["b_forget"] = db[0, :N_HEADS]
    gw["w_conv_out"] = _mm(sv["u3"], dyc, "tn", F32, f"conv_out_dw_{tag}")
    du3 = _mm(dyc, w["w_conv_out"], "nt", F32, f"conv_out_dx_{tag}")
    du1, gs["conv_ln_g"], gs["conv_ln_b"], gs["conv_b"] = _conv_norm_bwd(
        sv["u1"], du3, small["conv_ln_g"], small["conv_ln_b"], f"conv_norm_bwd_{tag}")
    dag, gs["conv_w"] = _conv_bwd(sv["ag"], du1, w["conv_w"], f"conv_bwd_{tag}")
    dqkv = jnp.concatenate([dq, dk, dv], axis=1)
    g_ag = _mm(sv["h"], dag, "tn", F32, f"proj_glu_dw_{tag}")
    g_qkv = _mm(sv["h"], dqkv, "tn", F32, f"proj_qkv_dw_{tag}")
    g_f = _mm(sv["h"], df, "tn", F32, f"proj_forget_dw_{tag}")
    g_gg = _mm(sv["h"], dgg, "tn", F32, f"proj_gates_dw_{tag}")
    gw["w_in"] = jnp.concatenate([g_ag, g_qkv, g_f[:, :N_HEADS], g_gg], axis=1)
    dh = _mm(df, w["w_f"], "nt", F32, f"proj_forget_dx_{tag}")
    dh = _mm(dag, w["w_ag"], "nt", F32, f"proj_glu_dx_{tag}", residual=dh)
    dh = _mm(dqkv, w["w_qkv"], "nt", F32, f"proj_qkv_dx_{tag}", residual=dh)
    dh = _mm(dgg, w["w_gg"], "nt", F32, f"proj_gates_dx_{tag}", residual=dh)
    dx, _, gs["norm_mix_g"] = _rms_bwd(sv["x"], small["norm_mix_g"], dh, dx1, f"rms_mix_bwd_{tag}")
    for k in ("norm_ple_g", "norm_ffn_g", "norm_mix_g", "conv_ln_g", "conv_ln_b", "conv_b"):
        gs[k] = gs[k][0]
    return dx, gw, gs


_COL_SHARDED = ("w_in", "w_conv_out", "w_attn_out", "w_gate_up", "w_ple_proj")
_ROW_SHARDED = ("w_out", "w_down", "w_ple_gate")
_BIG = _COL_SHARDED + _ROW_SHARDED
_SMALL = ("norm_mix_g", "b_forget", "conv_b", "conv_ln_g", "conv_ln_b", "norm_ffn_g", "norm_ple_g")
_ORDER = ("norm_mix_g", "w_in", "b_forget", "conv_w", "conv_b", "conv_ln_g", "conv_ln_b", "w_conv_out", "w_attn_out",
          "w_out", "norm_ffn_g", "w_gate_up", "w_down", "norm_ple_g", "w_ple_gate", "w_ple_proj", "final_g")


def _pack(vectors, rows):
    flat = jnp.concatenate([v.reshape(-1) for v in vectors])
    return jnp.pad(flat, (0, rows * LANES - flat.shape[0])).reshape(1, rows, LANES)


def _unpack(packed, like):
    flat, out, at = packed.reshape(-1), [], 0
    for v in like:
        out.append(flat[at:at + v.size].reshape(v.shape))
        at += v.size
    return out


def kernel(x, p, norm_mix_g, w_in, b_forget, conv_w, conv_b, conv_ln_g, conv_ln_b, w_conv_out, w_attn_out, w_out, norm_ffn_g, w_gate_up, w_down, norm_ple_g, w_ple_gate, w_ple_proj, final_g, loss_target, m_norm_mix_g, m_w_in, m_b_forget, m_conv_w, m_conv_b, m_conv_ln_g, m_conv_ln_b, m_w_conv_out, m_w_attn_out, m_w_out, m_norm_ffn_g, m_w_gate_up, m_w_down, m_norm_ple_g, m_w_ple_gate, m_w_ple_proj, m_final_g, v_norm_mix_g, v_w_in, v_b_forget, v_conv_w, v_conv_b, v_conv_ln_g, v_conv_ln_b, v_w_conv_out, v_w_attn_out, v_w_out, v_norm_ffn_g, v_w_gate_up, v_w_down, v_norm_ple_g, v_w_ple_gate, v_w_ple_proj, v_final_g):
    weights = dict(norm_mix_g=norm_mix_g, w_in=w_in, b_forget=b_forget, conv_w=conv_w, conv_b=conv_b, conv_ln_g=conv_ln_g, conv_ln_b=conv_ln_b, w_conv_out=w_conv_out, w_attn_out=w_attn_out, w_out=w_out, norm_ffn_g=norm_ffn_g, w_gate_up=w_gate_up, w_down=w_down, norm_ple_g=norm_ple_g, w_ple_gate=w_ple_gate, w_ple_proj=w_ple_proj, final_g=final_g)
    mom1 = dict(norm_mix_g=m_norm_mix_g, w_in=m_w_in, b_forget=m_b_forget, conv_w=m_conv_w, conv_b=m_conv_b, conv_ln_g=m_conv_ln_g, conv_ln_b=m_conv_ln_b, w_conv_out=m_w_conv_out, w_attn_out=m_w_attn_out, w_out=m_w_out, norm_ffn_g=m_norm_ffn_g, w_gate_up=m_w_gate_up, w_down=m_w_down, norm_ple_g=m_norm_ple_g, w_ple_gate=m_w_ple_gate, w_ple_proj=m_w_ple_proj, final_g=m_final_g)
    mom2 = dict(norm_mix_g=v_norm_mix_g, w_in=v_w_in, b_forget=v_b_forget, conv_w=v_conv_w, conv_b=v_conv_b, conv_ln_g=v_conv_ln_g, conv_ln_b=v_conv_ln_b, w_conv_out=v_w_conv_out, w_attn_out=v_w_attn_out, w_out=v_w_out, norm_ffn_g=v_norm_ffn_g, w_gate_up=v_w_gate_up, w_down=v_w_down, norm_ple_g=v_norm_ple_g, w_ple_gate=v_w_ple_gate, w_ple_proj=v_w_ple_proj, final_g=v_final_g)
    depth = w_in.shape[0]
    d_model = x.shape[-1]
    xi, yi, ci = lax.axis_index("x"), lax.axis_index("y"), lax.axis_index("c")
    me = 4 * xi + 2 * yi + ci

    names = list(_BIG) + ["conv_w"]
    shards = [weights[k].astype(BF16) for k in _BIG] + [conv_w]
    gathered = dict(zip(names, _all_gather(shards, "gather_weights")))

    act = x[0]
    layers = []
    for l in range(depth):
        w = _layer_weights(gathered, l, d_model)
        small = {k: weights[k][l][None, :] for k in _SMALL if k != "b_forget"}
        small["b_forget"] = _pad_lanes(b_forget[l])
        act, saved = _layer_fwd(act, p[l, 0], w, small, f"l{l}")
        layers.append((w, small, saved))
    dact, g_final, loss_part = _loss_head(act, final_g[None, :], loss_target[0], "loss_head")
    loss = lax.psum(loss_part[0, 0], ("x", "y", "c"))

    big_grads = {k: [None] * depth for k in _BIG}
    small_grads = {k: [None] * depth for k in _SMALL + ("conv_w",)}
    for l in reversed(range(depth)):
        w, small, saved = layers[l]
        dact, gw, gs = _layer_bwd(dact, saved, w, small, f"l{l}")
        for k in _BIG:
            big_grads[k][l] = gw[k]
        for k in small_grads:
            small_grads[k][l] = gs[k]
    grad_x = dact[None]

    def to_blocks(k):
        per_layer = [_shard_cols(g) if k in _COL_SHARDED else g.reshape((N_DEV, -1) + g.shape[1:])
                     for g in big_grads[k]]
        return jnp.stack(per_layer, axis=1).astype(BF16)

    blocks = [to_blocks(k) for k in _BIG]
    theirs = _pair_exchange(blocks, "grads_pair_exchange")
    core = ci.reshape(1).astype(jnp.int32)
    chip_parts = [_pair_sum(a, b, core, f"grads_pair_sum_{k}") for k, a, b in zip(_BIG, blocks, theirs)]
    arrived = _chip_exchange(chip_parts, "grads_chip_exchange")
    out = {}
    for k, parts in zip(_BIG, arrived):
        out[k] = _sum_adamw(parts, weights[k], mom1[k], mom2[k], f"adamw_{k}")

    small_names = list(_SMALL) + ["conv_w", "final_g"]
    small_list = [jnp.stack(small_grads[k]) for k in _SMALL + ("conv_w",)] + [g_final[0]]
    n_small = sum(v.size for v in small_list)
    rows = -(-n_small // (8 * LANES)) * 8
    all_parts = _all_gather([_pack(small_list, rows)], "gather_small_grads")[0]
    full_conv_w = jnp.transpose(gathered["conv_w"], (1, 2, 0, 3)).reshape(depth, CONV_K, -1)
    full_conv_m = jnp.zeros_like(full_conv_w)
    ch_shard = conv_w.shape[-1]
    at = (0, 0, me * ch_shard)
    small_w = [weights[k] for k in _SMALL] + [full_conv_w, final_g]
    small_m = [mom1[k] for k in _SMALL] + [lax.dynamic_update_slice(full_conv_m, mom1["conv_w"], at), mom1["final_g"]]
    small_v = [mom2[k] for k in _SMALL] + [lax.dynamic_update_slice(full_conv_m + 1.0, mom2["conv_w"], at), mom2["final_g"]]
    res = _sum_adamw(all_parts, _pack(small_w, rows), _pack(small_m, rows), _pack(small_v, rows), "adamw_small")
    unpacked = [_unpack(r, small_list) for r in res]
    for idx, k in enumerate(small_names):
        vals = [u[idx] for u in unpacked]
        if k == "conv_w":
            vals = [lax.dynamic_slice(v, at, conv_w.shape) for v in vals]
        out[k] = vals

    return (loss, grad_x, *[out[k][0] for k in _ORDER], *[out[k][1] for k in _ORDER],
            *[out[k][2] for k in _ORDER], *[out[k][3] for k in _ORDER])
```

```python
import functools
import math

import jax
import jax.numpy as jnp
from jax import lax
from jax.experimental import pallas as pl
from jax.experimental.pallas import tpu as pltpu

F32 = jnp.float32
BF16 = jnp.bfloat16
MESH = pl.DeviceIdType.MESH
ANY = pl.BlockSpec(memory_space=pl.ANY)

N_DEV = 8
N_HEADS = 8
HEAD_DIM = 128
CONV_K = 31
HALO = 32
EPS = 1e-6
NEG_INF = -1e30
ADAM_LR = 0.001
ADAM_B1 = 0.9
ADAM_B2 = 0.999
ADAM_EPS = 1e-08
ADAM_WD = 0.01
ADAM_STEP = 10

V7X_VMEM_BYTES = 64 * 1024 * 1024
VMEM_LIMIT = V7X_VMEM_BYTES * 3 // 4
LANES = 128
ROW_TILE = 256
ATTN_TILE = 512
ATTN_ROWS = 256
CUM_TILE = 256


def _params(*sem):
    return pltpu.CompilerParams(dimension_semantics=sem, vmem_limit_bytes=VMEM_LIMIT)


def _tile(dim, pref, mult=LANES):
    t = min(pref, dim)
    t -= t % mult
    while t >= mult:
        if dim % t == 0:
            return t
        t -= mult
    return dim


def _sigmoid(x):
    return 1.0 / (1.0 + jnp.exp(-x))


def _place():
    x, y, c = lax.axis_index("x"), lax.axis_index("y"), lax.axis_index("c")
    chips = [(1 - x, y), (x, 1 - y), (1 - x, 1 - y)]
    return x, y, c, chips


def _all_gather(shards, name):
    n = len(shards)

    def body(*refs):
        ins, outs = refs[:n], refs[n:2 * n]
        send_sems, recv_sems, local_sems = refs[2 * n:]
        x, y, c, chips = _place()
        me, sibling = (x, y, c), (x, y, 1 - c)

        def idx(px, py, pc):
            return 4 * px + 2 * py + pc

        def copy(t, k, block, to, src=None):
            rows = outs[t].at[idx(*block)]
            return pltpu.make_async_remote_copy(
                src_ref=rows if src is None else src, dst_ref=rows,
                send_sem=send_sems.at[t, k], recv_sem=recv_sems.at[t, k],
                device_id=to, device_id_type=MESH)

        local = [pltpu.make_async_copy(ins[t], outs[t].at[idx(*me)], local_sems.at[t]) for t in range(n)]
        for cp in local:
            cp.start()
        first = []
        for t in range(n):
            first.append(copy(t, 0, me, sibling, src=ins[t]))
            first += [copy(t, 1 + j, me, (*chip, c), src=ins[t]) for j, chip in enumerate(chips)]
        for cp in first:
            cp.start()
        passed = []
        for j, chip in enumerate(chips):
            for t in range(n):
                copy(t, 1 + j, (*chip, c), me).wait_recv()
                cp = copy(t, 4 + j, (*chip, c), sibling)
                cp.start()
                passed.append(cp)
        for t in range(n):
            copy(t, 0, sibling, me).wait_recv()
        for j, chip in enumerate(chips):
            for t in range(n):
                copy(t, 4 + j, (*chip, 1 - c), me).wait_recv()
        for cp in first + passed:
            cp.wait_send()
        for cp in local:
            cp.wait()

    return pl.pallas_call(
        body, name=name,
        out_shape=[jax.ShapeDtypeStruct((N_DEV,) + s.shape, s.dtype) for s in shards],
        in_specs=[ANY] * n, out_specs=[ANY] * n,
        scratch_shapes=[pltpu.SemaphoreType.DMA((n, 7)), pltpu.SemaphoreType.DMA((n, 7)),
                        pltpu.SemaphoreType.DMA((n,))],
    )(*shards)


def _pair_exchange(grads, name):
    n = len(grads)

    def body(*refs):
        ins, theirs = refs[:n], refs[n:2 * n]
        send_sems, recv_sems = refs[2 * n:]
        x, y, c, _ = _place()
        copies = []
        for t in range(n):
            for q in range(4):
                give = pltpu.make_async_remote_copy(
                    src_ref=ins[t].at[2 * q + 1 - c], dst_ref=theirs[t].at[q],
                    send_sem=send_sems.at[t, q], recv_sem=recv_sems.at[t, q],
                    device_id=(x, y, 1 - c), device_id_type=MESH)
                give.start()
                copies.append(give)
        for cp in copies:
            cp.wait()

    return pl.pallas_call(
        body, name=name, out_shape=[jax.ShapeDtypeStruct((4,) + g.shape[1:], g.dtype) for g in grads],
        in_specs=[ANY] * n, out_specs=[ANY] * n,
        scratch_shapes=[pltpu.SemaphoreType.DMA((n, 4)), pltpu.SemaphoreType.DMA((n, 4))],
    )(*grads)


def _chip_exchange(parts, name):
    n = len(parts)

    def body(*refs):
        ins, outs = refs[:n], refs[n:2 * n]
        send_sems, recv_sems, local_sems = refs[2 * n:]
        x, y, c, chips = _place()
        copies = []
        for t in range(n):
            own = pltpu.make_async_copy(ins[t].at[2 * x + y], outs[t].at[0], local_sems.at[t])
            own.start()
            copies.append(own)
            for j, (px, py) in enumerate(chips):
                cp = pltpu.make_async_remote_copy(
                    src_ref=ins[t].at[2 * px + py], dst_ref=outs[t].at[1 + j],
                    send_sem=send_sems.at[t, j], recv_sem=recv_sems.at[t, j],
                    device_id=(px, py, c), device_id_type=MESH)
                cp.start()
                copies.append(cp)
        for cp in copies:
            cp.wait()

    return pl.pallas_call(
        body, name=name, out_shape=[jax.ShapeDtypeStruct(p.shape, p.dtype) for p in parts],
        in_specs=[ANY] * n, out_specs=[ANY] * n,
        scratch_shapes=[pltpu.SemaphoreType.DMA((n, 3)), pltpu.SemaphoreType.DMA((n, 3)),
                        pltpu.SemaphoreType.DMA((n,))],
    )(*parts)


_DOT_DIMS = {"nn": (((1,), (0,)), ((), ())), "nt": (((1,), (1,)), ((), ())), "tn": (((0,), (0,)), ((), ()))}


def _mm(a, b, mode, out_dtype, name, residual=None, tm=1024, tn=1024, tk=2048):
    if mode == "tn":
        (k_dim, m_dim), n_dim = a.shape, b.shape[1]
    else:
        (m_dim, k_dim), n_dim = a.shape, (b.shape[1] if mode == "nn" else b.shape[0])
    tm, tn, tk = _tile(m_dim, tm), _tile(n_dim, tn), _tile(k_dim, tk)
    nk = k_dim // tk

    def body(*refs):
        a_ref, b_ref = refs[:2]
        r_ref = refs[2] if residual is not None else None
        o_ref = refs[2 if residual is None else 3]

        def finish(r):
            if r_ref is not None:
                r = r + r_ref[...].astype(F32)
            o_ref[...] = r.astype(o_ref.dtype)

        part = lax.dot_general(a_ref[...].astype(BF16), b_ref[...].astype(BF16), _DOT_DIMS[mode],
                               preferred_element_type=F32)
        if nk == 1:
            finish(part)
            return
        acc, k = refs[-1], pl.program_id(2)

        @pl.when(k == 0)
        def _():
            acc[...] = part

        @pl.when((k > 0) & (k < nk - 1))
        def _():
            acc[...] += part

        @pl.when(k == nk - 1)
        def _():
            finish(acc[...] + part)

    a_spec = pl.BlockSpec((tk, tm), lambda i, j, k: (k, i)) if mode == "tn" else pl.BlockSpec((tm, tk), lambda i, j, k: (i, k))
    b_spec = pl.BlockSpec((tn, tk), lambda i, j, k: (j, k)) if mode == "nt" else pl.BlockSpec((tk, tn), lambda i, j, k: (k, j))
    o_spec = pl.BlockSpec((tm, tn), lambda i, j, k: (i, j))
    ins, specs = [a, b], [a_spec, b_spec]
    if residual is not None:
        ins.append(residual)
        specs.append(o_spec)
    return pl.pallas_call(
        body, name=name, grid=(m_dim // tm, n_dim // tn, nk),
        out_shape=jax.ShapeDtypeStruct((m_dim, n_dim), out_dtype),
        in_specs=specs, out_specs=o_spec,
        scratch_shapes=[pltpu.VMEM((tm, tn), F32)] if nk > 1 else [],
        compiler_params=_params("parallel", "parallel", "arbitrary"),
    )(*ins)


def _rows(body, name, row_ins, full_ins, row_outs, acc_outs, ts=ROW_TILE):
    s = row_ins[0].shape[0]
    ts = _tile(s, ts, 16)
    n_ri, n_fi, n_ro = len(row_ins), len(full_ins), len(row_outs)

    def kernel(*refs):
        i = pl.program_id(0)
        ri, fi = refs[:n_ri], refs[n_ri:n_ri + n_fi]
        ro, ao = refs[n_ri + n_fi:n_ri + n_fi + n_ro], refs[n_ri + n_fi + n_ro:]

        if ao:
            @pl.when(i == 0)
            def _():
                for r in ao:
                    r[...] = jnp.zeros_like(r)

        body(i, ri, fi, ro, ao)

    def row_spec(cols):
        return pl.BlockSpec((ts, cols), lambda i: (i, 0))

    def full_spec(shape):
        return pl.BlockSpec(shape, lambda i: (0,) * len(shape))

    return pl.pallas_call(
        kernel, name=name, grid=(s // ts,),
        out_shape=[jax.ShapeDtypeStruct((s, c), d) for c, d in row_outs]
        + [jax.ShapeDtypeStruct(sh, d) for sh, d in acc_outs],
        in_specs=[row_spec(a.shape[1]) for a in row_ins] + [full_spec(a.shape) for a in full_ins],
        out_specs=[row_spec(c) for c, _ in row_outs] + [full_spec(sh) for sh, _ in acc_outs],
        compiler_params=_params("arbitrary"),
    )(*row_ins, *full_ins)


def _rms_stats(x):
    return lax.rsqrt(jnp.mean(x * x, axis=-1, keepdims=True) + EPS)


def _rms_fwd(x, g, name):
    def body(i, ri, fi, ro, ao):
        xv = ri[0][...]
        ro[0][...] = (xv * _rms_stats(xv) * fi[0][...]).astype(BF16)

    return _rows(body, name, [x], [g], [(x.shape[1], BF16)], [])[0]


def _rms_bwd_rows(xv, gv, dh):
    r = _rms_stats(xv)
    xhat = xv * r
    dxhat = dh * gv
    dx = r * (dxhat - xhat * jnp.mean(dxhat * xhat, axis=-1, keepdims=True))
    return dx, jnp.sum(dh * xhat, axis=0, keepdims=True)


def _rms_bwd(x, g, dh, dres, name):
    d = x.shape[1]

    def body(i, ri, fi, ro, ao):
        dx, dg = _rms_bwd_rows(ri[0][...], fi[0][...], ri[1][...])
        dx = dx + ri[2][...]
        ro[0][...] = dx
        ro[1][...] = dx.astype(BF16)
        ao[0][...] += dg

    return _rows(body, name, [x, dh, dres], [g], [(d, F32), (d, BF16)], [((1, d), F32)])


def _loss_head(x, g, target, name):
    d = x.shape[1]

    def body(i, ri, fi, ro, ao):
        xv, gv = ri[0][...], fi[0][...]
        y = xv * _rms_stats(xv) * gv
        err = y - ri[1][...]
        ao[1][...] += 0.5 * jnp.sum(jnp.mean(err * err, axis=-1, keepdims=True))
        dx, dg = _rms_bwd_rows(xv, gv, err * (1.0 / d))
        ro[0][...] = dx
        ao[0][...] += dg

    return _rows(body, name, [x, target], [g], [(d, F32)], [((1, d), F32), ((8, LANES), F32)])


def _merge_fwd(gg, y_conv, y_attn, name):
    d = y_conv.shape[1]

    def body(i, ri, fi, ro, ao):
        gv = ri[0][...]
        m = _sigmoid(gv[:, :d]) * ri[1][...] + _sigmoid(gv[:, d:]) * ri[2][...]
        ro[0][...] = m.astype(BF16)

    return _rows(body, name, [gg, y_conv, y_attn], [], [(d, BF16)], [])[0]


def _merge_bwd(gg, y_conv, y_attn, dm, name):
    d = y_conv.shape[1]

    def body(i, ri, fi, ro, ao):
        gv, yc, ya, dmv = ri[0][...], ri[1][...], ri[2][...], ri[3][...]
        sc, sa = _sigmoid(gv[:, :d]), _sigmoid(gv[:, d:])
        ro[0][...] = (dmv * sc).astype(BF16)
        ro[1][...] = (dmv * sa).astype(BF16)
        ro[2][:, :d] = (dmv * yc * sc * (1.0 - sc)).astype(BF16)
        ro[2][:, d:] = (dmv * ya * sa * (1.0 - sa)).astype(BF16)

    return _rows(body, name, [gg, y_conv, y_attn, dm], [], [(d, BF16), (d, BF16), (2 * d, BF16)], [])


def _swiglu_fwd(gu, name):
    f = gu.shape[1] // 2

    def body(i, ri, fi, ro, ao):
        v = ri[0][...]
        gate, up = v[:, :f], v[:, f:]
        ro[0][...] = (gate * _sigmoid(gate) * up).astype(BF16)

    return _rows(body, name, [gu], [], [(f, BF16)], [], ts=128)[0]


def _swiglu_bwd(gu, dact, name):
    f = gu.shape[1] // 2

    def body(i, ri, fi, ro, ao):
        v, dv = ri[0][...], ri[1][...]
        gate, up = v[:, :f], v[:, f:]
        sg = _sigmoid(gate)
        ro[0][:, :f] = (dv * up * sg * (1.0 + gate * (1.0 - sg))).astype(BF16)
        ro[0][:, f:] = (dv * gate * sg).astype(BF16)

    return _rows(body, name, [gu, dact], [], [(2 * f, BF16)], [], ts=128)[0]


def _ple_fwd(x, z, pp, name):
    d = x.shape[1]

    def body(i, ri, fi, ro, ao):
        ro[0][...] = ri[0][...] + _sigmoid(ri[1][...]) * ri[2][...]

    return _rows(body, name, [x, z, pp], [], [(d, F32)], [])[0]


def _ple_bwd(dx, z, pp, name):
    d = dx.shape[1]

    def body(i, ri, fi, ro, ao):
        dv, sz = ri[0][...], _sigmoid(ri[1][...])
        ro[0][...] = (dv * ri[2][...] * sz * (1.0 - sz)).astype(BF16)
        ro[1][...] = (dv * sz).astype(BF16)

    return _rows(body, name, [dx, z, pp], [], [(d, BF16), (d, BF16)], [])


def _triangle(n, lower):
    r = lax.broadcasted_iota(jnp.int32, (n, n), 0)
    c = lax.broadcasted_iota(jnp.int32, (n, n), 1)
    return jnp.where((c <= r) if lower else (c >= r), 1.0, 0.0).astype(F32)


def _forget_fwd(f, bias, name):
    s = f.shape[0]
    tb = _tile(s, CUM_TILE, 8)

    def body(f_ref, b_ref, c_ref, carry):
        @pl.when(pl.program_id(0) == 0)
        def _():
            carry[...] = jnp.zeros_like(carry)

        xv = f_ref[...] + b_ref[...]
        logf = jnp.minimum(xv, 0.0) - jnp.log(1.0 + jnp.exp(-jnp.abs(xv)))
        cv = jnp.dot(_triangle(tb, True), logf, precision=lax.Precision.HIGHEST, preferred_element_type=F32) + carry[...]
        c_ref[...] = cv
        carry[...] = cv[tb - 1:tb, :]

    return pl.pallas_call(
        body, name=name, grid=(s // tb,), out_shape=jax.ShapeDtypeStruct((s, LANES), F32),
        in_specs=[pl.BlockSpec((tb, LANES), lambda i: (i, 0)), pl.BlockSpec((1, LANES), lambda i: (0, 0))],
        out_specs=pl.BlockSpec((tb, LANES), lambda i: (i, 0)),
        scratch_shapes=[pltpu.VMEM((1, LANES), F32)],
        compiler_params=_params("arbitrary"),
    )(f, bias)


def _forget_bwd(f, bias, dc, name):
    s = f.shape[0]
    tb = _tile(s, CUM_TILE, 16)
    nb = s // tb

    def body(f_ref, b_ref, dc_ref, df_ref, db_ref, carry):
        @pl.when(pl.program_id(0) == 0)
        def _():
            carry[...] = jnp.zeros_like(carry)
            db_ref[...] = jnp.zeros_like(db_ref)

        dlog = jnp.dot(_triangle(tb, False), dc_ref[...], precision=lax.Precision.HIGHEST,
                       preferred_element_type=F32) + carry[...]
        carry[...] = dlog[0:1, :]
        dlogit = dlog * (1.0 - _sigmoid(f_ref[...] + b_ref[...]))
        df_ref[...] = dlogit.astype(BF16)
        db_ref[...] += jnp.sum(dlogit, axis=0, keepdims=True)

    back = lambda i: (nb - 1 - i, 0)
    return pl.pallas_call(
        body, name=name, grid=(nb,),
        out_shape=[jax.ShapeDtypeStruct((s, LANES), BF16), jax.ShapeDtypeStruct((1, LANES), F32)],
        in_specs=[pl.BlockSpec((tb, LANES), back), pl.BlockSpec((1, LANES), lambda i: (0, 0)),
                  pl.BlockSpec((tb, LANES), back)],
        out_specs=[pl.BlockSpec((tb, LANES), back), pl.BlockSpec((1, LANES), lambda i: (0, 0))],
        scratch_shapes=[pltpu.VMEM((1, LANES), F32)],
        compiler_params=_params("arbitrary"),
    )(f, bias, dc)


def _glu(block, ch):
    return block[:, :ch] * _sigmoid(block[:, ch:])


def _layernorm_rows(u1, g, b):
    mu = jnp.mean(u1, axis=-1, keepdims=True)
    cen = u1 - mu
    rstd = lax.rsqrt(jnp.mean(cen * cen, axis=-1, keepdims=True) + EPS)
    xhat = cen * rstd
    return xhat, rstd, xhat * g + b


def _conv_fwd(ag, conv_w, conv_b, ln_g, ln_b, name):
    s, ch = ag.shape[0], ag.shape[1] // 2
    ts = _tile(s, ROW_TILE, HALO)
    per = ts // HALO

    def body(cur_ref, halo_ref, w_ref, cb_ref, g_ref, b_ref, u1_ref, u3_ref, cat):
        i = pl.program_id(0)
        cat[HALO:, :] = _glu(cur_ref[...], ch)
        cat[:HALO, :] = jnp.where(i == 0, 0.0, _glu(halo_ref[...], ch))
        full = cat[...]
        acc = jnp.zeros((ts, ch), F32) + cb_ref[...]
        for j in range(CONV_K):
            shifted = full if j == CONV_K - 1 else pltpu.roll(full, CONV_K - 1 - j, 0)
            acc = acc + w_ref[j:j + 1, :] * shifted[HALO:, :]
        u1_ref[...] = acc
        _, _, u2 = _layernorm_rows(acc, g_ref[...], b_ref[...])
        u3_ref[...] = (u2 * _sigmoid(u2)).astype(BF16)

    whole = lambda shape: pl.BlockSpec(shape, lambda i: (0, 0))
    return pl.pallas_call(
        body, name=name, grid=(s // ts,),
        out_shape=[jax.ShapeDtypeStruct((s, ch), F32), jax.ShapeDtypeStruct((s, ch), BF16)],
        in_specs=[pl.BlockSpec((ts, 2 * ch), lambda i: (i, 0)),
                  pl.BlockSpec((HALO, 2 * ch), lambda i: (jnp.maximum(i * per - 1, 0), 0)),
                  whole(conv_w.shape), whole((1, ch)), whole((1, ch)), whole((1, ch))],
        out_specs=[pl.BlockSpec((ts, ch), lambda i: (i, 0)), pl.BlockSpec((ts, ch), lambda i: (i, 0))],
        scratch_shapes=[pltpu.VMEM((HALO + ts, ch), F32)],
        compiler_params=_params("arbitrary"),
    )(ag, ag, conv_w, conv_b, ln_g, ln_b)


def _conv_norm_bwd(u1, du3, ln_g, ln_b, name):
    ch = u1.shape[1]

    def body(i, ri, fi, ro, ao):
        g = fi[0][...]
        xhat, rstd, u2 = _layernorm_rows(ri[0][...], g, fi[1][...])
        sg = _sigmoid(u2)
        du2 = ri[1][...] * sg * (1.0 + u2 * (1.0 - sg))
        dxhat = du2 * g
        du1 = rstd * (dxhat - jnp.mean(dxhat, axis=-1, keepdims=True)
                      - xhat * jnp.mean(dxhat * xhat, axis=-1, keepdims=True))
        ro[0][...] = du1
        ao[0][...] += jnp.sum(du2 * xhat, axis=0, keepdims=True)
        ao[1][...] += jnp.sum(du2, axis=0, keepdims=True)
        ao[2][...] += jnp.sum(du1, axis=0, keepdims=True)

    return _rows(body, name, [u1, du3], [ln_g, ln_b], [(ch, F32)], [((1, ch), F32)] * 3)


def _conv_bwd(ag, du1, conv_w, name):
    s, ch = du1.shape
    ts = _tile(s, ROW_TILE, HALO)
    per = ts // HALO
    last = s // HALO - 1
    nb = s // ts

    def body(cur_ref, halo_ref, d_ref, dnext_ref, w_ref, dag_ref, dw_ref, cat, dcat):
        i = pl.program_id(0)

        @pl.when(i == 0)
        def _():
            dw_ref[...] = jnp.zeros_like(dw_ref)

        cur = cur_ref[...]
        cat[HALO:, :] = _glu(cur, ch)
        cat[:HALO, :] = jnp.where(i == 0, 0.0, _glu(halo_ref[...], ch))
        dcur = d_ref[...]
        dcat[:ts, :] = dcur
        dcat[ts:, :] = jnp.where(i == nb - 1, 0.0, dnext_ref[...])
        full, dfull = cat[...], dcat[...]
        du0 = jnp.zeros((ts, ch), F32)
        for j in range(CONV_K):
            back = CONV_K - 1 - j
            dshift = dfull if back == 0 else pltpu.roll(dfull, ts + HALO - back, 0)
            du0 = du0 + w_ref[j:j + 1, :] * dshift[:ts, :]
            ushift = full if back == 0 else pltpu.roll(full, back, 0)
            dw_ref[j:j + 1, :] += jnp.sum(dcur * ushift[HALO:, :], axis=0, keepdims=True)
        a, sg = cur[:, :ch], _sigmoid(cur[:, ch:])
        dag_ref[:, :ch] = (du0 * sg).astype(BF16)
        dag_ref[:, ch:] = (du0 * a * sg * (1.0 - sg)).astype(BF16)

    return pl.pallas_call(
        body, name=name, grid=(nb,),
        out_shape=[jax.ShapeDtypeStruct((s, 2 * ch), BF16), jax.ShapeDtypeStruct(conv_w.shape, F32)],
        in_specs=[pl.BlockSpec((ts, 2 * ch), lambda i: (i, 0)),
                  pl.BlockSpec((HALO, 2 * ch), lambda i: (jnp.maximum(i * per - 1, 0), 0)),
                  pl.BlockSpec((ts, ch), lambda i: (i, 0)),
                  pl.BlockSpec((HALO, ch), lambda i: (jnp.minimum((i + 1) * per, last), 0)),
                  pl.BlockSpec(conv_w.shape, lambda i: (0, 0))],
        out_specs=[pl.BlockSpec((ts, 2 * ch), lambda i: (i, 0)), pl.BlockSpec(conv_w.shape, lambda i: (0, 0))],
        scratch_shapes=[pltpu.VMEM((HALO + ts, ch), F32), pltpu.VMEM((ts + HALO, ch), F32)],
        compiler_params=_params("arbitrary"),
    )(ag, ag, du1, du1, conv_w)


_SCALE = 1.0 / math.sqrt(HEAD_DIM)


STAT_LANES = 8


def _causal(rows, cols, row0):
    r = row0 + lax.broadcasted_iota(jnp.int32, (rows, cols), 0)
    return lax.broadcasted_iota(jnp.int32, (rows, cols), 1) <= r


def _attn_fwd(qkv, c_row, name):
    s = qkv.shape[0]
    t = _tile(s, ATTN_TILE)
    nt, part_rows = s // t, min(ATTN_ROWS, t)
    n_parts = t // part_rows
    h_, d = N_HEADS, HEAD_DIM

    def body(q_ref, k_ref, v_ref, cr_ref, o_ref, lse_ref, *scratch):
        m_sc, l_sc, acc = scratch[:n_parts], scratch[n_parts:2 * n_parts], scratch[2 * n_parts:]
        i = pl.program_id(1)
        for part in range(n_parts):
            m_sc[part][...] = jnp.full_like(m_sc[part], NEG_INF)
            l_sc[part][...] = jnp.zeros_like(l_sc[part])
            acc[part][...] = jnp.zeros_like(acc[part])

        def block(j, diagonal):
            at = pl.multiple_of(j * t, t)
            k, v = k_ref[pl.ds(at, t), :], v_ref[pl.ds(at, t), :]
            cr = cr_ref[0, pl.ds(j, 1), :]
            for part in range(n_parts):
                rows = pl.ds(part * part_rows, part_rows)
                sc = lax.dot_general(q_ref[rows, :], k, _DOT_DIMS["nt"], preferred_element_type=F32) * _SCALE - cr
                if diagonal:
                    sc = jnp.where(_causal(part_rows, t, part * part_rows), sc, NEG_INF)
                m_old = m_sc[part][...]
                m_new = jnp.maximum(m_old, jnp.max(sc, axis=-1, keepdims=True))
                alpha = jnp.exp(m_old - m_new)
                p = jnp.exp(sc - m_new)
                l_sc[part][...] = alpha * l_sc[part][...] + jnp.sum(p, axis=-1, keepdims=True)
                acc[part][...] = alpha * acc[part][...] + jnp.dot(p.astype(BF16), v, preferred_element_type=F32)
                m_sc[part][...] = m_new

        def below(j, carry):
            block(j, False)
            return carry

        lax.fori_loop(0, i, below, 0)
        block(i, True)
        for part in range(n_parts):
            rows = pl.ds(part * part_rows, part_rows)
            o_ref[rows, :] = (acc[part][...] / l_sc[part][...]).astype(BF16)
            lse_ref[0, rows, :] = m_sc[part][...] + jnp.log(l_sc[part][...])

    return pl.pallas_call(
        body, name=name, grid=(h_, nt),
        out_shape=[jax.ShapeDtypeStruct((s, h_ * d), BF16), jax.ShapeDtypeStruct((h_, s, 1), F32)],
        in_specs=[pl.BlockSpec((t, d), lambda h, i: (i, h)),
                  pl.BlockSpec((s, d), lambda h, i: (0, h_ + h)),
                  pl.BlockSpec((s, d), lambda h, i: (0, 2 * h_ + h)),
                  pl.BlockSpec((1, nt, t), lambda h, i: (h, 0, 0))],
        out_specs=[pl.BlockSpec((t, d), lambda h, i: (i, h)), pl.BlockSpec((1, t, 1), lambda h, i: (h, i, 0))],
        scratch_shapes=[pltpu.VMEM((part_rows, 1), F32)] * (2 * n_parts) + [pltpu.VMEM((part_rows, d), F32)] * n_parts,
        compiler_params=_params("parallel", "arbitrary"),
    )(qkv, qkv, qkv, c_row.reshape(h_, nt, t))


def _attn_stats(o, do, lse, name):
    s = o.shape[0]
    t = _tile(s, ATTN_TILE)
    h_, d = N_HEADS, HEAD_DIM

    def body(o_ref, do_ref, lse_ref, st_ref, dob_ref):
        dov = do_ref[...]
        st_ref[...] = jnp.zeros_like(st_ref)
        st_ref[0, :, 0:1] = -lse_ref[0]
        st_ref[0, :, 1:2] = jnp.sum(o_ref[...].astype(F32) * dov, axis=-1, keepdims=True)
        dob_ref[...] = dov.astype(BF16)

    col = pl.BlockSpec((1, t, 1), lambda h, i: (h, i, 0))
    blk = pl.BlockSpec((t, d), lambda h, i: (i, h))
    return pl.pallas_call(
        body, name=name, grid=(h_, s // t),
        out_shape=[jax.ShapeDtypeStruct((h_, s, STAT_LANES), F32), jax.ShapeDtypeStruct(o.shape, BF16)],
        in_specs=[blk, blk, col],
        out_specs=[pl.BlockSpec((1, t, STAT_LANES), lambda h, i: (h, i, 0)), blk],
        compiler_params=_params("parallel", "parallel"),
    )(o, do, lse)


def _attn_bwd(qkv, do, stats, c_row, name):
    s = qkv.shape[0]
    t = _tile(s, ATTN_TILE)
    nt = s // t
    h_, d = N_HEADS, HEAD_DIM

    def body(q_ref, do_ref, k_ref, v_ref, st_ref, cr_ref, dq_ref, dk_ref, dv_ref, dcr_ref, dcc_ref,
             dq_acc, dk_acc, dv_acc, dcr_acc):
        j = pl.program_id(1)

        @pl.when(j == 0)
        def _():
            dq_acc[...] = jnp.zeros_like(dq_acc)
            dcc_ref[...] = jnp.zeros_like(dcc_ref)

        dk_acc[...] = jnp.zeros_like(dk_acc)
        dv_acc[...] = jnp.zeros_like(dv_acc)
        dcr_acc[...] = jnp.zeros_like(dcr_acc)
        k, v, cr = k_ref[...], v_ref[...], cr_ref[0]

        def block(i, diagonal):
            rows = pl.ds(pl.multiple_of(i * t, t), t)
            q, dov, st = q_ref[rows, :], do_ref[rows, :], st_ref[0, rows, :]
            sc = lax.dot_general(q, k, _DOT_DIMS["nt"], preferred_element_type=F32) * _SCALE + (st[:, 0:1] - cr)
            p = jnp.exp(sc)
            if diagonal:
                p = jnp.where(_causal(t, t, 0), p, 0.0)
            dp = lax.dot_general(dov, v, _DOT_DIMS["nt"], preferred_element_type=F32)
            ds = p * (dp - st[:, 1:2])
            ds_b = ds.astype(BF16)
            dv_acc[...] += lax.dot_general(p.astype(BF16), dov, _DOT_DIMS["tn"], preferred_element_type=F32)
            dk_acc[...] += lax.dot_general(ds_b, q, _DOT_DIMS["tn"], preferred_element_type=F32)
            dq_acc[rows, :] += jnp.dot(ds_b, k, preferred_element_type=F32)
            dcr_acc[...] -= jnp.sum(ds, axis=0, keepdims=True)
            dcc_ref[0, rows, :] += jnp.sum(ds, axis=-1, keepdims=True)

        def above(i, carry):
            block(i, False)
            return carry

        block(j, True)
        lax.fori_loop(j + 1, nt, above, 0)
        dk_ref[...] = (dk_acc[...] * _SCALE).astype(BF16)
        dv_ref[...] = dv_acc[...].astype(BF16)
        dcr_ref[0] = dcr_acc[...]

        @pl.when(j == nt - 1)
        def _():
            dq_ref[...] = (dq_acc[...] * _SCALE).astype(BF16)

    head = pl.BlockSpec((s, d), lambda h, j: (0, h))
    return pl.pallas_call(
        body, name=name, grid=(h_, nt),
        out_shape=[jax.ShapeDtypeStruct((s, h_ * d), BF16)] * 3
        + [jax.ShapeDtypeStruct((h_, 1, s), F32), jax.ShapeDtypeStruct((h_, s, 1), F32)],
        in_specs=[head, head,
                  pl.BlockSpec((t, d), lambda h, j: (j, h_ + h)),
                  pl.BlockSpec((t, d), lambda h, j: (j, 2 * h_ + h)),
                  pl.BlockSpec((1, s, STAT_LANES), lambda h, j: (h, 0, 0)),
                  pl.BlockSpec((1, 1, t), lambda h, j: (h, 0, j))],
        out_specs=[head, pl.BlockSpec((t, d), lambda h, j: (j, h)), pl.BlockSpec((t, d), lambda h, j: (j, h)),
                   pl.BlockSpec((1, 1, t), lambda h, j: (h, 0, j)), pl.BlockSpec((1, s, 1), lambda h, j: (h, 0, 0))],
        scratch_shapes=[pltpu.VMEM((s, d), F32), pltpu.VMEM((t, d), F32), pltpu.VMEM((t, d), F32),
                        pltpu.VMEM((1, t), F32)],
        compiler_params=_params("parallel", "arbitrary"),
    )(qkv, do, qkv, qkv, stats, c_row)


def _pair_sum(grads, theirs, core, name):
    _, l, r, c = theirs.shape
    tr = _tile(r, 256, 16)

    def body(core_ref, a_ref, b_ref, o_ref):
        o_ref[...] = (a_ref[...].astype(F32) + b_ref[...].astype(F32)).astype(BF16)

    spec = pl.BlockSpec((1, 1, tr, c), lambda q, k, i, core_ref: (q, k, i, 0))
    mine = pl.BlockSpec((1, 1, tr, c), lambda q, k, i, core_ref: (2 * q + core_ref[0], k, i, 0))
    return pl.pallas_call(
        body, name=name, out_shape=jax.ShapeDtypeStruct(theirs.shape, BF16),
        grid_spec=pltpu.PrefetchScalarGridSpec(num_scalar_prefetch=1, grid=(4, l, r // tr),
                                               in_specs=[mine, spec], out_specs=spec),
        compiler_params=_params("parallel", "parallel", "parallel"),
    )(core, grads, theirs)


def _adamw(g, w, m, v):
    m = ADAM_B1 * m + (1.0 - ADAM_B1) * g
    v = ADAM_B2 * v + (1.0 - ADAM_B2) * (g * g)
    m_hat = m / (1.0 - ADAM_B1 ** ADAM_STEP)
    v_hat = v / (1.0 - ADAM_B2 ** ADAM_STEP)
    return -ADAM_LR * (m_hat / (jnp.sqrt(v_hat) + ADAM_EPS) + ADAM_WD * w), m, v


def _sum_adamw(parts, w, m, v, name):
    n_parts, l, r, c = parts.shape
    tr = _tile(r, 128, 16)

    def body(p_ref, w_ref, m_ref, v_ref, g_ref, d_ref, nm_ref, nv_ref):
        g = p_ref[0].astype(F32)
        for k in range(1, n_parts):
            g = g + p_ref[k].astype(F32)
        g_ref[...] = g
        d_ref[...], nm_ref[...], nv_ref[...] = _adamw(g, w_ref[...], m_ref[...], v_ref[...])

    spec = pl.BlockSpec((1, tr, c), lambda k, i: (k, i, 0))
    return pl.pallas_call(
        body, name=name, grid=(l, r // tr), out_shape=[jax.ShapeDtypeStruct(w.shape, F32)] * 4,
        in_specs=[pl.BlockSpec((n_parts, 1, tr, c), lambda k, i: (0, k, i, 0)), spec, spec, spec],
        out_specs=[spec] * 4, compiler_params=_params("parallel", "parallel"),
    )(parts, w, m, v)


def _unshard_cols(gathered):
    return jnp.transpose(gathered, (1, 0, 2)).reshape(gathered.shape[1], -1)


def _shard_cols(full):
    k = full.shape[0]
    return jnp.transpose(full.reshape(k, N_DEV, -1), (1, 0, 2))


def _layer_weights(gathered, layer, d_model):
    w = {name: arr[:, layer] for name, arr in gathered.items()}
    hd = N_HEADS * HEAD_DIM
    w_in = _unshard_cols(w["w_in"])
    f0 = d_model + 3 * hd
    out = {
        "w_ag": w_in[:, :d_model], "w_qkv": w_in[:, d_model:f0],
        "w_f": jnp.pad(w_in[:, f0:f0 + N_HEADS], ((0, 0), (0, LANES - N_HEADS))),
        "w_gg": w_in[:, f0 + N_HEADS:],
        "w_conv_out": _unshard_cols(w["w_conv_out"]), "w_attn_out": _unshard_cols(w["w_attn_out"]),
        "w_out": w["w_out"].reshape(-1, d_model), "w_gate_up": _unshard_cols(w["w_gate_up"]),
        "w_down": w["w_down"].reshape(-1, d_model), "w_ple_gate": w["w_ple_gate"].reshape(-1, d_model),
        "w_ple_proj": _unshard_cols(w["w_ple_proj"]),
        "conv_w": jnp.transpose(w["conv_w"], (1, 0, 2)).reshape(CONV_K, -1),
    }
    return out


def _pad_lanes(vec):
    return jnp.pad(vec, (0, LANES - vec.shape[0]))[None, :]


def _layer_fwd(x, p, w, small, tag):
    h = _rms_fwd(x, small["norm_mix_g"], f"rms_mix_{tag}")
    ag = _mm(h, w["w_ag"], "nn", F32, f"proj_glu_{tag}")
    qkv = _mm(h, w["w_qkv"], "nn", BF16, f"proj_qkv_{tag}")
    gg = _mm(h, w["w_gg"], "nn", F32, f"proj_gates_{tag}")
    f = _mm(h, w["w_f"], "nn", F32, f"proj_forget_{tag}")
    u1, u3 = _conv_fwd(ag, w["conv_w"], small["conv_b"], small["conv_ln_g"], small["conv_ln_b"], f"conv_{tag}")
    y_conv = _mm(u3, w["w_conv_out"], "nn", F32, f"conv_out_{tag}")
    c = _forget_fwd(f, small["b_forget"], f"forget_{tag}")
    c_heads = jnp.transpose(c[:, :N_HEADS])
    c_row = c_heads[:, None, :]
    o, lse = _attn_fwd(qkv, c_row, f"attn_{tag}")
    y_attn = _mm(o, w["w_attn_out"], "nn", F32, f"attn_out_{tag}")
    merged = _merge_fwd(gg, y_conv, y_attn, f"merge_{tag}")
    x1 = _mm(merged, w["w_out"], "nn", F32, f"mix_out_{tag}", residual=x)
    hf = _rms_fwd(x1, small["norm_ffn_g"], f"rms_ffn_{tag}")
    gu = _mm(hf, w["w_gate_up"], "nn", F32, f"ffn_up_{tag}")
    act = _swiglu_fwd(gu, f"swiglu_{tag}")
    x2 = _mm(act, w["w_down"], "nn", F32, f"ffn_down_{tag}", residual=x1)
    hp = _rms_fwd(x2, small["norm_ple_g"], f"rms_ple_{tag}")
    z = _mm(hp, w["w_ple_gate"], "nn", F32, f"ple_gate_{tag}")
    pp = _mm(p, w["w_ple_proj"], "nn", F32, f"ple_proj_{tag}")
    x3 = _ple_fwd(x2, z, pp, f"ple_{tag}")
    saved = dict(x=x, p=p, h=h, ag=ag, qkv=qkv, gg=gg, f=f, u1=u1, u3=u3, y_conv=y_conv, c_row=c_row,
                 o=o, lse=lse, y_attn=y_attn, merged=merged, x1=x1, hf=hf, gu=gu, act=act, x2=x2, hp=hp, z=z, pp=pp)
    return x3, saved


def _layer_bwd(dx3, sv, w, small, tag):
    d_model = dx3.shape[1]
    gw, gs = {}, {}
    dz, dpp = _ple_bwd(dx3, sv["z"], sv["pp"], f"ple_bwd_{tag}")
    gw["w_ple_proj"] = _mm(sv["p"], dpp, "tn", BF16, f"ple_proj_dw_{tag}")
    gw["w_ple_gate"] = _mm(sv["hp"], dz, "tn", BF16, f"ple_gate_dw_{tag}")
    dhp = _mm(dz, w["w_ple_gate"], "nt", F32, f"ple_gate_dx_{tag}")
    dx2, dx2_b, gs["norm_ple_g"] = _rms_bwd(sv["x2"], small["norm_ple_g"], dhp, dx3, f"rms_ple_bwd_{tag}")
    gw["w_down"] = _mm(sv["act"], dx2_b, "tn", BF16, f"ffn_down_dw_{tag}")
    dact = _mm(dx2_b, w["w_down"], "nt", F32, f"ffn_down_dx_{tag}")
    dgu = _swiglu_bwd(sv["gu"], dact, f"swiglu_bwd_{tag}")
    gw["w_gate_up"] = _mm(sv["hf"], dgu, "tn", BF16, f"ffn_up_dw_{tag}")
    dhf = _mm(dgu, w["w_gate_up"], "nt", F32, f"ffn_up_dx_{tag}")
    dx1, dx1_b, gs["norm_ffn_g"] = _rms_bwd(sv["x1"], small["norm_ffn_g"], dhf, dx2, f"rms_ffn_bwd_{tag}")
    gw["w_out"] = _mm(sv["merged"], dx1_b, "tn", BF16, f"mix_out_dw_{tag}")
    dm = _mm(dx1_b, w["w_out"], "nt", F32, f"mix_out_dx_{tag}")
    dyc, dya, dgg = _merge_bwd(sv["gg"], sv["y_conv"], sv["y_attn"], dm, f"merge_bwd_{tag}")
    gw["w_attn_out"] = _mm(sv["o"], dya, "tn", BF16, f"attn_out_dw_{tag}")
    do = _mm(dya, w["w_attn_out"], "nt", F32, f"attn_out_dx_{tag}")
    stats, do_b = _attn_stats(sv["o"], do, sv["lse"], f"attn_stats_{tag}")
    dq, dk, dv, dc_row, dc_col = _attn_bwd(sv["qkv"], do_b, stats, sv["c_row"], f"attn_bwd_{tag}")
    dc = jnp.pad(jnp.transpose(dc_row[:, 0, :] + dc_col[:, :, 0]), ((0, 0), (0, LANES - N_HEADS)))
    df, db = _forget_bwd(sv["f"], small["b_forget"], dc, f"forget_bwd_{tag}")
    gs["b_forget"] = db[0, :N_HEADS]
    gw["w_conv_out"] = _mm(sv["u3"], dyc, "tn", BF16, f"conv_out_dw_{tag}")
    du3 = _mm(dyc, w["w_conv_out"], "nt", F32, f"conv_out_dx_{tag}")
    du1, gs["conv_ln_g"], gs["conv_ln_b"], gs["conv_b"] = _conv_norm_bwd(
        sv["u1"], du3, small["conv_ln_g"], small["conv_ln_b"], f"conv_norm_bwd_{tag}")
    dag, gs["conv_w"] = _conv_bwd(sv["ag"], du1, w["conv_w"], f"conv_bwd_{tag}")
    dqkv = jnp.concatenate([dq, dk, dv], axis=1)
    g_ag = _mm(sv["h"], dag, "tn", BF16, f"proj_glu_dw_{tag}")
    g_qkv = _mm(sv["h"], dqkv, "tn", BF16, f"proj_qkv_dw_{tag}")
    g_f = _mm(sv["h"], df, "tn", BF16, f"proj_forget_dw_{tag}")
    g_gg = _mm(sv["h"], dgg, "tn", BF16, f"proj_gates_dw_{tag}")
    gw["w_in"] = jnp.concatenate([g_ag, g_qkv, g_f[:, :N_HEADS], g_gg], axis=1)
    dh = _mm(df, w["w_f"], "nt", F32, f"proj_forget_dx_{tag}")
    dh = _mm(dag, w["w_ag"], "nt", F32, f"proj_glu_dx_{tag}", residual=dh)
    dh = _mm(dqkv, w["w_qkv"], "nt", F32, f"proj_qkv_dx_{tag}", residual=dh)
    dh = _mm(dgg, w["w_gg"], "nt", F32, f"proj_gates_dx_{tag}", residual=dh)
    dx, _, gs["norm_mix_g"] = _rms_bwd(sv["x"], small["norm_mix_g"], dh, dx1, f"rms_mix_bwd_{tag}")
    for k in ("norm_ple_g", "norm_ffn_g", "norm_mix_g", "conv_ln_g", "conv_ln_b", "conv_b"):
        gs[k] = gs[k][0]
    return dx, gw, gs


_COL_SHARDED = ("w_in", "w_conv_out", "w_attn_out", "w_gate_up", "w_ple_proj")
_ROW_SHARDED = ("w_out", "w_down", "w_ple_gate")
_BIG = _COL_SHARDED + _ROW_SHARDED
_SMALL = ("norm_mix_g", "b_forget", "conv_b", "conv_ln_g", "conv_ln_b", "norm_ffn_g", "norm_ple_g")
_ORDER = ("norm_mix_g", "w_in", "b_forget", "conv_w", "conv_b", "conv_ln_g", "conv_ln_b", "w_conv_out", "w_attn_out",
          "w_out", "norm_ffn_g", "w_gate_up", "w_down", "norm_ple_g", "w_ple_gate", "w_ple_proj", "final_g")


def _pack(vectors, rows):
    flat = jnp.concatenate([v.reshape(-1) for v in vectors])
    return jnp.pad(flat, (0, rows * LANES - flat.shape[0])).reshape(1, rows, LANES)


def _unpack(packed, like):
    flat, out, at = packed.reshape(-1), [], 0
    for v in like:
        out.append(flat[at:at + v.size].reshape(v.shape))
        at += v.size
    return out


def kernel(x, p, norm_mix_g, w_in, b_forget, conv_w, conv_b, conv_ln_g, conv_ln_b, w_conv_out, w_attn_out, w_out, norm_ffn_g, w_gate_up, w_down, norm_ple_g, w_ple_gate, w_ple_proj, final_g, loss_target, m_norm_mix_g, m_w_in, m_b_forget, m_conv_w, m_conv_b, m_conv_ln_g, m_conv_ln_b, m_w_conv_out, m_w_attn_out, m_w_out, m_norm_ffn_g, m_w_gate_up, m_w_down, m_norm_ple_g, m_w_ple_gate, m_w_ple_proj, m_final_g, v_norm_mix_g, v_w_in, v_b_forget, v_conv_w, v_conv_b, v_conv_ln_g, v_conv_ln_b, v_w_conv_out, v_w_attn_out, v_w_out, v_norm_ffn_g, v_w_gate_up, v_w_down, v_norm_ple_g, v_w_ple_gate, v_w_ple_proj, v_final_g):
    weights = dict(norm_mix_g=norm_mix_g, w_in=w_in, b_forget=b_forget, conv_w=conv_w, conv_b=conv_b, conv_ln_g=conv_ln_g, conv_ln_b=conv_ln_b, w_conv_out=w_conv_out, w_attn_out=w_attn_out, w_out=w_out, norm_ffn_g=norm_ffn_g, w_gate_up=w_gate_up, w_down=w_down, norm_ple_g=norm_ple_g, w_ple_gate=w_ple_gate, w_ple_proj=w_ple_proj, final_g=final_g)
    mom1 = dict(norm_mix_g=m_norm_mix_g, w_in=m_w_in, b_forget=m_b_forget, conv_w=m_conv_w, conv_b=m_conv_b, conv_ln_g=m_conv_ln_g, conv_ln_b=m_conv_ln_b, w_conv_out=m_w_conv_out, w_attn_out=m_w_attn_out, w_out=m_w_out, norm_ffn_g=m_norm_ffn_g, w_gate_up=m_w_gate_up, w_down=m_w_down, norm_ple_g=m_norm_ple_g, w_ple_gate=m_w_ple_gate, w_ple_proj=m_w_ple_proj, final_g=m_final_g)
    mom2 = dict(norm_mix_g=v_norm_mix_g, w_in=v_w_in, b_forget=v_b_forget, conv_w=v_conv_w, conv_b=v_conv_b, conv_ln_g=v_conv_ln_g, conv_ln_b=v_conv_ln_b, w_conv_out=v_w_conv_out, w_attn_out=v_w_attn_out, w_out=v_w_out, norm_ffn_g=v_norm_ffn_g, w_gate_up=v_w_gate_up, w_down=v_w_down, norm_ple_g=v_norm_ple_g, w_ple_gate=v_w_ple_gate, w_ple_proj=v_w_ple_proj, final_g=v_final_g)
    depth = w_in.shape[0]
    d_model = x.shape[-1]
    xi, yi, ci = lax.axis_index("x"), lax.axis_index("y"), lax.axis_index("c")
    me = 4 * xi + 2 * yi + ci

    names = list(_BIG) + ["conv_w"]
    shards = [weights[k].astype(BF16) for k in _BIG] + [conv_w]
    gathered = dict(zip(names, _all_gather(shards, "gather_weights")))

    act = x[0]
    layers = []
    for l in range(depth):
        w = _layer_weights(gathered, l, d_model)
        small = {k: weights[k][l][None, :] for k in _SMALL if k != "b_forget"}
        small["b_forget"] = _pad_lanes(b_forget[l])
        act, saved = _layer_fwd(act, p[l, 0], w, small, f"l{l}")
        layers.append((w, small, saved))
    dact, g_final, loss_part = _loss_head(act, final_g[None, :], loss_target[0], "loss_head")
    loss = lax.psum(loss_part[0, 0], ("x", "y", "c"))

    big_grads = {k: [None] * depth for k in _BIG}
    small_grads = {k: [None] * depth for k in _SMALL + ("conv_w",)}
    for l in reversed(range(depth)):
        w, small, saved = layers[l]
        dact, gw, gs = _layer_bwd(dact, saved, w, small, f"l{l}")
        for k in _BIG:
            big_grads[k][l] = gw[k]
        for k in small_grads:
            small_grads[k][l] = gs[k]
    grad_x = dact[None]

    def to_blocks(k):
        per_layer = [_shard_cols(g) if k in _COL_SHARDED else g.reshape((N_DEV, -1) + g.shape[1:])
                     for g in big_grads[k]]
        return jnp.stack(per_layer, axis=1).astype(BF16)

    blocks = [to_blocks(k) for k in _BIG]
    theirs = _pair_exchange(blocks, "grads_pair_exchange")
    core = ci.reshape(1).astype(jnp.int32)
    chip_parts = [_pair_sum(a, b, core, f"grads_pair_sum_{k}") for k, a, b in zip(_BIG, blocks, theirs)]
    arrived = _chip_exchange(chip_parts, "grads_chip_exchange")
    out = {}
    for k, parts in zip(_BIG, arrived):
        out[k] = _sum_adamw(parts, weights[k], mom1[k], mom2[k], f"adamw_{k}")

    small_names = list(_SMALL) + ["conv_w", "final_g"]
    small_list = [jnp.stack(small_grads[k]) for k in _SMALL + ("conv_w",)] + [g_final[0]]
    n_small = sum(v.size for v in small_list)
    rows = -(-n_small // (8 * LANES)) * 8
    all_parts = _all_gather([_pack(small_list, rows)], "gather_small_grads")[0]
    full_conv_w = jnp.transpose(gathered["conv_w"], (1, 2, 0, 3)).reshape(depth, CONV_K, -1)
    full_conv_m = jnp.zeros_like(full_conv_w)
    ch_shard = conv_w.shape[-1]
    at = (0, 0, me * ch_shard)
    small_w = [weights[k] for k in _SMALL] + [full_conv_w, final_g]
    small_m = [mom1[k] for k in _SMALL] + [lax.dynamic_update_slice(full_conv_m, mom1["conv_w"], at), mom1["final_g"]]
    small_v = [mom2[k] for k in _SMALL] + [lax.dynamic_update_slice(full_conv_m + 1.0, mom2["conv_w"], at), mom2["final_g"]]
    res = _sum_adamw(all_parts, _pack(small_w, rows), _pack(small_m, rows), _pack(small_v, rows), "adamw_small")
    unpacked = [_unpack(r, small_list) for r in res]
    for idx, k in enumerate(small_names):
        vals = [u[idx] for u in unpacked]
        if k == "conv_w":
            vals = [lax.dynamic_slice(v, at, conv_w.shape) for v in vals]
        out[k] = vals

    return (loss, grad_x, *[out[k][0] for k in _ORDER], *[out[k][1] for k in _ORDER],
            *[out[k][2] for k in _ORDER], *[out[k][3] for k in _ORDER])
```

```python
import functools
import math

import jax
import jax.numpy as jnp
from jax import lax
from jax.experimental import pallas as pl
from jax.experimental.pallas import tpu as pltpu
from jax.experimental.pallas import tpu_sc as plsc

F32 = jnp.float32
BF16 = jnp.bfloat16
MESH = pl.DeviceIdType.MESH
ANY = pl.BlockSpec(memory_space=pl.ANY)

N_DEV = 8
N_HEADS = 8
HEAD_DIM = 128
CONV_K = 31
HALO = 32
EPS = 1e-6
NEG_INF = -1e30
ADAM_LR = 0.001
ADAM_B1 = 0.9
ADAM_B2 = 0.999
ADAM_EPS = 1e-08
ADAM_WD = 0.01
ADAM_STEP = 10

V7X_VMEM_BYTES = 64 * 1024 * 1024
VMEM_LIMIT = V7X_VMEM_BYTES * 3 // 4
LANES = 128
ROW_TILE = 256
ATTN_TILE = 512
ATTN_ROWS = 256
CUM_TILE = 256


def _params(*sem):
    return pltpu.CompilerParams(dimension_semantics=sem, vmem_limit_bytes=VMEM_LIMIT)


def _tile(dim, pref, mult=LANES):
    t = min(pref, dim)
    t -= t % mult
    while t >= mult:
        if dim % t == 0:
            return t
        t -= mult
    return dim


def _sigmoid(x):
    return 1.0 / (1.0 + jnp.exp(-x))


def _place():
    x, y, c = lax.axis_index("x"), lax.axis_index("y"), lax.axis_index("c")
    chips = [(1 - x, y), (x, 1 - y), (1 - x, 1 - y)]
    return x, y, c, chips


def _all_gather(shards, name):
    n = len(shards)

    def body(*refs):
        ins, outs = refs[:n], refs[n:2 * n]
        send_sems, recv_sems, local_sems = refs[2 * n:]
        x, y, c, chips = _place()
        me, sibling = (x, y, c), (x, y, 1 - c)

        def idx(px, py, pc):
            return 4 * px + 2 * py + pc

        def copy(t, k, block, to, src=None):
            rows = outs[t].at[idx(*block)]
            return pltpu.make_async_remote_copy(
                src_ref=rows if src is None else src, dst_ref=rows,
                send_sem=send_sems.at[t, k], recv_sem=recv_sems.at[t, k],
                device_id=to, device_id_type=MESH)

        local = [pltpu.make_async_copy(ins[t], outs[t].at[idx(*me)], local_sems.at[t]) for t in range(n)]
        for cp in local:
            cp.start()
        first = []
        for t in range(n):
            first.append(copy(t, 0, me, sibling, src=ins[t]))
            first += [copy(t, 1 + j, me, (*chip, c), src=ins[t]) for j, chip in enumerate(chips)]
        for cp in first:
            cp.start()
        passed = []
        for j, chip in enumerate(chips):
            for t in range(n):
                copy(t, 1 + j, (*chip, c), me).wait_recv()
                cp = copy(t, 4 + j, (*chip, c), sibling)
                cp.start()
                passed.append(cp)
        for t in range(n):
            copy(t, 0, sibling, me).wait_recv()
        for j, chip in enumerate(chips):
            for t in range(n):
                copy(t, 4 + j, (*chip, 1 - c), me).wait_recv()
        for cp in first + passed:
            cp.wait_send()
        for cp in local:
            cp.wait()

    return pl.pallas_call(
        body, name=name,
        out_shape=[jax.ShapeDtypeStruct((N_DEV,) + s.shape, s.dtype) for s in shards],
        in_specs=[ANY] * n, out_specs=[ANY] * n,
        scratch_shapes=[pltpu.SemaphoreType.DMA((n, 7)), pltpu.SemaphoreType.DMA((n, 7)),
                        pltpu.SemaphoreType.DMA((n,))],
    )(*shards)


def _all_gather_beside(shards, name, collective_id):
    n = len(shards)
    src = [jax.new_ref(s, memory_space=pltpu.MemorySpace.HBM) for s in shards]
    dst = [jax.empty_ref(jax.ShapeDtypeStruct((N_DEV,) + s.shape, s.dtype), memory_space=pltpu.MemorySpace.HBM)
           for s in shards]

    @pl.kernel(mesh=plsc.ScalarSubcoreMesh(axis_name="sequencer", num_cores=1), name=name,
               scratch_types=(pltpu.SemaphoreType.DMA((n, 7)), pltpu.SemaphoreType.DMA((n, 7)),
                              pltpu.SemaphoreType.DMA((n,))),
               compiler_params=pltpu.CompilerParams(collective_id=collective_id))
    def launch(send_sems, recv_sems, local_sems):
        x, y, c, chips = _place()
        me, sibling = (x, y, c), (x, y, 1 - c)
        barrier = pltpu.get_barrier_semaphore()
        for peer in [sibling] + [(*chip, pc) for chip in chips for pc in (c, 1 - c)]:
            pl.semaphore_signal(barrier, inc=1, device_id=peer, device_id_type=MESH)
        pl.semaphore_wait(barrier, N_DEV - 1)

        def idx(px, py, pc):
            return 4 * px + 2 * py + pc

        def copy(t, k, block, to, own=False):
            rows = dst[t].at[idx(*block)]
            return pltpu.make_async_remote_copy(
                src_ref=src[t] if own else rows, dst_ref=rows,
                send_sem=send_sems.at[t, k], recv_sem=recv_sems.at[t, k],
                device_id=to, device_id_type=MESH)

        local = [pltpu.make_async_copy(src[t], dst[t].at[idx(*me)], local_sems.at[t]) for t in range(n)]
        for cp in local:
            cp.start()
        first = []
        for t in range(n):
            first.append(copy(t, 0, me, sibling, own=True))
            first += [copy(t, 1 + j, me, (*chip, c), own=True) for j, chip in enumerate(chips)]
        for cp in first:
            cp.start()
        passed = []
        for j, chip in enumerate(chips):
            for t in range(n):
                copy(t, 1 + j, (*chip, c), me).wait_recv()
                cp = copy(t, 4 + j, (*chip, c), sibling)
                cp.start()
                passed.append(cp)
        for t in range(n):
            copy(t, 0, sibling, me).wait_recv()
        for j, chip in enumerate(chips):
            for t in range(n):
                copy(t, 4 + j, (*chip, 1 - c), me).wait_recv()
        for cp in first + passed:
            cp.wait_send()
        for cp in local:
            cp.wait()

    launch()
    return [d[...] for d in dst]


def _pair_exchange(grads, name):
    n = len(grads)

    def body(*refs):
        ins, theirs = refs[:n], refs[n:2 * n]
        send_sems, recv_sems = refs[2 * n:]
        x, y, c, _ = _place()
        copies = []
        for t in range(n):
            for q in range(4):
                give = pltpu.make_async_remote_copy(
                    src_ref=ins[t].at[2 * q + 1 - c], dst_ref=theirs[t].at[q],
                    send_sem=send_sems.at[t, q], recv_sem=recv_sems.at[t, q],
                    device_id=(x, y, 1 - c), device_id_type=MESH)
                give.start()
                copies.append(give)
        for cp in copies:
            cp.wait()

    return pl.pallas_call(
        body, name=name, out_shape=[jax.ShapeDtypeStruct((4,) + g.shape[1:], g.dtype) for g in grads],
        in_specs=[ANY] * n, out_specs=[ANY] * n,
        scratch_shapes=[pltpu.SemaphoreType.DMA((n, 4)), pltpu.SemaphoreType.DMA((n, 4))],
    )(*grads)


def _chip_exchange(parts, name):
    n = len(parts)

    def body(*refs):
        ins, outs = refs[:n], refs[n:2 * n]
        send_sems, recv_sems, local_sems = refs[2 * n:]
        x, y, c, chips = _place()
        copies = []
        for t in range(n):
            own = pltpu.make_async_copy(ins[t].at[2 * x + y], outs[t].at[0], local_sems.at[t])
            own.start()
            copies.append(own)
            for j, (px, py) in enumerate(chips):
                cp = pltpu.make_async_remote_copy(
                    src_ref=ins[t].at[2 * px + py], dst_ref=outs[t].at[1 + j],
                    send_sem=send_sems.at[t, j], recv_sem=recv_sems.at[t, j],
                    device_id=(px, py, c), device_id_type=MESH)
                cp.start()
                copies.append(cp)
        for cp in copies:
            cp.wait()

    return pl.pallas_call(
        body, name=name, out_shape=[jax.ShapeDtypeStruct(p.shape, p.dtype) for p in parts],
        in_specs=[ANY] * n, out_specs=[ANY] * n,
        scratch_shapes=[pltpu.SemaphoreType.DMA((n, 3)), pltpu.SemaphoreType.DMA((n, 3)),
                        pltpu.SemaphoreType.DMA((n,))],
    )(*parts)


def _chip_exchange_beside(parts, name, collective_id):
    n = len(parts)
    src = [jax.new_ref(p, memory_space=pltpu.MemorySpace.HBM) for p in parts]
    dst = [jax.empty_ref(jax.ShapeDtypeStruct(p.shape, p.dtype), memory_space=pltpu.MemorySpace.HBM) for p in parts]

    @pl.kernel(mesh=plsc.ScalarSubcoreMesh(axis_name="sequencer", num_cores=1), name=name,
               scratch_types=(pltpu.SemaphoreType.DMA((n, 3)), pltpu.SemaphoreType.DMA((n, 3)),
                              pltpu.SemaphoreType.DMA((n,))),
               compiler_params=pltpu.CompilerParams(collective_id=collective_id))
    def launch(send_sems, recv_sems, local_sems):
        x, y, c, chips = _place()
        barrier = pltpu.get_barrier_semaphore()
        for px, py in chips:
            pl.semaphore_signal(barrier, inc=1, device_id=(px, py, c), device_id_type=MESH)
        pl.semaphore_wait(barrier, len(chips))
        copies = []
        for t in range(n):
            own = pltpu.make_async_copy(src[t].at[2 * x + y], dst[t].at[0], local_sems.at[t])
            own.start()
            copies.append(own)
            for j, (px, py) in enumerate(chips):
                cp = pltpu.make_async_remote_copy(
                    src_ref=src[t].at[2 * px + py], dst_ref=dst[t].at[1 + j],
                    send_sem=send_sems.at[t, j], recv_sem=recv_sems.at[t, j],
                    device_id=(px, py, c), device_id_type=MESH)
                cp.start()
                copies.append(cp)
        for cp in copies:
            cp.wait()

    launch()
    return [d[...] for d in dst]


_DOT_DIMS = {"nn": (((1,), (0,)), ((), ())), "nt": (((1,), (1,)), ((), ())), "tn": (((0,), (0,)), ((), ()))}


def _mm(a, b, mode, out_dtype, name, residual=None, tm=1024, tn=1024, tk=2048):
    if mode == "tn":
        (k_dim, m_dim), n_dim = a.shape, b.shape[1]
    else:
        (m_dim, k_dim), n_dim = a.shape, (b.shape[1] if mode == "nn" else b.shape[0])
    tm, tn, tk = _tile(m_dim, tm), _tile(n_dim, tn), _tile(k_dim, tk)
    nk = k_dim // tk

    def body(*refs):
        a_ref, b_ref = refs[:2]
        r_ref = refs[2] if residual is not None else None
        o_ref = refs[2 if residual is None else 3]

        def finish(r):
            if r_ref is not None:
                r = r + r_ref[...].astype(F32)
            o_ref[...] = r.astype(o_ref.dtype)

        part = lax.dot_general(a_ref[...].astype(BF16), b_ref[...].astype(BF16), _DOT_DIMS[mode],
                               preferred_element_type=F32)
        if nk == 1:
            finish(part)
            return
        acc, k = refs[-1], pl.program_id(2)

        @pl.when(k == 0)
        def _():
            acc[...] = part

        @pl.when((k > 0) & (k < nk - 1))
        def _():
            acc[...] += part

        @pl.when(k == nk - 1)
        def _():
            finish(acc[...] + part)

    a_spec = pl.BlockSpec((tk, tm), lambda i, j, k: (k, i)) if mode == "tn" else pl.BlockSpec((tm, tk), lambda i, j, k: (i, k))
    b_spec = pl.BlockSpec((tn, tk), lambda i, j, k: (j, k)) if mode == "nt" else pl.BlockSpec((tk, tn), lambda i, j, k: (k, j))
    o_spec = pl.BlockSpec((tm, tn), lambda i, j, k: (i, j))
    ins, specs = [a, b], [a_spec, b_spec]
    if residual is not None:
        ins.append(residual)
        specs.append(o_spec)
    return pl.pallas_call(
        body, name=name, grid=(m_dim // tm, n_dim // tn, nk),
        out_shape=jax.ShapeDtypeStruct((m_dim, n_dim), out_dtype),
        in_specs=specs, out_specs=o_spec,
        scratch_shapes=[pltpu.VMEM((tm, tn), F32)] if nk > 1 else [],
        compiler_params=_params("parallel", "parallel", "arbitrary"),
    )(*ins)


def _rows(body, name, row_ins, full_ins, row_outs, acc_outs, ts=ROW_TILE):
    s = row_ins[0].shape[0]
    ts = _tile(s, ts, 16)
    n_ri, n_fi, n_ro = len(row_ins), len(full_ins), len(row_outs)

    def kernel(*refs):
        i = pl.program_id(0)
        ri, fi = refs[:n_ri], refs[n_ri:n_ri + n_fi]
        ro, ao = refs[n_ri + n_fi:n_ri + n_fi + n_ro], refs[n_ri + n_fi + n_ro:]

        if ao:
            @pl.when(i == 0)
            def _():
                for r in ao:
                    r[...] = jnp.zeros_like(r)

        body(i, ri, fi, ro, ao)

    def row_spec(cols):
        return pl.BlockSpec((ts, cols), lambda i: (i, 0))

    def full_spec(shape):
        return pl.BlockSpec(shape, lambda i: (0,) * len(shape))

    return pl.pallas_call(
        kernel, name=name, grid=(s // ts,),
        out_shape=[jax.ShapeDtypeStruct((s, c), d) for c, d in row_outs]
        + [jax.ShapeDtypeStruct(sh, d) for sh, d in acc_outs],
        in_specs=[row_spec(a.shape[1]) for a in row_ins] + [full_spec(a.shape) for a in full_ins],
        out_specs=[row_spec(c) for c, _ in row_outs] + [full_spec(sh) for sh, _ in acc_outs],
        compiler_params=_params("arbitrary"),
    )(*row_ins, *full_ins)


def _rms_stats(x):
    return lax.rsqrt(jnp.mean(x * x, axis=-1, keepdims=True) + EPS)


def _rms_fwd(x, g, name):
    def body(i, ri, fi, ro, ao):
        xv = ri[0][...]
        ro[0][...] = (xv * _rms_stats(xv) * fi[0][...]).astype(BF16)

    return _rows(body, name, [x], [g], [(x.shape[1], BF16)], [])[0]


def _rms_bwd_rows(xv, gv, dh):
    r = _rms_stats(xv)
    xhat = xv * r
    dxhat = dh * gv
    dx = r * (dxhat - xhat * jnp.mean(dxhat * xhat, axis=-1, keepdims=True))
    return dx, jnp.sum(dh * xhat, axis=0, keepdims=True)


def _rms_bwd(x, g, dh, dres, name):
    d = x.shape[1]

    def body(i, ri, fi, ro, ao):
        dx, dg = _rms_bwd_rows(ri[0][...], fi[0][...], ri[1][...])
        dx = dx + ri[2][...]
        ro[0][...] = dx
        ro[1][...] = dx.astype(BF16)
        ao[0][...] += dg

    return _rows(body, name, [x, dh, dres], [g], [(d, F32), (d, BF16)], [((1, d), F32)])


def _loss_head(x, g, target, name):
    d = x.shape[1]

    def body(i, ri, fi, ro, ao):
        xv, gv = ri[0][...], fi[0][...]
        y = xv * _rms_stats(xv) * gv
        err = y - ri[1][...]
        ao[1][...] += 0.5 * jnp.sum(jnp.mean(err * err, axis=-1, keepdims=True))
        dx, dg = _rms_bwd_rows(xv, gv, err * (1.0 / d))
        ro[0][...] = dx
        ao[0][...] += dg

    return _rows(body, name, [x, target], [g], [(d, F32)], [((1, d), F32), ((8, LANES), F32)])


def _merge_fwd(gg, y_conv, y_attn, name):
    d = y_conv.shape[1]

    def body(i, ri, fi, ro, ao):
        gv = ri[0][...]
        m = _sigmoid(gv[:, :d]) * ri[1][...] + _sigmoid(gv[:, d:]) * ri[2][...]
        ro[0][...] = m.astype(BF16)

    return _rows(body, name, [gg, y_conv, y_attn], [], [(d, BF16)], [])[0]


def _merge_bwd(gg, y_conv, y_attn, dm, name):
    d = y_conv.shape[1]

    def body(i, ri, fi, ro, ao):
        gv, yc, ya, dmv = ri[0][...], ri[1][...], ri[2][...], ri[3][...]
        sc, sa = _sigmoid(gv[:, :d]), _sigmoid(gv[:, d:])
        ro[0][...] = (dmv * sc).astype(BF16)
        ro[1][...] = (dmv * sa).astype(BF16)
        ro[2][:, :d] = (dmv * yc * sc * (1.0 - sc)).astype(BF16)
        ro[2][:, d:] = (dmv * ya * sa * (1.0 - sa)).astype(BF16)

    return _rows(body, name, [gg, y_conv, y_attn, dm], [], [(d, BF16), (d, BF16), (2 * d, BF16)], [])


def _swiglu_fwd(gu, name):
    f = gu.shape[1] // 2

    def body(i, ri, fi, ro, ao):
        v = ri[0][...]
        gate, up = v[:, :f], v[:, f:]
        ro[0][...] = (gate * _sigmoid(gate) * up).astype(BF16)

    return _rows(body, name, [gu], [], [(f, BF16)], [], ts=128)[0]


def _swiglu_bwd(gu, dact, name):
    f = gu.shape[1] // 2

    def body(i, ri, fi, ro, ao):
        v, dv = ri[0][...], ri[1][...]
        gate, up = v[:, :f], v[:, f:]
        sg = _sigmoid(gate)
        ro[0][:, :f] = (dv * up * sg * (1.0 + gate * (1.0 - sg))).astype(BF16)
        ro[0][:, f:] = (dv * gate * sg).astype(BF16)

    return _rows(body, name, [gu, dact], [], [(2 * f, BF16)], [], ts=128)[0]


def _ple_fwd(x, z, pp, name):
    d = x.shape[1]

    def body(i, ri, fi, ro, ao):
        ro[0][...] = ri[0][...] + _sigmoid(ri[1][...]) * ri[2][...]

    return _rows(body, name, [x, z, pp], [], [(d, F32)], [])[0]


def _ple_bwd(dx, z, pp, name):
    d = dx.shape[1]

    def body(i, ri, fi, ro, ao):
        dv, sz = ri[0][...], _sigmoid(ri[1][...])
        ro[0][...] = (dv * ri[2][...] * sz * (1.0 - sz)).astype(BF16)
        ro[1][...] = (dv * sz).astype(BF16)

    return _rows(body, name, [dx, z, pp], [], [(d, BF16), (d, BF16)], [])


def _triangle(n, lower):
    r = lax.broadcasted_iota(jnp.int32, (n, n), 0)
    c = lax.broadcasted_iota(jnp.int32, (n, n), 1)
    return jnp.where((c <= r) if lower else (c >= r), 1.0, 0.0).astype(F32)


def _forget_fwd(f, bias, name):
    s = f.shape[0]
    tb = _tile(s, CUM_TILE, 8)

    def body(f_ref, b_ref, c_ref, carry):
        @pl.when(pl.program_id(0) == 0)
        def _():
            carry[...] = jnp.zeros_like(carry)

        xv = f_ref[...] + b_ref[...]
        logf = jnp.minimum(xv, 0.0) - jnp.log(1.0 + jnp.exp(-jnp.abs(xv)))
        cv = jnp.dot(_triangle(tb, True), logf, precision=lax.Precision.HIGHEST, preferred_element_type=F32) + carry[...]
        c_ref[...] = cv
        carry[...] = cv[tb - 1:tb, :]

    return pl.pallas_call(
        body, name=name, grid=(s // tb,), out_shape=jax.ShapeDtypeStruct((s, LANES), F32),
        in_specs=[pl.BlockSpec((tb, LANES), lambda i: (i, 0)), pl.BlockSpec((1, LANES), lambda i: (0, 0))],
        out_specs=pl.BlockSpec((tb, LANES), lambda i: (i, 0)),
        scratch_shapes=[pltpu.VMEM((1, LANES), F32)],
        compiler_params=_params("arbitrary"),
    )(f, bias)


def _forget_bwd(f, bias, dc, name):
    s = f.shape[0]
    tb = _tile(s, CUM_TILE, 16)
    nb = s // tb

    def body(f_ref, b_ref, dc_ref, df_ref, db_ref, carry):
        @pl.when(pl.program_id(0) == 0)
        def _():
            carry[...] = jnp.zeros_like(carry)
            db_ref[...] = jnp.zeros_like(db_ref)

        dlog = jnp.dot(_triangle(tb, False), dc_ref[...], precision=lax.Precision.HIGHEST,
                       preferred_element_type=F32) + carry[...]
        carry[...] = dlog[0:1, :]
        dlogit = dlog * (1.0 - _sigmoid(f_ref[...] + b_ref[...]))
        df_ref[...] = dlogit.astype(BF16)
        db_ref[...] += jnp.sum(dlogit, axis=0, keepdims=True)

    back = lambda i: (nb - 1 - i, 0)
    return pl.pallas_call(
        body, name=name, grid=(nb,),
        out_shape=[jax.ShapeDtypeStruct((s, LANES), BF16), jax.ShapeDtypeStruct((1, LANES), F32)],
        in_specs=[pl.BlockSpec((tb, LANES), back), pl.BlockSpec((1, LANES), lambda i: (0, 0)),
                  pl.BlockSpec((tb, LANES), back)],
        out_specs=[pl.BlockSpec((tb, LANES), back), pl.BlockSpec((1, LANES), lambda i: (0, 0))],
        scratch_shapes=[pltpu.VMEM((1, LANES), F32)],
        compiler_params=_params("arbitrary"),
    )(f, bias, dc)


def _glu(block, ch):
    return block[:, :ch] * _sigmoid(block[:, ch:])


def _layernorm_rows(u1, g, b):
    mu = jnp.mean(u1, axis=-1, keepdims=True)
    cen = u1 - mu
    rstd = lax.rsqrt(jnp.mean(cen * cen, axis=-1, keepdims=True) + EPS)
    xhat = cen * rstd
    return xhat, rstd, xhat * g + b


def _conv_fwd(ag, conv_w, conv_b, ln_g, ln_b, name):
    s, ch = ag.shape[0], ag.shape[1] // 2
    ts = _tile(s, ROW_TILE, HALO)
    per = ts // HALO

    def body(cur_ref, halo_ref, w_ref, cb_ref, g_ref, b_ref, u1_ref, u3_ref, cat):
        i = pl.program_id(0)
        cat[HALO:, :] = _glu(cur_ref[...], ch)
        cat[:HALO, :] = jnp.where(i == 0, 0.0, _glu(halo_ref[...], ch))
        full = cat[...]
        acc = jnp.zeros((ts, ch), F32) + cb_ref[...]
        for j in range(CONV_K):
            shifted = full if j == CONV_K - 1 else pltpu.roll(full, CONV_K - 1 - j, 0)
            acc = acc + w_ref[j:j + 1, :] * shifted[HALO:, :]
        u1_ref[...] = acc
        _, _, u2 = _layernorm_rows(acc, g_ref[...], b_ref[...])
        u3_ref[...] = (u2 * _sigmoid(u2)).astype(BF16)

    whole = lambda shape: pl.BlockSpec(shape, lambda i: (0, 0))
    return pl.pallas_call(
        body, name=name, grid=(s // ts,),
        out_shape=[jax.ShapeDtypeStruct((s, ch), F32), jax.ShapeDtypeStruct((s, ch), BF16)],
        in_specs=[pl.BlockSpec((ts, 2 * ch), lambda i: (i, 0)),
                  pl.BlockSpec((HALO, 2 * ch), lambda i: (jnp.maximum(i * per - 1, 0), 0)),
                  whole(conv_w.shape), whole((1, ch)), whole((1, ch)), whole((1, ch))],
        out_specs=[pl.BlockSpec((ts, ch), lambda i: (i, 0)), pl.BlockSpec((ts, ch), lambda i: (i, 0))],
        scratch_shapes=[pltpu.VMEM((HALO + ts, ch), F32)],
        compiler_params=_params("arbitrary"),
    )(ag, ag, conv_w, conv_b, ln_g, ln_b)


def _conv_norm_bwd(u1, du3, ln_g, ln_b, name):
    ch = u1.shape[1]

    def body(i, ri, fi, ro, ao):
        g = fi[0][...]
        xhat, rstd, u2 = _layernorm_rows(ri[0][...], g, fi[1][...])
        sg = _sigmoid(u2)
        du2 = ri[1][...] * sg * (1.0 + u2 * (1.0 - sg))
        dxhat = du2 * g
        du1 = rstd * (dxhat - jnp.mean(dxhat, axis=-1, keepdims=True)
                      - xhat * jnp.mean(dxhat * xhat, axis=-1, keepdims=True))
        ro[0][...] = du1
        ao[0][...] += jnp.sum(du2 * xhat, axis=0, keepdims=True)
        ao[1][...] += jnp.sum(du2, axis=0, keepdims=True)
        ao[2][...] += jnp.sum(du1, axis=0, keepdims=True)

    return _rows(body, name, [u1, du3], [ln_g, ln_b], [(ch, F32)], [((1, ch), F32)] * 3)


def _conv_bwd(ag, du1, conv_w, name):
    s, ch = du1.shape
    ts = _tile(s, ROW_TILE, HALO)
    per = ts // HALO
    last = s // HALO - 1
    nb = s // ts

    def body(cur_ref, halo_ref, d_ref, dnext_ref, w_ref, dag_ref, dw_ref, cat, dcat):
        i = pl.program_id(0)

        @pl.when(i == 0)
        def _():
            dw_ref[...] = jnp.zeros_like(dw_ref)

        cur = cur_ref[...]
        cat[HALO:, :] = _glu(cur, ch)
        cat[:HALO, :] = jnp.where(i == 0, 0.0, _glu(halo_ref[...], ch))
        dcur = d_ref[...]
        dcat[:ts, :] = dcur
        dcat[ts:, :] = jnp.where(i == nb - 1, 0.0, dnext_ref[...])
        full, dfull = cat[...], dcat[...]
        du0 = jnp.zeros((ts, ch), F32)
        for j in range(CONV_K):
            back = CONV_K - 1 - j
            dshift = dfull if back == 0 else pltpu.roll(dfull, ts + HALO - back, 0)
            du0 = du0 + w_ref[j:j + 1, :] * dshift[:ts, :]
            ushift = full if back == 0 else pltpu.roll(full, back, 0)
            dw_ref[j:j + 1, :] += jnp.sum(dcur * ushift[HALO:, :], axis=0, keepdims=True)
        a, sg = cur[:, :ch], _sigmoid(cur[:, ch:])
        dag_ref[:, :ch] = (du0 * sg).astype(BF16)
        dag_ref[:, ch:] = (du0 * a * sg * (1.0 - sg)).astype(BF16)

    return pl.pallas_call(
        body, name=name, grid=(nb,),
        out_shape=[jax.ShapeDtypeStruct((s, 2 * ch), BF16), jax.ShapeDtypeStruct(conv_w.shape, F32)],
        in_specs=[pl.BlockSpec((ts, 2 * ch), lambda i: (i, 0)),
                  pl.BlockSpec((HALO, 2 * ch), lambda i: (jnp.maximum(i * per - 1, 0), 0)),
                  pl.BlockSpec((ts, ch), lambda i: (i, 0)),
                  pl.BlockSpec((HALO, ch), lambda i: (jnp.minimum((i + 1) * per, last), 0)),
                  pl.BlockSpec(conv_w.shape, lambda i: (0, 0))],
        out_specs=[pl.BlockSpec((ts, 2 * ch), lambda i: (i, 0)), pl.BlockSpec(conv_w.shape, lambda i: (0, 0))],
        scratch_shapes=[pltpu.VMEM((HALO + ts, ch), F32), pltpu.VMEM((ts + HALO, ch), F32)],
        compiler_params=_params("arbitrary"),
    )(ag, ag, du1, du1, conv_w)


_SCALE = 1.0 / math.sqrt(HEAD_DIM)


STAT_LANES = 8


def _causal(rows, cols, row0):
    r = row0 + lax.broadcasted_iota(jnp.int32, (rows, cols), 0)
    return lax.broadcasted_iota(jnp.int32, (rows, cols), 1) <= r


def _attn_fwd(qkv, c_row, name):
    s = qkv.shape[0]
    t = _tile(s, ATTN_TILE)
    nt, part_rows = s // t, min(ATTN_ROWS, t)
    n_parts = t // part_rows
    h_, d = N_HEADS, HEAD_DIM

    def body(q_ref, k_ref, v_ref, cr_ref, o_ref, lse_ref, *scratch):
        m_sc, l_sc, acc = scratch[:n_parts], scratch[n_parts:2 * n_parts], scratch[2 * n_parts:]
        i = pl.program_id(1)
        for part in range(n_parts):
            m_sc[part][...] = jnp.full_like(m_sc[part], NEG_INF)
            l_sc[part][...] = jnp.zeros_like(l_sc[part])
            acc[part][...] = jnp.zeros_like(acc[part])

        def block(j, diagonal):
            at = pl.multiple_of(j * t, t)
            k, v = k_ref[pl.ds(at, t), :], v_ref[pl.ds(at, t), :]
            cr = cr_ref[0, pl.ds(j, 1), :]
            for part in range(n_parts):
                rows = pl.ds(part * part_rows, part_rows)
                sc = lax.dot_general(q_ref[rows, :], k, _DOT_DIMS["nt"], preferred_element_type=F32) * _SCALE - cr
                if diagonal:
                    sc = jnp.where(_causal(part_rows, t, part * part_rows), sc, NEG_INF)
                m_old = m_sc[part][...]
                m_new = jnp.maximum(m_old, jnp.max(sc, axis=-1, keepdims=True))
                alpha = jnp.exp(m_old - m_new)
                p = jnp.exp(sc - m_new)
                l_sc[part][...] = alpha * l_sc[part][...] + jnp.sum(p, axis=-1, keepdims=True)
                acc[part][...] = alpha * acc[part][...] + jnp.dot(p.astype(BF16), v, preferred_element_type=F32)
                m_sc[part][...] = m_new

        def below(j, carry):
            block(j, False)
            return carry

        lax.fori_loop(0, i, below, 0)
        block(i, True)
        for part in range(n_parts):
            rows = pl.ds(part * part_rows, part_rows)
            o_ref[rows, :] = (acc[part][...] / l_sc[part][...]).astype(BF16)
            lse_ref[0, rows, :] = m_sc[part][...] + jnp.log(l_sc[part][...])

    return pl.pallas_call(
        body, name=name, grid=(h_, nt),
        out_shape=[jax.ShapeDtypeStruct((s, h_ * d), BF16), jax.ShapeDtypeStruct((h_, s, 1), F32)],
        in_specs=[pl.BlockSpec((t, d), lambda h, i: (i, h)),
                  pl.BlockSpec((s, d), lambda h, i: (0, h_ + h)),
                  pl.BlockSpec((s, d), lambda h, i: (0, 2 * h_ + h)),
                  pl.BlockSpec((1, nt, t), lambda h, i: (h, 0, 0))],
        out_specs=[pl.BlockSpec((t, d), lambda h, i: (i, h)), pl.BlockSpec((1, t, 1), lambda h, i: (h, i, 0))],
        scratch_shapes=[pltpu.VMEM((part_rows, 1), F32)] * (2 * n_parts) + [pltpu.VMEM((part_rows, d), F32)] * n_parts,
        compiler_params=_params("parallel", "arbitrary"),
    )(qkv, qkv, qkv, c_row.reshape(h_, nt, t))


def _attn_stats(o, do, lse, name):
    s = o.shape[0]
    t = _tile(s, ATTN_TILE)
    h_, d = N_HEADS, HEAD_DIM

    def body(o_ref, do_ref, lse_ref, st_ref, dob_ref):
        dov = do_ref[...]
        st_ref[...] = jnp.zeros_like(st_ref)
        st_ref[0, :, 0:1] = -lse_ref[0]
        st_ref[0, :, 1:2] = jnp.sum(o_ref[...].astype(F32) * dov, axis=-1, keepdims=True)
        dob_ref[...] = dov.astype(BF16)

    col = pl.BlockSpec((1, t, 1), lambda h, i: (h, i, 0))
    blk = pl.BlockSpec((t, d), lambda h, i: (i, h))
    return pl.pallas_call(
        body, name=name, grid=(h_, s // t),
        out_shape=[jax.ShapeDtypeStruct((h_, s, STAT_LANES), F32), jax.ShapeDtypeStruct(o.shape, BF16)],
        in_specs=[blk, blk, col],
        out_specs=[pl.BlockSpec((1, t, STAT_LANES), lambda h, i: (h, i, 0)), blk],
        compiler_params=_params("parallel", "parallel"),
    )(o, do, lse)


def _attn_bwd(qkv, do, stats, c_row, name):
    s = qkv.shape[0]
    t = _tile(s, ATTN_TILE)
    nt = s // t
    h_, d = N_HEADS, HEAD_DIM

    def body(q_ref, do_ref, k_ref, v_ref, st_ref, cr_ref, dq_ref, dk_ref, dv_ref, dcr_ref, dcc_ref,
             dq_acc, dk_acc, dv_acc, dcr_acc):
        j = pl.program_id(1)

        @pl.when(j == 0)
        def _():
            dq_acc[...] = jnp.zeros_like(dq_acc)
            dcc_ref[...] = jnp.zeros_like(dcc_ref)

        dk_acc[...] = jnp.zeros_like(dk_acc)
        dv_acc[...] = jnp.zeros_like(dv_acc)
        dcr_acc[...] = jnp.zeros_like(dcr_acc)
        k, v, cr = k_ref[...], v_ref[...], cr_ref[0]

        def block(i, diagonal):
            rows = pl.ds(pl.multiple_of(i * t, t), t)
            q, dov, st = q_ref[rows, :], do_ref[rows, :], st_ref[0, rows, :]
            sc = lax.dot_general(q, k, _DOT_DIMS["nt"], preferred_element_type=F32) * _SCALE + (st[:, 0:1] - cr)
            p = jnp.exp(sc)
            if diagonal:
                p = jnp.where(_causal(t, t, 0), p, 0.0)
            dp = lax.dot_general(dov, v, _DOT_DIMS["nt"], preferred_element_type=F32)
            ds = p * (dp - st[:, 1:2])
            ds_b = ds.astype(BF16)
            dv_acc[...] += lax.dot_general(p.astype(BF16), dov, _DOT_DIMS["tn"], preferred_element_type=F32)
            dk_acc[...] += lax.dot_general(ds_b, q, _DOT_DIMS["tn"], preferred_element_type=F32)
            dq_acc[rows, :] += jnp.dot(ds_b, k, preferred_element_type=F32)
            dcr_acc[...] -= jnp.sum(ds, axis=0, keepdims=True)
            dcc_ref[0, rows, :] += jnp.sum(ds, axis=-1, keepdims=True)

        def above(i, carry):
            block(i, False)
            return carry

        block(j, True)
        lax.fori_loop(j + 1, nt, above, 0)
        dk_ref[...] = (dk_acc[...] * _SCALE).astype(BF16)
        dv_ref[...] = dv_acc[...].astype(BF16)
        dcr_ref[0] = dcr_acc[...]

        @pl.when(j == nt - 1)
        def _():
            dq_ref[...] = (dq_acc[...] * _SCALE).astype(BF16)

    head = pl.BlockSpec((s, d), lambda h, j: (0, h))
    return pl.pallas_call(
        body, name=name, grid=(h_, nt),
        out_shape=[jax.ShapeDtypeStruct((s, h_ * d), BF16)] * 3
        + [jax.ShapeDtypeStruct((h_, 1, s), F32), jax.ShapeDtypeStruct((h_, s, 1), F32)],
        in_specs=[head, head,
                  pl.BlockSpec((t, d), lambda h, j: (j, h_ + h)),
                  pl.BlockSpec((t, d), lambda h, j: (j, 2 * h_ + h)),
                  pl.BlockSpec((1, s, STAT_LANES), lambda h, j: (h, 0, 0)),
                  pl.BlockSpec((1, 1, t), lambda h, j: (h, 0, j))],
        out_specs=[head, pl.BlockSpec((t, d), lambda h, j: (j, h)), pl.BlockSpec((t, d), lambda h, j: (j, h)),
                   pl.BlockSpec((1, 1, t), lambda h, j: (h, 0, j)), pl.BlockSpec((1, s, 1), lambda h, j: (h, 0, 0))],
        scratch_shapes=[pltpu.VMEM((s, d), F32), pltpu.VMEM((t, d), F32), pltpu.VMEM((t, d), F32),
                        pltpu.VMEM((1, t), F32)],
        compiler_params=_params("parallel", "arbitrary"),
    )(qkv, do, qkv, qkv, stats, c_row)


def _pair_sum(grads, theirs, core, name):
    _, l, r, c = theirs.shape
    tr = _tile(r, 256, 16)

    def body(core_ref, a_ref, b_ref, o_ref):
        o_ref[...] = (a_ref[...].astype(F32) + b_ref[...].astype(F32)).astype(BF16)

    spec = pl.BlockSpec((1, 1, tr, c), lambda q, k, i, core_ref: (q, k, i, 0))
    mine = pl.BlockSpec((1, 1, tr, c), lambda q, k, i, core_ref: (2 * q + core_ref[0], k, i, 0))
    return pl.pallas_call(
        body, name=name, out_shape=jax.ShapeDtypeStruct(theirs.shape, BF16),
        grid_spec=pltpu.PrefetchScalarGridSpec(num_scalar_prefetch=1, grid=(4, l, r // tr),
                                               in_specs=[mine, spec], out_specs=spec),
        compiler_params=_params("parallel", "parallel", "parallel"),
    )(core, grads, theirs)


def _adamw(g, w, m, v):
    m = ADAM_B1 * m + (1.0 - ADAM_B1) * g
    v = ADAM_B2 * v + (1.0 - ADAM_B2) * (g * g)
    m_hat = m / (1.0 - ADAM_B1 ** ADAM_STEP)
    v_hat = v / (1.0 - ADAM_B2 ** ADAM_STEP)
    return -ADAM_LR * (m_hat / (jnp.sqrt(v_hat) + ADAM_EPS) + ADAM_WD * w), m, v


def _sum_adamw(parts, w, m, v, name):
    n_parts, l, r, c = parts.shape
    tr = _tile(r, 128, 16)

    def body(p_ref, w_ref, m_ref, v_ref, g_ref, d_ref, nm_ref, nv_ref):
        g = p_ref[0].astype(F32)
        for k in range(1, n_parts):
            g = g + p_ref[k].astype(F32)
        g_ref[...] = g
        d_ref[...], nm_ref[...], nv_ref[...] = _adamw(g, w_ref[...], m_ref[...], v_ref[...])

    spec = pl.BlockSpec((1, tr, c), lambda k, i: (k, i, 0))
    return pl.pallas_call(
        body, name=name, grid=(l, r // tr), out_shape=[jax.ShapeDtypeStruct(w.shape, F32)] * 4,
        in_specs=[pl.BlockSpec((n_parts, 1, tr, c), lambda k, i: (0, k, i, 0)), spec, spec, spec],
        out_specs=[spec] * 4, compiler_params=_params("parallel", "parallel"),
    )(parts, w, m, v)


def _unshard_cols(gathered):
    return jnp.transpose(gathered, (1, 0, 2)).reshape(gathered.shape[1], -1)


def _shard_cols(full):
    k = full.shape[0]
    return jnp.transpose(full.reshape(k, N_DEV, -1), (1, 0, 2))


def _layer_weights(w, d_model):
    hd = N_HEADS * HEAD_DIM
    w_in = _unshard_cols(w["w_in"])
    f0 = d_model + 3 * hd
    out = {
        "w_ag": w_in[:, :d_model], "w_qkv": w_in[:, d_model:f0],
        "w_f": jnp.pad(w_in[:, f0:f0 + N_HEADS], ((0, 0), (0, LANES - N_HEADS))),
        "w_gg": w_in[:, f0 + N_HEADS:],
        "w_conv_out": _unshard_cols(w["w_conv_out"]), "w_attn_out": _unshard_cols(w["w_attn_out"]),
        "w_out": w["w_out"].reshape(-1, d_model), "w_gate_up": _unshard_cols(w["w_gate_up"]),
        "w_down": w["w_down"].reshape(-1, d_model), "w_ple_gate": w["w_ple_gate"].reshape(-1, d_model),
        "w_ple_proj": _unshard_cols(w["w_ple_proj"]),
        "conv_w": jnp.transpose(w["conv_w"], (1, 0, 2)).reshape(CONV_K, -1),
    }
    return out


def _pad_lanes(vec):
    return jnp.pad(vec, (0, LANES - vec.shape[0]))[None, :]


def _layer_fwd(x, p, w, small, tag):
    h = _rms_fwd(x, small["norm_mix_g"], f"rms_mix_{tag}")
    ag = _mm(h, w["w_ag"], "nn", F32, f"proj_glu_{tag}")
    qkv = _mm(h, w["w_qkv"], "nn", BF16, f"proj_qkv_{tag}")
    gg = _mm(h, w["w_gg"], "nn", F32, f"proj_gates_{tag}")
    f = _mm(h, w["w_f"], "nn", F32, f"proj_forget_{tag}")
    u1, u3 = _conv_fwd(ag, w["conv_w"], small["conv_b"], small["conv_ln_g"], small["conv_ln_b"], f"conv_{tag}")
    y_conv = _mm(u3, w["w_conv_out"], "nn", F32, f"conv_out_{tag}")
    c = _forget_fwd(f, small["b_forget"], f"forget_{tag}")
    c_heads = jnp.transpose(c[:, :N_HEADS])
    c_row = c_heads[:, None, :]
    o, lse = _attn_fwd(qkv, c_row, f"attn_{tag}")
    y_attn = _mm(o, w["w_attn_out"], "nn", F32, f"attn_out_{tag}")
    merged = _merge_fwd(gg, y_conv, y_attn, f"merge_{tag}")
    x1 = _mm(merged, w["w_out"], "nn", F32, f"mix_out_{tag}", residual=x)
    hf = _rms_fwd(x1, small["norm_ffn_g"], f"rms_ffn_{tag}")
    gu = _mm(hf, w["w_gate_up"], "nn", F32, f"ffn_up_{tag}")
    act = _swiglu_fwd(gu, f"swiglu_{tag}")
    x2 = _mm(act, w["w_down"], "nn", F32, f"ffn_down_{tag}", residual=x1)
    hp = _rms_fwd(x2, small["norm_ple_g"], f"rms_ple_{tag}")
    z = _mm(hp, w["w_ple_gate"], "nn", F32, f"ple_gate_{tag}")
    pp = _mm(p, w["w_ple_proj"], "nn", F32, f"ple_proj_{tag}")
    x3 = _ple_fwd(x2, z, pp, f"ple_{tag}")
    saved = dict(x=x, p=p, h=h, ag=ag, qkv=qkv, gg=gg, f=f, u1=u1, u3=u3, y_conv=y_conv, c_row=c_row,
                 o=o, lse=lse, y_attn=y_attn, merged=merged, x1=x1, hf=hf, gu=gu, act=act, x2=x2, hp=hp, z=z, pp=pp)
    return x3, saved


def _layer_bwd(dx3, sv, w, small, tag):
    d_model = dx3.shape[1]
    gw, gs = {}, {}
    dz, dpp = _ple_bwd(dx3, sv["z"], sv["pp"], f"ple_bwd_{tag}")
    gw["w_ple_proj"] = _mm(sv["p"], dpp, "tn", BF16, f"ple_proj_dw_{tag}")
    gw["w_ple_gate"] = _mm(sv["hp"], dz, "tn", BF16, f"ple_gate_dw_{tag}")
    dhp = _mm(dz, w["w_ple_gate"], "nt", F32, f"ple_gate_dx_{tag}")
    dx2, dx2_b, gs["norm_ple_g"] = _rms_bwd(sv["x2"], small["norm_ple_g"], dhp, dx3, f"rms_ple_bwd_{tag}")
    gw["w_down"] = _mm(sv["act"], dx2_b, "tn", BF16, f"ffn_down_dw_{tag}")
    dact = _mm(dx2_b, w["w_down"], "nt", F32, f"ffn_down_dx_{tag}")
    dgu = _swiglu_bwd(sv["gu"], dact, f"swiglu_bwd_{tag}")
    gw["w_gate_up"] = _mm(sv["hf"], dgu, "tn", BF16, f"ffn_up_dw_{tag}")
    dhf = _mm(dgu, w["w_gate_up"], "nt", F32, f"ffn_up_dx_{tag}")
    dx1, dx1_b, gs["norm_ffn_g"] = _rms_bwd(sv["x1"], small["norm_ffn_g"], dhf, dx2, f"rms_ffn_bwd_{tag}")
    gw["w_out"] = _mm(sv["merged"], dx1_b, "tn", BF16, f"mix_out_dw_{tag}")
    dm = _mm(dx1_b, w["w_out"], "nt", F32, f"mix_out_dx_{tag}")
    dyc, dya, dgg = _merge_bwd(sv["gg"], sv["y_conv"], sv["y_attn"], dm, f"merge_bwd_{tag}")
    gw["w_attn_out"] = _mm(sv["o"], dya, "tn", BF16, f"attn_out_dw_{tag}")
    do = _mm(dya, w["w_attn_out"], "nt", F32, f"attn_out_dx_{tag}")
    stats, do_b = _attn_stats(sv["o"], do, sv["lse"], f"attn_stats_{tag}")
    dq, dk, dv, dc_row, dc_col = _attn_bwd(sv["qkv"], do_b, stats, sv["c_row"], f"attn_bwd_{tag}")
    dc = jnp.pad(jnp.transpose(dc_row[:, 0, :] + dc_col[:, :, 0]), ((0, 0), (0, LANES - N_HEADS)))
    df, db = _forget_bwd(sv["f"], small["b_forget"], dc, f"forget_bwd_{tag}")
    gs["b_forget"] = db[0, :N_HEADS]
    gw["w_conv_out"] = _mm(sv["u3"], dyc, "tn", BF16, f"conv_out_dw_{tag}")
    du3 = _mm(dyc, w["w_conv_out"], "nt", F32, f"conv_out_dx_{tag}")
    du1, gs["conv_ln_g"], gs["conv_ln_b"], gs["conv_b"] = _conv_norm_bwd(
        sv["u1"], du3, small["conv_ln_g"], small["conv_ln_b"], f"conv_norm_bwd_{tag}")
    dag, gs["conv_w"] = _conv_bwd(sv["ag"], du1, w["conv_w"], f"conv_bwd_{tag}")
    dqkv = jnp.concatenate([dq, dk, dv], axis=1)
    g_ag = _mm(sv["h"], dag, "tn", BF16, f"proj_glu_dw_{tag}")
    g_qkv = _mm(sv["h"], dqkv, "tn", BF16, f"proj_qkv_dw_{tag}")
    g_f = _mm(sv["h"], df, "tn", BF16, f"proj_forget_dw_{tag}")
    g_gg = _mm(sv["h"], dgg, "tn", BF16, f"proj_gates_dw_{tag}")
    gw["w_in"] = jnp.concatenate([g_ag, g_qkv, g_f[:, :N_HEADS], g_gg], axis=1)
    dh = _mm(df, w["w_f"], "nt", F32, f"proj_forget_dx_{tag}")
    dh = _mm(dag, w["w_ag"], "nt", F32, f"proj_glu_dx_{tag}", residual=dh)
    dh = _mm(dqkv, w["w_qkv"], "nt", F32, f"proj_qkv_dx_{tag}", residual=dh)
    dh = _mm(dgg, w["w_gg"], "nt", F32, f"proj_gates_dx_{tag}", residual=dh)
    dx, _, gs["norm_mix_g"] = _rms_bwd(sv["x"], small["norm_mix_g"], dh, dx1, f"rms_mix_bwd_{tag}")
    for k in ("norm_ple_g", "norm_ffn_g", "norm_mix_g", "conv_ln_g", "conv_ln_b", "conv_b"):
        gs[k] = gs[k][0]
    return dx, gw, gs


_COL_SHARDED = ("w_in", "w_conv_out", "w_attn_out", "w_gate_up", "w_ple_proj")
_ROW_SHARDED = ("w_out", "w_down", "w_ple_gate")
_BIG = _COL_SHARDED + _ROW_SHARDED
_SMALL = ("norm_mix_g", "b_forget", "conv_b", "conv_ln_g", "conv_ln_b", "norm_ffn_g", "norm_ple_g")
_ORDER = ("norm_mix_g", "w_in", "b_forget", "conv_w", "conv_b", "conv_ln_g", "conv_ln_b", "w_conv_out", "w_attn_out",
          "w_out", "norm_ffn_g", "w_gate_up", "w_down", "norm_ple_g", "w_ple_gate", "w_ple_proj", "final_g")


def _pack(vectors, rows):
    flat = jnp.concatenate([v.reshape(-1) for v in vectors])
    return jnp.pad(flat, (0, rows * LANES - flat.shape[0])).reshape(1, rows, LANES)


def _unpack(packed, like):
    flat, out, at = packed.reshape(-1), [], 0
    for v in like:
        out.append(flat[at:at + v.size].reshape(v.shape))
        at += v.size
    return out


def kernel(x, p, norm_mix_g, w_in, b_forget, conv_w, conv_b, conv_ln_g, conv_ln_b, w_conv_out, w_attn_out, w_out, norm_ffn_g, w_gate_up, w_down, norm_ple_g, w_ple_gate, w_ple_proj, final_g, loss_target, m_norm_mix_g, m_w_in, m_b_forget, m_conv_w, m_conv_b, m_conv_ln_g, m_conv_ln_b, m_w_conv_out, m_w_attn_out, m_w_out, m_norm_ffn_g, m_w_gate_up, m_w_down, m_norm_ple_g, m_w_ple_gate, m_w_ple_proj, m_final_g, v_norm_mix_g, v_w_in, v_b_forget, v_conv_w, v_conv_b, v_conv_ln_g, v_conv_ln_b, v_w_conv_out, v_w_attn_out, v_w_out, v_norm_ffn_g, v_w_gate_up, v_w_down, v_norm_ple_g, v_w_ple_gate, v_w_ple_proj, v_final_g):
    weights = dict(norm_mix_g=norm_mix_g, w_in=w_in, b_forget=b_forget, conv_w=conv_w, conv_b=conv_b, conv_ln_g=conv_ln_g, conv_ln_b=conv_ln_b, w_conv_out=w_conv_out, w_attn_out=w_attn_out, w_out=w_out, norm_ffn_g=norm_ffn_g, w_gate_up=w_gate_up, w_down=w_down, norm_ple_g=norm_ple_g, w_ple_gate=w_ple_gate, w_ple_proj=w_ple_proj, final_g=final_g)
    mom1 = dict(norm_mix_g=m_norm_mix_g, w_in=m_w_in, b_forget=m_b_forget, conv_w=m_conv_w, conv_b=m_conv_b, conv_ln_g=m_conv_ln_g, conv_ln_b=m_conv_ln_b, w_conv_out=m_w_conv_out, w_attn_out=m_w_attn_out, w_out=m_w_out, norm_ffn_g=m_norm_ffn_g, w_gate_up=m_w_gate_up, w_down=m_w_down, norm_ple_g=m_norm_ple_g, w_ple_gate=m_w_ple_gate, w_ple_proj=m_w_ple_proj, final_g=m_final_g)
    mom2 = dict(norm_mix_g=v_norm_mix_g, w_in=v_w_in, b_forget=v_b_forget, conv_w=v_conv_w, conv_b=v_conv_b, conv_ln_g=v_conv_ln_g, conv_ln_b=v_conv_ln_b, w_conv_out=v_w_conv_out, w_attn_out=v_w_attn_out, w_out=v_w_out, norm_ffn_g=v_norm_ffn_g, w_gate_up=v_w_gate_up, w_down=v_w_down, norm_ple_g=v_norm_ple_g, w_ple_gate=v_w_ple_gate, w_ple_proj=v_w_ple_proj, final_g=v_final_g)
    depth = w_in.shape[0]
    d_model = x.shape[-1]
    xi, yi, ci = lax.axis_index("x"), lax.axis_index("y"), lax.axis_index("c")
    me = 4 * xi + 2 * yi + ci

    first = _all_gather([weights[k][0].astype(BF16) for k in _BIG] + [conv_w], "gather_weights_l0")
    gathered_conv_w = first[-1]
    gathered = [dict(zip(_BIG, first[:-1]), conv_w=gathered_conv_w[:, 0])]
    for l in range(1, depth):
        shards = [weights[k][l].astype(BF16) for k in _BIG]
        shards, _ = lax.optimization_barrier((shards, first[0]))
        later = _all_gather_beside(shards, f"gather_weights_l{l}", collective_id=l)
        gathered.append(dict(zip(_BIG, later), conv_w=gathered_conv_w[:, l]))

    act = x[0]
    layers = []
    for l in range(depth):
        if l > 0:
            gathered[l], act = lax.optimization_barrier((gathered[l], act))
        w = _layer_weights(gathered[l], d_model)
        small = {k: weights[k][l][None, :] for k in _SMALL if k != "b_forget"}
        small["b_forget"] = _pad_lanes(b_forget[l])
        act, saved = _layer_fwd(act, p[l, 0], w, small, f"l{l}")
        layers.append((w, small, saved))
    dact, g_final, loss_part = _loss_head(act, final_g[None, :], loss_target[0], "loss_head")
    loss = lax.psum(loss_part[0, 0], ("x", "y", "c"))

    core = ci.reshape(1).astype(jnp.int32)
    arrived = [None] * depth
    small_grads = {k: [None] * depth for k in _SMALL + ("conv_w",)}
    for l in reversed(range(depth)):
        w, small, saved = layers[l]
        dact, gw, gs = _layer_bwd(dact, saved, w, small, f"l{l}")
        if l + 1 < depth:
            arrived[l + 1], dact = lax.optimization_barrier((arrived[l + 1], dact))
        for k in small_grads:
            small_grads[k][l] = gs[k]
        blocks = [(_shard_cols(gw[k]) if k in _COL_SHARDED else gw[k].reshape((N_DEV, -1) + gw[k].shape[1:]))[:, None]
                  for k in _BIG]
        theirs = _pair_exchange(blocks, f"grads_pair_exchange_l{l}")
        chip_parts = [_pair_sum(a, b, core, f"grads_pair_sum_{k}_l{l}") for k, a, b in zip(_BIG, blocks, theirs)]
        if l > 0:
            chip_parts, dact = lax.optimization_barrier((chip_parts, dact))
            arrived[l] = _chip_exchange_beside(chip_parts, f"grads_chip_exchange_l{l}", collective_id=depth + l)
        else:
            arrived[l] = _chip_exchange(chip_parts, f"grads_chip_exchange_l{l}")
    grad_x = dact[None]

    out = {}
    for i, k in enumerate(_BIG):
        parts = jnp.concatenate([arrived[l][i] for l in range(depth)], axis=1)
        out[k] = _sum_adamw(parts, weights[k], mom1[k], mom2[k], f"adamw_{k}")

    small_names = list(_SMALL) + ["conv_w", "final_g"]
    small_list = [jnp.stack(small_grads[k]) for k in _SMALL + ("conv_w",)] + [g_final[0]]
    n_small = sum(v.size for v in small_list)
    rows = -(-n_small // (8 * LANES)) * 8
    all_parts = _all_gather([_pack(small_list, rows)], "gather_small_grads")[0]
    full_conv_w = jnp.transpose(gathered_conv_w, (1, 2, 0, 3)).reshape(depth, CONV_K, -1)
    full_conv_m = jnp.zeros_like(full_conv_w)
    ch_shard = conv_w.shape[-1]
    at = (0, 0, me * ch_shard)
    small_w = [weights[k] for k in _SMALL] + [full_conv_w, final_g]
    small_m = [mom1[k] for k in _SMALL] + [lax.dynamic_update_slice(full_conv_m, mom1["conv_w"], at), mom1["final_g"]]
    small_v = [mom2[k] for k in _SMALL] + [lax.dynamic_update_slice(full_conv_m + 1.0, mom2["conv_w"], at), mom2["final_g"]]
    res = _sum_adamw(all_parts, _pack(small_w, rows), _pack(small_m, rows), _pack(small_v, rows), "adamw_small")
    unpacked = [_unpack(r, small_list) for r in res]
    for idx, k in enumerate(small_names):
        vals = [u[idx] for u in unpacked]
        if k == "conv_w":
            vals = [lax.dynamic_slice(v, at, conv_w.shape) for v in vals]
        out[k] = vals

    return (loss, grad_x, *[out[k][0] for k in _ORDER], *[out[k][1] for k in _ORDER],
            *[out[k][2] for k in _ORDER], *[out[k][3] for k in _ORDER])
```

```python
import functools
import math

import jax
import jax.numpy as jnp
from jax import lax
from jax.experimental import pallas as pl
from jax.experimental.pallas import tpu as pltpu
from jax.experimental.pallas import tpu_sc as plsc

F32 = jnp.float32
BF16 = jnp.bfloat16
MESH = pl.DeviceIdType.MESH
ANY = pl.BlockSpec(memory_space=pl.ANY)

N_DEV = 8
N_HEADS = 8
HEAD_DIM = 128
CONV_K = 31
HALO = 32
EPS = 1e-6
NEG_INF = -1e30
ADAM_LR = 0.001
ADAM_B1 = 0.9
ADAM_B2 = 0.999
ADAM_EPS = 1e-08
ADAM_WD = 0.01
ADAM_STEP = 10

V7X_VMEM_BYTES = 64 * 1024 * 1024
VMEM_LIMIT = V7X_VMEM_BYTES * 3 // 4
LANES = 128
ROW_TILE = 256
ATTN_TILE = 512
ATTN_BWD_KEYS = 1024
ATTN_FWD_TILE = 1024
ATTN_ROWS = 256
CUM_TILE = 256


def _params(*sem):
    return pltpu.CompilerParams(dimension_semantics=sem, vmem_limit_bytes=VMEM_LIMIT)


def _tile(dim, pref, mult=LANES):
    t = min(pref, dim)
    t -= t % mult
    while t >= mult:
        if dim % t == 0:
            return t
        t -= mult
    return dim


def _sigmoid(x):
    return 1.0 / (1.0 + jnp.exp(-x))


def _place():
    x, y, c = lax.axis_index("x"), lax.axis_index("y"), lax.axis_index("c")
    chips = [(1 - x, y), (x, 1 - y), (1 - x, 1 - y)]
    return x, y, c, chips


def _all_gather(shards, name):
    n = len(shards)

    def body(*refs):
        ins, outs = refs[:n], refs[n:2 * n]
        send_sems, recv_sems, local_sems = refs[2 * n:]
        x, y, c, chips = _place()
        me, sibling = (x, y, c), (x, y, 1 - c)

        def idx(px, py, pc):
            return 4 * px + 2 * py + pc

        def copy(t, k, block, to, src=None):
            rows = outs[t].at[idx(*block)]
            return pltpu.make_async_remote_copy(
                src_ref=rows if src is None else src, dst_ref=rows,
                send_sem=send_sems.at[t, k], recv_sem=recv_sems.at[t, k],
                device_id=to, device_id_type=MESH)

        local = [pltpu.make_async_copy(ins[t], outs[t].at[idx(*me)], local_sems.at[t]) for t in range(n)]
        for cp in local:
            cp.start()
        first = []
        for t in range(n):
            first.append(copy(t, 0, me, sibling, src=ins[t]))
            first += [copy(t, 1 + j, me, (*chip, c), src=ins[t]) for j, chip in enumerate(chips)]
        for cp in first:
            cp.start()
        passed = []
        for j, chip in enumerate(chips):
            for t in range(n):
                copy(t, 1 + j, (*chip, c), me).wait_recv()
                cp = copy(t, 4 + j, (*chip, c), sibling)
                cp.start()
                passed.append(cp)
        for t in range(n):
            copy(t, 0, sibling, me).wait_recv()
        for j, chip in enumerate(chips):
            for t in range(n):
                copy(t, 4 + j, (*chip, 1 - c), me).wait_recv()
        for cp in first + passed:
            cp.wait_send()
        for cp in local:
            cp.wait()

    return pl.pallas_call(
        body, name=name,
        out_shape=[jax.ShapeDtypeStruct((N_DEV,) + s.shape, s.dtype) for s in shards],
        in_specs=[ANY] * n, out_specs=[ANY] * n,
        scratch_shapes=[pltpu.SemaphoreType.DMA((n, 7)), pltpu.SemaphoreType.DMA((n, 7)),
                        pltpu.SemaphoreType.DMA((n,))],
    )(*shards)


def _all_gather_beside(shards, name, collective_id):
    n = len(shards)
    src = [jax.new_ref(s, memory_space=pltpu.MemorySpace.HBM) for s in shards]
    dst = [jax.empty_ref(jax.ShapeDtypeStruct((N_DEV,) + s.shape, s.dtype), memory_space=pltpu.MemorySpace.HBM)
           for s in shards]

    @pl.kernel(mesh=plsc.ScalarSubcoreMesh(axis_name="sequencer", num_cores=1), name=name,
               scratch_types=(pltpu.SemaphoreType.DMA((n, 7)), pltpu.SemaphoreType.DMA((n, 7)),
                              pltpu.SemaphoreType.DMA((n,))),
               compiler_params=pltpu.CompilerParams(collective_id=collective_id))
    def launch(send_sems, recv_sems, local_sems):
        x, y, c, chips = _place()
        me, sibling = (x, y, c), (x, y, 1 - c)
        barrier = pltpu.get_barrier_semaphore()
        for peer in [sibling] + [(*chip, pc) for chip in chips for pc in (c, 1 - c)]:
            pl.semaphore_signal(barrier, inc=1, device_id=peer, device_id_type=MESH)
        pl.semaphore_wait(barrier, N_DEV - 1)

        def idx(px, py, pc):
            return 4 * px + 2 * py + pc

        def copy(t, k, block, to, own=False):
            rows = dst[t].at[idx(*block)]
            return pltpu.make_async_remote_copy(
                src_ref=src[t] if own else rows, dst_ref=rows,
                send_sem=send_sems.at[t, k], recv_sem=recv_sems.at[t, k],
                device_id=to, device_id_type=MESH)

        local = [pltpu.make_async_copy(src[t], dst[t].at[idx(*me)], local_sems.at[t]) for t in range(n)]
        for cp in local:
            cp.start()
        first = []
        for t in range(n):
            first.append(copy(t, 0, me, sibling, own=True))
            first += [copy(t, 1 + j, me, (*chip, c), own=True) for j, chip in enumerate(chips)]
        for cp in first:
            cp.start()
        passed = []
        for j, chip in enumerate(chips):
            for t in range(n):
                copy(t, 1 + j, (*chip, c), me).wait_recv()
                cp = copy(t, 4 + j, (*chip, c), sibling)
                cp.start()
                passed.append(cp)
        for t in range(n):
            copy(t, 0, sibling, me).wait_recv()
        for j, chip in enumerate(chips):
            for t in range(n):
                copy(t, 4 + j, (*chip, 1 - c), me).wait_recv()
        for cp in first + passed:
            cp.wait_send()
        for cp in local:
            cp.wait()

    launch()
    return [d[...] for d in dst]


def _pair_exchange(grads, name):
    n = len(grads)

    def body(*refs):
        ins, theirs = refs[:n], refs[n:2 * n]
        send_sems, recv_sems = refs[2 * n:]
        x, y, c, _ = _place()
        copies = []
        for t in range(n):
            for q in range(4):
                give = pltpu.make_async_remote_copy(
                    src_ref=ins[t].at[2 * q + 1 - c], dst_ref=theirs[t].at[q],
                    send_sem=send_sems.at[t, q], recv_sem=recv_sems.at[t, q],
                    device_id=(x, y, 1 - c), device_id_type=MESH)
                give.start()
                copies.append(give)
        for cp in copies:
            cp.wait()

    return pl.pallas_call(
        body, name=name, out_shape=[jax.ShapeDtypeStruct((4,) + g.shape[1:], g.dtype) for g in grads],
        in_specs=[ANY] * n, out_specs=[ANY] * n,
        scratch_shapes=[pltpu.SemaphoreType.DMA((n, 4)), pltpu.SemaphoreType.DMA((n, 4))],
    )(*grads)


def _chip_exchange(parts, name):
    n = len(parts)

    def body(*refs):
        ins, outs = refs[:n], refs[n:2 * n]
        send_sems, recv_sems, local_sems = refs[2 * n:]
        x, y, c, chips = _place()
        copies = []
        for t in range(n):
            own = pltpu.make_async_copy(ins[t].at[2 * x + y], outs[t].at[0], local_sems.at[t])
            own.start()
            copies.append(own)
            for j, (px, py) in enumerate(chips):
                cp = pltpu.make_async_remote_copy(
                    src_ref=ins[t].at[2 * px + py], dst_ref=outs[t].at[1 + j],
                    send_sem=send_sems.at[t, j], recv_sem=recv_sems.at[t, j],
                    device_id=(px, py, c), device_id_type=MESH)
                cp.start()
                copies.append(cp)
        for cp in copies:
            cp.wait()

    return pl.pallas_call(
        body, name=name, out_shape=[jax.ShapeDtypeStruct(p.shape, p.dtype) for p in parts],
        in_specs=[ANY] * n, out_specs=[ANY] * n,
        scratch_shapes=[pltpu.SemaphoreType.DMA((n, 3)), pltpu.SemaphoreType.DMA((n, 3)),
                        pltpu.SemaphoreType.DMA((n,))],
    )(*parts)


def _chip_exchange_beside(parts, name, collective_id):
    n = len(parts)
    src = [jax.new_ref(p, memory_space=pltpu.MemorySpace.HBM) for p in parts]
    dst = [jax.empty_ref(jax.ShapeDtypeStruct(p.shape, p.dtype), memory_space=pltpu.MemorySpace.HBM) for p in parts]

    @pl.kernel(mesh=plsc.ScalarSubcoreMesh(axis_name="sequencer", num_cores=1), name=name,
               scratch_types=(pltpu.SemaphoreType.DMA((n, 3)), pltpu.SemaphoreType.DMA((n, 3)),
                              pltpu.SemaphoreType.DMA((n,))),
               compiler_params=pltpu.CompilerParams(collective_id=collective_id))
    def launch(send_sems, recv_sems, local_sems):
        x, y, c, chips = _place()
        barrier = pltpu.get_barrier_semaphore()
        for px, py in chips:
            pl.semaphore_signal(barrier, inc=1, device_id=(px, py, c), device_id_type=MESH)
        pl.semaphore_wait(barrier, len(chips))
        copies = []
        for t in range(n):
            own = pltpu.make_async_copy(src[t].at[2 * x + y], dst[t].at[0], local_sems.at[t])
            own.start()
            copies.append(own)
            for j, (px, py) in enumerate(chips):
                cp = pltpu.make_async_remote_copy(
                    src_ref=src[t].at[2 * px + py], dst_ref=dst[t].at[1 + j],
                    send_sem=send_sems.at[t, j], recv_sem=recv_sems.at[t, j],
                    device_id=(px, py, c), device_id_type=MESH)
                cp.start()
                copies.append(cp)
        for cp in copies:
            cp.wait()

    launch()
    return [d[...] for d in dst]


_DOT_DIMS = {"nn": (((1,), (0,)), ((), ())), "nt": (((1,), (1,)), ((), ())), "tn": (((0,), (0,)), ((), ()))}


def _mm(a, b, mode, out_dtype, name, residual=None, tm=1024, tn=1024, tk=2048):
    if mode == "tn":
        (k_dim, m_dim), n_dim = a.shape, b.shape[1]
    else:
        (m_dim, k_dim), n_dim = a.shape, (b.shape[1] if mode == "nn" else b.shape[0])
    tm, tn, tk = _tile(m_dim, tm), _tile(n_dim, tn), _tile(k_dim, tk)
    nk = k_dim // tk

    def body(*refs):
        a_ref, b_ref = refs[:2]
        r_ref = refs[2] if residual is not None else None
        o_ref = refs[2 if residual is None else 3]

        def finish(r):
            if r_ref is not None:
                r = r + r_ref[...].astype(F32)
            o_ref[...] = r.astype(o_ref.dtype)

        part = lax.dot_general(a_ref[...].astype(BF16), b_ref[...].astype(BF16), _DOT_DIMS[mode],
                               preferred_element_type=F32)
        if nk == 1:
            finish(part)
            return
        acc, k = refs[-1], pl.program_id(2)

        @pl.when(k == 0)
        def _():
            acc[...] = part

        @pl.when((k > 0) & (k < nk - 1))
        def _():
            acc[...] += part

        @pl.when(k == nk - 1)
        def _():
            finish(acc[...] + part)

    a_spec = pl.BlockSpec((tk, tm), lambda i, j, k: (k, i)) if mode == "tn" else pl.BlockSpec((tm, tk), lambda i, j, k: (i, k))
    b_spec = pl.BlockSpec((tn, tk), lambda i, j, k: (j, k)) if mode == "nt" else pl.BlockSpec((tk, tn), lambda i, j, k: (k, j))
    o_spec = pl.BlockSpec((tm, tn), lambda i, j, k: (i, j))
    ins, specs = [a, b], [a_spec, b_spec]
    if residual is not None:
        ins.append(residual)
        specs.append(o_spec)
    return pl.pallas_call(
        body, name=name, grid=(m_dim // tm, n_dim // tn, nk),
        out_shape=jax.ShapeDtypeStruct((m_dim, n_dim), out_dtype),
        in_specs=specs, out_specs=o_spec,
        scratch_shapes=[pltpu.VMEM((tm, tn), F32)] if nk > 1 else [],
        compiler_params=_params("parallel", "parallel", "arbitrary"),
    )(*ins)


def _rows(body, name, row_ins, full_ins, row_outs, acc_outs, ts=ROW_TILE):
    s = row_ins[0].shape[0]
    ts = _tile(s, ts, 16)
    n_ri, n_fi, n_ro = len(row_ins), len(full_ins), len(row_outs)

    def kernel(*refs):
        i = pl.program_id(0)
        ri, fi = refs[:n_ri], refs[n_ri:n_ri + n_fi]
        ro, ao = refs[n_ri + n_fi:n_ri + n_fi + n_ro], refs[n_ri + n_fi + n_ro:]

        if ao:
            @pl.when(i == 0)
            def _():
                for r in ao:
                    r[...] = jnp.zeros_like(r)

        body(i, ri, fi, ro, ao)

    def row_spec(cols):
        return pl.BlockSpec((ts, cols), lambda i: (i, 0))

    def full_spec(shape):
        return pl.BlockSpec(shape, lambda i: (0,) * len(shape))

    return pl.pallas_call(
        kernel, name=name, grid=(s // ts,),
        out_shape=[jax.ShapeDtypeStruct((s, c), d) for c, d in row_outs]
        + [jax.ShapeDtypeStruct(sh, d) for sh, d in acc_outs],
        in_specs=[row_spec(a.shape[1]) for a in row_ins] + [full_spec(a.shape) for a in full_ins],
        out_specs=[row_spec(c) for c, _ in row_outs] + [full_spec(sh) for sh, _ in acc_outs],
        compiler_params=_params("arbitrary"),
    )(*row_ins, *full_ins)


def _rms_stats(x):
    return lax.rsqrt(jnp.mean(x * x, axis=-1, keepdims=True) + EPS)


def _rms_fwd(x, g, name):
    def body(i, ri, fi, ro, ao):
        xv = ri[0][...]
        ro[0][...] = (xv * _rms_stats(xv) * fi[0][...]).astype(BF16)

    return _rows(body, name, [x], [g], [(x.shape[1], BF16)], [])[0]


def _rms_bwd_rows(xv, gv, dh):
    r = _rms_stats(xv)
    xhat = xv * r
    dxhat = dh * gv
    dx = r * (dxhat - xhat * jnp.mean(dxhat * xhat, axis=-1, keepdims=True))
    return dx, jnp.sum(dh * xhat, axis=0, keepdims=True)


def _rms_bwd(x, g, dh, dres, name):
    d = x.shape[1]

    def body(i, ri, fi, ro, ao):
        dx, dg = _rms_bwd_rows(ri[0][...], fi[0][...], ri[1][...])
        dx = dx + ri[2][...]
        ro[0][...] = dx
        ro[1][...] = dx.astype(BF16)
        ao[0][...] += dg

    return _rows(body, name, [x, dh, dres], [g], [(d, F32), (d, BF16)], [((1, d), F32)])


def _loss_head(x, g, target, name):
    d = x.shape[1]

    def body(i, ri, fi, ro, ao):
        xv, gv = ri[0][...], fi[0][...]
        y = xv * _rms_stats(xv) * gv
        err = y - ri[1][...]
        ao[1][...] += 0.5 * jnp.sum(jnp.mean(err * err, axis=-1, keepdims=True))
        dx, dg = _rms_bwd_rows(xv, gv, err * (1.0 / d))
        ro[0][...] = dx
        ao[0][...] += dg

    return _rows(body, name, [x, target], [g], [(d, F32)], [((1, d), F32), ((8, LANES), F32)])


def _merge_fwd(gg, y_conv, y_attn, name):
    d = y_conv.shape[1]

    def body(i, ri, fi, ro, ao):
        gv = ri[0][...]
        m = _sigmoid(gv[:, :d]) * ri[1][...] + _sigmoid(gv[:, d:]) * ri[2][...]
        ro[0][...] = m.astype(BF16)

    return _rows(body, name, [gg, y_conv, y_attn], [], [(d, BF16)], [])[0]


def _merge_bwd(gg, y_conv, y_attn, dm, name):
    d = y_conv.shape[1]

    def body(i, ri, fi, ro, ao):
        gv, yc, ya, dmv = ri[0][...], ri[1][...], ri[2][...], ri[3][...]
        sc, sa = _sigmoid(gv[:, :d]), _sigmoid(gv[:, d:])
        ro[0][...] = (dmv * sc).astype(BF16)
        ro[1][...] = (dmv * sa).astype(BF16)
        ro[2][:, :d] = (dmv * yc * sc * (1.0 - sc)).astype(BF16)
        ro[2][:, d:] = (dmv * ya * sa * (1.0 - sa)).astype(BF16)

    return _rows(body, name, [gg, y_conv, y_attn, dm], [], [(d, BF16), (d, BF16), (2 * d, BF16)], [])


def _swiglu_fwd(gu, name):
    f = gu.shape[1] // 2

    def body(i, ri, fi, ro, ao):
        v = ri[0][...]
        gate, up = v[:, :f], v[:, f:]
        ro[0][...] = (gate * _sigmoid(gate) * up).astype(BF16)

    return _rows(body, name, [gu], [], [(f, BF16)], [], ts=128)[0]


def _swiglu_bwd(gu, dact, name):
    f = gu.shape[1] // 2

    def body(i, ri, fi, ro, ao):
        v, dv = ri[0][...], ri[1][...]
        gate, up = v[:, :f], v[:, f:]
        sg = _sigmoid(gate)
        ro[0][:, :f] = (dv * up * sg * (1.0 + gate * (1.0 - sg))).astype(BF16)
        ro[0][:, f:] = (dv * gate * sg).astype(BF16)

    return _rows(body, name, [gu, dact], [], [(2 * f, BF16)], [], ts=128)[0]


def _ple_fwd(x, z, pp, name):
    d = x.shape[1]

    def body(i, ri, fi, ro, ao):
        ro[0][...] = ri[0][...] + _sigmoid(ri[1][...]) * ri[2][...]

    return _rows(body, name, [x, z, pp], [], [(d, F32)], [])[0]


def _ple_bwd(dx, z, pp, name):
    d = dx.shape[1]

    def body(i, ri, fi, ro, ao):
        dv, sz = ri[0][...], _sigmoid(ri[1][...])
        ro[0][...] = (dv * ri[2][...] * sz * (1.0 - sz)).astype(BF16)
        ro[1][...] = (dv * sz).astype(BF16)

    return _rows(body, name, [dx, z, pp], [], [(d, BF16), (d, BF16)], [])


def _triangle(n, lower):
    r = lax.broadcasted_iota(jnp.int32, (n, n), 0)
    c = lax.broadcasted_iota(jnp.int32, (n, n), 1)
    return jnp.where((c <= r) if lower else (c >= r), 1.0, 0.0).astype(F32)


def _forget_fwd(f, bias, name):
    s = f.shape[0]
    tb = _tile(s, CUM_TILE, 8)

    def body(f_ref, b_ref, c_ref, carry):
        @pl.when(pl.program_id(0) == 0)
        def _():
            carry[...] = jnp.zeros_like(carry)

        xv = f_ref[...] + b_ref[...]
        logf = jnp.minimum(xv, 0.0) - jnp.log(1.0 + jnp.exp(-jnp.abs(xv)))
        cv = jnp.dot(_triangle(tb, True), logf, precision=lax.Precision.HIGHEST, preferred_element_type=F32) + carry[...]
        c_ref[...] = cv
        carry[...] = cv[tb - 1:tb, :]

    return pl.pallas_call(
        body, name=name, grid=(s // tb,), out_shape=jax.ShapeDtypeStruct((s, LANES), F32),
        in_specs=[pl.BlockSpec((tb, LANES), lambda i: (i, 0)), pl.BlockSpec((1, LANES), lambda i: (0, 0))],
        out_specs=pl.BlockSpec((tb, LANES), lambda i: (i, 0)),
        scratch_shapes=[pltpu.VMEM((1, LANES), F32)],
        compiler_params=_params("arbitrary"),
    )(f, bias)


def _forget_bwd(f, bias, dc, name):
    s = f.shape[0]
    tb = _tile(s, CUM_TILE, 16)
    nb = s // tb

    def body(f_ref, b_ref, dc_ref, df_ref, db_ref, carry):
        @pl.when(pl.program_id(0) == 0)
        def _():
            carry[...] = jnp.zeros_like(carry)
            db_ref[...] = jnp.zeros_like(db_ref)

        dlog = jnp.dot(_triangle(tb, False), dc_ref[...], precision=lax.Precision.HIGHEST,
                       preferred_element_type=F32) + carry[...]
        carry[...] = dlog[0:1, :]
        dlogit = dlog * (1.0 - _sigmoid(f_ref[...] + b_ref[...]))
        df_ref[...] = dlogit.astype(BF16)
        db_ref[...] += jnp.sum(dlogit, axis=0, keepdims=True)

    back = lambda i: (nb - 1 - i, 0)
    return pl.pallas_call(
        body, name=name, grid=(nb,),
        out_shape=[jax.ShapeDtypeStruct((s, LANES), BF16), jax.ShapeDtypeStruct((1, LANES), F32)],
        in_specs=[pl.BlockSpec((tb, LANES), back), pl.BlockSpec((1, LANES), lambda i: (0, 0)),
                  pl.BlockSpec((tb, LANES), back)],
        out_specs=[pl.BlockSpec((tb, LANES), back), pl.BlockSpec((1, LANES), lambda i: (0, 0))],
        scratch_shapes=[pltpu.VMEM((1, LANES), F32)],
        compiler_params=_params("arbitrary"),
    )(f, bias, dc)


def _glu(block, ch):
    return block[:, :ch] * _sigmoid(block[:, ch:])


def _layernorm_rows(u1, g, b):
    mu = jnp.mean(u1, axis=-1, keepdims=True)
    cen = u1 - mu
    rstd = lax.rsqrt(jnp.mean(cen * cen, axis=-1, keepdims=True) + EPS)
    xhat = cen * rstd
    return xhat, rstd, xhat * g + b


def _conv_fwd(ag, conv_w, conv_b, ln_g, ln_b, name):
    s, ch = ag.shape[0], ag.shape[1] // 2
    ts = _tile(s, ROW_TILE, HALO)
    per = ts // HALO

    def body(cur_ref, halo_ref, w_ref, cb_ref, g_ref, b_ref, u1_ref, u3_ref, cat):
        i = pl.program_id(0)
        cat[HALO:, :] = _glu(cur_ref[...], ch)
        cat[:HALO, :] = jnp.where(i == 0, 0.0, _glu(halo_ref[...], ch))
        full = cat[...]
        acc = jnp.zeros((ts, ch), F32) + cb_ref[...]
        for j in range(CONV_K):
            shifted = full if j == CONV_K - 1 else pltpu.roll(full, CONV_K - 1 - j, 0)
            acc = acc + w_ref[j:j + 1, :] * shifted[HALO:, :]
        u1_ref[...] = acc
        _, _, u2 = _layernorm_rows(acc, g_ref[...], b_ref[...])
        u3_ref[...] = (u2 * _sigmoid(u2)).astype(BF16)

    whole = lambda shape: pl.BlockSpec(shape, lambda i: (0, 0))
    return pl.pallas_call(
        body, name=name, grid=(s // ts,),
        out_shape=[jax.ShapeDtypeStruct((s, ch), F32), jax.ShapeDtypeStruct((s, ch), BF16)],
        in_specs=[pl.BlockSpec((ts, 2 * ch), lambda i: (i, 0)),
                  pl.BlockSpec((HALO, 2 * ch), lambda i: (jnp.maximum(i * per - 1, 0), 0)),
                  whole(conv_w.shape), whole((1, ch)), whole((1, ch)), whole((1, ch))],
        out_specs=[pl.BlockSpec((ts, ch), lambda i: (i, 0)), pl.BlockSpec((ts, ch), lambda i: (i, 0))],
        scratch_shapes=[pltpu.VMEM((HALO + ts, ch), F32)],
        compiler_params=_params("arbitrary"),
    )(ag, ag, conv_w, conv_b, ln_g, ln_b)


def _conv_norm_bwd(u1, du3, ln_g, ln_b, name):
    ch = u1.shape[1]

    def body(i, ri, fi, ro, ao):
        g = fi[0][...]
        xhat, rstd, u2 = _layernorm_rows(ri[0][...], g, fi[1][...])
        sg = _sigmoid(u2)
        du2 = ri[1][...] * sg * (1.0 + u2 * (1.0 - sg))
        dxhat = du2 * g
        du1 = rstd * (dxhat - jnp.mean(dxhat, axis=-1, keepdims=True)
                      - xhat * jnp.mean(dxhat * xhat, axis=-1, keepdims=True))
        ro[0][...] = du1
        ao[0][...] += jnp.sum(du2 * xhat, axis=0, keepdims=True)
        ao[1][...] += jnp.sum(du2, axis=0, keepdims=True)
        ao[2][...] += jnp.sum(du1, axis=0, keepdims=True)

    return _rows(body, name, [u1, du3], [ln_g, ln_b], [(ch, F32)], [((1, ch), F32)] * 3)


def _conv_bwd(ag, du1, conv_w, name):
    s, ch = du1.shape
    ts = _tile(s, ROW_TILE, HALO)
    per = ts // HALO
    last = s // HALO - 1
    nb = s // ts

    def body(cur_ref, halo_ref, d_ref, dnext_ref, w_ref, dag_ref, dw_ref, cat, dcat):
        i = pl.program_id(0)

        @pl.when(i == 0)
        def _():
            dw_ref[...] = jnp.zeros_like(dw_ref)

        cur = cur_ref[...]
        cat[HALO:, :] = _glu(cur, ch)
        cat[:HALO, :] = jnp.where(i == 0, 0.0, _glu(halo_ref[...], ch))
        dcur = d_ref[...]
        dcat[:ts, :] = dcur
        dcat[ts:, :] = jnp.where(i == nb - 1, 0.0, dnext_ref[...])
        full, dfull = cat[...], dcat[...]
        du0 = jnp.zeros((ts, ch), F32)
        for j in range(CONV_K):
            back = CONV_K - 1 - j
            dshift = dfull if back == 0 else pltpu.roll(dfull, ts + HALO - back, 0)
            du0 = du0 + w_ref[j:j + 1, :] * dshift[:ts, :]
            ushift = full if back == 0 else pltpu.roll(full, back, 0)
            dw_ref[j:j + 1, :] += jnp.sum(dcur * ushift[HALO:, :], axis=0, keepdims=True)
        a, sg = cur[:, :ch], _sigmoid(cur[:, ch:])
        dag_ref[:, :ch] = (du0 * sg).astype(BF16)
        dag_ref[:, ch:] = (du0 * a * sg * (1.0 - sg)).astype(BF16)

    return pl.pallas_call(
        body, name=name, grid=(nb,),
        out_shape=[jax.ShapeDtypeStruct((s, 2 * ch), BF16), jax.ShapeDtypeStruct(conv_w.shape, F32)],
        in_specs=[pl.BlockSpec((ts, 2 * ch), lambda i: (i, 0)),
                  pl.BlockSpec((HALO, 2 * ch), lambda i: (jnp.maximum(i * per - 1, 0), 0)),
                  pl.BlockSpec((ts, ch), lambda i: (i, 0)),
                  pl.BlockSpec((HALO, ch), lambda i: (jnp.minimum((i + 1) * per, last), 0)),
                  pl.BlockSpec(conv_w.shape, lambda i: (0, 0))],
        out_specs=[pl.BlockSpec((ts, 2 * ch), lambda i: (i, 0)), pl.BlockSpec(conv_w.shape, lambda i: (0, 0))],
        scratch_shapes=[pltpu.VMEM((HALO + ts, ch), F32), pltpu.VMEM((ts + HALO, ch), F32)],
        compiler_params=_params("arbitrary"),
    )(ag, ag, du1, du1, conv_w)


_SCALE = 1.0 / math.sqrt(HEAD_DIM)


STAT_LANES = 8


def _causal(rows, cols, row0):
    r = row0 + lax.broadcasted_iota(jnp.int32, (rows, cols), 0)
    return lax.broadcasted_iota(jnp.int32, (rows, cols), 1) <= r


def _attn_fwd(qkv, c_row, name):
    s = qkv.shape[0]
    t = _tile(s, ATTN_FWD_TILE)
    nt, part_rows = s // t, min(ATTN_ROWS, t)
    n_parts = t // part_rows
    h_, d = N_HEADS, HEAD_DIM

    def body(q_ref, k_ref, v_ref, cr_ref, o_ref, lse_ref, *scratch):
        m_sc, l_sc, acc = scratch[:n_parts], scratch[n_parts:2 * n_parts], scratch[2 * n_parts:]
        i = pl.program_id(1)
        for part in range(n_parts):
            m_sc[part][...] = jnp.full_like(m_sc[part], NEG_INF)
            l_sc[part][...] = jnp.zeros_like(l_sc[part])
            acc[part][...] = jnp.zeros_like(acc[part])

        def block(j, diagonal):
            at = pl.multiple_of(j * t, t)
            k, v = k_ref[pl.ds(at, t), :], v_ref[pl.ds(at, t), :]
            cr = cr_ref[0, pl.ds(j, 1), :]
            for part in range(n_parts):
                rows = pl.ds(part * part_rows, part_rows)
                sc = lax.dot_general(q_ref[rows, :], k, _DOT_DIMS["nt"], preferred_element_type=F32) * _SCALE - cr
                if diagonal:
                    sc = jnp.where(_causal(part_rows, t, part * part_rows), sc, NEG_INF)
                m_old = m_sc[part][...]
                m_new = jnp.maximum(m_old, jnp.max(sc, axis=-1, keepdims=True))
                alpha = jnp.exp(m_old - m_new)
                p = jnp.exp(sc - m_new)
                l_sc[part][...] = alpha * l_sc[part][...] + jnp.sum(p, axis=-1, keepdims=True)
                acc[part][...] = alpha * acc[part][...] + jnp.dot(p.astype(BF16), v, preferred_element_type=F32)
                m_sc[part][...] = m_new

        def below(j, carry):
            block(j, False)
            return carry

        lax.fori_loop(0, i, below, 0)
        block(i, True)
        for part in range(n_parts):
            rows = pl.ds(part * part_rows, part_rows)
            o_ref[rows, :] = (acc[part][...] / l_sc[part][...]).astype(BF16)
            lse_ref[0, rows, :] = m_sc[part][...] + jnp.log(l_sc[part][...])

    return pl.pallas_call(
        body, name=name, grid=(h_, nt),
        out_shape=[jax.ShapeDtypeStruct((s, h_ * d), BF16), jax.ShapeDtypeStruct((h_, s, 1), F32)],
        in_specs=[pl.BlockSpec((t, d), lambda h, i: (i, h)),
                  pl.BlockSpec((s, d), lambda h, i: (0, h_ + h)),
                  pl.BlockSpec((s, d), lambda h, i: (0, 2 * h_ + h)),
                  pl.BlockSpec((1, nt, t), lambda h, i: (h, 0, 0))],
        out_specs=[pl.BlockSpec((t, d), lambda h, i: (i, h)), pl.BlockSpec((1, t, 1), lambda h, i: (h, i, 0))],
        scratch_shapes=[pltpu.VMEM((part_rows, 1), F32)] * (2 * n_parts) + [pltpu.VMEM((part_rows, d), F32)] * n_parts,
        compiler_params=_params("parallel", "arbitrary"),
    )(qkv, qkv, qkv, c_row.reshape(h_, nt, t))


def _attn_stats(o, do, lse, name):
    s = o.shape[0]
    t = _tile(s, ATTN_TILE)
    h_, d = N_HEADS, HEAD_DIM

    def body(o_ref, do_ref, lse_ref, st_ref, dob_ref):
        dov = do_ref[...]
        st_ref[...] = jnp.zeros_like(st_ref)
        st_ref[0, :, 0:1] = -lse_ref[0]
        st_ref[0, :, 1:2] = jnp.sum(o_ref[...].astype(F32) * dov, axis=-1, keepdims=True)
        dob_ref[...] = dov.astype(BF16)

    col = pl.BlockSpec((1, t, 1), lambda h, i: (h, i, 0))
    blk = pl.BlockSpec((t, d), lambda h, i: (i, h))
    return pl.pallas_call(
        body, name=name, grid=(h_, s // t),
        out_shape=[jax.ShapeDtypeStruct((h_, s, STAT_LANES), F32), jax.ShapeDtypeStruct(o.shape, BF16)],
        in_specs=[blk, blk, col],
        out_specs=[pl.BlockSpec((1, t, STAT_LANES), lambda h, i: (h, i, 0)), blk],
        compiler_params=_params("parallel", "parallel"),
    )(o, do, lse)


def _attn_bwd(qkv, do, stats, c_row, name):
    s = qkv.shape[0]
    tq = _tile(s, ATTN_TILE)
    tk = _tile(s, ATTN_BWD_KEYS, tq)
    nq, nk, per = s // tq, s // tk, tk // tq
    h_, d = N_HEADS, HEAD_DIM

    def body(q_ref, do_ref, k_ref, v_ref, st_ref, cr_ref, dq_ref, dk_ref, dv_ref, dcr_ref, dcc_ref,
             dq_acc, dk_acc, dv_acc, dcr_acc):
        j = pl.program_id(1)

        @pl.when(j == 0)
        def _():
            dq_acc[...] = jnp.zeros_like(dq_acc)
            dcc_ref[...] = jnp.zeros_like(dcc_ref)

        dk_acc[...] = jnp.zeros_like(dk_acc)
        dv_acc[...] = jnp.zeros_like(dv_acc)
        dcr_acc[...] = jnp.zeros_like(dcr_acc)
        k, v, cr = k_ref[...], v_ref[...], cr_ref[0]

        def block(i, row0):
            rows = pl.ds(pl.multiple_of(i * tq, tq), tq)
            q, dov, st = q_ref[rows, :], do_ref[rows, :], st_ref[0, rows, :]
            sc = lax.dot_general(q, k, _DOT_DIMS["nt"], preferred_element_type=F32) * _SCALE + (st[:, 0:1] - cr)
            p = jnp.exp(sc)
            if row0 is not None:
                p = jnp.where(_causal(tq, tk, row0), p, 0.0)
            dp = lax.dot_general(dov, v, _DOT_DIMS["nt"], preferred_element_type=F32)
            ds = p * (dp - st[:, 1:2])
            ds_b = ds.astype(BF16)
            dv_acc[...] += lax.dot_general(p.astype(BF16), dov, _DOT_DIMS["tn"], preferred_element_type=F32)
            dk_acc[...] += lax.dot_general(ds_b, q, _DOT_DIMS["tn"], preferred_element_type=F32)
            dq_acc[rows, :] += jnp.dot(ds_b, k, preferred_element_type=F32)
            dcr_acc[...] -= jnp.sum(ds, axis=0, keepdims=True)
            dcc_ref[0, rows, :] += jnp.sum(ds, axis=-1, keepdims=True)

        def below(i, carry):
            block(i, None)
            return carry

        for part in range(per):
            block(j * per + part, part * tq)
        lax.fori_loop((j + 1) * per, nq, below, 0)
        dk_ref[...] = (dk_acc[...] * _SCALE).astype(BF16)
        dv_ref[...] = dv_acc[...].astype(BF16)
        dcr_ref[0] = dcr_acc[...]

        @pl.when(j == nk - 1)
        def _():
            dq_ref[...] = (dq_acc[...] * _SCALE).astype(BF16)

    head = pl.BlockSpec((s, d), lambda h, j: (0, h))
    return pl.pallas_call(
        body, name=name, grid=(h_, nk),
        out_shape=[jax.ShapeDtypeStruct((s, h_ * d), BF16)] * 3
        + [jax.ShapeDtypeStruct((h_, 1, s), F32), jax.ShapeDtypeStruct((h_, s, 1), F32)],
        in_specs=[head, head,
                  pl.BlockSpec((tk, d), lambda h, j: (j, h_ + h)),
                  pl.BlockSpec((tk, d), lambda h, j: (j, 2 * h_ + h)),
                  pl.BlockSpec((1, s, STAT_LANES), lambda h, j: (h, 0, 0)),
                  pl.BlockSpec((1, 1, tk), lambda h, j: (h, 0, j))],
        out_specs=[head, pl.BlockSpec((tk, d), lambda h, j: (j, h)), pl.BlockSpec((tk, d), lambda h, j: (j, h)),
                   pl.BlockSpec((1, 1, tk), lambda h, j: (h, 0, j)), pl.BlockSpec((1, s, 1), lambda h, j: (h, 0, 0))],
        scratch_shapes=[pltpu.VMEM((s, d), F32), pltpu.VMEM((tk, d), F32), pltpu.VMEM((tk, d), F32),
                        pltpu.VMEM((1, tk), F32)],
        compiler_params=_params("parallel", "arbitrary"),
    )(qkv, do, qkv, qkv, stats, c_row)


def _pair_sum(grads, theirs, core, name):
    _, l, r, c = theirs.shape
    tr = _tile(r, 256, 16)

    def body(core_ref, a_ref, b_ref, o_ref):
        o_ref[...] = (a_ref[...].astype(F32) + b_ref[...].astype(F32)).astype(BF16)

    spec = pl.BlockSpec((1, 1, tr, c), lambda q, k, i, core_ref: (q, k, i, 0))
    mine = pl.BlockSpec((1, 1, tr, c), lambda q, k, i, core_ref: (2 * q + core_ref[0], k, i, 0))
    return pl.pallas_call(
        body, name=name, out_shape=jax.ShapeDtypeStruct(theirs.shape, BF16),
        grid_spec=pltpu.PrefetchScalarGridSpec(num_scalar_prefetch=1, grid=(4, l, r // tr),
                                               in_specs=[mine, spec], out_specs=spec),
        compiler_params=_params("parallel", "parallel", "parallel"),
    )(core, grads, theirs)


def _adamw(g, w, m, v):
    m = ADAM_B1 * m + (1.0 - ADAM_B1) * g
    v = ADAM_B2 * v + (1.0 - ADAM_B2) * (g * g)
    m_hat = m / (1.0 - ADAM_B1 ** ADAM_STEP)
    v_hat = v / (1.0 - ADAM_B2 ** ADAM_STEP)
    return -ADAM_LR * (m_hat / (jnp.sqrt(v_hat) + ADAM_EPS) + ADAM_WD * w), m, v


def _sum_adamw(parts, w, m, v, name):
    n_parts, l, r, c = parts.shape
    tr = _tile(r, 128, 16)

    def body(p_ref, w_ref, m_ref, v_ref, g_ref, d_ref, nm_ref, nv_ref):
        g = p_ref[0].astype(F32)
        for k in range(1, n_parts):
            g = g + p_ref[k].astype(F32)
        g_ref[...] = g
        d_ref[...], nm_ref[...], nv_ref[...] = _adamw(g, w_ref[...], m_ref[...], v_ref[...])

    spec = pl.BlockSpec((1, tr, c), lambda k, i: (k, i, 0))
    return pl.pallas_call(
        body, name=name, grid=(l, r // tr), out_shape=[jax.ShapeDtypeStruct(w.shape, F32)] * 4,
        in_specs=[pl.BlockSpec((n_parts, 1, tr, c), lambda k, i: (0, k, i, 0)), spec, spec, spec],
        out_specs=[spec] * 4, compiler_params=_params("parallel", "parallel"),
    )(parts, w, m, v)


def _unshard_cols(gathered):
    return jnp.transpose(gathered, (1, 0, 2)).reshape(gathered.shape[1], -1)


def _shard_cols(full):
    k = full.shape[0]
    return jnp.transpose(full.reshape(k, N_DEV, -1), (1, 0, 2))


def _layer_weights(w, d_model):
    hd = N_HEADS * HEAD_DIM
    w_in = _unshard_cols(w["w_in"])
    f0 = d_model + 3 * hd
    out = {
        "w_ag": w_in[:, :d_model], "w_qkv": w_in[:, d_model:f0],
        "w_f": jnp.pad(w_in[:, f0:f0 + N_HEADS], ((0, 0), (0, LANES - N_HEADS))),
        "w_gg": w_in[:, f0 + N_HEADS:],
        "w_conv_out": _unshard_cols(w["w_conv_out"]), "w_attn_out": _unshard_cols(w["w_attn_out"]),
        "w_out": w["w_out"].reshape(-1, d_model), "w_gate_up": _unshard_cols(w["w_gate_up"]),
        "w_down": w["w_down"].reshape(-1, d_model), "w_ple_gate": w["w_ple_gate"].reshape(-1, d_model),
        "w_ple_proj": _unshard_cols(w["w_ple_proj"]),
        "conv_w": jnp.transpose(w["conv_w"], (1, 0, 2)).reshape(CONV_K, -1),
    }
    return out


def _pad_lanes(vec):
    return jnp.pad(vec, (0, LANES - vec.shape[0]))[None, :]


def _layer_fwd(x, p, w, small, tag):
    h = _rms_fwd(x, small["norm_mix_g"], f"rms_mix_{tag}")
    ag = _mm(h, w["w_ag"], "nn", F32, f"proj_glu_{tag}")
    qkv = _mm(h, w["w_qkv"], "nn", BF16, f"proj_qkv_{tag}")
    gg = _mm(h, w["w_gg"], "nn", F32, f"proj_gates_{tag}")
    f = _mm(h, w["w_f"], "nn", F32, f"proj_forget_{tag}")
    u1, u3 = _conv_fwd(ag, w["conv_w"], small["conv_b"], small["conv_ln_g"], small["conv_ln_b"], f"conv_{tag}")
    y_conv = _mm(u3, w["w_conv_out"], "nn", F32, f"conv_out_{tag}")
    c = _forget_fwd(f, small["b_forget"], f"forget_{tag}")
    c_heads = jnp.transpose(c[:, :N_HEADS])
    c_row = c_heads[:, None, :]
    o, lse = _attn_fwd(qkv, c_row, f"attn_{tag}")
    y_attn = _mm(o, w["w_attn_out"], "nn", F32, f"attn_out_{tag}")
    merged = _merge_fwd(gg, y_conv, y_attn, f"merge_{tag}")
    x1 = _mm(merged, w["w_out"], "nn", F32, f"mix_out_{tag}", residual=x)
    hf = _rms_fwd(x1, small["norm_ffn_g"], f"rms_ffn_{tag}")
    gu = _mm(hf, w["w_gate_up"], "nn", F32, f"ffn_up_{tag}")
    act = _swiglu_fwd(gu, f"swiglu_{tag}")
    x2 = _mm(act, w["w_down"], "nn", F32, f"ffn_down_{tag}", residual=x1)
    hp = _rms_fwd(x2, small["norm_ple_g"], f"rms_ple_{tag}")
    z = _mm(hp, w["w_ple_gate"], "nn", F32, f"ple_gate_{tag}")
    pp = _mm(p, w["w_ple_proj"], "nn", F32, f"ple_proj_{tag}")
    x3 = _ple_fwd(x2, z, pp, f"ple_{tag}")
    saved = dict(x=x, p=p, h=h, ag=ag, qkv=qkv, gg=gg, f=f, u1=u1, u3=u3, y_conv=y_conv, c_row=c_row,
                 o=o, lse=lse, y_attn=y_attn, merged=merged, x1=x1, hf=hf, gu=gu, act=act, x2=x2, hp=hp, z=z, pp=pp)
    return x3, saved


def _layer_bwd(dx3, sv, w, small, tag):
    d_model = dx3.shape[1]
    gw, gs = {}, {}
    dz, dpp = _ple_bwd(dx3, sv["z"], sv["pp"], f"ple_bwd_{tag}")
    gw["w_ple_proj"] = _mm(sv["p"], dpp, "tn", BF16, f"ple_proj_dw_{tag}")
    gw["w_ple_gate"] = _mm(sv["hp"], dz, "tn", BF16, f"ple_gate_dw_{tag}")
    dhp = _mm(dz, w["w_ple_gate"], "nt", F32, f"ple_gate_dx_{tag}")
    dx2, dx2_b, gs["norm_ple_g"] = _rms_bwd(sv["x2"], small["norm_ple_g"], dhp, dx3, f"rms_ple_bwd_{tag}")
    gw["w_down"] = _mm(sv["act"], dx2_b, "tn", BF16, f"ffn_down_dw_{tag}")
    dact = _mm(dx2_b, w["w_down"], "nt", F32, f"ffn_down_dx_{tag}")
    dgu = _swiglu_bwd(sv["gu"], dact, f"swiglu_bwd_{tag}")
    gw["w_gate_up"] = _mm(sv["hf"], dgu, "tn", BF16, f"ffn_up_dw_{tag}")
    dhf = _mm(dgu, w["w_gate_up"], "nt", F32, f"ffn_up_dx_{tag}")
    dx1, dx1_b, gs["norm_ffn_g"] = _rms_bwd(sv["x1"], small["norm_ffn_g"], dhf, dx2, f"rms_ffn_bwd_{tag}")
    gw["w_out"] = _mm(sv["merged"], dx1_b, "tn", BF16, f"mix_out_dw_{tag}")
    dm = _mm(dx1_b, w["w_out"], "nt", F32, f"mix_out_dx_{tag}")
    dyc, dya, dgg = _merge_bwd(sv["gg"], sv["y_conv"], sv["y_attn"], dm, f"merge_bwd_{tag}")
    gw["w_attn_out"] = _mm(sv["o"], dya, "tn", BF16, f"attn_out_dw_{tag}")
    do = _mm(dya, w["w_attn_out"], "nt", F32, f"attn_out_dx_{tag}")
    stats, do_b = _attn_stats(sv["o"], do, sv["lse"], f"attn_stats_{tag}")
    dq, dk, dv, dc_row, dc_col = _attn_bwd(sv["qkv"], do_b, stats, sv["c_row"], f"attn_bwd_{tag}")
    dc = jnp.pad(jnp.transpose(dc_row[:, 0, :] + dc_col[:, :, 0]), ((0, 0), (0, LANES - N_HEADS)))
    df, db = _forget_bwd(sv["f"], small["b_forget"], dc, f"forget_bwd_{tag}")
    gs["b_forget"] = db[0, :N_HEADS]
    gw["w_conv_out"] = _mm(sv["u3"], dyc, "tn", BF16, f"conv_out_dw_{tag}")
    du3 = _mm(dyc, w["w_conv_out"], "nt", F32, f"conv_out_dx_{tag}")
    du1, gs["conv_ln_g"], gs["conv_ln_b"], gs["conv_b"] = _conv_norm_bwd(
        sv["u1"], du3, small["conv_ln_g"], small["conv_ln_b"], f"conv_norm_bwd_{tag}")
    dag, gs["conv_w"] = _conv_bwd(sv["ag"], du1, w["conv_w"], f"conv_bwd_{tag}")
    dqkv = jnp.concatenate([dq, dk, dv], axis=1)
    g_ag = _mm(sv["h"], dag, "tn", BF16, f"proj_glu_dw_{tag}")
    g_qkv = _mm(sv["h"], dqkv, "tn", BF16, f"proj_qkv_dw_{tag}")
    g_f = _mm(sv["h"], df, "tn", BF16, f"proj_forget_dw_{tag}")
    g_gg = _mm(sv["h"], dgg, "tn", BF16, f"proj_gates_dw_{tag}")
    gw["w_in"] = jnp.concatenate([g_ag, g_qkv, g_f[:, :N_HEADS], g_gg], axis=1)
    dh = _mm(df, w["w_f"], "nt", F32, f"proj_forget_dx_{tag}")
    dh = _mm(dag, w["w_ag"], "nt", F32, f"proj_glu_dx_{tag}", residual=dh)
    dh = _mm(dqkv, w["w_qkv"], "nt", F32, f"proj_qkv_dx_{tag}", residual=dh)
    dh = _mm(dgg, w["w_gg"], "nt", F32, f"proj_gates_dx_{tag}", residual=dh)
    dx, _, gs["norm_mix_g"] = _rms_bwd(sv["x"], small["norm_mix_g"], dh, dx1, f"rms_mix_bwd_{tag}")
    for k in ("norm_ple_g", "norm_ffn_g", "norm_mix_g", "conv_ln_g", "conv_ln_b", "conv_b"):
        gs[k] = gs[k][0]
    return dx, gw, gs


_COL_SHARDED = ("w_in", "w_conv_out", "w_attn_out", "w_gate_up", "w_ple_proj")
_ROW_SHARDED = ("w_out", "w_down", "w_ple_gate")
_BIG = _COL_SHARDED + _ROW_SHARDED
_SMALL = ("norm_mix_g", "b_forget", "conv_b", "conv_ln_g", "conv_ln_b", "norm_ffn_g", "norm_ple_g")
_ORDER = ("norm_mix_g", "w_in", "b_forget", "conv_w", "conv_b", "conv_ln_g", "conv_ln_b", "w_conv_out", "w_attn_out",
          "w_out", "norm_ffn_g", "w_gate_up", "w_down", "norm_ple_g", "w_ple_gate", "w_ple_proj", "final_g")


def _pack(vectors, rows):
    flat = jnp.concatenate([v.reshape(-1) for v in vectors])
    return jnp.pad(flat, (0, rows * LANES - flat.shape[0])).reshape(1, rows, LANES)


def _unpack(packed, like):
    flat, out, at = packed.reshape(-1), [], 0
    for v in like:
        out.append(flat[at:at + v.size].reshape(v.shape))
        at += v.size
    return out


def kernel(x, p, norm_mix_g, w_in, b_forget, conv_w, conv_b, conv_ln_g, conv_ln_b, w_conv_out, w_attn_out, w_out, norm_ffn_g, w_gate_up, w_down, norm_ple_g, w_ple_gate, w_ple_proj, final_g, loss_target, m_norm_mix_g, m_w_in, m_b_forget, m_conv_w, m_conv_b, m_conv_ln_g, m_conv_ln_b, m_w_conv_out, m_w_attn_out, m_w_out, m_norm_ffn_g, m_w_gate_up, m_w_down, m_norm_ple_g, m_w_ple_gate, m_w_ple_proj, m_final_g, v_norm_mix_g, v_w_in, v_b_forget, v_conv_w, v_conv_b, v_conv_ln_g, v_conv_ln_b, v_w_conv_out, v_w_attn_out, v_w_out, v_norm_ffn_g, v_w_gate_up, v_w_down, v_norm_ple_g, v_w_ple_gate, v_w_ple_proj, v_final_g):
    weights = dict(norm_mix_g=norm_mix_g, w_in=w_in, b_forget=b_forget, conv_w=conv_w, conv_b=conv_b, conv_ln_g=conv_ln_g, conv_ln_b=conv_ln_b, w_conv_out=w_conv_out, w_attn_out=w_attn_out, w_out=w_out, norm_ffn_g=norm_ffn_g, w_gate_up=w_gate_up, w_down=w_down, norm_ple_g=norm_ple_g, w_ple_gate=w_ple_gate, w_ple_proj=w_ple_proj, final_g=final_g)
    mom1 = dict(norm_mix_g=m_norm_mix_g, w_in=m_w_in, b_forget=m_b_forget, conv_w=m_conv_w, conv_b=m_conv_b, conv_ln_g=m_conv_ln_g, conv_ln_b=m_conv_ln_b, w_conv_out=m_w_conv_out, w_attn_out=m_w_attn_out, w_out=m_w_out, norm_ffn_g=m_norm_ffn_g, w_gate_up=m_w_gate_up, w_down=m_w_down, norm_ple_g=m_norm_ple_g, w_ple_gate=m_w_ple_gate, w_ple_proj=m_w_ple_proj, final_g=m_final_g)
    mom2 = dict(norm_mix_g=v_norm_mix_g, w_in=v_w_in, b_forget=v_b_forget, conv_w=v_conv_w, conv_b=v_conv_b, conv_ln_g=v_conv_ln_g, conv_ln_b=v_conv_ln_b, w_conv_out=v_w_conv_out, w_attn_out=v_w_attn_out, w_out=v_w_out, norm_ffn_g=v_norm_ffn_g, w_gate_up=v_w_gate_up, w_down=v_w_down, norm_ple_g=v_norm_ple_g, w_ple_gate=v_w_ple_gate, w_ple_proj=v_w_ple_proj, final_g=v_final_g)
    depth = w_in.shape[0]
    d_model = x.shape[-1]
    xi, yi, ci = lax.axis_index("x"), lax.axis_index("y"), lax.axis_index("c")
    me = 4 * xi + 2 * yi + ci

    first = _all_gather([weights[k][0].astype(BF16) for k in _BIG] + [conv_w], "gather_weights_l0")
    gathered_conv_w = first[-1]
    gathered = [dict(zip(_BIG, first[:-1]), conv_w=gathered_conv_w[:, 0])]
    for l in range(1, depth):
        shards = [weights[k][l].astype(BF16) for k in _BIG]
        shards, _ = lax.optimization_barrier((shards, first[0]))
        later = _all_gather_beside(shards, f"gather_weights_l{l}", collective_id=l)
        gathered.append(dict(zip(_BIG, later), conv_w=gathered_conv_w[:, l]))

    act = x[0]
    layers = []
    for l in range(depth):
        if l > 0:
            gathered[l], act = lax.optimization_barrier((gathered[l], act))
        w = _layer_weights(gathered[l], d_model)
        small = {k: weights[k][l][None, :] for k in _SMALL if k != "b_forget"}
        small["b_forget"] = _pad_lanes(b_forget[l])
        act, saved = _layer_fwd(act, p[l, 0], w, small, f"l{l}")
        layers.append((w, small, saved))
    dact, g_final, loss_part = _loss_head(act, final_g[None, :], loss_target[0], "loss_head")
    loss = lax.psum(loss_part[0, 0], ("x", "y", "c"))

    core = ci.reshape(1).astype(jnp.int32)
    arrived = [None] * depth
    small_grads = {k: [None] * depth for k in _SMALL + ("conv_w",)}
    for l in reversed(range(depth)):
        w, small, saved = layers[l]
        dact, gw, gs = _layer_bwd(dact, saved, w, small, f"l{l}")
        if l + 1 < depth:
            arrived[l + 1], dact = lax.optimization_barrier((arrived[l + 1], dact))
        for k in small_grads:
            small_grads[k][l] = gs[k]
        blocks = [(_shard_cols(gw[k]) if k in _COL_SHARDED else gw[k].reshape((N_DEV, -1) + gw[k].shape[1:]))[:, None]
                  for k in _BIG]
        theirs = _pair_exchange(blocks, f"grads_pair_exchange_l{l}")
        chip_parts = [_pair_sum(a, b, core, f"grads_pair_sum_{k}_l{l}") for k, a, b in zip(_BIG, blocks, theirs)]
        if l > 0:
            chip_parts, dact = lax.optimization_barrier((chip_parts, dact))
            arrived[l] = _chip_exchange_beside(chip_parts, f"grads_chip_exchange_l{l}", collective_id=depth + l)
        else:
            arrived[l] = _chip_exchange(chip_parts, f"grads_chip_exchange_l{l}")
    grad_x = dact[None]

    out = {}
    for i, k in enumerate(_BIG):
        parts = jnp.concatenate([arrived[l][i] for l in range(depth)], axis=1)
        out[k] = _sum_adamw(parts, weights[k], mom1[k], mom2[k], f"adamw_{k}")

    small_names = list(_SMALL) + ["conv_w", "final_g"]
    small_list = [jnp.stack(small_grads[k]) for k in _SMALL + ("conv_w",)] + [g_final[0]]
    n_small = sum(v.size for v in small_list)
    rows = -(-n_small // (8 * LANES)) * 8
    all_parts = _all_gather([_pack(small_list, rows)], "gather_small_grads")[0]
    full_conv_w = jnp.transpose(gathered_conv_w, (1, 2, 0, 3)).reshape(depth, CONV_K, -1)
    full_conv_m = jnp.zeros_like(full_conv_w)
    ch_shard = conv_w.shape[-1]
    at = (0, 0, me * ch_shard)
    small_w = [weights[k] for k in _SMALL] + [full_conv_w, final_g]
    small_m = [mom1[k] for k in _SMALL] + [lax.dynamic_update_slice(full_conv_m, mom1["conv_w"], at), mom1["final_g"]]
    small_v = [mom2[k] for k in _SMALL] + [lax.dynamic_update_slice(full_conv_m + 1.0, mom2["conv_w"], at), mom2["final_g"]]
    res = _sum_adamw(all_parts, _pack(small_w, rows), _pack(small_m, rows), _pack(small_v, rows), "adamw_small")
    unpacked = [_unpack(r, small_list) for r in res]
    for idx, k in enumerate(small_names):
        vals = [u[idx] for u in unpacked]
        if k == "conv_w":
            vals = [lax.dynamic_slice(v, at, conv_w.shape) for v in vals]
        out[k] = vals

    return (loss, grad_x, *[out[k][0] for k in _ORDER], *[out[k][1] for k in _ORDER],
            *[out[k][2] for k in _ORDER], *[out[k][3] for k in _ORDER])
```

```python
import functools
import math

import jax
import jax.numpy as jnp
from jax import lax
from jax.experimental import pallas as pl
from jax.experimental.pallas import tpu as pltpu
from jax.experimental.pallas import tpu_sc as plsc

F32 = jnp.float32
BF16 = jnp.bfloat16
MESH = pl.DeviceIdType.MESH
ANY = pl.BlockSpec(memory_space=pl.ANY)

N_DEV = 8
N_HEADS = 8
HEAD_DIM = 128
CONV_K = 31
HALO = 32
EPS = 1e-6
NEG_INF = -1e30
ADAM_LR = 0.001
ADAM_B1 = 0.9
ADAM_B2 = 0.999
ADAM_EPS = 1e-08
ADAM_WD = 0.01
ADAM_STEP = 10

V7X_VMEM_BYTES = 64 * 1024 * 1024
VMEM_LIMIT = V7X_VMEM_BYTES * 3 // 4
LANES = 128
ROW_TILE = 256
ATTN_TILE = 512
ATTN_BWD_KEYS = 1024
ATTN_FWD_TILE = 1024
ATTN_ROWS = 256
CUM_TILE = 256
FFN_K_TILE = 2816
FFN_N_TILE = 1408


def _params(*sem):
    return pltpu.CompilerParams(dimension_semantics=sem, vmem_limit_bytes=VMEM_LIMIT)


def _tile(dim, pref, mult=LANES):
    t = min(pref, dim)
    t -= t % mult
    while t >= mult:
        if dim % t == 0:
            return t
        t -= mult
    return dim


def _sigmoid(x):
    return 1.0 / (1.0 + jnp.exp(-x))


def _place():
    x, y, c = lax.axis_index("x"), lax.axis_index("y"), lax.axis_index("c")
    chips = [(1 - x, y), (x, 1 - y), (1 - x, 1 - y)]
    return x, y, c, chips


def _all_gather(shards, name):
    n = len(shards)

    def body(*refs):
        ins, outs = refs[:n], refs[n:2 * n]
        send_sems, recv_sems, local_sems = refs[2 * n:]
        x, y, c, chips = _place()
        me, sibling = (x, y, c), (x, y, 1 - c)

        def idx(px, py, pc):
            return 4 * px + 2 * py + pc

        def copy(t, k, block, to, src=None):
            rows = outs[t].at[idx(*block)]
            return pltpu.make_async_remote_copy(
                src_ref=rows if src is None else src, dst_ref=rows,
                send_sem=send_sems.at[t, k], recv_sem=recv_sems.at[t, k],
                device_id=to, device_id_type=MESH)

        local = [pltpu.make_async_copy(ins[t], outs[t].at[idx(*me)], local_sems.at[t]) for t in range(n)]
        for cp in local:
            cp.start()
        first = []
        for t in range(n):
            first.append(copy(t, 0, me, sibling, src=ins[t]))
            first += [copy(t, 1 + j, me, (*chip, c), src=ins[t]) for j, chip in enumerate(chips)]
        for cp in first:
            cp.start()
        passed = []
        for j, chip in enumerate(chips):
            for t in range(n):
                copy(t, 1 + j, (*chip, c), me).wait_recv()
                cp = copy(t, 4 + j, (*chip, c), sibling)
                cp.start()
                passed.append(cp)
        for t in range(n):
            copy(t, 0, sibling, me).wait_recv()
        for j, chip in enumerate(chips):
            for t in range(n):
                copy(t, 4 + j, (*chip, 1 - c), me).wait_recv()
        for cp in first + passed:
            cp.wait_send()
        for cp in local:
            cp.wait()

    return pl.pallas_call(
        body, name=name,
        out_shape=[jax.ShapeDtypeStruct((N_DEV,) + s.shape, s.dtype) for s in shards],
        in_specs=[ANY] * n, out_specs=[ANY] * n,
        scratch_shapes=[pltpu.SemaphoreType.DMA((n, 7)), pltpu.SemaphoreType.DMA((n, 7)),
                        pltpu.SemaphoreType.DMA((n,))],
    )(*shards)


def _all_gather_beside(shards, name, collective_id):
    n = len(shards)
    src = [jax.new_ref(s, memory_space=pltpu.MemorySpace.HBM) for s in shards]
    dst = [jax.empty_ref(jax.ShapeDtypeStruct((N_DEV,) + s.shape, s.dtype), memory_space=pltpu.MemorySpace.HBM)
           for s in shards]

    @pl.kernel(mesh=plsc.ScalarSubcoreMesh(axis_name="sequencer", num_cores=1), name=name,
               scratch_types=(pltpu.SemaphoreType.DMA((n, 7)), pltpu.SemaphoreType.DMA((n, 7)),
                              pltpu.SemaphoreType.DMA((n,))),
               compiler_params=pltpu.CompilerParams(collective_id=collective_id))
    def launch(send_sems, recv_sems, local_sems):
        x, y, c, chips = _place()
        me, sibling = (x, y, c), (x, y, 1 - c)
        barrier = pltpu.get_barrier_semaphore()
        for peer in [sibling] + [(*chip, pc) for chip in chips for pc in (c, 1 - c)]:
            pl.semaphore_signal(barrier, inc=1, device_id=peer, device_id_type=MESH)
        pl.semaphore_wait(barrier, N_DEV - 1)

        def idx(px, py, pc):
            return 4 * px + 2 * py + pc

        def copy(t, k, block, to, own=False):
            rows = dst[t].at[idx(*block)]
            return pltpu.make_async_remote_copy(
                src_ref=src[t] if own else rows, dst_ref=rows,
                send_sem=send_sems.at[t, k], recv_sem=recv_sems.at[t, k],
                device_id=to, device_id_type=MESH)

        local = [pltpu.make_async_copy(src[t], dst[t].at[idx(*me)], local_sems.at[t]) for t in range(n)]
        for cp in local:
            cp.start()
        first = []
        for t in range(n):
            first.append(copy(t, 0, me, sibling, own=True))
            first += [copy(t, 1 + j, me, (*chip, c), own=True) for j, chip in enumerate(chips)]
        for cp in first:
            cp.start()
        passed = []
        for j, chip in enumerate(chips):
            for t in range(n):
                copy(t, 1 + j, (*chip, c), me).wait_recv()
                cp = copy(t, 4 + j, (*chip, c), sibling)
                cp.start()
                passed.append(cp)
        for t in range(n):
            copy(t, 0, sibling, me).wait_recv()
        for j, chip in enumerate(chips):
            for t in range(n):
                copy(t, 4 + j, (*chip, 1 - c), me).wait_recv()
        for cp in first + passed:
            cp.wait_send()
        for cp in local:
            cp.wait()

    launch()
    return [d[...] for d in dst]


def _pair_exchange(grads, name):
    n = len(grads)

    def body(*refs):
        ins, theirs = refs[:n], refs[n:2 * n]
        send_sems, recv_sems = refs[2 * n:]
        x, y, c, _ = _place()
        copies = []
        for t in range(n):
            for q in range(4):
                give = pltpu.make_async_remote_copy(
                    src_ref=ins[t].at[2 * q + 1 - c], dst_ref=theirs[t].at[q],
                    send_sem=send_sems.at[t, q], recv_sem=recv_sems.at[t, q],
                    device_id=(x, y, 1 - c), device_id_type=MESH)
                give.start()
                copies.append(give)
        for cp in copies:
            cp.wait()

    return pl.pallas_call(
        body, name=name, out_shape=[jax.ShapeDtypeStruct((4,) + g.shape[1:], g.dtype) for g in grads],
        in_specs=[ANY] * n, out_specs=[ANY] * n,
        scratch_shapes=[pltpu.SemaphoreType.DMA((n, 4)), pltpu.SemaphoreType.DMA((n, 4))],
    )(*grads)


def _chip_exchange_beside(parts, name, collective_id):
    n = len(parts)
    src = [jax.new_ref(p, memory_space=pltpu.MemorySpace.HBM) for p in parts]
    dst = [jax.empty_ref(jax.ShapeDtypeStruct(p.shape, p.dtype), memory_space=pltpu.MemorySpace.HBM) for p in parts]

    @pl.kernel(mesh=plsc.ScalarSubcoreMesh(axis_name="sequencer", num_cores=1), name=name,
               scratch_types=(pltpu.SemaphoreType.DMA((n, 3)), pltpu.SemaphoreType.DMA((n, 3)),
                              pltpu.SemaphoreType.DMA((n,))),
               compiler_params=pltpu.CompilerParams(collective_id=collective_id))
    def launch(send_sems, recv_sems, local_sems):
        x, y, c, chips = _place()
        barrier = pltpu.get_barrier_semaphore()
        for px, py in chips:
            pl.semaphore_signal(barrier, inc=1, device_id=(px, py, c), device_id_type=MESH)
        pl.semaphore_wait(barrier, len(chips))
        copies = []
        for t in range(n):
            own = pltpu.make_async_copy(src[t].at[2 * x + y], dst[t].at[0], local_sems.at[t])
            own.start()
            copies.append(own)
            for j, (px, py) in enumerate(chips):
                cp = pltpu.make_async_remote_copy(
                    src_ref=src[t].at[2 * px + py], dst_ref=dst[t].at[1 + j],
                    send_sem=send_sems.at[t, j], recv_sem=recv_sems.at[t, j],
                    device_id=(px, py, c), device_id_type=MESH)
                cp.start()
                copies.append(cp)
        for cp in copies:
            cp.wait()

    launch()
    return [d[...] for d in dst]


_DOT_DIMS = {"nn": (((1,), (0,)), ((), ())), "nt": (((1,), (1,)), ((), ())), "tn": (((0,), (0,)), ((), ()))}


def _mm(a, b, mode, out_dtype, name, residual=None, tm=1024, tn=1024, tk=2048):
    if mode == "tn":
        (k_dim, m_dim), n_dim = a.shape, b.shape[1]
    else:
        (m_dim, k_dim), n_dim = a.shape, (b.shape[1] if mode == "nn" else b.shape[0])
    tm, tn, tk = _tile(m_dim, tm), _tile(n_dim, tn), _tile(k_dim, tk)
    nk = k_dim // tk

    def body(*refs):
        a_ref, b_ref = refs[:2]
        r_ref = refs[2] if residual is not None else None
        o_ref = refs[2 if residual is None else 3]

        def finish(r):
            if r_ref is not None:
                r = r + r_ref[...].astype(F32)
            o_ref[...] = r.astype(o_ref.dtype)

        part = lax.dot_general(a_ref[...].astype(BF16), b_ref[...].astype(BF16), _DOT_DIMS[mode],
                               preferred_element_type=F32)
        if nk == 1:
            finish(part)
            return
        acc, k = refs[-1], pl.program_id(2)

        @pl.when(k == 0)
        def _():
            acc[...] = part

        @pl.when((k > 0) & (k < nk - 1))
        def _():
            acc[...] += part

        @pl.when(k == nk - 1)
        def _():
            finish(acc[...] + part)

    a_spec = pl.BlockSpec((tk, tm), lambda i, j, k: (k, i)) if mode == "tn" else pl.BlockSpec((tm, tk), lambda i, j, k: (i, k))
    b_spec = pl.BlockSpec((tn, tk), lambda i, j, k: (j, k)) if mode == "nt" else pl.BlockSpec((tk, tn), lambda i, j, k: (k, j))
    o_spec = pl.BlockSpec((tm, tn), lambda i, j, k: (i, j))
    ins, specs = [a, b], [a_spec, b_spec]
    if residual is not None:
        ins.append(residual)
        specs.append(o_spec)
    return pl.pallas_call(
        body, name=name, grid=(m_dim // tm, n_dim // tn, nk),
        out_shape=jax.ShapeDtypeStruct((m_dim, n_dim), out_dtype),
        in_specs=specs, out_specs=o_spec,
        scratch_shapes=[pltpu.VMEM((tm, tn), F32)] if nk > 1 else [],
        compiler_params=_params("parallel", "parallel", "arbitrary"),
    )(*ins)


def _rows(body, name, row_ins, full_ins, row_outs, acc_outs, ts=ROW_TILE):
    s = row_ins[0].shape[0]
    ts = _tile(s, ts, 16)
    n_ri, n_fi, n_ro = len(row_ins), len(full_ins), len(row_outs)

    def kernel(*refs):
        i = pl.program_id(0)
        ri, fi = refs[:n_ri], refs[n_ri:n_ri + n_fi]
        ro, ao = refs[n_ri + n_fi:n_ri + n_fi + n_ro], refs[n_ri + n_fi + n_ro:]

        if ao:
            @pl.when(i == 0)
            def _():
                for r in ao:
                    r[...] = jnp.zeros_like(r)

        body(i, ri, fi, ro, ao)

    def row_spec(cols):
        return pl.BlockSpec((ts, cols), lambda i: (i, 0))

    def full_spec(shape):
        return pl.BlockSpec(shape, lambda i: (0,) * len(shape))

    return pl.pallas_call(
        kernel, name=name, grid=(s // ts,),
        out_shape=[jax.ShapeDtypeStruct((s, c), d) for c, d in row_outs]
        + [jax.ShapeDtypeStruct(sh, d) for sh, d in acc_outs],
        in_specs=[row_spec(a.shape[1]) for a in row_ins] + [full_spec(a.shape) for a in full_ins],
        out_specs=[row_spec(c) for c, _ in row_outs] + [full_spec(sh) for sh, _ in acc_outs],
        compiler_params=_params("arbitrary"),
    )(*row_ins, *full_ins)


def _rms_stats(x):
    return lax.rsqrt(jnp.mean(x * x, axis=-1, keepdims=True) + EPS)


def _rms_fwd(x, g, name):
    def body(i, ri, fi, ro, ao):
        xv = ri[0][...]
        ro[0][...] = (xv * _rms_stats(xv) * fi[0][...]).astype(BF16)

    return _rows(body, name, [x], [g], [(x.shape[1], BF16)], [])[0]


def _rms_bwd_rows(xv, gv, dh):
    r = _rms_stats(xv)
    xhat = xv * r
    dxhat = dh * gv
    dx = r * (dxhat - xhat * jnp.mean(dxhat * xhat, axis=-1, keepdims=True))
    return dx, jnp.sum(dh * xhat, axis=0, keepdims=True)


def _rms_bwd(x, g, dh, dres, name):
    d = x.shape[1]

    def body(i, ri, fi, ro, ao):
        dx, dg = _rms_bwd_rows(ri[0][...], fi[0][...], ri[1][...])
        dx = dx + ri[2][...]
        ro[0][...] = dx
        ro[1][...] = dx.astype(BF16)
        ao[0][...] += dg

    return _rows(body, name, [x, dh, dres], [g], [(d, F32), (d, BF16)], [((1, d), F32)])


def _loss_head(x, g, target, name):
    d = x.shape[1]

    def body(i, ri, fi, ro, ao):
        xv, gv = ri[0][...], fi[0][...]
        y = xv * _rms_stats(xv) * gv
        err = y - ri[1][...]
        ao[1][...] += 0.5 * jnp.sum(jnp.mean(err * err, axis=-1, keepdims=True))
        dx, dg = _rms_bwd_rows(xv, gv, err * (1.0 / d))
        ro[0][...] = dx
        ao[0][...] += dg

    return _rows(body, name, [x, target], [g], [(d, F32)], [((1, d), F32), ((8, LANES), F32)])


def _merge_fwd(gg, y_conv, y_attn, name):
    d = y_conv.shape[1]

    def body(i, ri, fi, ro, ao):
        gv = ri[0][...]
        m = _sigmoid(gv[:, :d]) * ri[1][...] + _sigmoid(gv[:, d:]) * ri[2][...]
        ro[0][...] = m.astype(BF16)

    return _rows(body, name, [gg, y_conv, y_attn], [], [(d, BF16)], [])[0]


def _merge_bwd(gg, y_conv, y_attn, dm, name):
    d = y_conv.shape[1]

    def body(i, ri, fi, ro, ao):
        gv, yc, ya, dmv = ri[0][...], ri[1][...], ri[2][...], ri[3][...]
        sc, sa = _sigmoid(gv[:, :d]), _sigmoid(gv[:, d:])
        ro[0][...] = (dmv * sc).astype(BF16)
        ro[1][...] = (dmv * sa).astype(BF16)
        ro[2][:, :d] = (dmv * yc * sc * (1.0 - sc)).astype(BF16)
        ro[2][:, d:] = (dmv * ya * sa * (1.0 - sa)).astype(BF16)

    return _rows(body, name, [gg, y_conv, y_attn, dm], [], [(d, BF16), (d, BF16), (2 * d, BF16)], [])


def _swiglu_fwd(gu, name):
    f = gu.shape[1] // 2

    def body(i, ri, fi, ro, ao):
        v = ri[0][...]
        gate, up = v[:, :f], v[:, f:]
        ro[0][...] = (gate * _sigmoid(gate) * up).astype(BF16)

    return _rows(body, name, [gu], [], [(f, BF16)], [], ts=128)[0]


def _swiglu_bwd(gu, dact, name):
    f = gu.shape[1] // 2

    def body(i, ri, fi, ro, ao):
        v, dv = ri[0][...], ri[1][...]
        gate, up = v[:, :f], v[:, f:]
        sg = _sigmoid(gate)
        ro[0][:, :f] = (dv * up * sg * (1.0 + gate * (1.0 - sg))).astype(BF16)
        ro[0][:, f:] = (dv * gate * sg).astype(BF16)

    return _rows(body, name, [gu, dact], [], [(2 * f, BF16)], [], ts=128)[0]


def _ple_fwd(x, z, pp, name):
    d = x.shape[1]

    def body(i, ri, fi, ro, ao):
        ro[0][...] = ri[0][...] + _sigmoid(ri[1][...]) * ri[2][...]

    return _rows(body, name, [x, z, pp], [], [(d, F32)], [])[0]


def _ple_bwd(dx, z, pp, name):
    d = dx.shape[1]

    def body(i, ri, fi, ro, ao):
        dv, sz = ri[0][...], _sigmoid(ri[1][...])
        ro[0][...] = (dv * ri[2][...] * sz * (1.0 - sz)).astype(BF16)
        ro[1][...] = (dv * sz).astype(BF16)

    return _rows(body, name, [dx, z, pp], [], [(d, BF16), (d, BF16)], [])


def _triangle(n, lower):
    r = lax.broadcasted_iota(jnp.int32, (n, n), 0)
    c = lax.broadcasted_iota(jnp.int32, (n, n), 1)
    return jnp.where((c <= r) if lower else (c >= r), 1.0, 0.0).astype(F32)


def _forget_fwd(f, bias, name):
    s = f.shape[0]
    tb = _tile(s, CUM_TILE, 8)

    def body(f_ref, b_ref, c_ref, carry):
        @pl.when(pl.program_id(0) == 0)
        def _():
            carry[...] = jnp.zeros_like(carry)

        xv = f_ref[...] + b_ref[...]
        logf = jnp.minimum(xv, 0.0) - jnp.log(1.0 + jnp.exp(-jnp.abs(xv)))
        cv = jnp.dot(_triangle(tb, True), logf, precision=lax.Precision.HIGHEST, preferred_element_type=F32) + carry[...]
        c_ref[...] = cv
        carry[...] = cv[tb - 1:tb, :]

    return pl.pallas_call(
        body, name=name, grid=(s // tb,), out_shape=jax.ShapeDtypeStruct((s, LANES), F32),
        in_specs=[pl.BlockSpec((tb, LANES), lambda i: (i, 0)), pl.BlockSpec((1, LANES), lambda i: (0, 0))],
        out_specs=pl.BlockSpec((tb, LANES), lambda i: (i, 0)),
        scratch_shapes=[pltpu.VMEM((1, LANES), F32)],
        compiler_params=_params("arbitrary"),
    )(f, bias)


def _forget_bwd(f, bias, dc, name):
    s = f.shape[0]
    tb = _tile(s, CUM_TILE, 16)
    nb = s // tb

    def body(f_ref, b_ref, dc_ref, df_ref, db_ref, carry):
        @pl.when(pl.program_id(0) == 0)
        def _():
            carry[...] = jnp.zeros_like(carry)
            db_ref[...] = jnp.zeros_like(db_ref)

        dlog = jnp.dot(_triangle(tb, False), dc_ref[...], precision=lax.Precision.HIGHEST,
                       preferred_element_type=F32) + carry[...]
        carry[...] = dlog[0:1, :]
        dlogit = dlog * (1.0 - _sigmoid(f_ref[...] + b_ref[...]))
        df_ref[...] = dlogit.astype(BF16)
        db_ref[...] += jnp.sum(dlogit, axis=0, keepdims=True)

    back = lambda i: (nb - 1 - i, 0)
    return pl.pallas_call(
        body, name=name, grid=(nb,),
        out_shape=[jax.ShapeDtypeStruct((s, LANES), BF16), jax.ShapeDtypeStruct((1, LANES), F32)],
        in_specs=[pl.BlockSpec((tb, LANES), back), pl.BlockSpec((1, LANES), lambda i: (0, 0)),
                  pl.BlockSpec((tb, LANES), back)],
        out_specs=[pl.BlockSpec((tb, LANES), back), pl.BlockSpec((1, LANES), lambda i: (0, 0))],
        scratch_shapes=[pltpu.VMEM((1, LANES), F32)],
        compiler_params=_params("arbitrary"),
    )(f, bias, dc)


def _glu(block, ch):
    return block[:, :ch] * _sigmoid(block[:, ch:])


def _layernorm_rows(u1, g, b):
    mu = jnp.mean(u1, axis=-1, keepdims=True)
    cen = u1 - mu
    rstd = lax.rsqrt(jnp.mean(cen * cen, axis=-1, keepdims=True) + EPS)
    xhat = cen * rstd
    return xhat, rstd, xhat * g + b


def _conv_fwd(ag, conv_w, conv_b, ln_g, ln_b, name):
    s, ch = ag.shape[0], ag.shape[1] // 2
    ts = _tile(s, ROW_TILE, HALO)
    per = ts // HALO

    def body(cur_ref, halo_ref, w_ref, cb_ref, g_ref, b_ref, u1_ref, u3_ref, cat):
        i = pl.program_id(0)
        cat[HALO:, :] = _glu(cur_ref[...], ch)
        cat[:HALO, :] = jnp.where(i == 0, 0.0, _glu(halo_ref[...], ch))
        full = cat[...]
        acc = jnp.zeros((ts, ch), F32) + cb_ref[...]
        for j in range(CONV_K):
            shifted = full if j == CONV_K - 1 else pltpu.roll(full, CONV_K - 1 - j, 0)
            acc = acc + w_ref[j:j + 1, :] * shifted[HALO:, :]
        u1_ref[...] = acc
        _, _, u2 = _layernorm_rows(acc, g_ref[...], b_ref[...])
        u3_ref[...] = (u2 * _sigmoid(u2)).astype(BF16)

    whole = lambda shape: pl.BlockSpec(shape, lambda i: (0, 0))
    return pl.pallas_call(
        body, name=name, grid=(s // ts,),
        out_shape=[jax.ShapeDtypeStruct((s, ch), F32), jax.ShapeDtypeStruct((s, ch), BF16)],
        in_specs=[pl.BlockSpec((ts, 2 * ch), lambda i: (i, 0)),
                  pl.BlockSpec((HALO, 2 * ch), lambda i: (jnp.maximum(i * per - 1, 0), 0)),
                  whole(conv_w.shape), whole((1, ch)), whole((1, ch)), whole((1, ch))],
        out_specs=[pl.BlockSpec((ts, ch), lambda i: (i, 0)), pl.BlockSpec((ts, ch), lambda i: (i, 0))],
        scratch_shapes=[pltpu.VMEM((HALO + ts, ch), F32)],
        compiler_params=_params("arbitrary"),
    )(ag, ag, conv_w, conv_b, ln_g, ln_b)


def _conv_norm_bwd(u1, du3, ln_g, ln_b, name):
    ch = u1.shape[1]

    def body(i, ri, fi, ro, ao):
        g = fi[0][...]
        xhat, rstd, u2 = _layernorm_rows(ri[0][...], g, fi[1][...])
        sg = _sigmoid(u2)
        du2 = ri[1][...] * sg * (1.0 + u2 * (1.0 - sg))
        dxhat = du2 * g
        du1 = rstd * (dxhat - jnp.mean(dxhat, axis=-1, keepdims=True)
                      - xhat * jnp.mean(dxhat * xhat, axis=-1, keepdims=True))
        ro[0][...] = du1
        ao[0][...] += jnp.sum(du2 * xhat, axis=0, keepdims=True)
        ao[1][...] += jnp.sum(du2, axis=0, keepdims=True)
        ao[2][...] += jnp.sum(du1, axis=0, keepdims=True)

    return _rows(body, name, [u1, du3], [ln_g, ln_b], [(ch, F32)], [((1, ch), F32)] * 3)


def _conv_bwd(ag, du1, conv_w, name):
    s, ch = du1.shape
    ts = _tile(s, ROW_TILE, HALO)
    per = ts // HALO
    last = s // HALO - 1
    nb = s // ts

    def body(cur_ref, halo_ref, d_ref, dnext_ref, w_ref, dag_ref, dw_ref, cat, dcat):
        i = pl.program_id(0)

        @pl.when(i == 0)
        def _():
            dw_ref[...] = jnp.zeros_like(dw_ref)

        cur = cur_ref[...]
        cat[HALO:, :] = _glu(cur, ch)
        cat[:HALO, :] = jnp.where(i == 0, 0.0, _glu(halo_ref[...], ch))
        dcur = d_ref[...]
        dcat[:ts, :] = dcur
        dcat[ts:, :] = jnp.where(i == nb - 1, 0.0, dnext_ref[...])
        full, dfull = cat[...], dcat[...]
        du0 = jnp.zeros((ts, ch), F32)
        for j in range(CONV_K):
            back = CONV_K - 1 - j
            dshift = dfull if back == 0 else pltpu.roll(dfull, ts + HALO - back, 0)
            du0 = du0 + w_ref[j:j + 1, :] * dshift[:ts, :]
            ushift = full if back == 0 else pltpu.roll(full, back, 0)
            dw_ref[j:j + 1, :] += jnp.sum(dcur * ushift[HALO:, :], axis=0, keepdims=True)
        a, sg = cur[:, :ch], _sigmoid(cur[:, ch:])
        dag_ref[:, :ch] = (du0 * sg).astype(BF16)
        dag_ref[:, ch:] = (du0 * a * sg * (1.0 - sg)).astype(BF16)

    return pl.pallas_call(
        body, name=name, grid=(nb,),
        out_shape=[jax.ShapeDtypeStruct((s, 2 * ch), BF16), jax.ShapeDtypeStruct(conv_w.shape, F32)],
        in_specs=[pl.BlockSpec((ts, 2 * ch), lambda i: (i, 0)),
                  pl.BlockSpec((HALO, 2 * ch), lambda i: (jnp.maximum(i * per - 1, 0), 0)),
                  pl.BlockSpec((ts, ch), lambda i: (i, 0)),
                  pl.BlockSpec((HALO, ch), lambda i: (jnp.minimum((i + 1) * per, last), 0)),
                  pl.BlockSpec(conv_w.shape, lambda i: (0, 0))],
        out_specs=[pl.BlockSpec((ts, 2 * ch), lambda i: (i, 0)), pl.BlockSpec(conv_w.shape, lambda i: (0, 0))],
        scratch_shapes=[pltpu.VMEM((HALO + ts, ch), F32), pltpu.VMEM((ts + HALO, ch), F32)],
        compiler_params=_params("arbitrary"),
    )(ag, ag, du1, du1, conv_w)


_SCALE = 1.0 / math.sqrt(HEAD_DIM)


STAT_LANES = 8


def _causal(rows, cols, row0):
    r = row0 + lax.broadcasted_iota(jnp.int32, (rows, cols), 0)
    return lax.broadcasted_iota(jnp.int32, (rows, cols), 1) <= r


def _attn_fwd(qkv, c_row, name):
    s = qkv.shape[0]
    t = _tile(s, ATTN_FWD_TILE)
    nt, part_rows = s // t, min(ATTN_ROWS, t)
    n_parts = t // part_rows
    h_, d = N_HEADS, HEAD_DIM

    def body(q_ref, k_ref, v_ref, cr_ref, o_ref, lse_ref, *scratch):
        m_sc, l_sc, acc = scratch[:n_parts], scratch[n_parts:2 * n_parts], scratch[2 * n_parts:]
        i = pl.program_id(1)
        for part in range(n_parts):
            m_sc[part][...] = jnp.full_like(m_sc[part], NEG_INF)
            l_sc[part][...] = jnp.zeros_like(l_sc[part])
            acc[part][...] = jnp.zeros_like(acc[part])

        def block(j, diagonal):
            at = pl.multiple_of(j * t, t)
            k, v = k_ref[pl.ds(at, t), :], v_ref[pl.ds(at, t), :]
            cr = cr_ref[0, pl.ds(j, 1), :]
            for part in range(n_parts):
                rows = pl.ds(part * part_rows, part_rows)
                sc = lax.dot_general(q_ref[rows, :], k, _DOT_DIMS["nt"], preferred_element_type=F32) * _SCALE - cr
                if diagonal:
                    sc = jnp.where(_causal(part_rows, t, part * part_rows), sc, NEG_INF)
                m_old = m_sc[part][...]
                m_new = jnp.maximum(m_old, jnp.max(sc, axis=-1, keepdims=True))
                alpha = jnp.exp(m_old - m_new)
                p = jnp.exp(sc - m_new)
                l_sc[part][...] = alpha * l_sc[part][...] + jnp.sum(p, axis=-1, keepdims=True)
                acc[part][...] = alpha * acc[part][...] + jnp.dot(p.astype(BF16), v, preferred_element_type=F32)
                m_sc[part][...] = m_new

        def below(j, carry):
            block(j, False)
            return carry

        lax.fori_loop(0, i, below, 0)
        block(i, True)
        for part in range(n_parts):
            rows = pl.ds(part * part_rows, part_rows)
            o_ref[rows, :] = (acc[part][...] / l_sc[part][...]).astype(BF16)
            lse_ref[0, rows, :] = m_sc[part][...] + jnp.log(l_sc[part][...])

    return pl.pallas_call(
        body, name=name, grid=(h_, nt),
        out_shape=[jax.ShapeDtypeStruct((s, h_ * d), BF16), jax.ShapeDtypeStruct((h_, s, 1), F32)],
        in_specs=[pl.BlockSpec((t, d), lambda h, i: (i, h)),
                  pl.BlockSpec((s, d), lambda h, i: (0, h_ + h)),
                  pl.BlockSpec((s, d), lambda h, i: (0, 2 * h_ + h)),
                  pl.BlockSpec((1, nt, t), lambda h, i: (h, 0, 0))],
        out_specs=[pl.BlockSpec((t, d), lambda h, i: (i, h)), pl.BlockSpec((1, t, 1), lambda h, i: (h, i, 0))],
        scratch_shapes=[pltpu.VMEM((part_rows, 1), F32)] * (2 * n_parts) + [pltpu.VMEM((part_rows, d), F32)] * n_parts,
        compiler_params=_params("parallel", "arbitrary"),
    )(qkv, qkv, qkv, c_row.reshape(h_, nt, t))


def _attn_stats(o, do, lse, name):
    s = o.shape[0]
    t = _tile(s, ATTN_TILE)
    h_, d = N_HEADS, HEAD_DIM

    def body(o_ref, do_ref, lse_ref, st_ref, dob_ref):
        dov = do_ref[...]
        st_ref[...] = jnp.zeros_like(st_ref)
        st_ref[0, :, 0:1] = -lse_ref[0]
        st_ref[0, :, 1:2] = jnp.sum(o_ref[...].astype(F32) * dov, axis=-1, keepdims=True)
        dob_ref[...] = dov.astype(BF16)

    col = pl.BlockSpec((1, t, 1), lambda h, i: (h, i, 0))
    blk = pl.BlockSpec((t, d), lambda h, i: (i, h))
    return pl.pallas_call(
        body, name=name, grid=(h_, s // t),
        out_shape=[jax.ShapeDtypeStruct((h_, s, STAT_LANES), F32), jax.ShapeDtypeStruct(o.shape, BF16)],
        in_specs=[blk, blk, col],
        out_specs=[pl.BlockSpec((1, t, STAT_LANES), lambda h, i: (h, i, 0)), blk],
        compiler_params=_params("parallel", "parallel"),
    )(o, do, lse)


def _attn_bwd(qkv, do, stats, c_row, name):
    s = qkv.shape[0]
    tq = _tile(s, ATTN_TILE)
    tk = _tile(s, ATTN_BWD_KEYS, tq)
    nq, nk, per = s // tq, s // tk, tk // tq
    h_, d = N_HEADS, HEAD_DIM

    def body(q_ref, do_ref, k_ref, v_ref, st_ref, cr_ref, dq_ref, dk_ref, dv_ref, dcr_ref, dcc_ref,
             dq_acc, dk_acc, dv_acc, dcr_acc):
        j = pl.program_id(1)

        @pl.when(j == 0)
        def _():
            dq_acc[...] = jnp.zeros_like(dq_acc)
            dcc_ref[...] = jnp.zeros_like(dcc_ref)

        dk_acc[...] = jnp.zeros_like(dk_acc)
        dv_acc[...] = jnp.zeros_like(dv_acc)
        dcr_acc[...] = jnp.zeros_like(dcr_acc)
        k, v, cr = k_ref[...], v_ref[...], cr_ref[0]

        def block(i, row0):
            rows = pl.ds(pl.multiple_of(i * tq, tq), tq)
            q, dov, st = q_ref[rows, :], do_ref[rows, :], st_ref[0, rows, :]
            sc = lax.dot_general(q, k, _DOT_DIMS["nt"], preferred_element_type=F32) * _SCALE + (st[:, 0:1] - cr)
            p = jnp.exp(sc)
            if row0 is not None:
                p = jnp.where(_causal(tq, tk, row0), p, 0.0)
            dp = lax.dot_general(dov, v, _DOT_DIMS["nt"], preferred_element_type=F32)
            ds = p * (dp - st[:, 1:2])
            ds_b = ds.astype(BF16)
            dv_acc[...] += lax.dot_general(p.astype(BF16), dov, _DOT_DIMS["tn"], preferred_element_type=F32)
            dk_acc[...] += lax.dot_general(ds_b, q, _DOT_DIMS["tn"], preferred_element_type=F32)
            dq_acc[rows, :] += jnp.dot(ds_b, k, preferred_element_type=F32)
            dcr_acc[...] -= jnp.sum(ds, axis=0, keepdims=True)
            dcc_ref[0, rows, :] += jnp.sum(ds, axis=-1, keepdims=True)

        def below(i, carry):
            block(i, None)
            return carry

        for part in range(per):
            block(j * per + part, part * tq)
        lax.fori_loop((j + 1) * per, nq, below, 0)
        dk_ref[...] = (dk_acc[...] * _SCALE).astype(BF16)
        dv_ref[...] = dv_acc[...].astype(BF16)
        dcr_ref[0] = dcr_acc[...]

        @pl.when(j == nk - 1)
        def _():
            dq_ref[...] = (dq_acc[...] * _SCALE).astype(BF16)

    head = pl.BlockSpec((s, d), lambda h, j: (0, h))
    return pl.pallas_call(
        body, name=name, grid=(h_, nk),
        out_shape=[jax.ShapeDtypeStruct((s, h_ * d), BF16)] * 3
        + [jax.ShapeDtypeStruct((h_, 1, s), F32), jax.ShapeDtypeStruct((h_, s, 1), F32)],
        in_specs=[head, head,
                  pl.BlockSpec((tk, d), lambda h, j: (j, h_ + h)),
                  pl.BlockSpec((tk, d), lambda h, j: (j, 2 * h_ + h)),
                  pl.BlockSpec((1, s, STAT_LANES), lambda h, j: (h, 0, 0)),
                  pl.BlockSpec((1, 1, tk), lambda h, j: (h, 0, j))],
        out_specs=[head, pl.BlockSpec((tk, d), lambda h, j: (j, h)), pl.BlockSpec((tk, d), lambda h, j: (j, h)),
                   pl.BlockSpec((1, 1, tk), lambda h, j: (h, 0, j)), pl.BlockSpec((1, s, 1), lambda h, j: (h, 0, 0))],
        scratch_shapes=[pltpu.VMEM((s, d), F32), pltpu.VMEM((tk, d), F32), pltpu.VMEM((tk, d), F32),
                        pltpu.VMEM((1, tk), F32)],
        compiler_params=_params("parallel", "arbitrary"),
    )(qkv, do, qkv, qkv, stats, c_row)


def _pair_sum(grads, theirs, core, name):
    _, l, r, c = theirs.shape
    tr = _tile(r, 256, 16)

    def body(core_ref, a_ref, b_ref, o_ref):
        o_ref[...] = (a_ref[...].astype(F32) + b_ref[...].astype(F32)).astype(BF16)

    spec = pl.BlockSpec((1, 1, tr, c), lambda q, k, i, core_ref: (q, k, i, 0))
    mine = pl.BlockSpec((1, 1, tr, c), lambda q, k, i, core_ref: (2 * q + core_ref[0], k, i, 0))
    return pl.pallas_call(
        body, name=name, out_shape=jax.ShapeDtypeStruct(theirs.shape, BF16),
        grid_spec=pltpu.PrefetchScalarGridSpec(num_scalar_prefetch=1, grid=(4, l, r // tr),
                                               in_specs=[mine, spec], out_specs=spec),
        compiler_params=_params("parallel", "parallel", "parallel"),
    )(core, grads, theirs)


def _adamw(g, w, m, v):
    m = ADAM_B1 * m + (1.0 - ADAM_B1) * g
    v = ADAM_B2 * v + (1.0 - ADAM_B2) * (g * g)
    m_hat = m / (1.0 - ADAM_B1 ** ADAM_STEP)
    v_hat = v / (1.0 - ADAM_B2 ** ADAM_STEP)
    return -ADAM_LR * (m_hat / (jnp.sqrt(v_hat) + ADAM_EPS) + ADAM_WD * w), m, v


def _sum_adamw(parts, w, m, v, name):
    n_parts, l, r, c = parts.shape
    tr = _tile(r, 128, 16)

    def body(p_ref, w_ref, m_ref, v_ref, g_ref, d_ref, nm_ref, nv_ref):
        g = p_ref[0].astype(F32)
        for k in range(1, n_parts):
            g = g + p_ref[k].astype(F32)
        g_ref[...] = g
        d_ref[...], nm_ref[...], nv_ref[...] = _adamw(g, w_ref[...], m_ref[...], v_ref[...])

    spec = pl.BlockSpec((1, tr, c), lambda k, i: (k, i, 0))
    return pl.pallas_call(
        body, name=name, grid=(l, r // tr), out_shape=[jax.ShapeDtypeStruct(w.shape, F32)] * 4,
        in_specs=[pl.BlockSpec((n_parts, 1, tr, c), lambda k, i: (0, k, i, 0)), spec, spec, spec],
        out_specs=[spec] * 4, compiler_params=_params("parallel", "parallel"),
    )(parts, w, m, v)


def _unshard_cols(gathered):
    return jnp.transpose(gathered, (1, 0, 2)).reshape(gathered.shape[1], -1)


def _shard_cols(full):
    k = full.shape[0]
    return jnp.transpose(full.reshape(k, N_DEV, -1), (1, 0, 2))


def _in_weights(w, d_model):
    hd = N_HEADS * HEAD_DIM
    w_in = _unshard_cols(w["w_in"])
    f0 = d_model + 3 * hd
    return {
        "w_ag": w_in[:, :d_model], "w_qkv": w_in[:, d_model:f0],
        "w_f": jnp.pad(w_in[:, f0:f0 + N_HEADS], ((0, 0), (0, LANES - N_HEADS))),
        "w_gg": w_in[:, f0 + N_HEADS:],
        "conv_w": jnp.transpose(w["conv_w"], (1, 0, 2)).reshape(CONV_K, -1),
    }


def _rest_weights(w, d_model):
    return {
        "w_conv_out": _unshard_cols(w["w_conv_out"]), "w_attn_out": _unshard_cols(w["w_attn_out"]),
        "w_out": w["w_out"].reshape(-1, d_model), "w_gate_up": _unshard_cols(w["w_gate_up"]),
        "w_down": w["w_down"].reshape(-1, d_model), "w_ple_gate": w["w_ple_gate"].reshape(-1, d_model),
        "w_ple_proj": _unshard_cols(w["w_ple_proj"]),
    }


def _pad_lanes(vec):
    return jnp.pad(vec, (0, LANES - vec.shape[0]))[None, :]


def _layer_fwd(x, p, shards, small, tag, after_mixers=None):
    d_model = x.shape[1]
    w = _in_weights(shards, d_model)
    h = _rms_fwd(x, small["norm_mix_g"], f"rms_mix_{tag}")
    ag = _mm(h, w["w_ag"], "nn", F32, f"proj_glu_{tag}")
    qkv = _mm(h, w["w_qkv"], "nn", BF16, f"proj_qkv_{tag}")
    gg = _mm(h, w["w_gg"], "nn", F32, f"proj_gates_{tag}")
    f = _mm(h, w["w_f"], "nn", F32, f"proj_forget_{tag}")
    c = _forget_fwd(f, small["b_forget"], f"forget_{tag}")
    c_heads = jnp.transpose(c[:, :N_HEADS])
    c_row = c_heads[:, None, :]
    o, lse = _attn_fwd(qkv, c_row, f"attn_{tag}")
    u1, u3 = _conv_fwd(ag, w["conv_w"], small["conv_b"], small["conv_ln_g"], small["conv_ln_b"], f"conv_{tag}")
    rest, (u3, o) = lax.optimization_barrier(({k: shards[k] for k in _BIG if k != "w_in"}, (u3, o)))
    w.update(_rest_weights(rest, d_model))
    y_conv = _mm(u3, w["w_conv_out"], "nn", F32, f"conv_out_{tag}")
    y_attn = _mm(o, w["w_attn_out"], "nn", F32, f"attn_out_{tag}")
    if after_mixers is not None:
        after_mixers(y_attn)
    merged = _merge_fwd(gg, y_conv, y_attn, f"merge_{tag}")
    x1 = _mm(merged, w["w_out"], "nn", F32, f"mix_out_{tag}", residual=x)
    hf = _rms_fwd(x1, small["norm_ffn_g"], f"rms_ffn_{tag}")
    gu = _mm(hf, w["w_gate_up"], "nn", F32, f"ffn_up_{tag}")
    act = _swiglu_fwd(gu, f"swiglu_{tag}")
    x2 = _mm(act, w["w_down"], "nn", F32, f"ffn_down_{tag}", residual=x1)
    hp = _rms_fwd(x2, small["norm_ple_g"], f"rms_ple_{tag}")
    z = _mm(hp, w["w_ple_gate"], "nn", F32, f"ple_gate_{tag}")
    pp = _mm(p, w["w_ple_proj"], "nn", F32, f"ple_proj_{tag}")
    x3 = _ple_fwd(x2, z, pp, f"ple_{tag}")
    saved = dict(x=x, p=p, h=h, ag=ag, qkv=qkv, gg=gg, f=f, u1=u1, u3=u3, y_conv=y_conv, c_row=c_row,
                 o=o, lse=lse, y_attn=y_attn, merged=merged, x1=x1, hf=hf, gu=gu, act=act, x2=x2, hp=hp, z=z, pp=pp)
    return x3, saved, w


def _layer_bwd_ffn(dx3, sv, w, small, tag):
    gw, gs = {}, {}
    dz, dpp = _ple_bwd(dx3, sv["z"], sv["pp"], f"ple_bwd_{tag}")
    gw["w_ple_proj"] = _mm(sv["p"], dpp, "tn", BF16, f"ple_proj_dw_{tag}")
    gw["w_ple_gate"] = _mm(sv["hp"], dz, "tn", BF16, f"ple_gate_dw_{tag}")
    dhp = _mm(dz, w["w_ple_gate"], "nt", F32, f"ple_gate_dx_{tag}")
    dx2, dx2_b, gs["norm_ple_g"] = _rms_bwd(sv["x2"], small["norm_ple_g"], dhp, dx3, f"rms_ple_bwd_{tag}")
    gw["w_down"] = _mm(sv["act"], dx2_b, "tn", BF16, f"ffn_down_dw_{tag}", tm=FFN_N_TILE)
    dact = _mm(dx2_b, w["w_down"], "nt", F32, f"ffn_down_dx_{tag}", tn=FFN_N_TILE)
    dgu = _swiglu_bwd(sv["gu"], dact, f"swiglu_bwd_{tag}")
    gw["w_gate_up"] = _mm(sv["hf"], dgu, "tn", BF16, f"ffn_up_dw_{tag}")
    dhf = _mm(dgu, w["w_gate_up"], "nt", F32, f"ffn_up_dx_{tag}", tk=FFN_K_TILE)
    dx1, dx1_b, gs["norm_ffn_g"] = _rms_bwd(sv["x1"], small["norm_ffn_g"], dhf, dx2, f"rms_ffn_bwd_{tag}")
    for k in ("norm_ple_g", "norm_ffn_g"):
        gs[k] = gs[k][0]
    return dx1, dx1_b, gw, gs


def _layer_bwd_mix(dx1, dx1_b, sv, w, small, tag):
    gw, gs = {}, {}
    gw["w_out"] = _mm(sv["merged"], dx1_b, "tn", BF16, f"mix_out_dw_{tag}")
    dm = _mm(dx1_b, w["w_out"], "nt", F32, f"mix_out_dx_{tag}")
    dyc, dya, dgg = _merge_bwd(sv["gg"], sv["y_conv"], sv["y_attn"], dm, f"merge_bwd_{tag}")
    gw["w_attn_out"] = _mm(sv["o"], dya, "tn", BF16, f"attn_out_dw_{tag}")
    do = _mm(dya, w["w_attn_out"], "nt", F32, f"attn_out_dx_{tag}")
    stats, do_b = _attn_stats(sv["o"], do, sv["lse"], f"attn_stats_{tag}")
    dq, dk, dv, dc_row, dc_col = _attn_bwd(sv["qkv"], do_b, stats, sv["c_row"], f"attn_bwd_{tag}")
    dc = jnp.pad(jnp.transpose(dc_row[:, 0, :] + dc_col[:, :, 0]), ((0, 0), (0, LANES - N_HEADS)))
    df, db = _forget_bwd(sv["f"], small["b_forget"], dc, f"forget_bwd_{tag}")
    gs["b_forget"] = db[0, :N_HEADS]
    gw["w_conv_out"] = _mm(sv["u3"], dyc, "tn", BF16, f"conv_out_dw_{tag}")
    du3 = _mm(dyc, w["w_conv_out"], "nt", F32, f"conv_out_dx_{tag}")
    du1, gs["conv_ln_g"], gs["conv_ln_b"], gs["conv_b"] = _conv_norm_bwd(
        sv["u1"], du3, small["conv_ln_g"], small["conv_ln_b"], f"conv_norm_bwd_{tag}")
    dag, gs["conv_w"] = _conv_bwd(sv["ag"], du1, w["conv_w"], f"conv_bwd_{tag}")
    dqkv = jnp.concatenate([dq, dk, dv], axis=1)
    g_ag = _mm(sv["h"], dag, "tn", BF16, f"proj_glu_dw_{tag}")
    g_qkv = _mm(sv["h"], dqkv, "tn", BF16, f"proj_qkv_dw_{tag}")
    g_f = _mm(sv["h"], df, "tn", BF16, f"proj_forget_dw_{tag}")
    g_gg = _mm(sv["h"], dgg, "tn", BF16, f"proj_gates_dw_{tag}")
    gw["w_in"] = jnp.concatenate([g_ag, g_qkv, g_f[:, :N_HEADS], g_gg], axis=1)
    dh = _mm(df, w["w_f"], "nt", F32, f"proj_forget_dx_{tag}")
    dh = _mm(dag, w["w_ag"], "nt", F32, f"proj_glu_dx_{tag}", residual=dh)
    dh = _mm(dqkv, w["w_qkv"], "nt", F32, f"proj_qkv_dx_{tag}", residual=dh)
    dh = _mm(dgg, w["w_gg"], "nt", F32, f"proj_gates_dx_{tag}", residual=dh)
    dx, _, gs["norm_mix_g"] = _rms_bwd(sv["x"], small["norm_mix_g"], dh, dx1, f"rms_mix_bwd_{tag}")
    for k in ("norm_mix_g", "conv_ln_g", "conv_ln_b", "conv_b"):
        gs[k] = gs[k][0]
    return dx, gw, gs


_COL_SHARDED = ("w_in", "w_conv_out", "w_attn_out", "w_gate_up", "w_ple_proj")
_ROW_SHARDED = ("w_out", "w_down", "w_ple_gate")
_BIG = _COL_SHARDED + _ROW_SHARDED
_SMALL = ("norm_mix_g", "b_forget", "conv_b", "conv_ln_g", "conv_ln_b", "norm_ffn_g", "norm_ple_g")
_ORDER = ("norm_mix_g", "w_in", "b_forget", "conv_w", "conv_b", "conv_ln_g", "conv_ln_b", "w_conv_out", "w_attn_out",
          "w_out", "norm_ffn_g", "w_gate_up", "w_down", "norm_ple_g", "w_ple_gate", "w_ple_proj", "final_g")


def _pack(vectors, rows):
    flat = jnp.concatenate([v.reshape(-1) for v in vectors])
    return jnp.pad(flat, (0, rows * LANES - flat.shape[0])).reshape(1, rows, LANES)


def _unpack(packed, like):
    flat, out, at = packed.reshape(-1), [], 0
    for v in like:
        out.append(flat[at:at + v.size].reshape(v.shape))
        at += v.size
    return out


def kernel(x, p, norm_mix_g, w_in, b_forget, conv_w, conv_b, conv_ln_g, conv_ln_b, w_conv_out, w_attn_out, w_out, norm_ffn_g, w_gate_up, w_down, norm_ple_g, w_ple_gate, w_ple_proj, final_g, loss_target, m_norm_mix_g, m_w_in, m_b_forget, m_conv_w, m_conv_b, m_conv_ln_g, m_conv_ln_b, m_w_conv_out, m_w_attn_out, m_w_out, m_norm_ffn_g, m_w_gate_up, m_w_down, m_norm_ple_g, m_w_ple_gate, m_w_ple_proj, m_final_g, v_norm_mix_g, v_w_in, v_b_forget, v_conv_w, v_conv_b, v_conv_ln_g, v_conv_ln_b, v_w_conv_out, v_w_attn_out, v_w_out, v_norm_ffn_g, v_w_gate_up, v_w_down, v_norm_ple_g, v_w_ple_gate, v_w_ple_proj, v_final_g):
    weights = dict(norm_mix_g=norm_mix_g, w_in=w_in, b_forget=b_forget, conv_w=conv_w, conv_b=conv_b, conv_ln_g=conv_ln_g, conv_ln_b=conv_ln_b, w_conv_out=w_conv_out, w_attn_out=w_attn_out, w_out=w_out, norm_ffn_g=norm_ffn_g, w_gate_up=w_gate_up, w_down=w_down, norm_ple_g=norm_ple_g, w_ple_gate=w_ple_gate, w_ple_proj=w_ple_proj, final_g=final_g)
    mom1 = dict(norm_mix_g=m_norm_mix_g, w_in=m_w_in, b_forget=m_b_forget, conv_w=m_conv_w, conv_b=m_conv_b, conv_ln_g=m_conv_ln_g, conv_ln_b=m_conv_ln_b, w_conv_out=m_w_conv_out, w_attn_out=m_w_attn_out, w_out=m_w_out, norm_ffn_g=m_norm_ffn_g, w_gate_up=m_w_gate_up, w_down=m_w_down, norm_ple_g=m_norm_ple_g, w_ple_gate=m_w_ple_gate, w_ple_proj=m_w_ple_proj, final_g=m_final_g)
    mom2 = dict(norm_mix_g=v_norm_mix_g, w_in=v_w_in, b_forget=v_b_forget, conv_w=v_conv_w, conv_b=v_conv_b, conv_ln_g=v_conv_ln_g, conv_ln_b=v_conv_ln_b, w_conv_out=v_w_conv_out, w_attn_out=v_w_attn_out, w_out=v_w_out, norm_ffn_g=v_norm_ffn_g, w_gate_up=v_w_gate_up, w_down=v_w_down, norm_ple_g=v_norm_ple_g, w_ple_gate=v_w_ple_gate, w_ple_proj=v_w_ple_proj, final_g=v_final_g)
    depth = w_in.shape[0]
    xi, yi, ci = lax.axis_index("x"), lax.axis_index("y"), lax.axis_index("c")
    me = 4 * xi + 2 * yi + ci

    rest_names = [k for k in _BIG if k != "w_in"]
    first = _all_gather([weights["w_in"][0].astype(BF16), conv_w], "gather_weights_in_l0")
    gathered_conv_w = first[1]
    shards, _ = lax.optimization_barrier(([weights[k][0].astype(BF16) for k in rest_names], first[0]))
    rest = _all_gather_beside(shards, "gather_weights_rest_l0", collective_id=0)
    gathered = [dict(zip(rest_names, rest), w_in=first[0], conv_w=gathered_conv_w[:, 0])] + [None] * (depth - 1)

    def gather_layer(l, tie):
        shards, _ = lax.optimization_barrier(([weights[k][l].astype(BF16) for k in _BIG], tie))
        later = _all_gather_beside(shards, f"gather_weights_l{l}", collective_id=l)
        gathered[l] = dict(zip(_BIG, later), conv_w=gathered_conv_w[:, l])

    act = x[0]
    layers = []
    for l in range(depth):
        if l > 0:
            gathered[l], act = lax.optimization_barrier((gathered[l], act))
        small = {k: weights[k][l][None, :] for k in _SMALL if k != "b_forget"}
        small["b_forget"] = _pad_lanes(b_forget[l])
        gather_next = functools.partial(gather_layer, l + 1) if l + 1 < depth else None
        act, saved, w = _layer_fwd(act, p[l, 0], gathered[l], small, f"l{l}", after_mixers=gather_next)
        layers.append((w, small, saved))
    dact, g_final, loss_part = _loss_head(act, final_g[None, :], loss_target[0], "loss_head")
    loss = lax.psum(loss_part[0, 0], ("x", "y", "c"))

    core = ci.reshape(1).astype(jnp.int32)
    ffn_group = ("w_ple_gate", "w_ple_proj", "w_down", "w_gate_up")
    mix_group = ("w_out", "w_attn_out", "w_conv_out", "w_in")

    def reduce_scatter(gw, names, tie, tag, collective_id):
        blocks = [(_shard_cols(gw[k]) if k in _COL_SHARDED else gw[k].reshape((N_DEV, -1) + gw[k].shape[1:]))[:, None]
                  for k in names]
        theirs = _pair_exchange(blocks, f"grads_pair_exchange_{tag}")
        chip_parts = [_pair_sum(a, b, core, f"grads_pair_sum_{k}_{tag}") for k, a, b in zip(names, blocks, theirs)]
        chip_parts, tie = lax.optimization_barrier((chip_parts, tie))
        return _chip_exchange_beside(chip_parts, f"grads_chip_exchange_{tag}", collective_id), tie

    arrived = {}
    small_grads = {k: [None] * depth for k in _SMALL + ("conv_w",)}
    waiting = None
    for l in reversed(range(depth)):
        w, small, saved = layers[l]
        dx1, dx1_b, gw, gs = _layer_bwd_ffn(dact, saved, w, small, f"l{l}")
        if waiting is not None:
            arrived[waiting], dx1 = lax.optimization_barrier((arrived[waiting], dx1))
        arrived[l, 0], dx1 = reduce_scatter(gw, ffn_group, dx1, f"ffn_l{l}", depth + 2 * l)
        dact, gw, gs_mix = _layer_bwd_mix(dx1, dx1_b, saved, w, small, f"l{l}")
        arrived[l, 0], dact = lax.optimization_barrier((arrived[l, 0], dact))
        arrived[l, 1], dact = reduce_scatter(gw, mix_group, dact, f"mix_l{l}", depth + 2 * l + 1)
        waiting = (l, 1)
        for k, g in {**gs, **gs_mix}.items():
            small_grads[k][l] = g
    grad_x = dact[None]

    out = {}
    for group, names in enumerate((ffn_group, mix_group)):
        for i, k in enumerate(names):
            parts = jnp.concatenate([arrived[l, group][i] for l in range(depth)], axis=1)
            out[k] = _sum_adamw(parts, weights[k], mom1[k], mom2[k], f"adamw_{k}")

    small_names = list(_SMALL) + ["conv_w", "final_g"]
    small_list = [jnp.stack(small_grads[k]) for k in _SMALL + ("conv_w",)] + [g_final[0]]
    n_small = sum(v.size for v in small_list)
    rows = -(-n_small // (8 * LANES)) * 8
    all_parts = _all_gather([_pack(small_list, rows)], "gather_small_grads")[0]
    full_conv_w = jnp.transpose(gathered_conv_w, (1, 2, 0, 3)).reshape(depth, CONV_K, -1)
    full_conv_m = jnp.zeros_like(full_conv_w)
    ch_shard = conv_w.shape[-1]
    at = (0, 0, me * ch_shard)
    small_w = [weights[k] for k in _SMALL] + [full_conv_w, final_g]
    small_m = [mom1[k] for k in _SMALL] + [lax.dynamic_update_slice(full_conv_m, mom1["conv_w"], at), mom1["final_g"]]
    small_v = [mom2[k] for k in _SMALL] + [lax.dynamic_update_slice(full_conv_m + 1.0, mom2["conv_w"], at), mom2["final_g"]]
    res = _sum_adamw(all_parts, _pack(small_w, rows), _pack(small_m, rows), _pack(small_v, rows), "adamw_small")
    unpacked = [_unpack(r, small_list) for r in res]
    for idx, k in enumerate(small_names):
        vals = [u[idx] for u in unpacked]
        if k == "conv_w":
            vals = [lax.dynamic_slice(v, at, conv_w.shape) for v in vals]
        out[k] = vals

    return (loss, grad_x, *[out[k][0] for k in _ORDER], *[out[k][1] for k in _ORDER],
            *[out[k][2] for k in _ORDER], *[out[k][3] for k in _ORDER])
```

```python
import functools
import math

import jax
import jax.numpy as jnp
from jax import lax
from jax.experimental import pallas as pl
from jax.experimental.pallas import tpu as pltpu
from jax.experimental.pallas import tpu_sc as plsc

F32 = jnp.float32
BF16 = jnp.bfloat16
MESH = pl.DeviceIdType.MESH
ANY = pl.BlockSpec(memory_space=pl.ANY)

N_DEV = 8
N_HEADS = 8
HEAD_DIM = 128
CONV_K = 31
HALO = 32
EPS = 1e-6
NEG_INF = -1e30
ADAM_LR = 0.001
ADAM_B1 = 0.9
ADAM_B2 = 0.999
ADAM_EPS = 1e-08
ADAM_WD = 0.01
ADAM_STEP = 10

V7X_VMEM_BYTES = 64 * 1024 * 1024
VMEM_LIMIT = V7X_VMEM_BYTES * 3 // 4
LANES = 128
ROW_TILE = 256
ATTN_TILE = 512
ATTN_BWD_KEYS = 1024
ATTN_FWD_TILE = 1024
ATTN_ROWS = 256
CUM_TILE = 256
FFN_K_TILE = 2816
FFN_N_TILE = 1408


def _params(*sem):
    return pltpu.CompilerParams(dimension_semantics=sem, vmem_limit_bytes=VMEM_LIMIT)


def _tile(dim, pref, mult=LANES):
    t = min(pref, dim)
    t -= t % mult
    while t >= mult:
        if dim % t == 0:
            return t
        t -= mult
    return dim


def _sigmoid(x):
    return 1.0 / (1.0 + jnp.exp(-x))


def _place():
    x, y, c = lax.axis_index("x"), lax.axis_index("y"), lax.axis_index("c")
    chips = [(1 - x, y), (x, 1 - y), (1 - x, 1 - y)]
    return x, y, c, chips


def _all_gather(shards, name):
    n = len(shards)

    def body(*refs):
        ins, outs = refs[:n], refs[n:2 * n]
        send_sems, recv_sems, local_sems = refs[2 * n:]
        x, y, c, chips = _place()
        me, sibling = (x, y, c), (x, y, 1 - c)

        def idx(px, py, pc):
            return 4 * px + 2 * py + pc

        def copy(t, k, block, to, src=None):
            rows = outs[t].at[idx(*block)]
            return pltpu.make_async_remote_copy(
                src_ref=rows if src is None else src, dst_ref=rows,
                send_sem=send_sems.at[t, k], recv_sem=recv_sems.at[t, k],
                device_id=to, device_id_type=MESH)

        local = [pltpu.make_async_copy(ins[t], outs[t].at[idx(*me)], local_sems.at[t]) for t in range(n)]
        for cp in local:
            cp.start()
        first = []
        for t in range(n):
            first.append(copy(t, 0, me, sibling, src=ins[t]))
            first += [copy(t, 1 + j, me, (*chip, c), src=ins[t]) for j, chip in enumerate(chips)]
        for cp in first:
            cp.start()
        passed = []
        for j, chip in enumerate(chips):
            for t in range(n):
                copy(t, 1 + j, (*chip, c), me).wait_recv()
                cp = copy(t, 4 + j, (*chip, c), sibling)
                cp.start()
                passed.append(cp)
        for t in range(n):
            copy(t, 0, sibling, me).wait_recv()
        for j, chip in enumerate(chips):
            for t in range(n):
                copy(t, 4 + j, (*chip, 1 - c), me).wait_recv()
        for cp in first + passed:
            cp.wait_send()
        for cp in local:
            cp.wait()

    return pl.pallas_call(
        body, name=name,
        out_shape=[jax.ShapeDtypeStruct((N_DEV,) + s.shape, s.dtype) for s in shards],
        in_specs=[ANY] * n, out_specs=[ANY] * n,
        scratch_shapes=[pltpu.SemaphoreType.DMA((n, 7)), pltpu.SemaphoreType.DMA((n, 7)),
                        pltpu.SemaphoreType.DMA((n,))],
    )(*shards)


def _all_gather_beside(shards, name, collective_id):
    n = len(shards)
    src = [jax.new_ref(s, memory_space=pltpu.MemorySpace.HBM) for s in shards]
    dst = [jax.empty_ref(jax.ShapeDtypeStruct((N_DEV,) + s.shape, s.dtype), memory_space=pltpu.MemorySpace.HBM)
           for s in shards]

    @pl.kernel(mesh=plsc.ScalarSubcoreMesh(axis_name="sequencer", num_cores=1), name=name,
               scratch_types=(pltpu.SemaphoreType.DMA((n, 7)), pltpu.SemaphoreType.DMA((n, 7)),
                              pltpu.SemaphoreType.DMA((n,))),
               compiler_params=pltpu.CompilerParams(collective_id=collective_id))
    def launch(send_sems, recv_sems, local_sems):
        x, y, c, chips = _place()
        me, sibling = (x, y, c), (x, y, 1 - c)
        barrier = pltpu.get_barrier_semaphore()
        for peer in [sibling] + [(*chip, pc) for chip in chips for pc in (c, 1 - c)]:
            pl.semaphore_signal(barrier, inc=1, device_id=peer, device_id_type=MESH)
        pl.semaphore_wait(barrier, N_DEV - 1)

        def idx(px, py, pc):
            return 4 * px + 2 * py + pc

        def copy(t, k, block, to, own=False):
            rows = dst[t].at[idx(*block)]
            return pltpu.make_async_remote_copy(
                src_ref=src[t] if own else rows, dst_ref=rows,
                send_sem=send_sems.at[t, k], recv_sem=recv_sems.at[t, k],
                device_id=to, device_id_type=MESH)

        local = [pltpu.make_async_copy(src[t], dst[t].at[idx(*me)], local_sems.at[t]) for t in range(n)]
        for cp in local:
            cp.start()
        first = []
        for t in range(n):
            first.append(copy(t, 0, me, sibling, own=True))
            first += [copy(t, 1 + j, me, (*chip, c), own=True) for j, chip in enumerate(chips)]
        for cp in first:
            cp.start()
        passed = []
        for j, chip in enumerate(chips):
            for t in range(n):
                copy(t, 1 + j, (*chip, c), me).wait_recv()
                cp = copy(t, 4 + j, (*chip, c), sibling)
                cp.start()
                passed.append(cp)
        for t in range(n):
            copy(t, 0, sibling, me).wait_recv()
        for j, chip in enumerate(chips):
            for t in range(n):
                copy(t, 4 + j, (*chip, 1 - c), me).wait_recv()
        for cp in first + passed:
            cp.wait_send()
        for cp in local:
            cp.wait()

    launch()
    return [d[...] for d in dst]


def _pair_exchange(grads, name):
    n = len(grads)

    def body(*refs):
        ins, theirs = refs[:n], refs[n:2 * n]
        send_sems, recv_sems = refs[2 * n:]
        x, y, c, _ = _place()
        copies = []
        for t in range(n):
            for q in range(4):
                give = pltpu.make_async_remote_copy(
                    src_ref=ins[t].at[2 * q + 1 - c], dst_ref=theirs[t].at[q],
                    send_sem=send_sems.at[t, q], recv_sem=recv_sems.at[t, q],
                    device_id=(x, y, 1 - c), device_id_type=MESH)
                give.start()
                copies.append(give)
        for cp in copies:
            cp.wait()

    return pl.pallas_call(
        body, name=name, out_shape=[jax.ShapeDtypeStruct((4,) + g.shape[1:], g.dtype) for g in grads],
        in_specs=[ANY] * n, out_specs=[ANY] * n,
        scratch_shapes=[pltpu.SemaphoreType.DMA((n, 4)), pltpu.SemaphoreType.DMA((n, 4))],
    )(*grads)


def _chip_exchange_beside(parts, name, collective_id):
    n = len(parts)
    src = [jax.new_ref(p, memory_space=pltpu.MemorySpace.HBM) for p in parts]
    dst = [jax.empty_ref(jax.ShapeDtypeStruct(p.shape, p.dtype), memory_space=pltpu.MemorySpace.HBM) for p in parts]

    @pl.kernel(mesh=plsc.ScalarSubcoreMesh(axis_name="sequencer", num_cores=1), name=name,
               scratch_types=(pltpu.SemaphoreType.DMA((n, 3)), pltpu.SemaphoreType.DMA((n, 3)),
                              pltpu.SemaphoreType.DMA((n,))),
               compiler_params=pltpu.CompilerParams(collective_id=collective_id))
    def launch(send_sems, recv_sems, local_sems):
        x, y, c, chips = _place()
        barrier = pltpu.get_barrier_semaphore()
        for px, py in chips:
            pl.semaphore_signal(barrier, inc=1, device_id=(px, py, c), device_id_type=MESH)
        pl.semaphore_wait(barrier, len(chips))
        copies = []
        for t in range(n):
            own = pltpu.make_async_copy(src[t].at[2 * x + y], dst[t].at[0], local_sems.at[t])
            own.start()
            copies.append(own)
            for j, (px, py) in enumerate(chips):
                cp = pltpu.make_async_remote_copy(
                    src_ref=src[t].at[2 * px + py], dst_ref=dst[t].at[1 + j],
                    send_sem=send_sems.at[t, j], recv_sem=recv_sems.at[t, j],
                    device_id=(px, py, c), device_id_type=MESH)
                cp.start()
                copies.append(cp)
        for cp in copies:
            cp.wait()

    launch()
    return [d[...] for d in dst]


_DOT_DIMS = {"nn": (((1,), (0,)), ((), ())), "nt": (((1,), (1,)), ((), ())), "tn": (((0,), (0,)), ((), ()))}


def _mm(a, b, mode, out_dtype, name, residual=None, tm=1024, tn=1024, tk=2048, out_blocks=None):
    if mode == "tn":
        (k_dim, m_dim), n_dim = a.shape, b.shape[1]
    else:
        (m_dim, k_dim), n_dim = a.shape, (b.shape[1] if mode == "nn" else b.shape[0])
    tm, tn, tk = _tile(m_dim, tm), _tile(n_dim // (out_blocks or 1), tn), _tile(k_dim, tk)
    nk = k_dim // tk

    def body(*refs):
        a_ref, b_ref = refs[:2]
        r_ref = refs[2] if residual is not None else None
        o_ref = refs[2 if residual is None else 3]

        def finish(r):
            if r_ref is not None:
                r = r + r_ref[...].astype(F32)
            o_ref[...] = r.astype(o_ref.dtype)

        part = lax.dot_general(a_ref[...].astype(BF16), b_ref[...].astype(BF16), _DOT_DIMS[mode],
                               preferred_element_type=F32)
        if nk == 1:
            finish(part)
            return
        acc, k = refs[-1], pl.program_id(2)

        @pl.when(k == 0)
        def _():
            acc[...] = part

        @pl.when((k > 0) & (k < nk - 1))
        def _():
            acc[...] += part

        @pl.when(k == nk - 1)
        def _():
            finish(acc[...] + part)

    a_spec = pl.BlockSpec((tk, tm), lambda i, j, k: (k, i)) if mode == "tn" else pl.BlockSpec((tm, tk), lambda i, j, k: (i, k))
    b_spec = pl.BlockSpec((tn, tk), lambda i, j, k: (j, k)) if mode == "nt" else pl.BlockSpec((tk, tn), lambda i, j, k: (k, j))
    o_spec, o_shape = pl.BlockSpec((tm, tn), lambda i, j, k: (i, j)), (m_dim, n_dim)
    if out_blocks:
        per = n_dim // out_blocks // tn
        o_spec = pl.BlockSpec((None, tm, tn), lambda i, j, k: (j // per, i, j % per))
        o_shape = (out_blocks, m_dim, n_dim // out_blocks)
    ins, specs = [a, b], [a_spec, b_spec]
    if residual is not None:
        ins.append(residual)
        specs.append(o_spec)
    return pl.pallas_call(
        body, name=name, grid=(m_dim // tm, n_dim // tn, nk),
        out_shape=jax.ShapeDtypeStruct(o_shape, out_dtype),
        in_specs=specs, out_specs=o_spec,
        scratch_shapes=[pltpu.VMEM((tm, tn), F32)] if nk > 1 else [],
        compiler_params=_params("parallel", "parallel", "arbitrary"),
    )(*ins)


def _rows(body, name, row_ins, full_ins, row_outs, acc_outs, ts=ROW_TILE):
    s = row_ins[0].shape[0]
    ts = _tile(s, ts, 16)
    n_ri, n_fi, n_ro = len(row_ins), len(full_ins), len(row_outs)

    def kernel(*refs):
        i = pl.program_id(0)
        ri, fi = refs[:n_ri], refs[n_ri:n_ri + n_fi]
        ro, ao = refs[n_ri + n_fi:n_ri + n_fi + n_ro], refs[n_ri + n_fi + n_ro:]

        if ao:
            @pl.when(i == 0)
            def _():
                for r in ao:
                    r[...] = jnp.zeros_like(r)

        body(i, ri, fi, ro, ao)

    def row_spec(cols):
        return pl.BlockSpec((ts, cols), lambda i: (i, 0))

    def full_spec(shape):
        return pl.BlockSpec(shape, lambda i: (0,) * len(shape))

    return pl.pallas_call(
        kernel, name=name, grid=(s // ts,),
        out_shape=[jax.ShapeDtypeStruct((s, c), d) for c, d in row_outs]
        + [jax.ShapeDtypeStruct(sh, d) for sh, d in acc_outs],
        in_specs=[row_spec(a.shape[1]) for a in row_ins] + [full_spec(a.shape) for a in full_ins],
        out_specs=[row_spec(c) for c, _ in row_outs] + [full_spec(sh) for sh, _ in acc_outs],
        compiler_params=_params("arbitrary"),
    )(*row_ins, *full_ins)


def _rms_stats(x):
    return lax.rsqrt(jnp.mean(x * x, axis=-1, keepdims=True) + EPS)


def _rms_fwd(x, g, name):
    def body(i, ri, fi, ro, ao):
        xv = ri[0][...]
        ro[0][...] = (xv * _rms_stats(xv) * fi[0][...]).astype(BF16)

    return _rows(body, name, [x], [g], [(x.shape[1], BF16)], [])[0]


def _rms_bwd_rows(xv, gv, dh):
    r = _rms_stats(xv)
    xhat = xv * r
    dxhat = dh * gv
    dx = r * (dxhat - xhat * jnp.mean(dxhat * xhat, axis=-1, keepdims=True))
    return dx, jnp.sum(dh * xhat, axis=0, keepdims=True)


def _rms_bwd(x, g, dh, dres, name):
    d = x.shape[1]

    def body(i, ri, fi, ro, ao):
        dx, dg = _rms_bwd_rows(ri[0][...], fi[0][...], ri[1][...])
        dx = dx + ri[2][...]
        ro[0][...] = dx
        ro[1][...] = dx.astype(BF16)
        ao[0][...] += dg

    return _rows(body, name, [x, dh, dres], [g], [(d, F32), (d, BF16)], [((1, d), F32)])


def _loss_head(x, g, target, name):
    d = x.shape[1]

    def body(i, ri, fi, ro, ao):
        xv, gv = ri[0][...], fi[0][...]
        y = xv * _rms_stats(xv) * gv
        err = y - ri[1][...]
        ao[1][...] += 0.5 * jnp.sum(jnp.mean(err * err, axis=-1, keepdims=True))
        dx, dg = _rms_bwd_rows(xv, gv, err * (1.0 / d))
        ro[0][...] = dx
        ao[0][...] += dg

    return _rows(body, name, [x, target], [g], [(d, F32)], [((1, d), F32), ((8, LANES), F32)])


def _merge_fwd(gg, y_conv, y_attn, name):
    d = y_conv.shape[1]

    def body(i, ri, fi, ro, ao):
        gv, yc, ya = (r[...].astype(F32) for r in ri)
        m = _sigmoid(gv[:, :d]) * yc + _sigmoid(gv[:, d:]) * ya
        ro[0][...] = m.astype(BF16)

    return _rows(body, name, [gg, y_conv, y_attn], [], [(d, BF16)], [])[0]


def _merge_bwd(gg, y_conv, y_attn, dm, name):
    d = y_conv.shape[1]

    def body(i, ri, fi, ro, ao):
        gv, yc, ya, dmv = (r[...].astype(F32) for r in ri)
        sc, sa = _sigmoid(gv[:, :d]), _sigmoid(gv[:, d:])
        ro[0][...] = (dmv * sc).astype(BF16)
        ro[1][...] = (dmv * sa).astype(BF16)
        ro[2][:, :d] = (dmv * yc * sc * (1.0 - sc)).astype(BF16)
        ro[2][:, d:] = (dmv * ya * sa * (1.0 - sa)).astype(BF16)

    return _rows(body, name, [gg, y_conv, y_attn, dm], [], [(d, BF16), (d, BF16), (2 * d, BF16)], [])


def _swiglu_fwd(gu, name):
    f = gu.shape[1] // 2

    def body(i, ri, fi, ro, ao):
        v = ri[0][...].astype(F32)
        gate, up = v[:, :f], v[:, f:]
        ro[0][...] = (gate * _sigmoid(gate) * up).astype(BF16)

    return _rows(body, name, [gu], [], [(f, BF16)], [], ts=128)[0]


def _swiglu_bwd(gu, dact, name):
    f = gu.shape[1] // 2

    def body(i, ri, fi, ro, ao):
        v, dv = ri[0][...].astype(F32), ri[1][...]
        gate, up = v[:, :f], v[:, f:]
        sg = _sigmoid(gate)
        ro[0][:, :f] = (dv * up * sg * (1.0 + gate * (1.0 - sg))).astype(BF16)
        ro[0][:, f:] = (dv * gate * sg).astype(BF16)

    return _rows(body, name, [gu, dact], [], [(2 * f, BF16)], [], ts=128)[0]


def _ple_fwd(x, z, pp, name):
    d = x.shape[1]

    def body(i, ri, fi, ro, ao):
        ro[0][...] = ri[0][...] + _sigmoid(ri[1][...].astype(F32)) * ri[2][...].astype(F32)

    return _rows(body, name, [x, z, pp], [], [(d, F32)], [])[0]


def _ple_bwd(dx, z, pp, name):
    d = dx.shape[1]

    def body(i, ri, fi, ro, ao):
        dv, sz = ri[0][...], _sigmoid(ri[1][...].astype(F32))
        ro[0][...] = (dv * ri[2][...].astype(F32) * sz * (1.0 - sz)).astype(BF16)
        ro[1][...] = (dv * sz).astype(BF16)

    return _rows(body, name, [dx, z, pp], [], [(d, BF16), (d, BF16)], [])


def _triangle(n, lower):
    r = lax.broadcasted_iota(jnp.int32, (n, n), 0)
    c = lax.broadcasted_iota(jnp.int32, (n, n), 1)
    return jnp.where((c <= r) if lower else (c >= r), 1.0, 0.0).astype(F32)


def _forget_fwd(f, bias, name):
    s = f.shape[0]
    tb = _tile(s, CUM_TILE, 8)

    def body(f_ref, b_ref, c_ref, carry):
        @pl.when(pl.program_id(0) == 0)
        def _():
            carry[...] = jnp.zeros_like(carry)

        xv = f_ref[...] + b_ref[...]
        logf = jnp.minimum(xv, 0.0) - jnp.log(1.0 + jnp.exp(-jnp.abs(xv)))
        cv = jnp.dot(_triangle(tb, True), logf, precision=lax.Precision.HIGHEST, preferred_element_type=F32) + carry[...]
        c_ref[...] = cv
        carry[...] = cv[tb - 1:tb, :]

    return pl.pallas_call(
        body, name=name, grid=(s // tb,), out_shape=jax.ShapeDtypeStruct((s, LANES), F32),
        in_specs=[pl.BlockSpec((tb, LANES), lambda i: (i, 0)), pl.BlockSpec((1, LANES), lambda i: (0, 0))],
        out_specs=pl.BlockSpec((tb, LANES), lambda i: (i, 0)),
        scratch_shapes=[pltpu.VMEM((1, LANES), F32)],
        compiler_params=_params("arbitrary"),
    )(f, bias)


def _forget_bwd(f, bias, dc, name):
    s = f.shape[0]
    tb = _tile(s, CUM_TILE, 16)
    nb = s // tb

    def body(f_ref, b_ref, dc_ref, df_ref, db_ref, carry):
        @pl.when(pl.program_id(0) == 0)
        def _():
            carry[...] = jnp.zeros_like(carry)
            db_ref[...] = jnp.zeros_like(db_ref)

        dlog = jnp.dot(_triangle(tb, False), dc_ref[...], precision=lax.Precision.HIGHEST,
                       preferred_element_type=F32) + carry[...]
        carry[...] = dlog[0:1, :]
        dlogit = dlog * (1.0 - _sigmoid(f_ref[...] + b_ref[...]))
        df_ref[...] = dlogit.astype(BF16)
        db_ref[...] += jnp.sum(dlogit, axis=0, keepdims=True)

    back = lambda i: (nb - 1 - i, 0)
    return pl.pallas_call(
        body, name=name, grid=(nb,),
        out_shape=[jax.ShapeDtypeStruct((s, LANES), BF16), jax.ShapeDtypeStruct((1, LANES), F32)],
        in_specs=[pl.BlockSpec((tb, LANES), back), pl.BlockSpec((1, LANES), lambda i: (0, 0)),
                  pl.BlockSpec((tb, LANES), back)],
        out_specs=[pl.BlockSpec((tb, LANES), back), pl.BlockSpec((1, LANES), lambda i: (0, 0))],
        scratch_shapes=[pltpu.VMEM((1, LANES), F32)],
        compiler_params=_params("arbitrary"),
    )(f, bias, dc)


def _glu(block, ch):
    return block[:, :ch] * _sigmoid(block[:, ch:])


def _layernorm_rows(u1, g, b):
    mu = jnp.mean(u1, axis=-1, keepdims=True)
    cen = u1 - mu
    rstd = lax.rsqrt(jnp.mean(cen * cen, axis=-1, keepdims=True) + EPS)
    xhat = cen * rstd
    return xhat, rstd, xhat * g + b


def _conv_fwd(ag, conv_w, conv_b, ln_g, ln_b, name):
    s, ch = ag.shape[0], ag.shape[1] // 2
    ts = _tile(s, ROW_TILE, HALO)
    per = ts // HALO

    def body(cur_ref, halo_ref, w_ref, cb_ref, g_ref, b_ref, u1_ref, u3_ref, cat):
        i = pl.program_id(0)
        cat[HALO:, :] = _glu(cur_ref[...], ch)
        cat[:HALO, :] = jnp.where(i == 0, 0.0, _glu(halo_ref[...], ch))
        full = cat[...]
        acc = jnp.zeros((ts, ch), F32) + cb_ref[...]
        for j in range(CONV_K):
            shifted = full if j == CONV_K - 1 else pltpu.roll(full, CONV_K - 1 - j, 0)
            acc = acc + w_ref[j:j + 1, :] * shifted[HALO:, :]
        u1_ref[...] = acc
        _, _, u2 = _layernorm_rows(acc, g_ref[...], b_ref[...])
        u3_ref[...] = (u2 * _sigmoid(u2)).astype(BF16)

    whole = lambda shape: pl.BlockSpec(shape, lambda i: (0, 0))
    return pl.pallas_call(
        body, name=name, grid=(s // ts,),
        out_shape=[jax.ShapeDtypeStruct((s, ch), F32), jax.ShapeDtypeStruct((s, ch), BF16)],
        in_specs=[pl.BlockSpec((ts, 2 * ch), lambda i: (i, 0)),
                  pl.BlockSpec((HALO, 2 * ch), lambda i: (jnp.maximum(i * per - 1, 0), 0)),
                  whole(conv_w.shape), whole((1, ch)), whole((1, ch)), whole((1, ch))],
        out_specs=[pl.BlockSpec((ts, ch), lambda i: (i, 0)), pl.BlockSpec((ts, ch), lambda i: (i, 0))],
        scratch_shapes=[pltpu.VMEM((HALO + ts, ch), F32)],
        compiler_params=_params("arbitrary"),
    )(ag, ag, conv_w, conv_b, ln_g, ln_b)


def _conv_norm_bwd(u1, du3, ln_g, ln_b, name):
    ch = u1.shape[1]

    def body(i, ri, fi, ro, ao):
        g = fi[0][...]
        xhat, rstd, u2 = _layernorm_rows(ri[0][...], g, fi[1][...])
        sg = _sigmoid(u2)
        du2 = ri[1][...] * sg * (1.0 + u2 * (1.0 - sg))
        dxhat = du2 * g
        du1 = rstd * (dxhat - jnp.mean(dxhat, axis=-1, keepdims=True)
                      - xhat * jnp.mean(dxhat * xhat, axis=-1, keepdims=True))
        ro[0][...] = du1
        ao[0][...] += jnp.sum(du2 * xhat, axis=0, keepdims=True)
        ao[1][...] += jnp.sum(du2, axis=0, keepdims=True)
        ao[2][...] += jnp.sum(du1, axis=0, keepdims=True)

    return _rows(body, name, [u1, du3], [ln_g, ln_b], [(ch, F32)], [((1, ch), F32)] * 3)


def _conv_bwd(ag, du1, conv_w, name):
    s, ch = du1.shape
    ts = _tile(s, ROW_TILE, HALO)
    per = ts // HALO
    last = s // HALO - 1
    nb = s // ts

    def body(cur_ref, halo_ref, d_ref, dnext_ref, w_ref, dag_ref, dw_ref, cat, dcat):
        i = pl.program_id(0)

        @pl.when(i == 0)
        def _():
            dw_ref[...] = jnp.zeros_like(dw_ref)

        cur = cur_ref[...]
        cat[HALO:, :] = _glu(cur, ch)
        cat[:HALO, :] = jnp.where(i == 0, 0.0, _glu(halo_ref[...], ch))
        dcur = d_ref[...]
        dcat[:ts, :] = dcur
        dcat[ts:, :] = jnp.where(i == nb - 1, 0.0, dnext_ref[...])
        full, dfull = cat[...], dcat[...]
        du0 = jnp.zeros((ts, ch), F32)
        for j in range(CONV_K):
            back = CONV_K - 1 - j
            dshift = dfull if back == 0 else pltpu.roll(dfull, ts + HALO - back, 0)
            du0 = du0 + w_ref[j:j + 1, :] * dshift[:ts, :]
            ushift = full if back == 0 else pltpu.roll(full, back, 0)
            dw_ref[j:j + 1, :] += jnp.sum(dcur * ushift[HALO:, :], axis=0, keepdims=True)
        a, sg = cur[:, :ch], _sigmoid(cur[:, ch:])
        dag_ref[:, :ch] = (du0 * sg).astype(BF16)
        dag_ref[:, ch:] = (du0 * a * sg * (1.0 - sg)).astype(BF16)

    return pl.pallas_call(
        body, name=name, grid=(nb,),
        out_shape=[jax.ShapeDtypeStruct((s, 2 * ch), BF16), jax.ShapeDtypeStruct(conv_w.shape, F32)],
        in_specs=[pl.BlockSpec((ts, 2 * ch), lambda i: (i, 0)),
                  pl.BlockSpec((HALO, 2 * ch), lambda i: (jnp.maximum(i * per - 1, 0), 0)),
                  pl.BlockSpec((ts, ch), lambda i: (i, 0)),
                  pl.BlockSpec((HALO, ch), lambda i: (jnp.minimum((i + 1) * per, last), 0)),
                  pl.BlockSpec(conv_w.shape, lambda i: (0, 0))],
        out_specs=[pl.BlockSpec((ts, 2 * ch), lambda i: (i, 0)), pl.BlockSpec(conv_w.shape, lambda i: (0, 0))],
        scratch_shapes=[pltpu.VMEM((HALO + ts, ch), F32), pltpu.VMEM((ts + HALO, ch), F32)],
        compiler_params=_params("arbitrary"),
    )(ag, ag, du1, du1, conv_w)


_SCALE = 1.0 / math.sqrt(HEAD_DIM)


STAT_LANES = 8


def _causal(rows, cols, row0):
    r = row0 + lax.broadcasted_iota(jnp.int32, (rows, cols), 0)
    return lax.broadcasted_iota(jnp.int32, (rows, cols), 1) <= r


def _attn_fwd(qkv, c_row, name):
    s = qkv.shape[0]
    t = _tile(s, ATTN_FWD_TILE)
    nt, part_rows = s // t, min(ATTN_ROWS, t)
    n_parts = t // part_rows
    h_, d = N_HEADS, HEAD_DIM

    def body(q_ref, k_ref, v_ref, cr_ref, o_ref, lse_ref, *scratch):
        m_sc, l_sc, acc = scratch[:n_parts], scratch[n_parts:2 * n_parts], scratch[2 * n_parts:]
        i = pl.program_id(1)
        for part in range(n_parts):
            m_sc[part][...] = jnp.full_like(m_sc[part], NEG_INF)
            l_sc[part][...] = jnp.zeros_like(l_sc[part])
            acc[part][...] = jnp.zeros_like(acc[part])

        def block(j, diagonal):
            at = pl.multiple_of(j * t, t)
            k, v = k_ref[pl.ds(at, t), :], v_ref[pl.ds(at, t), :]
            cr = cr_ref[0, pl.ds(j, 1), :]
            for part in range(n_parts):
                rows = pl.ds(part * part_rows, part_rows)
                sc = lax.dot_general(q_ref[rows, :], k, _DOT_DIMS["nt"], preferred_element_type=F32) * _SCALE - cr
                if diagonal:
                    sc = jnp.where(_causal(part_rows, t, part * part_rows), sc, NEG_INF)
                m_old = m_sc[part][...]
                m_new = jnp.maximum(m_old, jnp.max(sc, axis=-1, keepdims=True))
                alpha = jnp.exp(m_old - m_new)
                p = jnp.exp(sc - m_new)
                l_sc[part][...] = alpha * l_sc[part][...] + jnp.sum(p, axis=-1, keepdims=True)
                acc[part][...] = alpha * acc[part][...] + jnp.dot(p.astype(BF16), v, preferred_element_type=F32)
                m_sc[part][...] = m_new

        def below(j, carry):
            block(j, False)
            return carry

        lax.fori_loop(0, i, below, 0)
        block(i, True)
        for part in range(n_parts):
            rows = pl.ds(part * part_rows, part_rows)
            o_ref[rows, :] = (acc[part][...] / l_sc[part][...]).astype(BF16)
            lse_ref[0, rows, :] = m_sc[part][...] + jnp.log(l_sc[part][...])

    return pl.pallas_call(
        body, name=name, grid=(h_, nt),
        out_shape=[jax.ShapeDtypeStruct((s, h_ * d), BF16), jax.ShapeDtypeStruct((h_, s, 1), F32)],
        in_specs=[pl.BlockSpec((t, d), lambda h, i: (i, h)),
                  pl.BlockSpec((s, d), lambda h, i: (0, h_ + h)),
                  pl.BlockSpec((s, d), lambda h, i: (0, 2 * h_ + h)),
                  pl.BlockSpec((1, nt, t), lambda h, i: (h, 0, 0))],
        out_specs=[pl.BlockSpec((t, d), lambda h, i: (i, h)), pl.BlockSpec((1, t, 1), lambda h, i: (h, i, 0))],
        scratch_shapes=[pltpu.VMEM((part_rows, 1), F32)] * (2 * n_parts) + [pltpu.VMEM((part_rows, d), F32)] * n_parts,
        compiler_params=_params("parallel", "arbitrary"),
    )(qkv, qkv, qkv, c_row.reshape(h_, nt, t))


def _attn_stats(o, do, lse, name):
    s = o.shape[0]
    t = _tile(s, ATTN_TILE)
    h_, d = N_HEADS, HEAD_DIM

    def body(o_ref, do_ref, lse_ref, st_ref, dob_ref):
        dov = do_ref[...]
        st_ref[...] = jnp.zeros_like(st_ref)
        st_ref[0, :, 0:1] = -lse_ref[0]
        st_ref[0, :, 1:2] = jnp.sum(o_ref[...].astype(F32) * dov, axis=-1, keepdims=True)
        dob_ref[...] = dov.astype(BF16)

    col = pl.BlockSpec((1, t, 1), lambda h, i: (h, i, 0))
    blk = pl.BlockSpec((t, d), lambda h, i: (i, h))
    return pl.pallas_call(
        body, name=name, grid=(h_, s // t),
        out_shape=[jax.ShapeDtypeStruct((h_, s, STAT_LANES), F32), jax.ShapeDtypeStruct(o.shape, BF16)],
        in_specs=[blk, blk, col],
        out_specs=[pl.BlockSpec((1, t, STAT_LANES), lambda h, i: (h, i, 0)), blk],
        compiler_params=_params("parallel", "parallel"),
    )(o, do, lse)


def _attn_bwd(qkv, do, stats, c_row, name):
    s = qkv.shape[0]
    tq = _tile(s, ATTN_TILE)
    tk = _tile(s, ATTN_BWD_KEYS, tq)
    nq, nk, per = s // tq, s // tk, tk // tq
    h_, d = N_HEADS, HEAD_DIM

    def body(q_ref, do_ref, k_ref, v_ref, st_ref, cr_ref, dq_ref, dk_ref, dv_ref, dcr_ref, dcc_ref,
             dq_acc, dk_acc, dv_acc, dcr_acc):
        j = pl.program_id(1)

        @pl.when(j == 0)
        def _():
            dq_acc[...] = jnp.zeros_like(dq_acc)
            dcc_ref[...] = jnp.zeros_like(dcc_ref)

        dk_acc[...] = jnp.zeros_like(dk_acc)
        dv_acc[...] = jnp.zeros_like(dv_acc)
        dcr_acc[...] = jnp.zeros_like(dcr_acc)
        k, v, cr = k_ref[...], v_ref[...], cr_ref[0]

        def block(i, row0):
            rows = pl.ds(pl.multiple_of(i * tq, tq), tq)
            q, dov, st = q_ref[rows, :], do_ref[rows, :], st_ref[0, rows, :]
            sc = lax.dot_general(q, k, _DOT_DIMS["nt"], preferred_element_type=F32) * _SCALE + (st[:, 0:1] - cr)
            p = jnp.exp(sc)
            if row0 is not None:
                p = jnp.where(_causal(tq, tk, row0), p, 0.0)
            dp = lax.dot_general(dov, v, _DOT_DIMS["nt"], preferred_element_type=F32)
            ds = p * (dp - st[:, 1:2])
            ds_b = ds.astype(BF16)
            dv_acc[...] += lax.dot_general(p.astype(BF16), dov, _DOT_DIMS["tn"], preferred_element_type=F32)
            dk_acc[...] += lax.dot_general(ds_b, q, _DOT_DIMS["tn"], preferred_element_type=F32)
            dq_acc[rows, :] += jnp.dot(ds_b, k, preferred_element_type=F32)
            dcr_acc[...] -= jnp.sum(ds, axis=0, keepdims=True)
            dcc_ref[0, rows, :] += jnp.sum(ds, axis=-1, keepdims=True)

        def below(i, carry):
            block(i, None)
            return carry

        for part in range(per):
            block(j * per + part, part * tq)
        lax.fori_loop((j + 1) * per, nq, below, 0)
        dk_ref[...] = (dk_acc[...] * _SCALE).astype(BF16)
        dv_ref[...] = dv_acc[...].astype(BF16)
        dcr_ref[0] = dcr_acc[...]

        @pl.when(j == nk - 1)
        def _():
            dq_ref[...] = (dq_acc[...] * _SCALE).astype(BF16)

    head = pl.BlockSpec((s, d), lambda h, j: (0, h))
    return pl.pallas_call(
        body, name=name, grid=(h_, nk),
        out_shape=[jax.ShapeDtypeStruct((s, h_ * d), BF16)] * 3
        + [jax.ShapeDtypeStruct((h_, 1, s), F32), jax.ShapeDtypeStruct((h_, s, 1), F32)],
        in_specs=[head, head,
                  pl.BlockSpec((tk, d), lambda h, j: (j, h_ + h)),
                  pl.BlockSpec((tk, d), lambda h, j: (j, 2 * h_ + h)),
                  pl.BlockSpec((1, s, STAT_LANES), lambda h, j: (h, 0, 0)),
                  pl.BlockSpec((1, 1, tk), lambda h, j: (h, 0, j))],
        out_specs=[head, pl.BlockSpec((tk, d), lambda h, j: (j, h)), pl.BlockSpec((tk, d), lambda h, j: (j, h)),
                   pl.BlockSpec((1, 1, tk), lambda h, j: (h, 0, j)), pl.BlockSpec((1, s, 1), lambda h, j: (h, 0, 0))],
        scratch_shapes=[pltpu.VMEM((s, d), F32), pltpu.VMEM((tk, d), F32), pltpu.VMEM((tk, d), F32),
                        pltpu.VMEM((1, tk), F32)],
        compiler_params=_params("parallel", "arbitrary"),
    )(qkv, do, qkv, qkv, stats, c_row)


def _pair_sum(grads, theirs, core, name):
    _, l, r, c = theirs.shape
    tr = _tile(r, 256, 16)

    def body(core_ref, a_ref, b_ref, o_ref):
        o_ref[...] = (a_ref[...].astype(F32) + b_ref[...].astype(F32)).astype(BF16)

    spec = pl.BlockSpec((1, 1, tr, c), lambda q, k, i, core_ref: (q, k, i, 0))
    mine = pl.BlockSpec((1, 1, tr, c), lambda q, k, i, core_ref: (2 * q + core_ref[0], k, i, 0))
    return pl.pallas_call(
        body, name=name, out_shape=jax.ShapeDtypeStruct(theirs.shape, BF16),
        grid_spec=pltpu.PrefetchScalarGridSpec(num_scalar_prefetch=1, grid=(4, l, r // tr),
                                               in_specs=[mine, spec], out_specs=spec),
        compiler_params=_params("parallel", "parallel", "parallel"),
    )(core, grads, theirs)


def _adamw(g, w, m, v):
    m = ADAM_B1 * m + (1.0 - ADAM_B1) * g
    v = ADAM_B2 * v + (1.0 - ADAM_B2) * (g * g)
    m_hat = m / (1.0 - ADAM_B1 ** ADAM_STEP)
    v_hat = v / (1.0 - ADAM_B2 ** ADAM_STEP)
    return -ADAM_LR * (m_hat / (jnp.sqrt(v_hat) + ADAM_EPS) + ADAM_WD * w), m, v


def _sum_adamw(parts, w, m, v, name):
    n_parts, l, r, c = parts.shape
    tr = _tile(r, 128, 16)

    def body(p_ref, w_ref, m_ref, v_ref, g_ref, d_ref, nm_ref, nv_ref):
        g = p_ref[0].astype(F32)
        for k in range(1, n_parts):
            g = g + p_ref[k].astype(F32)
        g_ref[...] = g
        d_ref[...], nm_ref[...], nv_ref[...] = _adamw(g, w_ref[...], m_ref[...], v_ref[...])

    spec = pl.BlockSpec((1, tr, c), lambda k, i: (k, i, 0))
    return pl.pallas_call(
        body, name=name, grid=(l, r // tr), out_shape=[jax.ShapeDtypeStruct(w.shape, F32)] * 4,
        in_specs=[pl.BlockSpec((n_parts, 1, tr, c), lambda k, i: (0, k, i, 0)), spec, spec, spec],
        out_specs=[spec] * 4, compiler_params=_params("parallel", "parallel"),
    )(parts, w, m, v)


def _unshard_cols(gathered):
    return jnp.transpose(gathered, (1, 0, 2)).reshape(gathered.shape[1], -1)


def _shard_cols(full):
    k = full.shape[0]
    return jnp.transpose(full.reshape(k, N_DEV, -1), (1, 0, 2))


def _in_weights(w, d_model):
    hd = N_HEADS * HEAD_DIM
    w_in = _unshard_cols(w["w_in"])
    f0 = d_model + 3 * hd
    return {
        "w_ag": w_in[:, :d_model], "w_qkv": w_in[:, d_model:f0],
        "w_f": jnp.pad(w_in[:, f0:f0 + N_HEADS], ((0, 0), (0, LANES - N_HEADS))),
        "w_gg": w_in[:, f0 + N_HEADS:],
        "conv_w": jnp.transpose(w["conv_w"], (1, 0, 2)).reshape(CONV_K, -1),
    }


def _rest_weights(w, d_model):
    return {
        "w_conv_out": _unshard_cols(w["w_conv_out"]), "w_attn_out": _unshard_cols(w["w_attn_out"]),
        "w_out": w["w_out"].reshape(-1, d_model), "w_gate_up": _unshard_cols(w["w_gate_up"]),
        "w_down": w["w_down"].reshape(-1, d_model), "w_ple_gate": w["w_ple_gate"].reshape(-1, d_model),
        "w_ple_proj": _unshard_cols(w["w_ple_proj"]),
    }


def _pad_lanes(vec):
    return jnp.pad(vec, (0, LANES - vec.shape[0]))[None, :]


def _layer_fwd(x, p, shards, small, tag, after_mixers=None):
    d_model = x.shape[1]
    w = _in_weights(shards, d_model)
    h = _rms_fwd(x, small["norm_mix_g"], f"rms_mix_{tag}")
    ag = _mm(h, w["w_ag"], "nn", F32, f"proj_glu_{tag}")
    qkv = _mm(h, w["w_qkv"], "nn", BF16, f"proj_qkv_{tag}")
    gg = _mm(h, w["w_gg"], "nn", BF16, f"proj_gates_{tag}")
    f = _mm(h, w["w_f"], "nn", F32, f"proj_forget_{tag}")
    c = _forget_fwd(f, small["b_forget"], f"forget_{tag}")
    c_heads = jnp.transpose(c[:, :N_HEADS])
    c_row = c_heads[:, None, :]
    o, lse = _attn_fwd(qkv, c_row, f"attn_{tag}")
    u1, u3 = _conv_fwd(ag, w["conv_w"], small["conv_b"], small["conv_ln_g"], small["conv_ln_b"], f"conv_{tag}")
    rest, (u3, o) = lax.optimization_barrier(({k: shards[k] for k in _BIG if k != "w_in"}, (u3, o)))
    w.update(_rest_weights(rest, d_model))
    y_conv = _mm(u3, w["w_conv_out"], "nn", BF16, f"conv_out_{tag}")
    y_attn = _mm(o, w["w_attn_out"], "nn", BF16, f"attn_out_{tag}")
    if after_mixers is not None:
        after_mixers(y_attn)
    merged = _merge_fwd(gg, y_conv, y_attn, f"merge_{tag}")
    x1 = _mm(merged, w["w_out"], "nn", F32, f"mix_out_{tag}", residual=x)
    hf = _rms_fwd(x1, small["norm_ffn_g"], f"rms_ffn_{tag}")
    gu = _mm(hf, w["w_gate_up"], "nn", BF16, f"ffn_up_{tag}")
    act = _swiglu_fwd(gu, f"swiglu_{tag}")
    x2 = _mm(act, w["w_down"], "nn", F32, f"ffn_down_{tag}", residual=x1)
    hp = _rms_fwd(x2, small["norm_ple_g"], f"rms_ple_{tag}")
    z = _mm(hp, w["w_ple_gate"], "nn", BF16, f"ple_gate_{tag}")
    pp = _mm(p, w["w_ple_proj"], "nn", BF16, f"ple_proj_{tag}")
    x3 = _ple_fwd(x2, z, pp, f"ple_{tag}")
    saved = dict(x=x, p=p, h=h, ag=ag, qkv=qkv, gg=gg, f=f, u1=u1, u3=u3, y_conv=y_conv, c_row=c_row,
                 o=o, lse=lse, y_attn=y_attn, merged=merged, x1=x1, hf=hf, gu=gu, act=act, x2=x2, hp=hp, z=z, pp=pp)
    return x3, saved, w


def _layer_bwd_ffn(dx3, sv, w, small, tag):
    gw, gs = {}, {}
    dz, dpp = _ple_bwd(dx3, sv["z"], sv["pp"], f"ple_bwd_{tag}")
    gw["w_ple_proj"] = _mm(sv["p"], dpp, "tn", BF16, f"ple_proj_dw_{tag}")
    gw["w_ple_gate"] = _mm(sv["hp"], dz, "tn", BF16, f"ple_gate_dw_{tag}")
    dhp = _mm(dz, w["w_ple_gate"], "nt", F32, f"ple_gate_dx_{tag}")
    dx2, dx2_b, gs["norm_ple_g"] = _rms_bwd(sv["x2"], small["norm_ple_g"], dhp, dx3, f"rms_ple_bwd_{tag}")
    gw["w_down"] = _mm(sv["act"], dx2_b, "tn", BF16, f"ffn_down_dw_{tag}", tm=FFN_N_TILE)
    dact = _mm(dx2_b, w["w_down"], "nt", F32, f"ffn_down_dx_{tag}", tn=FFN_N_TILE)
    dgu = _swiglu_bwd(sv["gu"], dact, f"swiglu_bwd_{tag}")
    gw["w_gate_up"] = _mm(sv["hf"], dgu, "tn", BF16, f"ffn_up_dw_{tag}", tn=FFN_N_TILE, out_blocks=N_DEV)
    dhf = _mm(dgu, w["w_gate_up"], "nt", F32, f"ffn_up_dx_{tag}", tk=FFN_K_TILE)
    dx1, dx1_b, gs["norm_ffn_g"] = _rms_bwd(sv["x1"], small["norm_ffn_g"], dhf, dx2, f"rms_ffn_bwd_{tag}")
    for k in ("norm_ple_g", "norm_ffn_g"):
        gs[k] = gs[k][0]
    return dx1, dx1_b, gw, gs


def _layer_bwd_mix(dx1, dx1_b, sv, w, small, tag):
    gw, gs = {}, {}
    gw["w_out"] = _mm(sv["merged"], dx1_b, "tn", BF16, f"mix_out_dw_{tag}")
    dm = _mm(dx1_b, w["w_out"], "nt", F32, f"mix_out_dx_{tag}")
    dyc, dya, dgg = _merge_bwd(sv["gg"], sv["y_conv"], sv["y_attn"], dm, f"merge_bwd_{tag}")
    gw["w_attn_out"] = _mm(sv["o"], dya, "tn", BF16, f"attn_out_dw_{tag}")
    do = _mm(dya, w["w_attn_out"], "nt", F32, f"attn_out_dx_{tag}")
    stats, do_b = _attn_stats(sv["o"], do, sv["lse"], f"attn_stats_{tag}")
    dq, dk, dv, dc_row, dc_col = _attn_bwd(sv["qkv"], do_b, stats, sv["c_row"], f"attn_bwd_{tag}")
    dc = jnp.pad(jnp.transpose(dc_row[:, 0, :] + dc_col[:, :, 0]), ((0, 0), (0, LANES - N_HEADS)))
    df, db = _forget_bwd(sv["f"], small["b_forget"], dc, f"forget_bwd_{tag}")
    gs["b_forget"] = db[0, :N_HEADS]
    gw["w_conv_out"] = _mm(sv["u3"], dyc, "tn", BF16, f"conv_out_dw_{tag}")
    du3 = _mm(dyc, w["w_conv_out"], "nt", F32, f"conv_out_dx_{tag}")
    du1, gs["conv_ln_g"], gs["conv_ln_b"], gs["conv_b"] = _conv_norm_bwd(
        sv["u1"], du3, small["conv_ln_g"], small["conv_ln_b"], f"conv_norm_bwd_{tag}")
    dag, gs["conv_w"] = _conv_bwd(sv["ag"], du1, w["conv_w"], f"conv_bwd_{tag}")
    dqkv = jnp.concatenate([dq, dk, dv], axis=1)
    g_ag = _mm(sv["h"], dag, "tn", BF16, f"proj_glu_dw_{tag}")
    g_qkv = _mm(sv["h"], dqkv, "tn", BF16, f"proj_qkv_dw_{tag}")
    g_f = _mm(sv["h"], df, "tn", BF16, f"proj_forget_dw_{tag}")
    g_gg = _mm(sv["h"], dgg, "tn", BF16, f"proj_gates_dw_{tag}")
    gw["w_in"] = jnp.concatenate([g_ag, g_qkv, g_f[:, :N_HEADS], g_gg], axis=1)
    dh = _mm(df, w["w_f"], "nt", F32, f"proj_forget_dx_{tag}")
    dh = _mm(dag, w["w_ag"], "nt", F32, f"proj_glu_dx_{tag}", residual=dh)
    dh = _mm(dqkv, w["w_qkv"], "nt", F32, f"proj_qkv_dx_{tag}", residual=dh)
    dh = _mm(dgg, w["w_gg"], "nt", F32, f"proj_gates_dx_{tag}", residual=dh)
    dx, _, gs["norm_mix_g"] = _rms_bwd(sv["x"], small["norm_mix_g"], dh, dx1, f"rms_mix_bwd_{tag}")
    for k in ("norm_mix_g", "conv_ln_g", "conv_ln_b", "conv_b"):
        gs[k] = gs[k][0]
    return dx, gw, gs


_COL_SHARDED = ("w_in", "w_conv_out", "w_attn_out", "w_gate_up", "w_ple_proj")
_ROW_SHARDED = ("w_out", "w_down", "w_ple_gate")
_BIG = _COL_SHARDED + _ROW_SHARDED
_SMALL = ("norm_mix_g", "b_forget", "conv_b", "conv_ln_g", "conv_ln_b", "norm_ffn_g", "norm_ple_g")
_ORDER = ("norm_mix_g", "w_in", "b_forget", "conv_w", "conv_b", "conv_ln_g", "conv_ln_b", "w_conv_out", "w_attn_out",
          "w_out", "norm_ffn_g", "w_gate_up", "w_down", "norm_ple_g", "w_ple_gate", "w_ple_proj", "final_g")


def _pack(vectors, rows):
    flat = jnp.concatenate([v.reshape(-1) for v in vectors])
    return jnp.pad(flat, (0, rows * LANES - flat.shape[0])).reshape(1, rows, LANES)


def _unpack(packed, like):
    flat, out, at = packed.reshape(-1), [], 0
    for v in like:
        out.append(flat[at:at + v.size].reshape(v.shape))
        at += v.size
    return out


def kernel(x, p, norm_mix_g, w_in, b_forget, conv_w, conv_b, conv_ln_g, conv_ln_b, w_conv_out, w_attn_out, w_out, norm_ffn_g, w_gate_up, w_down, norm_ple_g, w_ple_gate, w_ple_proj, final_g, loss_target, m_norm_mix_g, m_w_in, m_b_forget, m_conv_w, m_conv_b, m_conv_ln_g, m_conv_ln_b, m_w_conv_out, m_w_attn_out, m_w_out, m_norm_ffn_g, m_w_gate_up, m_w_down, m_norm_ple_g, m_w_ple_gate, m_w_ple_proj, m_final_g, v_norm_mix_g, v_w_in, v_b_forget, v_conv_w, v_conv_b, v_conv_ln_g, v_conv_ln_b, v_w_conv_out, v_w_attn_out, v_w_out, v_norm_ffn_g, v_w_gate_up, v_w_down, v_norm_ple_g, v_w_ple_gate, v_w_ple_proj, v_final_g):
    weights = dict(norm_mix_g=norm_mix_g, w_in=w_in, b_forget=b_forget, conv_w=conv_w, conv_b=conv_b, conv_ln_g=conv_ln_g, conv_ln_b=conv_ln_b, w_conv_out=w_conv_out, w_attn_out=w_attn_out, w_out=w_out, norm_ffn_g=norm_ffn_g, w_gate_up=w_gate_up, w_down=w_down, norm_ple_g=norm_ple_g, w_ple_gate=w_ple_gate, w_ple_proj=w_ple_proj, final_g=final_g)
    mom1 = dict(norm_mix_g=m_norm_mix_g, w_in=m_w_in, b_forget=m_b_forget, conv_w=m_conv_w, conv_b=m_conv_b, conv_ln_g=m_conv_ln_g, conv_ln_b=m_conv_ln_b, w_conv_out=m_w_conv_out, w_attn_out=m_w_attn_out, w_out=m_w_out, norm_ffn_g=m_norm_ffn_g, w_gate_up=m_w_gate_up, w_down=m_w_down, norm_ple_g=m_norm_ple_g, w_ple_gate=m_w_ple_gate, w_ple_proj=m_w_ple_proj, final_g=m_final_g)
    mom2 = dict(norm_mix_g=v_norm_mix_g, w_in=v_w_in, b_forget=v_b_forget, conv_w=v_conv_w, conv_b=v_conv_b, conv_ln_g=v_conv_ln_g, conv_ln_b=v_conv_ln_b, w_conv_out=v_w_conv_out, w_attn_out=v_w_attn_out, w_out=v_w_out, norm_ffn_g=v_norm_ffn_g, w_gate_up=v_w_gate_up, w_down=v_w_down, norm_ple_g=v_norm_ple_g, w_ple_gate=v_w_ple_gate, w_ple_proj=v_w_ple_proj, final_g=v_final_g)
    depth = w_in.shape[0]
    xi, yi, ci = lax.axis_index("x"), lax.axis_index("y"), lax.axis_index("c")
    me = 4 * xi + 2 * yi + ci

    rest_names = [k for k in _BIG if k != "w_in"]
    first = _all_gather([weights["w_in"][0].astype(BF16), conv_w], "gather_weights_in_l0")
    gathered_conv_w = first[1]
    shards, _ = lax.optimization_barrier(([weights[k][0].astype(BF16) for k in rest_names], first[0]))
    rest = _all_gather_beside(shards, "gather_weights_rest_l0", collective_id=0)
    gathered = [dict(zip(rest_names, rest), w_in=first[0], conv_w=gathered_conv_w[:, 0])] + [None] * (depth - 1)

    def gather_layer(l, tie):
        shards, _ = lax.optimization_barrier(([weights[k][l].astype(BF16) for k in _BIG], tie))
        later = _all_gather_beside(shards, f"gather_weights_l{l}", collective_id=l)
        gathered[l] = dict(zip(_BIG, later), conv_w=gathered_conv_w[:, l])

    act = x[0]
    layers = []
    for l in range(depth):
        if l > 0:
            gathered[l], act = lax.optimization_barrier((gathered[l], act))
        small = {k: weights[k][l][None, :] for k in _SMALL if k != "b_forget"}
        small["b_forget"] = _pad_lanes(b_forget[l])
        gather_next = functools.partial(gather_layer, l + 1) if l + 1 < depth else None
        act, saved, w = _layer_fwd(act, p[l, 0], gathered[l], small, f"l{l}", after_mixers=gather_next)
        layers.append((w, small, saved))
    dact, g_final, loss_part = _loss_head(act, final_g[None, :], loss_target[0], "loss_head")
    loss = lax.psum(loss_part[0, 0], ("x", "y", "c"))

    core = ci.reshape(1).astype(jnp.int32)
    ffn_group = ("w_ple_gate", "w_ple_proj", "w_down", "w_gate_up")
    mix_group = ("w_out", "w_attn_out", "w_conv_out", "w_in")

    def reduce_scatter(gw, names, tie, tag, collective_id):
        blocks = [(gw[k] if gw[k].ndim == 3 else _shard_cols(gw[k]) if k in _COL_SHARDED
                   else gw[k].reshape((N_DEV, -1) + gw[k].shape[1:]))[:, None] for k in names]
        theirs = _pair_exchange(blocks, f"grads_pair_exchange_{tag}")
        chip_parts = [_pair_sum(a, b, core, f"grads_pair_sum_{k}_{tag}") for k, a, b in zip(names, blocks, theirs)]
        chip_parts, tie = lax.optimization_barrier((chip_parts, tie))
        return _chip_exchange_beside(chip_parts, f"grads_chip_exchange_{tag}", collective_id), tie

    arrived = {}
    small_grads = {k: [None] * depth for k in _SMALL + ("conv_w",)}
    waiting = None
    for l in reversed(range(depth)):
        w, small, saved = layers[l]
        dx1, dx1_b, gw, gs = _layer_bwd_ffn(dact, saved, w, small, f"l{l}")
        if waiting is not None:
            arrived[waiting], dx1 = lax.optimization_barrier((arrived[waiting], dx1))
        arrived[l, 0], dx1 = reduce_scatter(gw, ffn_group, dx1, f"ffn_l{l}", depth + 2 * l)
        dact, gw, gs_mix = _layer_bwd_mix(dx1, dx1_b, saved, w, small, f"l{l}")
        arrived[l, 0], dact = lax.optimization_barrier((arrived[l, 0], dact))
        arrived[l, 1], dact = reduce_scatter(gw, mix_group, dact, f"mix_l{l}", depth + 2 * l + 1)
        waiting = (l, 1)
        for k, g in {**gs, **gs_mix}.items():
            small_grads[k][l] = g
    grad_x = dact[None]

    out = {}
    for group, names in enumerate((ffn_group, mix_group)):
        for i, k in enumerate(names):
            parts = jnp.concatenate([arrived[l, group][i] for l in range(depth)], axis=1)
            out[k] = _sum_adamw(parts, weights[k], mom1[k], mom2[k], f"adamw_{k}")

    small_names = list(_SMALL) + ["conv_w", "final_g"]
    small_list = [jnp.stack(small_grads[k]) for k in _SMALL + ("conv_w",)] + [g_final[0]]
    n_small = sum(v.size for v in small_list)
    rows = -(-n_small // (8 * LANES)) * 8
    all_parts = _all_gather([_pack(small_list, rows)], "gather_small_grads")[0]
    full_conv_w = jnp.transpose(gathered_conv_w, (1, 2, 0, 3)).reshape(depth, CONV_K, -1)
    full_conv_m = jnp.zeros_like(full_conv_w)
    ch_shard = conv_w.shape[-1]
    at = (0, 0, me * ch_shard)
    small_w = [weights[k] for k in _SMALL] + [full_conv_w, final_g]
    small_m = [mom1[k] for k in _SMALL] + [lax.dynamic_update_slice(full_conv_m, mom1["conv_w"], at), mom1["final_g"]]
    small_v = [mom2[k] for k in _SMALL] + [lax.dynamic_update_slice(full_conv_m + 1.0, mom2["conv_w"], at), mom2["final_g"]]
    res = _sum_adamw(all_parts, _pack(small_w, rows), _pack(small_m, rows), _pack(small_v, rows), "adamw_small")
    unpacked = [_unpack(r, small_list) for r in res]
    for idx, k in enumerate(small_names):
        vals = [u[idx] for u in unpacked]
        if k == "conv_w":
            vals = [lax.dynamic_slice(v, at, conv_w.shape) for v in vals]
        out[k] = vals

    return (loss, grad_x, *[out[k][0] for k in _ORDER], *[out[k][1] for k in _ORDER],
            *[out[k][2] for k in _ORDER], *[out[k][3] for k in _ORDER])
```

```python
import functools
import math

import jax
import jax.numpy as jnp
from jax import lax
from jax.experimental import pallas as pl
from jax.experimental.pallas import tpu as pltpu
from jax.experimental.pallas import tpu_sc as plsc

F32 = jnp.float32
BF16 = jnp.bfloat16
MESH = pl.DeviceIdType.MESH
ANY = pl.BlockSpec(memory_space=pl.ANY)

N_DEV = 8
N_HEADS = 8
HEAD_DIM = 128
CONV_K = 31
HALO = 32
EPS = 1e-6
NEG_INF = -1e30
ADAM_LR = 0.001
ADAM_B1 = 0.9
ADAM_B2 = 0.999
ADAM_EPS = 1e-08
ADAM_WD = 0.01
ADAM_STEP = 10

V7X_VMEM_BYTES = 64 * 1024 * 1024
VMEM_LIMIT = V7X_VMEM_BYTES * 3 // 4
LANES = 128
ROW_TILE = 256
ATTN_TILE = 512
ATTN_BWD_KEYS = 1024
ATTN_FWD_TILE = 1024
ATTN_ROWS = 256
CUM_TILE = 256
FFN_K_TILE = 2816
FFN_N_TILE = 1408


def _params(*sem):
    return pltpu.CompilerParams(dimension_semantics=sem, vmem_limit_bytes=VMEM_LIMIT)


def _tile(dim, pref, mult=LANES):
    t = min(pref, dim)
    t -= t % mult
    while t >= mult:
        if dim % t == 0:
            return t
        t -= mult
    return dim


def _sigmoid(x):
    return 1.0 / (1.0 + jnp.exp(-x))


def _place():
    x, y, c = lax.axis_index("x"), lax.axis_index("y"), lax.axis_index("c")
    chips = [(1 - x, y), (x, 1 - y), (1 - x, 1 - y)]
    return x, y, c, chips


def _all_gather(shards, name):
    n = len(shards)

    def body(*refs):
        ins, outs = refs[:n], refs[n:2 * n]
        send_sems, recv_sems, local_sems = refs[2 * n:]
        x, y, c, chips = _place()
        me, sibling = (x, y, c), (x, y, 1 - c)

        def idx(px, py, pc):
            return 4 * px + 2 * py + pc

        def copy(t, k, block, to, src=None):
            rows = outs[t].at[idx(*block)]
            return pltpu.make_async_remote_copy(
                src_ref=rows if src is None else src, dst_ref=rows,
                send_sem=send_sems.at[t, k], recv_sem=recv_sems.at[t, k],
                device_id=to, device_id_type=MESH)

        local = [pltpu.make_async_copy(ins[t], outs[t].at[idx(*me)], local_sems.at[t]) for t in range(n)]
        for cp in local:
            cp.start()
        first = []
        for t in range(n):
            first.append(copy(t, 0, me, sibling, src=ins[t]))
            first += [copy(t, 1 + j, me, (*chip, c), src=ins[t]) for j, chip in enumerate(chips)]
        for cp in first:
            cp.start()
        passed = []
        for j, chip in enumerate(chips):
            for t in range(n):
                copy(t, 1 + j, (*chip, c), me).wait_recv()
                cp = copy(t, 4 + j, (*chip, c), sibling)
                cp.start()
                passed.append(cp)
        for t in range(n):
            copy(t, 0, sibling, me).wait_recv()
        for j, chip in enumerate(chips):
            for t in range(n):
                copy(t, 4 + j, (*chip, 1 - c), me).wait_recv()
        for cp in first + passed:
            cp.wait_send()
        for cp in local:
            cp.wait()

    return pl.pallas_call(
        body, name=name,
        out_shape=[jax.ShapeDtypeStruct((N_DEV,) + s.shape, s.dtype) for s in shards],
        in_specs=[ANY] * n, out_specs=[ANY] * n,
        scratch_shapes=[pltpu.SemaphoreType.DMA((n, 7)), pltpu.SemaphoreType.DMA((n, 7)),
                        pltpu.SemaphoreType.DMA((n,))],
    )(*shards)


def _all_gather_beside(shards, name, collective_id):
    n = len(shards)
    src = [jax.new_ref(s, memory_space=pltpu.MemorySpace.HBM) for s in shards]
    dst = [jax.empty_ref(jax.ShapeDtypeStruct((N_DEV,) + s.shape, s.dtype), memory_space=pltpu.MemorySpace.HBM)
           for s in shards]

    @pl.kernel(mesh=plsc.ScalarSubcoreMesh(axis_name="sequencer", num_cores=1), name=name,
               scratch_types=(pltpu.SemaphoreType.DMA((n, 7)), pltpu.SemaphoreType.DMA((n, 7)),
                              pltpu.SemaphoreType.DMA((n,))),
               compiler_params=pltpu.CompilerParams(collective_id=collective_id))
    def launch(send_sems, recv_sems, local_sems):
        x, y, c, chips = _place()
        me, sibling = (x, y, c), (x, y, 1 - c)
        barrier = pltpu.get_barrier_semaphore()
        for peer in [sibling] + [(*chip, pc) for chip in chips for pc in (c, 1 - c)]:
            pl.semaphore_signal(barrier, inc=1, device_id=peer, device_id_type=MESH)
        pl.semaphore_wait(barrier, N_DEV - 1)

        def idx(px, py, pc):
            return 4 * px + 2 * py + pc

        def copy(t, k, block, to, own=False):
            rows = dst[t].at[idx(*block)]
            return pltpu.make_async_remote_copy(
                src_ref=src[t] if own else rows, dst_ref=rows,
                send_sem=send_sems.at[t, k], recv_sem=recv_sems.at[t, k],
                device_id=to, device_id_type=MESH)

        local = [pltpu.make_async_copy(src[t], dst[t].at[idx(*me)], local_sems.at[t]) for t in range(n)]
        for cp in local:
            cp.start()
        first = []
        for t in range(n):
            first.append(copy(t, 0, me, sibling, own=True))
            first += [copy(t, 1 + j, me, (*chip, c), own=True) for j, chip in enumerate(chips)]
        for cp in first:
            cp.start()
        passed = []
        for j, chip in enumerate(chips):
            for t in range(n):
                copy(t, 1 + j, (*chip, c), me).wait_recv()
                cp = copy(t, 4 + j, (*chip, c), sibling)
                cp.start()
                passed.append(cp)
        for t in range(n):
            copy(t, 0, sibling, me).wait_recv()
        for j, chip in enumerate(chips):
            for t in range(n):
                copy(t, 4 + j, (*chip, 1 - c), me).wait_recv()
        for cp in first + passed:
            cp.wait_send()
        for cp in local:
            cp.wait()

    launch()
    return [d[...] for d in dst]


def _pair_exchange(grads, name):
    n = len(grads)

    def body(*refs):
        ins, theirs = refs[:n], refs[n:2 * n]
        send_sems, recv_sems = refs[2 * n:]
        x, y, c, _ = _place()
        copies = []
        for t in range(n):
            for q in range(4):
                give = pltpu.make_async_remote_copy(
                    src_ref=ins[t].at[2 * q + 1 - c], dst_ref=theirs[t].at[q],
                    send_sem=send_sems.at[t, q], recv_sem=recv_sems.at[t, q],
                    device_id=(x, y, 1 - c), device_id_type=MESH)
                give.start()
                copies.append(give)
        for cp in copies:
            cp.wait()

    return pl.pallas_call(
        body, name=name, out_shape=[jax.ShapeDtypeStruct((4,) + g.shape[1:], g.dtype) for g in grads],
        in_specs=[ANY] * n, out_specs=[ANY] * n,
        scratch_shapes=[pltpu.SemaphoreType.DMA((n, 4)), pltpu.SemaphoreType.DMA((n, 4))],
    )(*grads)


def _chip_exchange_beside(parts, name, collective_id):
    n = len(parts)
    src = [jax.new_ref(p, memory_space=pltpu.MemorySpace.HBM) for p in parts]
    dst = [jax.empty_ref(jax.ShapeDtypeStruct(p.shape, p.dtype), memory_space=pltpu.MemorySpace.HBM) for p in parts]

    @pl.kernel(mesh=plsc.ScalarSubcoreMesh(axis_name="sequencer", num_cores=1), name=name,
               scratch_types=(pltpu.SemaphoreType.DMA((n, 3)), pltpu.SemaphoreType.DMA((n, 3)),
                              pltpu.SemaphoreType.DMA((n,))),
               compiler_params=pltpu.CompilerParams(collective_id=collective_id))
    def launch(send_sems, recv_sems, local_sems):
        x, y, c, chips = _place()
        barrier = pltpu.get_barrier_semaphore()
        for px, py in chips:
            pl.semaphore_signal(barrier, inc=1, device_id=(px, py, c), device_id_type=MESH)
        pl.semaphore_wait(barrier, len(chips))
        copies = []
        for t in range(n):
            own = pltpu.make_async_copy(src[t].at[2 * x + y], dst[t].at[0], local_sems.at[t])
            own.start()
            copies.append(own)
            for j, (px, py) in enumerate(chips):
                cp = pltpu.make_async_remote_copy(
                    src_ref=src[t].at[2 * px + py], dst_ref=dst[t].at[1 + j],
                    send_sem=send_sems.at[t, j], recv_sem=recv_sems.at[t, j],
                    device_id=(px, py, c), device_id_type=MESH)
                cp.start()
                copies.append(cp)
        for cp in copies:
            cp.wait()

    launch()
    return [d[...] for d in dst]


_DOT_DIMS = {"nn": (((1,), (0,)), ((), ())), "nt": (((1,), (1,)), ((), ())), "tn": (((0,), (0,)), ((), ()))}


def _mm(a, b, mode, out_dtype, name, residual=None, tm=1024, tn=1024, tk=2048, out_blocks=None):
    if mode == "tn":
        (k_dim, m_dim), n_dim = a.shape, b.shape[1]
    else:
        (m_dim, k_dim), n_dim = a.shape, (b.shape[1] if mode == "nn" else b.shape[0])
    tm, tn, tk = _tile(m_dim, tm), _tile(n_dim // (out_blocks or 1), tn), _tile(k_dim, tk)
    nk = k_dim // tk

    def body(*refs):
        a_ref, b_ref = refs[:2]
        r_ref = refs[2] if residual is not None else None
        o_ref = refs[2 if residual is None else 3]

        def finish(r):
            if r_ref is not None:
                r = r + r_ref[...].astype(F32)
            o_ref[...] = r.astype(o_ref.dtype)

        part = lax.dot_general(a_ref[...].astype(BF16), b_ref[...].astype(BF16), _DOT_DIMS[mode],
                               preferred_element_type=F32)
        if nk == 1:
            finish(part)
            return
        acc, k = refs[-1], pl.program_id(2)

        @pl.when(k == 0)
        def _():
            acc[...] = part

        @pl.when((k > 0) & (k < nk - 1))
        def _():
            acc[...] += part

        @pl.when(k == nk - 1)
        def _():
            finish(acc[...] + part)

    a_spec = pl.BlockSpec((tk, tm), lambda i, j, k: (k, i)) if mode == "tn" else pl.BlockSpec((tm, tk), lambda i, j, k: (i, k))
    b_spec = pl.BlockSpec((tn, tk), lambda i, j, k: (j, k)) if mode == "nt" else pl.BlockSpec((tk, tn), lambda i, j, k: (k, j))
    o_spec, o_shape = pl.BlockSpec((tm, tn), lambda i, j, k: (i, j)), (m_dim, n_dim)
    if out_blocks:
        per = n_dim // out_blocks // tn
        o_spec = pl.BlockSpec((None, tm, tn), lambda i, j, k: (j // per, i, j % per))
        o_shape = (out_blocks, m_dim, n_dim // out_blocks)
    ins, specs = [a, b], [a_spec, b_spec]
    if residual is not None:
        ins.append(residual)
        specs.append(o_spec)
    return pl.pallas_call(
        body, name=name, grid=(m_dim // tm, n_dim // tn, nk),
        out_shape=jax.ShapeDtypeStruct(o_shape, out_dtype),
        in_specs=specs, out_specs=o_spec,
        scratch_shapes=[pltpu.VMEM((tm, tn), F32)] if nk > 1 else [],
        compiler_params=_params("parallel", "parallel", "arbitrary"),
    )(*ins)


def _rows(body, name, row_ins, full_ins, row_outs, acc_outs, ts=ROW_TILE):
    s = row_ins[0].shape[0]
    ts = _tile(s, ts, 16)
    n_ri, n_fi, n_ro = len(row_ins), len(full_ins), len(row_outs)

    def kernel(*refs):
        i = pl.program_id(0)
        ri, fi = refs[:n_ri], refs[n_ri:n_ri + n_fi]
        ro, ao = refs[n_ri + n_fi:n_ri + n_fi + n_ro], refs[n_ri + n_fi + n_ro:]

        if ao:
            @pl.when(i == 0)
            def _():
                for r in ao:
                    r[...] = jnp.zeros_like(r)

        body(i, ri, fi, ro, ao)

    def row_spec(cols):
        return pl.BlockSpec((ts, cols), lambda i: (i, 0))

    def full_spec(shape):
        return pl.BlockSpec(shape, lambda i: (0,) * len(shape))

    return pl.pallas_call(
        kernel, name=name, grid=(s // ts,),
        out_shape=[jax.ShapeDtypeStruct((s, c), d) for c, d in row_outs]
        + [jax.ShapeDtypeStruct(sh, d) for sh, d in acc_outs],
        in_specs=[row_spec(a.shape[1]) for a in row_ins] + [full_spec(a.shape) for a in full_ins],
        out_specs=[row_spec(c) for c, _ in row_outs] + [full_spec(sh) for sh, _ in acc_outs],
        compiler_params=_params("arbitrary"),
    )(*row_ins, *full_ins)


def _rms_stats(x):
    return lax.rsqrt(jnp.mean(x * x, axis=-1, keepdims=True) + EPS)


def _rms_fwd(x, g, name):
    def body(i, ri, fi, ro, ao):
        xv = ri[0][...]
        ro[0][...] = (xv * _rms_stats(xv) * fi[0][...]).astype(BF16)

    return _rows(body, name, [x], [g], [(x.shape[1], BF16)], [])[0]


def _rms_bwd_rows(xv, gv, dh):
    r = _rms_stats(xv)
    xhat = xv * r
    dxhat = dh * gv
    dx = r * (dxhat - xhat * jnp.mean(dxhat * xhat, axis=-1, keepdims=True))
    return dx, jnp.sum(dh * xhat, axis=0, keepdims=True)


def _rms_bwd(x, g, dh, dres, name):
    d = x.shape[1]

    def body(i, ri, fi, ro, ao):
        dx, dg = _rms_bwd_rows(ri[0][...], fi[0][...], ri[1][...].astype(F32))
        dx = dx + ri[2][...]
        ro[0][...] = dx
        ro[1][...] = dx.astype(BF16)
        ao[0][...] += dg

    return _rows(body, name, [x, dh, dres], [g], [(d, F32), (d, BF16)], [((1, d), F32)])


def _loss_head(x, g, target, name):
    d = x.shape[1]

    def body(i, ri, fi, ro, ao):
        xv, gv = ri[0][...], fi[0][...]
        y = xv * _rms_stats(xv) * gv
        err = y - ri[1][...]
        ao[1][...] += 0.5 * jnp.sum(jnp.mean(err * err, axis=-1, keepdims=True))
        dx, dg = _rms_bwd_rows(xv, gv, err * (1.0 / d))
        ro[0][...] = dx
        ao[0][...] += dg

    return _rows(body, name, [x, target], [g], [(d, F32)], [((1, d), F32), ((8, LANES), F32)])


def _merge_fwd(gg, y_conv, y_attn, name):
    d = y_conv.shape[1]

    def body(i, ri, fi, ro, ao):
        gv, yc, ya = (r[...].astype(F32) for r in ri)
        m = _sigmoid(gv[:, :d]) * yc + _sigmoid(gv[:, d:]) * ya
        ro[0][...] = m.astype(BF16)

    return _rows(body, name, [gg, y_conv, y_attn], [], [(d, BF16)], [])[0]


def _merge_bwd(gg, y_conv, y_attn, dm, name):
    d = y_conv.shape[1]

    def body(i, ri, fi, ro, ao):
        gv, yc, ya, dmv = (r[...].astype(F32) for r in ri)
        sc, sa = _sigmoid(gv[:, :d]), _sigmoid(gv[:, d:])
        ro[0][...] = (dmv * sc).astype(BF16)
        ro[1][...] = (dmv * sa).astype(BF16)
        ro[2][:, :d] = (dmv * yc * sc * (1.0 - sc)).astype(BF16)
        ro[2][:, d:] = (dmv * ya * sa * (1.0 - sa)).astype(BF16)

    return _rows(body, name, [gg, y_conv, y_attn, dm], [], [(d, BF16), (d, BF16), (2 * d, BF16)], [])


def _swiglu_fwd(gu, name):
    f = gu.shape[1] // 2

    def body(i, ri, fi, ro, ao):
        v = ri[0][...].astype(F32)
        gate, up = v[:, :f], v[:, f:]
        ro[0][...] = (gate * _sigmoid(gate) * up).astype(BF16)

    return _rows(body, name, [gu], [], [(f, BF16)], [], ts=128)[0]


def _swiglu_bwd(gu, dact, name):
    f = gu.shape[1] // 2

    def body(i, ri, fi, ro, ao):
        v, dv = ri[0][...].astype(F32), ri[1][...].astype(F32)
        gate, up = v[:, :f], v[:, f:]
        sg = _sigmoid(gate)
        ro[0][:, :f] = (dv * up * sg * (1.0 + gate * (1.0 - sg))).astype(BF16)
        ro[0][:, f:] = (dv * gate * sg).astype(BF16)

    return _rows(body, name, [gu, dact], [], [(2 * f, BF16)], [], ts=128)[0]


def _ple_fwd(x, z, pp, name):
    d = x.shape[1]

    def body(i, ri, fi, ro, ao):
        ro[0][...] = ri[0][...] + _sigmoid(ri[1][...].astype(F32)) * ri[2][...].astype(F32)

    return _rows(body, name, [x, z, pp], [], [(d, F32)], [])[0]


def _ple_bwd(dx, z, pp, name):
    d = dx.shape[1]

    def body(i, ri, fi, ro, ao):
        dv, sz = ri[0][...], _sigmoid(ri[1][...].astype(F32))
        ro[0][...] = (dv * ri[2][...].astype(F32) * sz * (1.0 - sz)).astype(BF16)
        ro[1][...] = (dv * sz).astype(BF16)

    return _rows(body, name, [dx, z, pp], [], [(d, BF16), (d, BF16)], [])


def _triangle(n, lower):
    r = lax.broadcasted_iota(jnp.int32, (n, n), 0)
    c = lax.broadcasted_iota(jnp.int32, (n, n), 1)
    return jnp.where((c <= r) if lower else (c >= r), 1.0, 0.0).astype(F32)


def _forget_fwd(f, bias, name):
    s = f.shape[0]
    tb = _tile(s, CUM_TILE, 8)

    def body(f_ref, b_ref, c_ref, carry):
        @pl.when(pl.program_id(0) == 0)
        def _():
            carry[...] = jnp.zeros_like(carry)

        xv = f_ref[...] + b_ref[...]
        logf = jnp.minimum(xv, 0.0) - jnp.log(1.0 + jnp.exp(-jnp.abs(xv)))
        cv = jnp.dot(_triangle(tb, True), logf, precision=lax.Precision.HIGHEST, preferred_element_type=F32) + carry[...]
        c_ref[...] = cv
        carry[...] = cv[tb - 1:tb, :]

    return pl.pallas_call(
        body, name=name, grid=(s // tb,), out_shape=jax.ShapeDtypeStruct((s, LANES), F32),
        in_specs=[pl.BlockSpec((tb, LANES), lambda i: (i, 0)), pl.BlockSpec((1, LANES), lambda i: (0, 0))],
        out_specs=pl.BlockSpec((tb, LANES), lambda i: (i, 0)),
        scratch_shapes=[pltpu.VMEM((1, LANES), F32)],
        compiler_params=_params("arbitrary"),
    )(f, bias)


def _forget_bwd(f, bias, dc, name):
    s = f.shape[0]
    tb = _tile(s, CUM_TILE, 16)
    nb = s // tb

    def body(f_ref, b_ref, dc_ref, df_ref, db_ref, carry):
        @pl.when(pl.program_id(0) == 0)
        def _():
            carry[...] = jnp.zeros_like(carry)
            db_ref[...] = jnp.zeros_like(db_ref)

        dlog = jnp.dot(_triangle(tb, False), dc_ref[...], precision=lax.Precision.HIGHEST,
                       preferred_element_type=F32) + carry[...]
        carry[...] = dlog[0:1, :]
        dlogit = dlog * (1.0 - _sigmoid(f_ref[...] + b_ref[...]))
        df_ref[...] = dlogit.astype(BF16)
        db_ref[...] += jnp.sum(dlogit, axis=0, keepdims=True)

    back = lambda i: (nb - 1 - i, 0)
    return pl.pallas_call(
        body, name=name, grid=(nb,),
        out_shape=[jax.ShapeDtypeStruct((s, LANES), BF16), jax.ShapeDtypeStruct((1, LANES), F32)],
        in_specs=[pl.BlockSpec((tb, LANES), back), pl.BlockSpec((1, LANES), lambda i: (0, 0)),
                  pl.BlockSpec((tb, LANES), back)],
        out_specs=[pl.BlockSpec((tb, LANES), back), pl.BlockSpec((1, LANES), lambda i: (0, 0))],
        scratch_shapes=[pltpu.VMEM((1, LANES), F32)],
        compiler_params=_params("arbitrary"),
    )(f, bias, dc)


def _glu(block, ch):
    return block[:, :ch] * _sigmoid(block[:, ch:])


def _layernorm_rows(u1, g, b):
    mu = jnp.mean(u1, axis=-1, keepdims=True)
    cen = u1 - mu
    rstd = lax.rsqrt(jnp.mean(cen * cen, axis=-1, keepdims=True) + EPS)
    xhat = cen * rstd
    return xhat, rstd, xhat * g + b


def _conv_fwd(ag, conv_w, conv_b, ln_g, ln_b, name):
    s, ch = ag.shape[0], ag.shape[1] // 2
    ts = _tile(s, ROW_TILE, HALO)
    per = ts // HALO

    def body(cur_ref, halo_ref, w_ref, cb_ref, g_ref, b_ref, u1_ref, u3_ref, cat):
        i = pl.program_id(0)
        cat[HALO:, :] = _glu(cur_ref[...], ch)
        cat[:HALO, :] = jnp.where(i == 0, 0.0, _glu(halo_ref[...], ch))
        for c0 in range(0, ch, LANES):
            cols = slice(c0, c0 + LANES)
            full = cat[:, cols]
            acc = jnp.zeros((ts, LANES), F32) + cb_ref[:, cols]
            for j in range(CONV_K):
                shifted = full if j == CONV_K - 1 else pltpu.roll(full, CONV_K - 1 - j, 0)
                acc = acc + w_ref[j:j + 1, cols] * shifted[HALO:, :]
            u1_ref[:, cols] = acc
        _, _, u2 = _layernorm_rows(u1_ref[...], g_ref[...], b_ref[...])
        u3_ref[...] = (u2 * _sigmoid(u2)).astype(BF16)

    whole = lambda shape: pl.BlockSpec(shape, lambda i: (0, 0))
    return pl.pallas_call(
        body, name=name, grid=(s // ts,),
        out_shape=[jax.ShapeDtypeStruct((s, ch), F32), jax.ShapeDtypeStruct((s, ch), BF16)],
        in_specs=[pl.BlockSpec((ts, 2 * ch), lambda i: (i, 0)),
                  pl.BlockSpec((HALO, 2 * ch), lambda i: (jnp.maximum(i * per - 1, 0), 0)),
                  whole(conv_w.shape), whole((1, ch)), whole((1, ch)), whole((1, ch))],
        out_specs=[pl.BlockSpec((ts, ch), lambda i: (i, 0)), pl.BlockSpec((ts, ch), lambda i: (i, 0))],
        scratch_shapes=[pltpu.VMEM((HALO + ts, ch), F32)],
        compiler_params=_params("arbitrary"),
    )(ag, ag, conv_w, conv_b, ln_g, ln_b)


def _conv_norm_bwd(u1, du3, ln_g, ln_b, name):
    ch = u1.shape[1]

    def body(i, ri, fi, ro, ao):
        g = fi[0][...]
        xhat, rstd, u2 = _layernorm_rows(ri[0][...], g, fi[1][...])
        sg = _sigmoid(u2)
        du2 = ri[1][...].astype(F32) * sg * (1.0 + u2 * (1.0 - sg))
        dxhat = du2 * g
        du1 = rstd * (dxhat - jnp.mean(dxhat, axis=-1, keepdims=True)
                      - xhat * jnp.mean(dxhat * xhat, axis=-1, keepdims=True))
        ro[0][...] = du1
        ao[0][...] += jnp.sum(du2 * xhat, axis=0, keepdims=True)
        ao[1][...] += jnp.sum(du2, axis=0, keepdims=True)
        ao[2][...] += jnp.sum(du1, axis=0, keepdims=True)

    return _rows(body, name, [u1, du3], [ln_g, ln_b], [(ch, F32)], [((1, ch), F32)] * 3)


def _conv_bwd(ag, du1, conv_w, name):
    s, ch = du1.shape
    ts = _tile(s, ROW_TILE, HALO)
    per = ts // HALO
    last = s // HALO - 1
    nb = s // ts

    def body(cur_ref, halo_ref, d_ref, dnext_ref, w_ref, dag_ref, dw_ref, cat, dcat):
        i = pl.program_id(0)

        @pl.when(i == 0)
        def _():
            dw_ref[...] = jnp.zeros_like(dw_ref)

        cat[HALO:, :] = _glu(cur_ref[...], ch)
        cat[:HALO, :] = jnp.where(i == 0, 0.0, _glu(halo_ref[...], ch))
        dcat[:ts, :] = d_ref[...]
        dcat[ts:, :] = jnp.where(i == nb - 1, 0.0, dnext_ref[...])
        for c0 in range(0, ch, LANES):
            cols, gate_cols = slice(c0, c0 + LANES), slice(ch + c0, ch + c0 + LANES)
            full, dfull = cat[:, cols], dcat[:, cols]
            dcur = dfull[:ts, :]
            du0 = jnp.zeros((ts, LANES), F32)
            for j in range(CONV_K):
                back = CONV_K - 1 - j
                dshift = dfull if back == 0 else pltpu.roll(dfull, ts + HALO - back, 0)
                du0 = du0 + w_ref[j:j + 1, cols] * dshift[:ts, :]
                ushift = full if back == 0 else pltpu.roll(full, back, 0)
                dw_ref[j:j + 1, cols] += jnp.sum(dcur * ushift[HALO:, :], axis=0, keepdims=True)
            a, sg = cur_ref[:, cols], _sigmoid(cur_ref[:, gate_cols])
            dag_ref[:, cols] = (du0 * sg).astype(BF16)
            dag_ref[:, gate_cols] = (du0 * a * sg * (1.0 - sg)).astype(BF16)

    return pl.pallas_call(
        body, name=name, grid=(nb,),
        out_shape=[jax.ShapeDtypeStruct((s, 2 * ch), BF16), jax.ShapeDtypeStruct(conv_w.shape, F32)],
        in_specs=[pl.BlockSpec((ts, 2 * ch), lambda i: (i, 0)),
                  pl.BlockSpec((HALO, 2 * ch), lambda i: (jnp.maximum(i * per - 1, 0), 0)),
                  pl.BlockSpec((ts, ch), lambda i: (i, 0)),
                  pl.BlockSpec((HALO, ch), lambda i: (jnp.minimum((i + 1) * per, last), 0)),
                  pl.BlockSpec(conv_w.shape, lambda i: (0, 0))],
        out_specs=[pl.BlockSpec((ts, 2 * ch), lambda i: (i, 0)), pl.BlockSpec(conv_w.shape, lambda i: (0, 0))],
        scratch_shapes=[pltpu.VMEM((HALO + ts, ch), F32), pltpu.VMEM((ts + HALO, ch), F32)],
        compiler_params=_params("arbitrary"),
    )(ag, ag, du1, du1, conv_w)


_SCALE = 1.0 / math.sqrt(HEAD_DIM)


STAT_LANES = 8


def _causal(rows, cols, row0):
    r = row0 + lax.broadcasted_iota(jnp.int32, (rows, cols), 0)
    return lax.broadcasted_iota(jnp.int32, (rows, cols), 1) <= r


def _attn_fwd(qkv, c_row, name):
    s = qkv.shape[0]
    t = _tile(s, ATTN_FWD_TILE)
    nt, part_rows = s // t, min(ATTN_ROWS, t)
    n_parts = t // part_rows
    h_, d = N_HEADS, HEAD_DIM

    def body(q_ref, k_ref, v_ref, cr_ref, o_ref, lse_ref, *scratch):
        m_sc, l_sc, acc = scratch[:n_parts], scratch[n_parts:2 * n_parts], scratch[2 * n_parts:]
        i = pl.program_id(1)
        for part in range(n_parts):
            m_sc[part][...] = jnp.full_like(m_sc[part], NEG_INF)
            l_sc[part][...] = jnp.zeros_like(l_sc[part])
            acc[part][...] = jnp.zeros_like(acc[part])

        def block(j, diagonal):
            at = pl.multiple_of(j * t, t)
            k, v = k_ref[pl.ds(at, t), :], v_ref[pl.ds(at, t), :]
            cr = cr_ref[0, pl.ds(j, 1), :]
            for part in range(n_parts):
                rows = pl.ds(part * part_rows, part_rows)
                sc = lax.dot_general(q_ref[rows, :], k, _DOT_DIMS["nt"], preferred_element_type=F32) * _SCALE - cr
                if diagonal:
                    sc = jnp.where(_causal(part_rows, t, part * part_rows), sc, NEG_INF)
                m_old = m_sc[part][...]
                m_new = jnp.maximum(m_old, jnp.max(sc, axis=-1, keepdims=True))
                alpha = jnp.exp(m_old - m_new)
                p = jnp.exp(sc - m_new)
                l_sc[part][...] = alpha * l_sc[part][...] + jnp.sum(p, axis=-1, keepdims=True)
                acc[part][...] = alpha * acc[part][...] + jnp.dot(p.astype(BF16), v, preferred_element_type=F32)
                m_sc[part][...] = m_new

        def below(j, carry):
            block(j, False)
            return carry

        lax.fori_loop(0, i, below, 0)
        block(i, True)
        for part in range(n_parts):
            rows = pl.ds(part * part_rows, part_rows)
            o_ref[rows, :] = (acc[part][...] / l_sc[part][...]).astype(BF16)
            lse_ref[0, rows, :] = m_sc[part][...] + jnp.log(l_sc[part][...])

    return pl.pallas_call(
        body, name=name, grid=(h_, nt),
        out_shape=[jax.ShapeDtypeStruct((s, h_ * d), BF16), jax.ShapeDtypeStruct((h_, s, 1), F32)],
        in_specs=[pl.BlockSpec((t, d), lambda h, i: (i, h)),
                  pl.BlockSpec((s, d), lambda h, i: (0, h_ + h)),
                  pl.BlockSpec((s, d), lambda h, i: (0, 2 * h_ + h)),
                  pl.BlockSpec((1, nt, t), lambda h, i: (h, 0, 0))],
        out_specs=[pl.BlockSpec((t, d), lambda h, i: (i, h)), pl.BlockSpec((1, t, 1), lambda h, i: (h, i, 0))],
        scratch_shapes=[pltpu.VMEM((part_rows, 1), F32)] * (2 * n_parts) + [pltpu.VMEM((part_rows, d), F32)] * n_parts,
        compiler_params=_params("parallel", "arbitrary"),
    )(qkv, qkv, qkv, c_row.reshape(h_, nt, t))


def _attn_stats(o, do, lse, name):
    s = o.shape[0]
    t = _tile(s, ATTN_TILE)
    h_, d = N_HEADS, HEAD_DIM

    def body(o_ref, do_ref, lse_ref, st_ref, dob_ref):
        dov = do_ref[...]
        st_ref[...] = jnp.zeros_like(st_ref)
        st_ref[0, :, 0:1] = -lse_ref[0]
        st_ref[0, :, 1:2] = jnp.sum(o_ref[...].astype(F32) * dov, axis=-1, keepdims=True)
        dob_ref[...] = dov.astype(BF16)

    col = pl.BlockSpec((1, t, 1), lambda h, i: (h, i, 0))
    blk = pl.BlockSpec((t, d), lambda h, i: (i, h))
    return pl.pallas_call(
        body, name=name, grid=(h_, s // t),
        out_shape=[jax.ShapeDtypeStruct((h_, s, STAT_LANES), F32), jax.ShapeDtypeStruct(o.shape, BF16)],
        in_specs=[blk, blk, col],
        out_specs=[pl.BlockSpec((1, t, STAT_LANES), lambda h, i: (h, i, 0)), blk],
        compiler_params=_params("parallel", "parallel"),
    )(o, do, lse)


def _attn_bwd(qkv, do, stats, c_row, name):
    s = qkv.shape[0]
    tq = _tile(s, ATTN_TILE)
    tk = _tile(s, ATTN_BWD_KEYS, tq)
    nq, nk, per = s // tq, s // tk, tk // tq
    h_, d = N_HEADS, HEAD_DIM

    def body(q_ref, do_ref, k_ref, v_ref, st_ref, cr_ref, dq_ref, dk_ref, dv_ref, dcr_ref, dcc_ref,
             dq_acc, dk_acc, dv_acc, dcr_acc):
        j = pl.program_id(1)

        @pl.when(j == 0)
        def _():
            dq_acc[...] = jnp.zeros_like(dq_acc)
            dcc_ref[...] = jnp.zeros_like(dcc_ref)

        dk_acc[...] = jnp.zeros_like(dk_acc)
        dv_acc[...] = jnp.zeros_like(dv_acc)
        dcr_acc[...] = jnp.zeros_like(dcr_acc)
        k, v, cr = k_ref[...], v_ref[...], cr_ref[0]

        def block(i, row0):
            rows = pl.ds(pl.multiple_of(i * tq, tq), tq)
            q, dov, st = q_ref[rows, :], do_ref[rows, :], st_ref[0, rows, :]
            sc = lax.dot_general(q, k, _DOT_DIMS["nt"], preferred_element_type=F32) * _SCALE + (st[:, 0:1] - cr)
            p = jnp.exp(sc)
            if row0 is not None:
                p = jnp.where(_causal(tq, tk, row0), p, 0.0)
            dp = lax.dot_general(dov, v, _DOT_DIMS["nt"], preferred_element_type=F32)
            ds = p * (dp - st[:, 1:2])
            ds_b = ds.astype(BF16)
            dv_acc[...] += lax.dot_general(p.astype(BF16), dov, _DOT_DIMS["tn"], preferred_element_type=F32)
            dk_acc[...] += lax.dot_general(ds_b, q, _DOT_DIMS["tn"], preferred_element_type=F32)
            dq_acc[rows, :] += jnp.dot(ds_b, k, preferred_element_type=F32)
            dcr_acc[...] -= jnp.sum(ds, axis=0, keepdims=True)
            dcc_ref[0, rows, :] += jnp.sum(ds, axis=-1, keepdims=True)

        def below(i, carry):
            block(i, None)
            return carry

        for part in range(per):
            block(j * per + part, part * tq)
        lax.fori_loop((j + 1) * per, nq, below, 0)
        dk_ref[...] = (dk_acc[...] * _SCALE).astype(BF16)
        dv_ref[...] = dv_acc[...].astype(BF16)
        dcr_ref[0] = dcr_acc[...]

        @pl.when(j == nk - 1)
        def _():
            dq_ref[...] = (dq_acc[...] * _SCALE).astype(BF16)

    head = pl.BlockSpec((s, d), lambda h, j: (0, h))
    return pl.pallas_call(
        body, name=name, grid=(h_, nk),
        out_shape=[jax.ShapeDtypeStruct((s, h_ * d), BF16)] * 3
        + [jax.ShapeDtypeStruct((h_, 1, s), F32), jax.ShapeDtypeStruct((h_, s, 1), F32)],
        in_specs=[head, head,
                  pl.BlockSpec((tk, d), lambda h, j: (j, h_ + h)),
                  pl.BlockSpec((tk, d), lambda h, j: (j, 2 * h_ + h)),
                  pl.BlockSpec((1, s, STAT_LANES), lambda h, j: (h, 0, 0)),
                  pl.BlockSpec((1, 1, tk), lambda h, j: (h, 0, j))],
        out_specs=[head, pl.BlockSpec((tk, d), lambda h, j: (j, h)), pl.BlockSpec((tk, d), lambda h, j: (j, h)),
                   pl.BlockSpec((1, 1, tk), lambda h, j: (h, 0, j)), pl.BlockSpec((1, s, 1), lambda h, j: (h, 0, 0))],
        scratch_shapes=[pltpu.VMEM((s, d), F32), pltpu.VMEM((tk, d), F32), pltpu.VMEM((tk, d), F32),
                        pltpu.VMEM((1, tk), F32)],
        compiler_params=_params("parallel", "arbitrary"),
    )(qkv, do, qkv, qkv, stats, c_row)


def _pair_sum(grads, theirs, core, name):
    _, l, r, c = theirs.shape
    tr = _tile(r, 256, 16)

    def body(core_ref, a_ref, b_ref, o_ref):
        o_ref[...] = (a_ref[...].astype(F32) + b_ref[...].astype(F32)).astype(BF16)

    spec = pl.BlockSpec((1, 1, tr, c), lambda q, k, i, core_ref: (q, k, i, 0))
    mine = pl.BlockSpec((1, 1, tr, c), lambda q, k, i, core_ref: (2 * q + core_ref[0], k, i, 0))
    return pl.pallas_call(
        body, name=name, out_shape=jax.ShapeDtypeStruct(theirs.shape, BF16),
        grid_spec=pltpu.PrefetchScalarGridSpec(num_scalar_prefetch=1, grid=(4, l, r // tr),
                                               in_specs=[mine, spec], out_specs=spec),
        compiler_params=_params("parallel", "parallel", "parallel"),
    )(core, grads, theirs)


def _adamw(g, w, m, v):
    m = ADAM_B1 * m + (1.0 - ADAM_B1) * g
    v = ADAM_B2 * v + (1.0 - ADAM_B2) * (g * g)
    m_hat = m / (1.0 - ADAM_B1 ** ADAM_STEP)
    v_hat = v / (1.0 - ADAM_B2 ** ADAM_STEP)
    return -ADAM_LR * (m_hat / (jnp.sqrt(v_hat) + ADAM_EPS) + ADAM_WD * w), m, v


def _sum_adamw(parts, w, m, v, name):
    n_parts, l, r, c = parts.shape
    tr = _tile(r, 128, 16)

    def body(p_ref, w_ref, m_ref, v_ref, g_ref, d_ref, nm_ref, nv_ref):
        g = p_ref[0].astype(F32)
        for k in range(1, n_parts):
            g = g + p_ref[k].astype(F32)
        g_ref[...] = g
        d_ref[...], nm_ref[...], nv_ref[...] = _adamw(g, w_ref[...], m_ref[...], v_ref[...])

    spec = pl.BlockSpec((1, tr, c), lambda k, i: (k, i, 0))
    return pl.pallas_call(
        body, name=name, grid=(l, r // tr), out_shape=[jax.ShapeDtypeStruct(w.shape, F32)] * 4,
        in_specs=[pl.BlockSpec((n_parts, 1, tr, c), lambda k, i: (0, k, i, 0)), spec, spec, spec],
        out_specs=[spec] * 4, compiler_params=_params("parallel", "parallel"),
    )(parts, w, m, v)


def _unshard_cols(gathered):
    return jnp.transpose(gathered, (1, 0, 2)).reshape(gathered.shape[1], -1)


def _shard_cols(full):
    k = full.shape[0]
    return jnp.transpose(full.reshape(k, N_DEV, -1), (1, 0, 2))


def _in_weights(w, d_model):
    hd = N_HEADS * HEAD_DIM
    w_in = _unshard_cols(w["w_in"])
    f0 = d_model + 3 * hd
    return {
        "w_ag": w_in[:, :d_model], "w_qkv": w_in[:, d_model:f0],
        "w_f": jnp.pad(w_in[:, f0:f0 + N_HEADS], ((0, 0), (0, LANES - N_HEADS))),
        "w_gg": w_in[:, f0 + N_HEADS:],
        "conv_w": jnp.transpose(w["conv_w"], (1, 0, 2)).reshape(CONV_K, -1),
    }


def _rest_weights(w, d_model):
    return {
        "w_conv_out": _unshard_cols(w["w_conv_out"]), "w_attn_out": _unshard_cols(w["w_attn_out"]),
        "w_out": w["w_out"].reshape(-1, d_model), "w_gate_up": _unshard_cols(w["w_gate_up"]),
        "w_down": w["w_down"].reshape(-1, d_model), "w_ple_gate": w["w_ple_gate"].reshape(-1, d_model),
        "w_ple_proj": _unshard_cols(w["w_ple_proj"]),
    }


def _pad_lanes(vec):
    return jnp.pad(vec, (0, LANES - vec.shape[0]))[None, :]


def _layer_fwd(x, p, shards, small, tag, after_mixers=None):
    d_model = x.shape[1]
    w = _in_weights(shards, d_model)
    h = _rms_fwd(x, small["norm_mix_g"], f"rms_mix_{tag}")
    ag = _mm(h, w["w_ag"], "nn", F32, f"proj_glu_{tag}")
    qkv = _mm(h, w["w_qkv"], "nn", BF16, f"proj_qkv_{tag}")
    gg = _mm(h, w["w_gg"], "nn", BF16, f"proj_gates_{tag}")
    f = _mm(h, w["w_f"], "nn", F32, f"proj_forget_{tag}")
    c = _forget_fwd(f, small["b_forget"], f"forget_{tag}")
    c_heads = jnp.transpose(c[:, :N_HEADS])
    c_row = c_heads[:, None, :]
    o, lse = _attn_fwd(qkv, c_row, f"attn_{tag}")
    u1, u3 = _conv_fwd(ag, w["conv_w"], small["conv_b"], small["conv_ln_g"], small["conv_ln_b"], f"conv_{tag}")
    rest, (u3, o) = lax.optimization_barrier(({k: shards[k] for k in _BIG if k != "w_in"}, (u3, o)))
    w.update(_rest_weights(rest, d_model))
    y_conv = _mm(u3, w["w_conv_out"], "nn", BF16, f"conv_out_{tag}")
    y_attn = _mm(o, w["w_attn_out"], "nn", BF16, f"attn_out_{tag}")
    if after_mixers is not None:
        after_mixers(y_attn)
    merged = _merge_fwd(gg, y_conv, y_attn, f"merge_{tag}")
    x1 = _mm(merged, w["w_out"], "nn", F32, f"mix_out_{tag}", residual=x)
    hf = _rms_fwd(x1, small["norm_ffn_g"], f"rms_ffn_{tag}")
    gu = _mm(hf, w["w_gate_up"], "nn", BF16, f"ffn_up_{tag}")
    act = _swiglu_fwd(gu, f"swiglu_{tag}")
    x2 = _mm(act, w["w_down"], "nn", F32, f"ffn_down_{tag}", residual=x1)
    hp = _rms_fwd(x2, small["norm_ple_g"], f"rms_ple_{tag}")
    z = _mm(hp, w["w_ple_gate"], "nn", BF16, f"ple_gate_{tag}")
    pp = _mm(p, w["w_ple_proj"], "nn", BF16, f"ple_proj_{tag}")
    x3 = _ple_fwd(x2, z, pp, f"ple_{tag}")
    saved = dict(x=x, p=p, h=h, ag=ag, qkv=qkv, gg=gg, f=f, u1=u1, u3=u3, y_conv=y_conv, c_row=c_row,
                 o=o, lse=lse, y_attn=y_attn, merged=merged, x1=x1, hf=hf, gu=gu, act=act, x2=x2, hp=hp, z=z, pp=pp)
    return x3, saved, w


def _layer_bwd_ffn(dx3, sv, w, small, tag):
    gw, gs = {}, {}
    dz, dpp = _ple_bwd(dx3, sv["z"], sv["pp"], f"ple_bwd_{tag}")
    gw["w_ple_proj"] = _mm(sv["p"], dpp, "tn", BF16, f"ple_proj_dw_{tag}")
    gw["w_ple_gate"] = _mm(sv["hp"], dz, "tn", BF16, f"ple_gate_dw_{tag}")
    dhp = _mm(dz, w["w_ple_gate"], "nt", BF16, f"ple_gate_dx_{tag}")
    dx2, dx2_b, gs["norm_ple_g"] = _rms_bwd(sv["x2"], small["norm_ple_g"], dhp, dx3, f"rms_ple_bwd_{tag}")
    gw["w_down"] = _mm(sv["act"], dx2_b, "tn", BF16, f"ffn_down_dw_{tag}", tm=FFN_N_TILE)
    dact = _mm(dx2_b, w["w_down"], "nt", BF16, f"ffn_down_dx_{tag}", tn=FFN_N_TILE)
    dgu = _swiglu_bwd(sv["gu"], dact, f"swiglu_bwd_{tag}")
    gw["w_gate_up"] = _mm(sv["hf"], dgu, "tn", BF16, f"ffn_up_dw_{tag}", tn=FFN_N_TILE, out_blocks=N_DEV)
    dhf = _mm(dgu, w["w_gate_up"], "nt", BF16, f"ffn_up_dx_{tag}", tk=FFN_K_TILE)
    dx1, dx1_b, gs["norm_ffn_g"] = _rms_bwd(sv["x1"], small["norm_ffn_g"], dhf, dx2, f"rms_ffn_bwd_{tag}")
    for k in ("norm_ple_g", "norm_ffn_g"):
        gs[k] = gs[k][0]
    return dx1, dx1_b, gw, gs


def _layer_bwd_mix(dx1, dx1_b, sv, w, small, tag):
    gw, gs = {}, {}
    gw["w_out"] = _mm(sv["merged"], dx1_b, "tn", BF16, f"mix_out_dw_{tag}")
    dm = _mm(dx1_b, w["w_out"], "nt", BF16, f"mix_out_dx_{tag}")
    dyc, dya, dgg = _merge_bwd(sv["gg"], sv["y_conv"], sv["y_attn"], dm, f"merge_bwd_{tag}")
    gw["w_attn_out"] = _mm(sv["o"], dya, "tn", BF16, f"attn_out_dw_{tag}")
    do = _mm(dya, w["w_attn_out"], "nt", F32, f"attn_out_dx_{tag}")
    stats, do_b = _attn_stats(sv["o"], do, sv["lse"], f"attn_stats_{tag}")
    dq, dk, dv, dc_row, dc_col = _attn_bwd(sv["qkv"], do_b, stats, sv["c_row"], f"attn_bwd_{tag}")
    dc = jnp.pad(jnp.transpose(dc_row[:, 0, :] + dc_col[:, :, 0]), ((0, 0), (0, LANES - N_HEADS)))
    df, db = _forget_bwd(sv["f"], small["b_forget"], dc, f"forget_bwd_{tag}")
    gs["b_forget"] = db[0, :N_HEADS]
    gw["w_conv_out"] = _mm(sv["u3"], dyc, "tn", BF16, f"conv_out_dw_{tag}")
    du3 = _mm(dyc, w["w_conv_out"], "nt", BF16, f"conv_out_dx_{tag}")
    du1, gs["conv_ln_g"], gs["conv_ln_b"], gs["conv_b"] = _conv_norm_bwd(
        sv["u1"], du3, small["conv_ln_g"], small["conv_ln_b"], f"conv_norm_bwd_{tag}")
    dag, gs["conv_w"] = _conv_bwd(sv["ag"], du1, w["conv_w"], f"conv_bwd_{tag}")
    dqkv = jnp.concatenate([dq, dk, dv], axis=1)
    g_ag = _mm(sv["h"], dag, "tn", BF16, f"proj_glu_dw_{tag}")
    g_qkv = _mm(sv["h"], dqkv, "tn", BF16, f"proj_qkv_dw_{tag}")
    g_f = _mm(sv["h"], df, "tn", BF16, f"proj_forget_dw_{tag}")
    g_gg = _mm(sv["h"], dgg, "tn", BF16, f"proj_gates_dw_{tag}")
    gw["w_in"] = jnp.concatenate([g_ag, g_qkv, g_f[:, :N_HEADS], g_gg], axis=1)
    dh = _mm(df, w["w_f"], "nt", F32, f"proj_forget_dx_{tag}")
    dh = _mm(dag, w["w_ag"], "nt", F32, f"proj_glu_dx_{tag}", residual=dh)
    dh = _mm(dqkv, w["w_qkv"], "nt", F32, f"proj_qkv_dx_{tag}", residual=dh)
    dh = _mm(dgg, w["w_gg"], "nt", F32, f"proj_gates_dx_{tag}", residual=dh)
    dx, _, gs["norm_mix_g"] = _rms_bwd(sv["x"], small["norm_mix_g"], dh, dx1, f"rms_mix_bwd_{tag}")
    for k in ("norm_mix_g", "conv_ln_g", "conv_ln_b", "conv_b"):
        gs[k] = gs[k][0]
    return dx, gw, gs


_COL_SHARDED = ("w_in", "w_conv_out", "w_attn_out", "w_gate_up", "w_ple_proj")
_ROW_SHARDED = ("w_out", "w_down", "w_ple_gate")
_BIG = _COL_SHARDED + _ROW_SHARDED
_SMALL = ("norm_mix_g", "b_forget", "conv_b", "conv_ln_g", "conv_ln_b", "norm_ffn_g", "norm_ple_g")
_ORDER = ("norm_mix_g", "w_in", "b_forget", "conv_w", "conv_b", "conv_ln_g", "conv_ln_b", "w_conv_out", "w_attn_out",
          "w_out", "norm_ffn_g", "w_gate_up", "w_down", "norm_ple_g", "w_ple_gate", "w_ple_proj", "final_g")


def _pack(vectors, rows):
    flat = jnp.concatenate([v.reshape(-1) for v in vectors])
    return jnp.pad(flat, (0, rows * LANES - flat.shape[0])).reshape(1, rows, LANES)


def _unpack(packed, like):
    flat, out, at = packed.reshape(-1), [], 0
    for v in like:
        out.append(flat[at:at + v.size].reshape(v.shape))
        at += v.size
    return out


def kernel(x, p, norm_mix_g, w_in, b_forget, conv_w, conv_b, conv_ln_g, conv_ln_b, w_conv_out, w_attn_out, w_out, norm_ffn_g, w_gate_up, w_down, norm_ple_g, w_ple_gate, w_ple_proj, final_g, loss_target, m_norm_mix_g, m_w_in, m_b_forget, m_conv_w, m_conv_b, m_conv_ln_g, m_conv_ln_b, m_w_conv_out, m_w_attn_out, m_w_out, m_norm_ffn_g, m_w_gate_up, m_w_down, m_norm_ple_g, m_w_ple_gate, m_w_ple_proj, m_final_g, v_norm_mix_g, v_w_in, v_b_forget, v_conv_w, v_conv_b, v_conv_ln_g, v_conv_ln_b, v_w_conv_out, v_w_attn_out, v_w_out, v_norm_ffn_g, v_w_gate_up, v_w_down, v_norm_ple_g, v_w_ple_gate, v_w_ple_proj, v_final_g):
    weights = dict(norm_mix_g=norm_mix_g, w_in=w_in, b_forget=b_forget, conv_w=conv_w, conv_b=conv_b, conv_ln_g=conv_ln_g, conv_ln_b=conv_ln_b, w_conv_out=w_conv_out, w_attn_out=w_attn_out, w_out=w_out, norm_ffn_g=norm_ffn_g, w_gate_up=w_gate_up, w_down=w_down, norm_ple_g=norm_ple_g, w_ple_gate=w_ple_gate, w_ple_proj=w_ple_proj, final_g=final_g)
    mom1 = dict(norm_mix_g=m_norm_mix_g, w_in=m_w_in, b_forget=m_b_forget, conv_w=m_conv_w, conv_b=m_conv_b, conv_ln_g=m_conv_ln_g, conv_ln_b=m_conv_ln_b, w_conv_out=m_w_conv_out, w_attn_out=m_w_attn_out, w_out=m_w_out, norm_ffn_g=m_norm_ffn_g, w_gate_up=m_w_gate_up, w_down=m_w_down, norm_ple_g=m_norm_ple_g, w_ple_gate=m_w_ple_gate, w_ple_proj=m_w_ple_proj, final_g=m_final_g)
    mom2 = dict(norm_mix_g=v_norm_mix_g, w_in=v_w_in, b_forget=v_b_forget, conv_w=v_conv_w, conv_b=v_conv_b, conv_ln_g=v_conv_ln_g, conv_ln_b=v_conv_ln_b, w_conv_out=v_w_conv_out, w_attn_out=v_w_attn_out, w_out=v_w_out, norm_ffn_g=v_norm_ffn_g, w_gate_up=v_w_gate_up, w_down=v_w_down, norm_ple_g=v_norm_ple_g, w_ple_gate=v_w_ple_gate, w_ple_proj=v_w_ple_proj, final_g=v_final_g)
    depth = w_in.shape[0]
    xi, yi, ci = lax.axis_index("x"), lax.axis_index("y"), lax.axis_index("c")
    me = 4 * xi + 2 * yi + ci

    rest_names = [k for k in _BIG if k != "w_in"]
    first = _all_gather([weights["w_in"][0].astype(BF16), conv_w], "gather_weights_in_l0")
    gathered_conv_w = first[1]
    shards, _ = lax.optimization_barrier(([weights[k][0].astype(BF16) for k in rest_names], first[0]))
    rest = _all_gather_beside(shards, "gather_weights_rest_l0", collective_id=0)
    gathered = [dict(zip(rest_names, rest), w_in=first[0], conv_w=gathered_conv_w[:, 0])] + [None] * (depth - 1)

    def gather_layer(l, tie):
        shards, _ = lax.optimization_barrier(([weights[k][l].astype(BF16) for k in _BIG], tie))
        later = _all_gather_beside(shards, f"gather_weights_l{l}", collective_id=l)
        gathered[l] = dict(zip(_BIG, later), conv_w=gathered_conv_w[:, l])

    act = x[0]
    layers = []
    for l in range(depth):
        if l > 0:
            gathered[l], act = lax.optimization_barrier((gathered[l], act))
        small = {k: weights[k][l][None, :] for k in _SMALL if k != "b_forget"}
        small["b_forget"] = _pad_lanes(b_forget[l])
        gather_next = functools.partial(gather_layer, l + 1) if l + 1 < depth else None
        act, saved, w = _layer_fwd(act, p[l, 0], gathered[l], small, f"l{l}", after_mixers=gather_next)
        layers.append((w, small, saved))
    dact, g_final, loss_part = _loss_head(act, final_g[None, :], loss_target[0], "loss_head")
    loss = lax.psum(loss_part[0, 0], ("x", "y", "c"))

    core = ci.reshape(1).astype(jnp.int32)
    ffn_group = ("w_ple_gate", "w_ple_proj", "w_down", "w_gate_up")
    mix_group = ("w_out", "w_attn_out", "w_conv_out", "w_in")

    def reduce_scatter(gw, names, tie, tag, collective_id):
        blocks = [(gw[k] if gw[k].ndim == 3 else _shard_cols(gw[k]) if k in _COL_SHARDED
                   else gw[k].reshape((N_DEV, -1) + gw[k].shape[1:]))[:, None] for k in names]
        theirs = _pair_exchange(blocks, f"grads_pair_exchange_{tag}")
        chip_parts = [_pair_sum(a, b, core, f"grads_pair_sum_{k}_{tag}") for k, a, b in zip(names, blocks, theirs)]
        chip_parts, tie = lax.optimization_barrier((chip_parts, tie))
        return _chip_exchange_beside(chip_parts, f"grads_chip_exchange_{tag}", collective_id), tie

    arrived = {}
    small_grads = {k: [None] * depth for k in _SMALL + ("conv_w",)}
    waiting = None
    for l in reversed(range(depth)):
        w, small, saved = layers[l]
        dx1, dx1_b, gw, gs = _layer_bwd_ffn(dact, saved, w, small, f"l{l}")
        if waiting is not None:
            arrived[waiting], dx1 = lax.optimization_barrier((arrived[waiting], dx1))
        arrived[l, 0], dx1 = reduce_scatter(gw, ffn_group, dx1, f"ffn_l{l}", depth + 2 * l)
        dact, gw, gs_mix = _layer_bwd_mix(dx1, dx1_b, saved, w, small, f"l{l}")
        arrived[l, 0], dact = lax.optimization_barrier((arrived[l, 0], dact))
        arrived[l, 1], dact = reduce_scatter(gw, mix_group, dact, f"mix_l{l}", depth + 2 * l + 1)
        waiting = (l, 1)
        for k, g in {**gs, **gs_mix}.items():
            small_grads[k][l] = g
    grad_x = dact[None]

    out = {}
    for group, names in enumerate((ffn_group, mix_group)):
        for i, k in enumerate(names):
            parts = jnp.concatenate([arrived[l, group][i] for l in range(depth)], axis=1)
            out[k] = _sum_adamw(parts, weights[k], mom1[k], mom2[k], f"adamw_{k}")

    small_names = list(_SMALL) + ["conv_w", "final_g"]
    small_list = [jnp.stack(small_grads[k]) for k in _SMALL + ("conv_w",)] + [g_final[0]]
    n_small = sum(v.size for v in small_list)
    rows = -(-n_small // (8 * LANES)) * 8
    all_parts = _all_gather([_pack(small_list, rows)], "gather_small_grads")[0]
    full_conv_w = jnp.transpose(gathered_conv_w, (1, 2, 0, 3)).reshape(depth, CONV_K, -1)
    full_conv_m = jnp.zeros_like(full_conv_w)
    ch_shard = conv_w.shape[-1]
    at = (0, 0, me * ch_shard)
    small_w = [weights[k] for k in _SMALL] + [full_conv_w, final_g]
    small_m = [mom1[k] for k in _SMALL] + [lax.dynamic_update_slice(full_conv_m, mom1["conv_w"], at), mom1["final_g"]]
    small_v = [mom2[k] for k in _SMALL] + [lax.dynamic_update_slice(full_conv_m + 1.0, mom2["conv_w"], at), mom2["final_g"]]
    res = _sum_adamw(all_parts, _pack(small_w, rows), _pack(small_m, rows), _pack(small_v, rows), "adamw_small")
    unpacked = [_unpack(r, small_list) for r in res]
    for idx, k in enumerate(small_names):
        vals = [u[idx] for u in unpacked]
        if k == "conv_w":
            vals = [lax.dynamic_slice(v, at, conv_w.shape) for v in vals]
        out[k] = vals

    return (loss, grad_x, *[out[k][0] for k in _ORDER], *[out[k][1] for k in _ORDER],
            *[out[k][2] for k in _ORDER], *[out[k][3] for k in _ORDER])
```

```python
import functools
import math

import jax
import jax.numpy as jnp
from jax import lax
from jax.experimental import pallas as pl
from jax.experimental.pallas import tpu as pltpu
from jax.experimental.pallas import tpu_sc as plsc

F32 = jnp.float32
BF16 = jnp.bfloat16
MESH = pl.DeviceIdType.MESH
ANY = pl.BlockSpec(memory_space=pl.ANY)

N_DEV = 8
N_HEADS = 8
HEAD_DIM = 128
CONV_K = 31
HALO = 32
EPS = 1e-6
NEG_INF = -1e30
ADAM_LR = 0.001
ADAM_B1 = 0.9
ADAM_B2 = 0.999
ADAM_EPS = 1e-08
ADAM_WD = 0.01
ADAM_STEP = 10

V7X_VMEM_BYTES = 64 * 1024 * 1024
VMEM_LIMIT = V7X_VMEM_BYTES * 3 // 4
LANES = 128
ROW_TILE = 512
NORM_BWD_TILE = 256
CONV_TILE = 256
ATTN_TILE = 512
ATTN_BWD_KEYS = 1024
ATTN_FWD_TILE = 1024
ATTN_ROWS = 256
CUM_TILE = 256
FFN_K_TILE = 2816
FFN_N_TILE = 1408


def _params(*sem):
    return pltpu.CompilerParams(dimension_semantics=sem, vmem_limit_bytes=VMEM_LIMIT)


def _tile(dim, pref, mult=LANES):
    t = min(pref, dim)
    t -= t % mult
    while t >= mult:
        if dim % t == 0:
            return t
        t -= mult
    return dim


def _sigmoid(x):
    return 1.0 / (1.0 + jnp.exp(-x))


def _place():
    x, y, c = lax.axis_index("x"), lax.axis_index("y"), lax.axis_index("c")
    chips = [(1 - x, y), (x, 1 - y), (1 - x, 1 - y)]
    return x, y, c, chips


def _all_gather(shards, name):
    n = len(shards)

    def body(*refs):
        ins, outs = refs[:n], refs[n:2 * n]
        send_sems, recv_sems, local_sems = refs[2 * n:]
        x, y, c, chips = _place()
        me, sibling = (x, y, c), (x, y, 1 - c)

        def idx(px, py, pc):
            return 4 * px + 2 * py + pc

        def copy(t, k, block, to, src=None):
            rows = outs[t].at[idx(*block)]
            return pltpu.make_async_remote_copy(
                src_ref=rows if src is None else src, dst_ref=rows,
                send_sem=send_sems.at[t, k], recv_sem=recv_sems.at[t, k],
                device_id=to, device_id_type=MESH)

        local = [pltpu.make_async_copy(ins[t], outs[t].at[idx(*me)], local_sems.at[t]) for t in range(n)]
        for cp in local:
            cp.start()
        first = []
        for t in range(n):
            first.append(copy(t, 0, me, sibling, src=ins[t]))
            first += [copy(t, 1 + j, me, (*chip, c), src=ins[t]) for j, chip in enumerate(chips)]
        for cp in first:
            cp.start()
        passed = []
        for j, chip in enumerate(chips):
            for t in range(n):
                copy(t, 1 + j, (*chip, c), me).wait_recv()
                cp = copy(t, 4 + j, (*chip, c), sibling)
                cp.start()
                passed.append(cp)
        for t in range(n):
            copy(t, 0, sibling, me).wait_recv()
        for j, chip in enumerate(chips):
            for t in range(n):
                copy(t, 4 + j, (*chip, 1 - c), me).wait_recv()
        for cp in first + passed:
            cp.wait_send()
        for cp in local:
            cp.wait()

    return pl.pallas_call(
        body, name=name,
        out_shape=[jax.ShapeDtypeStruct((N_DEV,) + s.shape, s.dtype) for s in shards],
        in_specs=[ANY] * n, out_specs=[ANY] * n,
        scratch_shapes=[pltpu.SemaphoreType.DMA((n, 7)), pltpu.SemaphoreType.DMA((n, 7)),
                        pltpu.SemaphoreType.DMA((n,))],
    )(*shards)


def _all_gather_beside(shards, name, collective_id):
    n = len(shards)
    src = [jax.new_ref(s, memory_space=pltpu.MemorySpace.HBM) for s in shards]
    dst = [jax.empty_ref(jax.ShapeDtypeStruct((N_DEV,) + s.shape, s.dtype), memory_space=pltpu.MemorySpace.HBM)
           for s in shards]

    @pl.kernel(mesh=plsc.ScalarSubcoreMesh(axis_name="sequencer", num_cores=1), name=name,
               scratch_types=(pltpu.SemaphoreType.DMA((n, 7)), pltpu.SemaphoreType.DMA((n, 7)),
                              pltpu.SemaphoreType.DMA((n,))),
               compiler_params=pltpu.CompilerParams(collective_id=collective_id))
    def launch(send_sems, recv_sems, local_sems):
        x, y, c, chips = _place()
        me, sibling = (x, y, c), (x, y, 1 - c)
        barrier = pltpu.get_barrier_semaphore()
        for peer in [sibling] + [(*chip, pc) for chip in chips for pc in (c, 1 - c)]:
            pl.semaphore_signal(barrier, inc=1, device_id=peer, device_id_type=MESH)
        pl.semaphore_wait(barrier, N_DEV - 1)

        def idx(px, py, pc):
            return 4 * px + 2 * py + pc

        def copy(t, k, block, to, own=False):
            rows = dst[t].at[idx(*block)]
            return pltpu.make_async_remote_copy(
                src_ref=src[t] if own else rows, dst_ref=rows,
                send_sem=send_sems.at[t, k], recv_sem=recv_sems.at[t, k],
                device_id=to, device_id_type=MESH)

        local = [pltpu.make_async_copy(src[t], dst[t].at[idx(*me)], local_sems.at[t]) for t in range(n)]
        for cp in local:
            cp.start()
        first = []
        for t in range(n):
            first.append(copy(t, 0, me, sibling, own=True))
            first += [copy(t, 1 + j, me, (*chip, c), own=True) for j, chip in enumerate(chips)]
        for cp in first:
            cp.start()
        passed = []
        for j, chip in enumerate(chips):
            for t in range(n):
                copy(t, 1 + j, (*chip, c), me).wait_recv()
                cp = copy(t, 4 + j, (*chip, c), sibling)
                cp.start()
                passed.append(cp)
        for t in range(n):
            copy(t, 0, sibling, me).wait_recv()
        for j, chip in enumerate(chips):
            for t in range(n):
                copy(t, 4 + j, (*chip, 1 - c), me).wait_recv()
        for cp in first + passed:
            cp.wait_send()
        for cp in local:
            cp.wait()

    launch()
    return [d[...] for d in dst]


def _pair_exchange(grads, name):
    n = len(grads)

    def body(*refs):
        ins, theirs = refs[:n], refs[n:2 * n]
        send_sems, recv_sems = refs[2 * n:]
        x, y, c, _ = _place()
        copies = []
        for t in range(n):
            for q in range(4):
                give = pltpu.make_async_remote_copy(
                    src_ref=ins[t].at[2 * q + 1 - c], dst_ref=theirs[t].at[q],
                    send_sem=send_sems.at[t, q], recv_sem=recv_sems.at[t, q],
                    device_id=(x, y, 1 - c), device_id_type=MESH)
                give.start()
                copies.append(give)
        for cp in copies:
            cp.wait()

    return pl.pallas_call(
        body, name=name, out_shape=[jax.ShapeDtypeStruct((4,) + g.shape[1:], g.dtype) for g in grads],
        in_specs=[ANY] * n, out_specs=[ANY] * n,
        scratch_shapes=[pltpu.SemaphoreType.DMA((n, 4)), pltpu.SemaphoreType.DMA((n, 4))],
    )(*grads)


def _chip_exchange_beside(parts, name, collective_id):
    n = len(parts)
    src = [jax.new_ref(p, memory_space=pltpu.MemorySpace.HBM) for p in parts]
    dst = [jax.empty_ref(jax.ShapeDtypeStruct(p.shape, p.dtype), memory_space=pltpu.MemorySpace.HBM) for p in parts]

    @pl.kernel(mesh=plsc.ScalarSubcoreMesh(axis_name="sequencer", num_cores=1), name=name,
               scratch_types=(pltpu.SemaphoreType.DMA((n, 3)), pltpu.SemaphoreType.DMA((n, 3)),
                              pltpu.SemaphoreType.DMA((n,))),
               compiler_params=pltpu.CompilerParams(collective_id=collective_id))
    def launch(send_sems, recv_sems, local_sems):
        x, y, c, chips = _place()
        barrier = pltpu.get_barrier_semaphore()
        for px, py in chips:
            pl.semaphore_signal(barrier, inc=1, device_id=(px, py, c), device_id_type=MESH)
        pl.semaphore_wait(barrier, len(chips))
        copies = []
        for t in range(n):
            own = pltpu.make_async_copy(src[t].at[2 * x + y], dst[t].at[0], local_sems.at[t])
            own.start()
            copies.append(own)
            for j, (px, py) in enumerate(chips):
                cp = pltpu.make_async_remote_copy(
                    src_ref=src[t].at[2 * px + py], dst_ref=dst[t].at[1 + j],
                    send_sem=send_sems.at[t, j], recv_sem=recv_sems.at[t, j],
                    device_id=(px, py, c), device_id_type=MESH)
                cp.start()
                copies.append(cp)
        for cp in copies:
            cp.wait()

    launch()
    return [d[...] for d in dst]


_DOT_DIMS = {"nn": (((1,), (0,)), ((), ())), "nt": (((1,), (1,)), ((), ())), "tn": (((0,), (0,)), ((), ()))}


def _mm(a, b, mode, out_dtype, name, residual=None, tm=1024, tn=1024, tk=2048, out_blocks=None):
    if mode == "tn":
        (k_dim, m_dim), n_dim = a.shape, b.shape[1]
    else:
        (m_dim, k_dim), n_dim = a.shape, (b.shape[1] if mode == "nn" else b.shape[0])
    tm, tn, tk = _tile(m_dim, tm), _tile(n_dim // (out_blocks or 1), tn), _tile(k_dim, tk)
    nk = k_dim // tk

    def body(*refs):
        a_ref, b_ref = refs[:2]
        r_ref = refs[2] if residual is not None else None
        o_ref = refs[2 if residual is None else 3]

        def finish(r):
            if r_ref is not None:
                r = r + r_ref[...].astype(F32)
            o_ref[...] = r.astype(o_ref.dtype)

        part = lax.dot_general(a_ref[...].astype(BF16), b_ref[...].astype(BF16), _DOT_DIMS[mode],
                               preferred_element_type=F32)
        if nk == 1:
            finish(part)
            return
        acc, k = refs[-1], pl.program_id(2)

        @pl.when(k == 0)
        def _():
            acc[...] = part

        @pl.when((k > 0) & (k < nk - 1))
        def _():
            acc[...] += part

        @pl.when(k == nk - 1)
        def _():
            finish(acc[...] + part)

    a_spec = pl.BlockSpec((tk, tm), lambda i, j, k: (k, i)) if mode == "tn" else pl.BlockSpec((tm, tk), lambda i, j, k: (i, k))
    b_spec = pl.BlockSpec((tn, tk), lambda i, j, k: (j, k)) if mode == "nt" else pl.BlockSpec((tk, tn), lambda i, j, k: (k, j))
    o_spec, o_shape = pl.BlockSpec((tm, tn), lambda i, j, k: (i, j)), (m_dim, n_dim)
    if out_blocks:
        per = n_dim // out_blocks // tn
        o_spec = pl.BlockSpec((None, tm, tn), lambda i, j, k: (j // per, i, j % per))
        o_shape = (out_blocks, m_dim, n_dim // out_blocks)
    ins, specs = [a, b], [a_spec, b_spec]
    if residual is not None:
        ins.append(residual)
        specs.append(o_spec)
    return pl.pallas_call(
        body, name=name, grid=(m_dim // tm, n_dim // tn, nk),
        out_shape=jax.ShapeDtypeStruct(o_shape, out_dtype),
        in_specs=specs, out_specs=o_spec,
        scratch_shapes=[pltpu.VMEM((tm, tn), F32)] if nk > 1 else [],
        compiler_params=_params("parallel", "parallel", "arbitrary"),
    )(*ins)


def _rows(body, name, row_ins, full_ins, row_outs, acc_outs, ts=ROW_TILE):
    s = row_ins[0].shape[0]
    ts = _tile(s, ts, 16)
    n_ri, n_fi, n_ro = len(row_ins), len(full_ins), len(row_outs)

    def kernel(*refs):
        i = pl.program_id(0)
        ri, fi = refs[:n_ri], refs[n_ri:n_ri + n_fi]
        ro, ao = refs[n_ri + n_fi:n_ri + n_fi + n_ro], refs[n_ri + n_fi + n_ro:]

        if ao:
            @pl.when(i == 0)
            def _():
                for r in ao:
                    r[...] = jnp.zeros_like(r)

        body(i, ri, fi, ro, ao)

    def row_spec(cols):
        return pl.BlockSpec((ts, cols), lambda i: (i, 0))

    def full_spec(shape):
        return pl.BlockSpec(shape, lambda i: (0,) * len(shape))

    return pl.pallas_call(
        kernel, name=name, grid=(s // ts,),
        out_shape=[jax.ShapeDtypeStruct((s, c), d) for c, d in row_outs]
        + [jax.ShapeDtypeStruct(sh, d) for sh, d in acc_outs],
        in_specs=[row_spec(a.shape[1]) for a in row_ins] + [full_spec(a.shape) for a in full_ins],
        out_specs=[row_spec(c) for c, _ in row_outs] + [full_spec(sh) for sh, _ in acc_outs],
        compiler_params=_params("arbitrary"),
    )(*row_ins, *full_ins)


def _rms_stats(x):
    return lax.rsqrt(jnp.mean(x * x, axis=-1, keepdims=True) + EPS)


def _rms_fwd(x, g, name):
    def body(i, ri, fi, ro, ao):
        xv = ri[0][...]
        ro[0][...] = (xv * _rms_stats(xv) * fi[0][...]).astype(BF16)

    return _rows(body, name, [x], [g], [(x.shape[1], BF16)], [])[0]


def _rms_bwd_rows(xv, gv, dh):
    r = _rms_stats(xv)
    xhat = xv * r
    dxhat = dh * gv
    dx = r * (dxhat - xhat * jnp.mean(dxhat * xhat, axis=-1, keepdims=True))
    return dx, jnp.sum(dh * xhat, axis=0, keepdims=True)


def _rms_bwd(x, g, dh, dres, name):
    d = x.shape[1]

    def body(i, ri, fi, ro, ao):
        dx, dg = _rms_bwd_rows(ri[0][...], fi[0][...], ri[1][...].astype(F32))
        dx = dx + ri[2][...]
        ro[0][...] = dx
        ro[1][...] = dx.astype(BF16)
        ao[0][...] += dg

    return _rows(body, name, [x, dh, dres], [g], [(d, F32), (d, BF16)], [((1, d), F32)], ts=NORM_BWD_TILE)


def _loss_head(x, g, target, name):
    d = x.shape[1]

    def body(i, ri, fi, ro, ao):
        xv, gv = ri[0][...], fi[0][...]
        y = xv * _rms_stats(xv) * gv
        err = y - ri[1][...]
        ao[1][...] += 0.5 * jnp.sum(jnp.mean(err * err, axis=-1, keepdims=True))
        dx, dg = _rms_bwd_rows(xv, gv, err * (1.0 / d))
        ro[0][...] = dx
        ao[0][...] += dg

    return _rows(body, name, [x, target], [g], [(d, F32)], [((1, d), F32), ((8, LANES), F32)], ts=NORM_BWD_TILE)


def _merge_fwd(gg, y_conv, y_attn, name):
    d = y_conv.shape[1]

    def body(i, ri, fi, ro, ao):
        gv, yc, ya = (r[...].astype(F32) for r in ri)
        m = _sigmoid(gv[:, :d]) * yc + _sigmoid(gv[:, d:]) * ya
        ro[0][...] = m.astype(BF16)

    return _rows(body, name, [gg, y_conv, y_attn], [], [(d, BF16)], [])[0]


def _merge_bwd(gg, y_conv, y_attn, dm, name):
    d = y_conv.shape[1]

    def body(i, ri, fi, ro, ao):
        gv, yc, ya, dmv = (r[...].astype(F32) for r in ri)
        sc, sa = _sigmoid(gv[:, :d]), _sigmoid(gv[:, d:])
        ro[0][...] = (dmv * sc).astype(BF16)
        ro[1][...] = (dmv * sa).astype(BF16)
        ro[2][:, :d] = (dmv * yc * sc * (1.0 - sc)).astype(BF16)
        ro[2][:, d:] = (dmv * ya * sa * (1.0 - sa)).astype(BF16)

    return _rows(body, name, [gg, y_conv, y_attn, dm], [], [(d, BF16), (d, BF16), (2 * d, BF16)], [])


def _swiglu_fwd(gu, name):
    f = gu.shape[1] // 2

    def body(i, ri, fi, ro, ao):
        v = ri[0][...].astype(F32)
        gate, up = v[:, :f], v[:, f:]
        ro[0][...] = (gate * _sigmoid(gate) * up).astype(BF16)

    return _rows(body, name, [gu], [], [(f, BF16)], [], ts=256)[0]


def _swiglu_bwd(gu, dact, name):
    f = gu.shape[1] // 2

    def body(i, ri, fi, ro, ao):
        v, dv = ri[0][...].astype(F32), ri[1][...].astype(F32)
        gate, up = v[:, :f], v[:, f:]
        sg = _sigmoid(gate)
        ro[0][:, :f] = (dv * up * sg * (1.0 + gate * (1.0 - sg))).astype(BF16)
        ro[0][:, f:] = (dv * gate * sg).astype(BF16)

    return _rows(body, name, [gu, dact], [], [(2 * f, BF16)], [], ts=256)[0]


def _ple_fwd(x, z, pp, name):
    d = x.shape[1]

    def body(i, ri, fi, ro, ao):
        ro[0][...] = ri[0][...] + _sigmoid(ri[1][...].astype(F32)) * ri[2][...].astype(F32)

    return _rows(body, name, [x, z, pp], [], [(d, F32)], [])[0]


def _ple_bwd(dx, z, pp, name):
    d = dx.shape[1]

    def body(i, ri, fi, ro, ao):
        dv, sz = ri[0][...], _sigmoid(ri[1][...].astype(F32))
        ro[0][...] = (dv * ri[2][...].astype(F32) * sz * (1.0 - sz)).astype(BF16)
        ro[1][...] = (dv * sz).astype(BF16)

    return _rows(body, name, [dx, z, pp], [], [(d, BF16), (d, BF16)], [])


def _triangle(n, lower):
    r = lax.broadcasted_iota(jnp.int32, (n, n), 0)
    c = lax.broadcasted_iota(jnp.int32, (n, n), 1)
    return jnp.where((c <= r) if lower else (c >= r), 1.0, 0.0).astype(F32)


def _forget_fwd(f, bias, name):
    s = f.shape[0]
    tb = _tile(s, CUM_TILE, 8)

    def body(f_ref, b_ref, c_ref, carry):
        @pl.when(pl.program_id(0) == 0)
        def _():
            carry[...] = jnp.zeros_like(carry)

        xv = f_ref[...] + b_ref[...]
        logf = jnp.minimum(xv, 0.0) - jnp.log(1.0 + jnp.exp(-jnp.abs(xv)))
        cv = jnp.dot(_triangle(tb, True), logf, precision=lax.Precision.HIGHEST, preferred_element_type=F32) + carry[...]
        c_ref[...] = cv
        carry[...] = cv[tb - 1:tb, :]

    return pl.pallas_call(
        body, name=name, grid=(s // tb,), out_shape=jax.ShapeDtypeStruct((s, LANES), F32),
        in_specs=[pl.BlockSpec((tb, LANES), lambda i: (i, 0)), pl.BlockSpec((1, LANES), lambda i: (0, 0))],
        out_specs=pl.BlockSpec((tb, LANES), lambda i: (i, 0)),
        scratch_shapes=[pltpu.VMEM((1, LANES), F32)],
        compiler_params=_params("arbitrary"),
    )(f, bias)


def _forget_bwd(f, bias, dc, name):
    s = f.shape[0]
    tb = _tile(s, CUM_TILE, 16)
    nb = s // tb

    def body(f_ref, b_ref, dc_ref, df_ref, db_ref, carry):
        @pl.when(pl.program_id(0) == 0)
        def _():
            carry[...] = jnp.zeros_like(carry)
            db_ref[...] = jnp.zeros_like(db_ref)

        dlog = jnp.dot(_triangle(tb, False), dc_ref[...], precision=lax.Precision.HIGHEST,
                       preferred_element_type=F32) + carry[...]
        carry[...] = dlog[0:1, :]
        dlogit = dlog * (1.0 - _sigmoid(f_ref[...] + b_ref[...]))
        df_ref[...] = dlogit.astype(BF16)
        db_ref[...] += jnp.sum(dlogit, axis=0, keepdims=True)

    back = lambda i: (nb - 1 - i, 0)
    return pl.pallas_call(
        body, name=name, grid=(nb,),
        out_shape=[jax.ShapeDtypeStruct((s, LANES), BF16), jax.ShapeDtypeStruct((1, LANES), F32)],
        in_specs=[pl.BlockSpec((tb, LANES), back), pl.BlockSpec((1, LANES), lambda i: (0, 0)),
                  pl.BlockSpec((tb, LANES), back)],
        out_specs=[pl.BlockSpec((tb, LANES), back), pl.BlockSpec((1, LANES), lambda i: (0, 0))],
        scratch_shapes=[pltpu.VMEM((1, LANES), F32)],
        compiler_params=_params("arbitrary"),
    )(f, bias, dc)


def _glu(block, ch):
    return block[:, :ch] * _sigmoid(block[:, ch:])


def _layernorm_rows(u1, g, b):
    mu = jnp.mean(u1, axis=-1, keepdims=True)
    cen = u1 - mu
    rstd = lax.rsqrt(jnp.mean(cen * cen, axis=-1, keepdims=True) + EPS)
    xhat = cen * rstd
    return xhat, rstd, xhat * g + b


def _conv_fwd(ag, conv_w, conv_b, ln_g, ln_b, name):
    s, ch = ag.shape[0], ag.shape[1] // 2
    ts = _tile(s, CONV_TILE, HALO)
    per = ts // HALO

    def body(cur_ref, halo_ref, w_ref, cb_ref, g_ref, b_ref, u1_ref, u3_ref, cat):
        i = pl.program_id(0)
        cat[HALO:, :] = _glu(cur_ref[...], ch)
        cat[:HALO, :] = jnp.where(i == 0, 0.0, _glu(halo_ref[...], ch))
        for c0 in range(0, ch, LANES):
            cols = slice(c0, c0 + LANES)
            full = cat[:, cols]
            acc = jnp.zeros((ts, LANES), F32) + cb_ref[:, cols]
            for j in range(CONV_K):
                shifted = full if j == CONV_K - 1 else pltpu.roll(full, CONV_K - 1 - j, 0)
                acc = acc + w_ref[j:j + 1, cols] * shifted[HALO:, :]
            u1_ref[:, cols] = acc
        _, _, u2 = _layernorm_rows(u1_ref[...], g_ref[...], b_ref[...])
        u3_ref[...] = (u2 * _sigmoid(u2)).astype(BF16)

    whole = lambda shape: pl.BlockSpec(shape, lambda i: (0, 0))
    return pl.pallas_call(
        body, name=name, grid=(s // ts,),
        out_shape=[jax.ShapeDtypeStruct((s, ch), F32), jax.ShapeDtypeStruct((s, ch), BF16)],
        in_specs=[pl.BlockSpec((ts, 2 * ch), lambda i: (i, 0)),
                  pl.BlockSpec((HALO, 2 * ch), lambda i: (jnp.maximum(i * per - 1, 0), 0)),
                  whole(conv_w.shape), whole((1, ch)), whole((1, ch)), whole((1, ch))],
        out_specs=[pl.BlockSpec((ts, ch), lambda i: (i, 0)), pl.BlockSpec((ts, ch), lambda i: (i, 0))],
        scratch_shapes=[pltpu.VMEM((HALO + ts, ch), F32)],
        compiler_params=_params("arbitrary"),
    )(ag, ag, conv_w, conv_b, ln_g, ln_b)


def _conv_norm_bwd(u1, du3, ln_g, ln_b, name):
    ch = u1.shape[1]

    def body(i, ri, fi, ro, ao):
        g = fi[0][...]
        xhat, rstd, u2 = _layernorm_rows(ri[0][...], g, fi[1][...])
        sg = _sigmoid(u2)
        du2 = ri[1][...].astype(F32) * sg * (1.0 + u2 * (1.0 - sg))
        dxhat = du2 * g
        du1 = rstd * (dxhat - jnp.mean(dxhat, axis=-1, keepdims=True)
                      - xhat * jnp.mean(dxhat * xhat, axis=-1, keepdims=True))
        ro[0][...] = du1
        ao[0][...] += jnp.sum(du2 * xhat, axis=0, keepdims=True)
        ao[1][...] += jnp.sum(du2, axis=0, keepdims=True)
        ao[2][...] += jnp.sum(du1, axis=0, keepdims=True)

    return _rows(body, name, [u1, du3], [ln_g, ln_b], [(ch, F32)], [((1, ch), F32)] * 3)


def _conv_bwd(ag, du1, conv_w, name):
    s, ch = du1.shape
    ts = _tile(s, CONV_TILE, HALO)
    per = ts // HALO
    last = s // HALO - 1
    nb = s // ts

    def body(cur_ref, halo_ref, d_ref, dnext_ref, w_ref, dag_ref, dw_ref, cat, dcat):
        i = pl.program_id(0)

        @pl.when(i == 0)
        def _():
            dw_ref[...] = jnp.zeros_like(dw_ref)

        cat[HALO:, :] = _glu(cur_ref[...], ch)
        cat[:HALO, :] = jnp.where(i == 0, 0.0, _glu(halo_ref[...], ch))
        dcat[:ts, :] = d_ref[...]
        dcat[ts:, :] = jnp.where(i == nb - 1, 0.0, dnext_ref[...])
        for c0 in range(0, ch, LANES):
            cols, gate_cols = slice(c0, c0 + LANES), slice(ch + c0, ch + c0 + LANES)
            full, dfull = cat[:, cols], dcat[:, cols]
            dcur = dfull[:ts, :]
            du0 = jnp.zeros((ts, LANES), F32)
            for j in range(CONV_K):
                back = CONV_K - 1 - j
                dshift = dfull if back == 0 else pltpu.roll(dfull, ts + HALO - back, 0)
                du0 = du0 + w_ref[j:j + 1, cols] * dshift[:ts, :]
                ushift = full if back == 0 else pltpu.roll(full, back, 0)
                dw_ref[j:j + 1, cols] += jnp.sum(dcur * ushift[HALO:, :], axis=0, keepdims=True)
            a, sg = cur_ref[:, cols], _sigmoid(cur_ref[:, gate_cols])
            dag_ref[:, cols] = (du0 * sg).astype(BF16)
            dag_ref[:, gate_cols] = (du0 * a * sg * (1.0 - sg)).astype(BF16)

    return pl.pallas_call(
        body, name=name, grid=(nb,),
        out_shape=[jax.ShapeDtypeStruct((s, 2 * ch), BF16), jax.ShapeDtypeStruct(conv_w.shape, F32)],
        in_specs=[pl.BlockSpec((ts, 2 * ch), lambda i: (i, 0)),
                  pl.BlockSpec((HALO, 2 * ch), lambda i: (jnp.maximum(i * per - 1, 0), 0)),
                  pl.BlockSpec((ts, ch), lambda i: (i, 0)),
                  pl.BlockSpec((HALO, ch), lambda i: (jnp.minimum((i + 1) * per, last), 0)),
                  pl.BlockSpec(conv_w.shape, lambda i: (0, 0))],
        out_specs=[pl.BlockSpec((ts, 2 * ch), lambda i: (i, 0)), pl.BlockSpec(conv_w.shape, lambda i: (0, 0))],
        scratch_shapes=[pltpu.VMEM((HALO + ts, ch), F32), pltpu.VMEM((ts + HALO, ch), F32)],
        compiler_params=_params("arbitrary"),
    )(ag, ag, du1, du1, conv_w)


_SCALE = 1.0 / math.sqrt(HEAD_DIM)
_LOG2E = math.log2(math.e)


STAT_LANES = 8


def _causal(rows, cols, row0):
    r = row0 + lax.broadcasted_iota(jnp.int32, (rows, cols), 0)
    return lax.broadcasted_iota(jnp.int32, (rows, cols), 1) <= r


def _attn_fwd(qkv, c_row, name):
    s = qkv.shape[0]
    t = _tile(s, ATTN_FWD_TILE)
    nt, part_rows = s // t, min(ATTN_ROWS, t)
    n_parts = t // part_rows
    h_, d = N_HEADS, HEAD_DIM

    def body(q_ref, k_ref, v_ref, cr_ref, o_ref, lse_ref, *scratch):
        m_sc, l_sc, acc = scratch[:n_parts], scratch[n_parts:2 * n_parts], scratch[2 * n_parts:]
        i = pl.program_id(1)
        for part in range(n_parts):
            m_sc[part][...] = jnp.full_like(m_sc[part], NEG_INF)
            l_sc[part][...] = jnp.zeros_like(l_sc[part])
            acc[part][...] = jnp.zeros_like(acc[part])

        def block(j, diagonal):
            at = pl.multiple_of(j * t, t)
            k, v = k_ref[pl.ds(at, t), :], v_ref[pl.ds(at, t), :]
            cr = cr_ref[0, pl.ds(j, 1), :] * _LOG2E
            for part in range(n_parts):
                rows = pl.ds(part * part_rows, part_rows)
                sc = lax.dot_general(q_ref[rows, :], k, _DOT_DIMS["nt"], preferred_element_type=F32)
                sc = sc * (_SCALE * _LOG2E) - cr
                if diagonal:
                    sc = jnp.where(_causal(part_rows, t, part * part_rows), sc, NEG_INF)
                m_old = m_sc[part][...]
                m_new = jnp.maximum(m_old, jnp.max(sc, axis=-1, keepdims=True))
                alpha = jnp.exp2(m_old - m_new)
                p = jnp.exp2(sc - m_new)
                l_sc[part][...] = alpha * l_sc[part][...] + jnp.sum(p, axis=-1, keepdims=True)
                acc[part][...] = alpha * acc[part][...] + jnp.dot(p.astype(BF16), v, preferred_element_type=F32)
                m_sc[part][...] = m_new

        def below(j, carry):
            block(j, False)
            return carry

        lax.fori_loop(0, i, below, 0)
        block(i, True)
        for part in range(n_parts):
            rows = pl.ds(part * part_rows, part_rows)
            o_ref[rows, :] = (acc[part][...] / l_sc[part][...]).astype(BF16)
            lse_ref[0, rows, :] = m_sc[part][...] * (1.0 / _LOG2E) + jnp.log(l_sc[part][...])

    return pl.pallas_call(
        body, name=name, grid=(h_, nt),
        out_shape=[jax.ShapeDtypeStruct((s, h_ * d), BF16), jax.ShapeDtypeStruct((h_, s, 1), F32)],
        in_specs=[pl.BlockSpec((t, d), lambda h, i: (i, h)),
                  pl.BlockSpec((s, d), lambda h, i: (0, h_ + h)),
                  pl.BlockSpec((s, d), lambda h, i: (0, 2 * h_ + h)),
                  pl.BlockSpec((1, nt, t), lambda h, i: (h, 0, 0))],
        out_specs=[pl.BlockSpec((t, d), lambda h, i: (i, h)), pl.BlockSpec((1, t, 1), lambda h, i: (h, i, 0))],
        scratch_shapes=[pltpu.VMEM((part_rows, 1), F32)] * (2 * n_parts) + [pltpu.VMEM((part_rows, d), F32)] * n_parts,
        compiler_params=_params("parallel", "arbitrary"),
    )(qkv, qkv, qkv, c_row.reshape(h_, nt, t))


def _attn_stats(o, do, lse, name):
    s = o.shape[0]
    t = _tile(s, ATTN_TILE)
    h_, d = N_HEADS, HEAD_DIM

    def body(o_ref, do_ref, lse_ref, st_ref, dob_ref):
        dov = do_ref[...]
        st_ref[...] = jnp.zeros_like(st_ref)
        st_ref[0, :, 0:1] = -lse_ref[0]
        st_ref[0, :, 1:2] = jnp.sum(o_ref[...].astype(F32) * dov, axis=-1, keepdims=True)
        dob_ref[...] = dov.astype(BF16)

    col = pl.BlockSpec((1, t, 1), lambda h, i: (h, i, 0))
    blk = pl.BlockSpec((t, d), lambda h, i: (i, h))
    return pl.pallas_call(
        body, name=name, grid=(h_, s // t),
        out_shape=[jax.ShapeDtypeStruct((h_, s, STAT_LANES), F32), jax.ShapeDtypeStruct(o.shape, BF16)],
        in_specs=[blk, blk, col],
        out_specs=[pl.BlockSpec((1, t, STAT_LANES), lambda h, i: (h, i, 0)), blk],
        compiler_params=_params("parallel", "parallel"),
    )(o, do, lse)


def _attn_bwd(qkv, do, stats, c_row, name):
    s = qkv.shape[0]
    tq = _tile(s, ATTN_TILE)
    tk = _tile(s, ATTN_BWD_KEYS, tq)
    nq, nk, per = s // tq, s // tk, tk // tq
    h_, d = N_HEADS, HEAD_DIM

    def body(q_ref, do_ref, k_ref, v_ref, st_ref, cr_ref, dq_ref, dk_ref, dv_ref, dcr_ref, dcc_ref,
             dq_acc, dk_acc, dv_acc, dcr_acc):
        j = pl.program_id(1)

        @pl.when(j == 0)
        def _():
            dq_acc[...] = jnp.zeros_like(dq_acc)
            dcc_ref[...] = jnp.zeros_like(dcc_ref)

        dk_acc[...] = jnp.zeros_like(dk_acc)
        dv_acc[...] = jnp.zeros_like(dv_acc)
        dcr_acc[...] = jnp.zeros_like(dcr_acc)
        k, v, cr = k_ref[...], v_ref[...], cr_ref[0]

        def block(i, row0):
            rows = pl.ds(pl.multiple_of(i * tq, tq), tq)
            q, dov, st = q_ref[rows, :], do_ref[rows, :], st_ref[0, rows, :]
            sc = lax.dot_general(q, k, _DOT_DIMS["nt"], preferred_element_type=F32) * _SCALE + (st[:, 0:1] - cr)
            p = jnp.exp(sc)
            if row0 is not None:
                p = jnp.where(_causal(tq, tk, row0), p, 0.0)
            dp = lax.dot_general(dov, v, _DOT_DIMS["nt"], preferred_element_type=F32)
            ds = p * (dp - st[:, 1:2])
            ds_b = ds.astype(BF16)
            dv_acc[...] += lax.dot_general(p.astype(BF16), dov, _DOT_DIMS["tn"], preferred_element_type=F32)
            dk_acc[...] += lax.dot_general(ds_b, q, _DOT_DIMS["tn"], preferred_element_type=F32)
            dq_acc[rows, :] += jnp.dot(ds_b, k, preferred_element_type=F32)
            dcr_acc[...] -= jnp.sum(ds, axis=0, keepdims=True)
            dcc_ref[0, rows, :] += jnp.sum(ds, axis=-1, keepdims=True)

        def below(i, carry):
            block(i, None)
            return carry

        for part in range(per):
            block(j * per + part, part * tq)
        lax.fori_loop((j + 1) * per, nq, below, 0)
        dk_ref[...] = (dk_acc[...] * _SCALE).astype(BF16)
        dv_ref[...] = dv_acc[...].astype(BF16)
        dcr_ref[0] = dcr_acc[...]

        @pl.when(j == nk - 1)
        def _():
            dq_ref[...] = (dq_acc[...] * _SCALE).astype(BF16)

    head = pl.BlockSpec((s, d), lambda h, j: (0, h))
    return pl.pallas_call(
        body, name=name, grid=(h_, nk),
        out_shape=[jax.ShapeDtypeStruct((s, h_ * d), BF16)] * 3
        + [jax.ShapeDtypeStruct((h_, 1, s), F32), jax.ShapeDtypeStruct((h_, s, 1), F32)],
        in_specs=[head, head,
                  pl.BlockSpec((tk, d), lambda h, j: (j, h_ + h)),
                  pl.BlockSpec((tk, d), lambda h, j: (j, 2 * h_ + h)),
                  pl.BlockSpec((1, s, STAT_LANES), lambda h, j: (h, 0, 0)),
                  pl.BlockSpec((1, 1, tk), lambda h, j: (h, 0, j))],
        out_specs=[head, pl.BlockSpec((tk, d), lambda h, j: (j, h)), pl.BlockSpec((tk, d), lambda h, j: (j, h)),
                   pl.BlockSpec((1, 1, tk), lambda h, j: (h, 0, j)), pl.BlockSpec((1, s, 1), lambda h, j: (h, 0, 0))],
        scratch_shapes=[pltpu.VMEM((s, d), F32), pltpu.VMEM((tk, d), F32), pltpu.VMEM((tk, d), F32),
                        pltpu.VMEM((1, tk), F32)],
        compiler_params=_params("parallel", "arbitrary"),
    )(qkv, do, qkv, qkv, stats, c_row)


def _pair_sum(grads, theirs, core, name):
    _, l, r, c = theirs.shape
    tr = _tile(r, 256, 16)

    def body(core_ref, a_ref, b_ref, o_ref):
        o_ref[...] = (a_ref[...].astype(F32) + b_ref[...].astype(F32)).astype(BF16)

    spec = pl.BlockSpec((1, 1, tr, c), lambda q, k, i, core_ref: (q, k, i, 0))
    mine = pl.BlockSpec((1, 1, tr, c), lambda q, k, i, core_ref: (2 * q + core_ref[0], k, i, 0))
    return pl.pallas_call(
        body, name=name, out_shape=jax.ShapeDtypeStruct(theirs.shape, BF16),
        grid_spec=pltpu.PrefetchScalarGridSpec(num_scalar_prefetch=1, grid=(4, l, r // tr),
                                               in_specs=[mine, spec], out_specs=spec),
        compiler_params=_params("parallel", "parallel", "parallel"),
    )(core, grads, theirs)


def _adamw(g, w, m, v):
    m = ADAM_B1 * m + (1.0 - ADAM_B1) * g
    v = ADAM_B2 * v + (1.0 - ADAM_B2) * (g * g)
    m_hat = m / (1.0 - ADAM_B1 ** ADAM_STEP)
    v_hat = v / (1.0 - ADAM_B2 ** ADAM_STEP)
    return -ADAM_LR * (m_hat / (jnp.sqrt(v_hat) + ADAM_EPS) + ADAM_WD * w), m, v


def _sum_adamw(parts, w, m, v, name):
    n_parts, l, r, c = parts.shape
    tr = _tile(r, 128, 16)

    def body(p_ref, w_ref, m_ref, v_ref, g_ref, d_ref, nm_ref, nv_ref):
        g = p_ref[0].astype(F32)
        for k in range(1, n_parts):
            g = g + p_ref[k].astype(F32)
        g_ref[...] = g
        d_ref[...], nm_ref[...], nv_ref[...] = _adamw(g, w_ref[...], m_ref[...], v_ref[...])

    spec = pl.BlockSpec((1, tr, c), lambda k, i: (k, i, 0))
    return pl.pallas_call(
        body, name=name, grid=(l, r // tr), out_shape=[jax.ShapeDtypeStruct(w.shape, F32)] * 4,
        in_specs=[pl.BlockSpec((n_parts, 1, tr, c), lambda k, i: (0, k, i, 0)), spec, spec, spec],
        out_specs=[spec] * 4, compiler_params=_params("parallel", "parallel"),
    )(parts, w, m, v)


def _unshard_cols(gathered):
    return jnp.transpose(gathered, (1, 0, 2)).reshape(gathered.shape[1], -1)


def _shard_cols(full):
    k = full.shape[0]
    return jnp.transpose(full.reshape(k, N_DEV, -1), (1, 0, 2))


def _in_weights(w, d_model):
    hd = N_HEADS * HEAD_DIM
    w_in = _unshard_cols(w["w_in"])
    f0 = d_model + 3 * hd
    return {
        "w_ag": w_in[:, :d_model], "w_qkv": w_in[:, d_model:f0],
        "w_f": jnp.pad(w_in[:, f0:f0 + N_HEADS], ((0, 0), (0, LANES - N_HEADS))),
        "w_gg": w_in[:, f0 + N_HEADS:],
        "conv_w": jnp.transpose(w["conv_w"], (1, 0, 2)).reshape(CONV_K, -1),
    }


def _rest_weights(w, d_model):
    return {
        "w_conv_out": _unshard_cols(w["w_conv_out"]), "w_attn_out": _unshard_cols(w["w_attn_out"]),
        "w_out": w["w_out"].reshape(-1, d_model), "w_gate_up": _unshard_cols(w["w_gate_up"]),
        "w_down": w["w_down"].reshape(-1, d_model), "w_ple_gate": w["w_ple_gate"].reshape(-1, d_model),
        "w_ple_proj": _unshard_cols(w["w_ple_proj"]),
    }


def _pad_lanes(vec):
    return jnp.pad(vec, (0, LANES - vec.shape[0]))[None, :]


def _layer_fwd(x, p, shards, small, tag, after_mixers=None):
    d_model = x.shape[1]
    w = _in_weights(shards, d_model)
    h = _rms_fwd(x, small["norm_mix_g"], f"rms_mix_{tag}")
    ag = _mm(h, w["w_ag"], "nn", F32, f"proj_glu_{tag}")
    qkv = _mm(h, w["w_qkv"], "nn", BF16, f"proj_qkv_{tag}")
    gg = _mm(h, w["w_gg"], "nn", BF16, f"proj_gates_{tag}")
    f = _mm(h, w["w_f"], "nn", F32, f"proj_forget_{tag}")
    c = _forget_fwd(f, small["b_forget"], f"forget_{tag}")
    c_heads = jnp.transpose(c[:, :N_HEADS])
    c_row = c_heads[:, None, :]
    o, lse = _attn_fwd(qkv, c_row, f"attn_{tag}")
    u1, u3 = _conv_fwd(ag, w["conv_w"], small["conv_b"], small["conv_ln_g"], small["conv_ln_b"], f"conv_{tag}")
    rest, (u3, o) = lax.optimization_barrier(({k: shards[k] for k in _BIG if k != "w_in"}, (u3, o)))
    w.update(_rest_weights(rest, d_model))
    y_conv = _mm(u3, w["w_conv_out"], "nn", BF16, f"conv_out_{tag}")
    y_attn = _mm(o, w["w_attn_out"], "nn", BF16, f"attn_out_{tag}")
    if after_mixers is not None:
        after_mixers(y_attn)
    merged = _merge_fwd(gg, y_conv, y_attn, f"merge_{tag}")
    x1 = _mm(merged, w["w_out"], "nn", F32, f"mix_out_{tag}", residual=x)
    hf = _rms_fwd(x1, small["norm_ffn_g"], f"rms_ffn_{tag}")
    gu = _mm(hf, w["w_gate_up"], "nn", BF16, f"ffn_up_{tag}")
    act = _swiglu_fwd(gu, f"swiglu_{tag}")
    x2 = _mm(act, w["w_down"], "nn", F32, f"ffn_down_{tag}", residual=x1)
    hp = _rms_fwd(x2, small["norm_ple_g"], f"rms_ple_{tag}")
    z = _mm(hp, w["w_ple_gate"], "nn", BF16, f"ple_gate_{tag}")
    pp = _mm(p, w["w_ple_proj"], "nn", BF16, f"ple_proj_{tag}")
    x3 = _ple_fwd(x2, z, pp, f"ple_{tag}")
    saved = dict(x=x, p=p, h=h, ag=ag, qkv=qkv, gg=gg, f=f, u1=u1, u3=u3, y_conv=y_conv, c_row=c_row,
                 o=o, lse=lse, y_attn=y_attn, merged=merged, x1=x1, hf=hf, gu=gu, act=act, x2=x2, hp=hp, z=z, pp=pp)
    return x3, saved, w


def _layer_bwd_ffn(dx3, sv, w, small, tag):
    gw, gs = {}, {}
    dz, dpp = _ple_bwd(dx3, sv["z"], sv["pp"], f"ple_bwd_{tag}")
    gw["w_ple_proj"] = _mm(sv["p"], dpp, "tn", BF16, f"ple_proj_dw_{tag}")
    gw["w_ple_gate"] = _mm(sv["hp"], dz, "tn", BF16, f"ple_gate_dw_{tag}")
    dhp = _mm(dz, w["w_ple_gate"], "nt", BF16, f"ple_gate_dx_{tag}")
    dx2, dx2_b, gs["norm_ple_g"] = _rms_bwd(sv["x2"], small["norm_ple_g"], dhp, dx3, f"rms_ple_bwd_{tag}")
    gw["w_down"] = _mm(sv["act"], dx2_b, "tn", BF16, f"ffn_down_dw_{tag}", tm=FFN_N_TILE)
    dact = _mm(dx2_b, w["w_down"], "nt", BF16, f"ffn_down_dx_{tag}", tn=FFN_N_TILE)
    dgu = _swiglu_bwd(sv["gu"], dact, f"swiglu_bwd_{tag}")
    gw["w_gate_up"] = _mm(sv["hf"], dgu, "tn", BF16, f"ffn_up_dw_{tag}", tn=FFN_N_TILE, out_blocks=N_DEV)
    dhf = _mm(dgu, w["w_gate_up"], "nt", BF16, f"ffn_up_dx_{tag}", tk=FFN_K_TILE)
    dx1, dx1_b, gs["norm_ffn_g"] = _rms_bwd(sv["x1"], small["norm_ffn_g"], dhf, dx2, f"rms_ffn_bwd_{tag}")
    for k in ("norm_ple_g", "norm_ffn_g"):
        gs[k] = gs[k][0]
    return dx1, dx1_b, gw, gs


def _layer_bwd_mix(dx1, dx1_b, sv, w, small, tag):
    gw, gs = {}, {}
    gw["w_out"] = _mm(sv["merged"], dx1_b, "tn", BF16, f"mix_out_dw_{tag}")
    dm = _mm(dx1_b, w["w_out"], "nt", BF16, f"mix_out_dx_{tag}")
    dyc, dya, dgg = _merge_bwd(sv["gg"], sv["y_conv"], sv["y_attn"], dm, f"merge_bwd_{tag}")
    gw["w_attn_out"] = _mm(sv["o"], dya, "tn", BF16, f"attn_out_dw_{tag}")
    do = _mm(dya, w["w_attn_out"], "nt", F32, f"attn_out_dx_{tag}")
    stats, do_b = _attn_stats(sv["o"], do, sv["lse"], f"attn_stats_{tag}")
    dq, dk, dv, dc_row, dc_col = _attn_bwd(sv["qkv"], do_b, stats, sv["c_row"], f"attn_bwd_{tag}")
    dc = jnp.pad(jnp.transpose(dc_row[:, 0, :] + dc_col[:, :, 0]), ((0, 0), (0, LANES - N_HEADS)))
    df, db = _forget_bwd(sv["f"], small["b_forget"], dc, f"forget_bwd_{tag}")
    gs["b_forget"] = db[0, :N_HEADS]
    gw["w_conv_out"] = _mm(sv["u3"], dyc, "tn", BF16, f"conv_out_dw_{tag}")
    du3 = _mm(dyc, w["w_conv_out"], "nt", BF16, f"conv_out_dx_{tag}")
    du1, gs["conv_ln_g"], gs["conv_ln_b"], gs["conv_b"] = _conv_norm_bwd(
        sv["u1"], du3, small["conv_ln_g"], small["conv_ln_b"], f"conv_norm_bwd_{tag}")
    dag, gs["conv_w"] = _conv_bwd(sv["ag"], du1, w["conv_w"], f"conv_bwd_{tag}")
    dqkv = jnp.concatenate([dq, dk, dv], axis=1)
    g_ag = _mm(sv["h"], dag, "tn", BF16, f"proj_glu_dw_{tag}")
    g_qkv = _mm(sv["h"], dqkv, "tn", BF16, f"proj_qkv_dw_{tag}")
    g_f = _mm(sv["h"], df, "tn", BF16, f"proj_forget_dw_{tag}")
    g_gg = _mm(sv["h"], dgg, "tn", BF16, f"proj_gates_dw_{tag}")
    gw["w_in"] = jnp.concatenate([g_ag, g_qkv, g_f[:, :N_HEADS], g_gg], axis=1)
    dh = _mm(df, w["w_f"], "nt", F32, f"proj_forget_dx_{tag}")
    dh = _mm(dag, w["w_ag"], "nt", F32, f"proj_glu_dx_{tag}", residual=dh)
    dh = _mm(dqkv, w["w_qkv"], "nt", F32, f"proj_qkv_dx_{tag}", residual=dh)
    dh = _mm(dgg, w["w_gg"], "nt", F32, f"proj_gates_dx_{tag}", residual=dh)
    dx, _, gs["norm_mix_g"] = _rms_bwd(sv["x"], small["norm_mix_g"], dh, dx1, f"rms_mix_bwd_{tag}")
    for k in ("norm_mix_g", "conv_ln_g", "conv_ln_b", "conv_b"):
        gs[k] = gs[k][0]
    return dx, gw, gs


_COL_SHARDED = ("w_in", "w_conv_out", "w_attn_out", "w_gate_up", "w_ple_proj")
_ROW_SHARDED = ("w_out", "w_down", "w_ple_gate")
_BIG = _COL_SHARDED + _ROW_SHARDED
_SMALL = ("norm_mix_g", "b_forget", "conv_b", "conv_ln_g", "conv_ln_b", "norm_ffn_g", "norm_ple_g")
_ORDER = ("norm_mix_g", "w_in", "b_forget", "conv_w", "conv_b", "conv_ln_g", "conv_ln_b", "w_conv_out", "w_attn_out",
          "w_out", "norm_ffn_g", "w_gate_up", "w_down", "norm_ple_g", "w_ple_gate", "w_ple_proj", "final_g")


def _pack(vectors, rows):
    flat = jnp.concatenate([v.reshape(-1) for v in vectors])
    return jnp.pad(flat, (0, rows * LANES - flat.shape[0])).reshape(1, rows, LANES)


def _unpack(packed, like):
    flat, out, at = packed.reshape(-1), [], 0
    for v in like:
        out.append(flat[at:at + v.size].reshape(v.shape))
        at += v.size
    return out


def kernel(x, p, norm_mix_g, w_in, b_forget, conv_w, conv_b, conv_ln_g, conv_ln_b, w_conv_out, w_attn_out, w_out, norm_ffn_g, w_gate_up, w_down, norm_ple_g, w_ple_gate, w_ple_proj, final_g, loss_target, m_norm_mix_g, m_w_in, m_b_forget, m_conv_w, m_conv_b, m_conv_ln_g, m_conv_ln_b, m_w_conv_out, m_w_attn_out, m_w_out, m_norm_ffn_g, m_w_gate_up, m_w_down, m_norm_ple_g, m_w_ple_gate, m_w_ple_proj, m_final_g, v_norm_mix_g, v_w_in, v_b_forget, v_conv_w, v_conv_b, v_conv_ln_g, v_conv_ln_b, v_w_conv_out, v_w_attn_out, v_w_out, v_norm_ffn_g, v_w_gate_up, v_w_down, v_norm_ple_g, v_w_ple_gate, v_w_ple_proj, v_final_g):
    weights = dict(norm_mix_g=norm_mix_g, w_in=w_in, b_forget=b_forget, conv_w=conv_w, conv_b=conv_b, conv_ln_g=conv_ln_g, conv_ln_b=conv_ln_b, w_conv_out=w_conv_out, w_attn_out=w_attn_out, w_out=w_out, norm_ffn_g=norm_ffn_g, w_gate_up=w_gate_up, w_down=w_down, norm_ple_g=norm_ple_g, w_ple_gate=w_ple_gate, w_ple_proj=w_ple_proj, final_g=final_g)
    mom1 = dict(norm_mix_g=m_norm_mix_g, w_in=m_w_in, b_forget=m_b_forget, conv_w=m_conv_w, conv_b=m_conv_b, conv_ln_g=m_conv_ln_g, conv_ln_b=m_conv_ln_b, w_conv_out=m_w_conv_out, w_attn_out=m_w_attn_out, w_out=m_w_out, norm_ffn_g=m_norm_ffn_g, w_gate_up=m_w_gate_up, w_down=m_w_down, norm_ple_g=m_norm_ple_g, w_ple_gate=m_w_ple_gate, w_ple_proj=m_w_ple_proj, final_g=m_final_g)
    mom2 = dict(norm_mix_g=v_norm_mix_g, w_in=v_w_in, b_forget=v_b_forget, conv_w=v_conv_w, conv_b=v_conv_b, conv_ln_g=v_conv_ln_g, conv_ln_b=v_conv_ln_b, w_conv_out=v_w_conv_out, w_attn_out=v_w_attn_out, w_out=v_w_out, norm_ffn_g=v_norm_ffn_g, w_gate_up=v_w_gate_up, w_down=v_w_down, norm_ple_g=v_norm_ple_g, w_ple_gate=v_w_ple_gate, w_ple_proj=v_w_ple_proj, final_g=v_final_g)
    depth = w_in.shape[0]
    xi, yi, ci = lax.axis_index("x"), lax.axis_index("y"), lax.axis_index("c")
    me = 4 * xi + 2 * yi + ci

    rest_names = [k for k in _BIG if k != "w_in"]
    first = _all_gather([weights["w_in"][0].astype(BF16), conv_w], "gather_weights_in_l0")
    gathered_conv_w = first[1]
    shards, _ = lax.optimization_barrier(([weights[k][0].astype(BF16) for k in rest_names], first[0]))
    rest = _all_gather_beside(shards, "gather_weights_rest_l0", collective_id=0)
    gathered = [dict(zip(rest_names, rest), w_in=first[0], conv_w=gathered_conv_w[:, 0])] + [None] * (depth - 1)

    def gather_layer(l, tie):
        shards, _ = lax.optimization_barrier(([weights[k][l].astype(BF16) for k in _BIG], tie))
        later = _all_gather_beside(shards, f"gather_weights_l{l}", collective_id=l)
        gathered[l] = dict(zip(_BIG, later), conv_w=gathered_conv_w[:, l])

    act = x[0]
    layers = []
    for l in range(depth):
        if l > 0:
            gathered[l], act = lax.optimization_barrier((gathered[l], act))
        small = {k: weights[k][l][None, :] for k in _SMALL if k != "b_forget"}
        small["b_forget"] = _pad_lanes(b_forget[l])
        gather_next = functools.partial(gather_layer, l + 1) if l + 1 < depth else None
        act, saved, w = _layer_fwd(act, p[l, 0], gathered[l], small, f"l{l}", after_mixers=gather_next)
        layers.append((w, small, saved))
    dact, g_final, loss_part = _loss_head(act, final_g[None, :], loss_target[0], "loss_head")
    loss = lax.psum(loss_part[0, 0], ("x", "y", "c"))

    core = ci.reshape(1).astype(jnp.int32)
    ffn_group = ("w_ple_gate", "w_ple_proj", "w_down", "w_gate_up")
    mix_group = ("w_out", "w_attn_out", "w_conv_out", "w_in")

    def reduce_scatter(gw, names, tie, tag, collective_id):
        blocks = [(gw[k] if gw[k].ndim == 3 else _shard_cols(gw[k]) if k in _COL_SHARDED
                   else gw[k].reshape((N_DEV, -1) + gw[k].shape[1:]))[:, None] for k in names]
        theirs = _pair_exchange(blocks, f"grads_pair_exchange_{tag}")
        chip_parts = [_pair_sum(a, b, core, f"grads_pair_sum_{k}_{tag}") for k, a, b in zip(names, blocks, theirs)]
        chip_parts, tie = lax.optimization_barrier((chip_parts, tie))
        return _chip_exchange_beside(chip_parts, f"grads_chip_exchange_{tag}", collective_id), tie

    arrived = {}
    small_grads = {k: [None] * depth for k in _SMALL + ("conv_w",)}
    waiting = None
    for l in reversed(range(depth)):
        w, small, saved = layers[l]
        dx1, dx1_b, gw, gs = _layer_bwd_ffn(dact, saved, w, small, f"l{l}")
        if waiting is not None:
            arrived[waiting], dx1 = lax.optimization_barrier((arrived[waiting], dx1))
        arrived[l, 0], dx1 = reduce_scatter(gw, ffn_group, dx1, f"ffn_l{l}", depth + 2 * l)
        dact, gw, gs_mix = _layer_bwd_mix(dx1, dx1_b, saved, w, small, f"l{l}")
        arrived[l, 0], dact = lax.optimization_barrier((arrived[l, 0], dact))
        arrived[l, 1], dact = reduce_scatter(gw, mix_group, dact, f"mix_l{l}", depth + 2 * l + 1)
        waiting = (l, 1)
        for k, g in {**gs, **gs_mix}.items():
            small_grads[k][l] = g
    grad_x = dact[None]

    out = {}
    for group, names in enumerate((ffn_group, mix_group)):
        for i, k in enumerate(names):
            parts = jnp.concatenate([arrived[l, group][i] for l in range(depth)], axis=1)
            out[k] = _sum_adamw(parts, weights[k], mom1[k], mom2[k], f"adamw_{k}")

    small_names = list(_SMALL) + ["conv_w", "final_g"]
    small_list = [jnp.stack(small_grads[k]) for k in _SMALL + ("conv_w",)] + [g_final[0]]
    n_small = sum(v.size for v in small_list)
    rows = -(-n_small // (8 * LANES)) * 8
    all_parts = _all_gather([_pack(small_list, rows)], "gather_small_grads")[0]
    full_conv_w = jnp.transpose(gathered_conv_w, (1, 2, 0, 3)).reshape(depth, CONV_K, -1)
    full_conv_m = jnp.zeros_like(full_conv_w)
    ch_shard = conv_w.shape[-1]
    at = (0, 0, me * ch_shard)
    small_w = [weights[k] for k in _SMALL] + [full_conv_w, final_g]
    small_m = [mom1[k] for k in _SMALL] + [lax.dynamic_update_slice(full_conv_m, mom1["conv_w"], at), mom1["final_g"]]
    small_v = [mom2[k] for k in _SMALL] + [lax.dynamic_update_slice(full_conv_m + 1.0, mom2["conv_w"], at), mom2["final_g"]]
    res = _sum_adamw(all_parts, _pack(small_w, rows), _pack(small_m, rows), _pack(small_v, rows), "adamw_small")
    unpacked = [_unpack(r, small_list) for r in res]
    for idx, k in enumerate(small_names):
        vals = [u[idx] for u in unpacked]
        if k == "conv_w":
            vals = [lax.dynamic_slice(v, at, conv_w.shape) for v in vals]
        out[k] = vals

    return (loss, grad_x, *[out[k][0] for k in _ORDER], *[out[k][1] for k in _ORDER],
            *[out[k][2] for k in _ORDER], *[out[k][3] for k in _ORDER])
```

```python
import functools
import math

import jax
import jax.numpy as jnp
from jax import lax
from jax.experimental import pallas as pl
from jax.experimental.pallas import tpu as pltpu
from jax.experimental.pallas import tpu_sc as plsc

F32 = jnp.float32
BF16 = jnp.bfloat16
MESH = pl.DeviceIdType.MESH
ANY = pl.BlockSpec(memory_space=pl.ANY)

N_DEV = 8
N_HEADS = 8
HEAD_DIM = 128
CONV_K = 31
HALO = 32
EPS = 1e-6
NEG_INF = -1e30
ADAM_LR = 0.001
ADAM_B1 = 0.9
ADAM_B2 = 0.999
ADAM_EPS = 1e-08
ADAM_WD = 0.01
ADAM_STEP = 10

V7X_VMEM_BYTES = 64 * 1024 * 1024
VMEM_LIMIT = V7X_VMEM_BYTES * 3 // 4
LANES = 128
ROW_TILE = 512
NORM_BWD_TILE = 256
CONV_TILE = 256
ATTN_TILE = 512
ATTN_BWD_KEYS = 1024
ATTN_FWD_TILE = 1024
ATTN_ROWS = 256
CUM_TILE = 256
FFN_K_TILE = 2816
FFN_N_TILE = 1408


def _params(*sem):
    return pltpu.CompilerParams(dimension_semantics=sem, vmem_limit_bytes=VMEM_LIMIT)


def _tile(dim, pref, mult=LANES):
    t = min(pref, dim)
    t -= t % mult
    while t >= mult:
        if dim % t == 0:
            return t
        t -= mult
    return dim


def _sigmoid(x):
    return 1.0 / (1.0 + jnp.exp(-x))


def _place():
    x, y, c = lax.axis_index("x"), lax.axis_index("y"), lax.axis_index("c")
    chips = [(1 - x, y), (x, 1 - y), (1 - x, 1 - y)]
    return x, y, c, chips


def _all_gather(shards, name):
    n = len(shards)

    def body(*refs):
        ins, outs = refs[:n], refs[n:2 * n]
        send_sems, recv_sems, local_sems = refs[2 * n:]
        x, y, c, chips = _place()
        me, sibling = (x, y, c), (x, y, 1 - c)

        def idx(px, py, pc):
            return 4 * px + 2 * py + pc

        def copy(t, k, block, to, src=None):
            rows = outs[t].at[idx(*block)]
            return pltpu.make_async_remote_copy(
                src_ref=rows if src is None else src, dst_ref=rows,
                send_sem=send_sems.at[t, k], recv_sem=recv_sems.at[t, k],
                device_id=to, device_id_type=MESH)

        local = [pltpu.make_async_copy(ins[t], outs[t].at[idx(*me)], local_sems.at[t]) for t in range(n)]
        for cp in local:
            cp.start()
        first = []
        for t in range(n):
            first.append(copy(t, 0, me, sibling, src=ins[t]))
            first += [copy(t, 1 + j, me, (*chip, c), src=ins[t]) for j, chip in enumerate(chips)]
        for cp in first:
            cp.start()
        passed = []
        for j, chip in enumerate(chips):
            for t in range(n):
                copy(t, 1 + j, (*chip, c), me).wait_recv()
                cp = copy(t, 4 + j, (*chip, c), sibling)
                cp.start()
                passed.append(cp)
        for t in range(n):
            copy(t, 0, sibling, me).wait_recv()
        for j, chip in enumerate(chips):
            for t in range(n):
                copy(t, 4 + j, (*chip, 1 - c), me).wait_recv()
        for cp in first + passed:
            cp.wait_send()
        for cp in local:
            cp.wait()

    return pl.pallas_call(
        body, name=name,
        out_shape=[jax.ShapeDtypeStruct((N_DEV,) + s.shape, s.dtype) for s in shards],
        in_specs=[ANY] * n, out_specs=[ANY] * n,
        scratch_shapes=[pltpu.SemaphoreType.DMA((n, 7)), pltpu.SemaphoreType.DMA((n, 7)),
                        pltpu.SemaphoreType.DMA((n,))],
    )(*shards)


def _all_gather_beside(shards, name, collective_id):
    n = len(shards)
    src = [jax.new_ref(s, memory_space=pltpu.MemorySpace.HBM) for s in shards]
    dst = [jax.empty_ref(jax.ShapeDtypeStruct((N_DEV,) + s.shape, s.dtype), memory_space=pltpu.MemorySpace.HBM)
           for s in shards]

    @pl.kernel(mesh=plsc.ScalarSubcoreMesh(axis_name="sequencer", num_cores=1), name=name,
               scratch_types=(pltpu.SemaphoreType.DMA((n, 7)), pltpu.SemaphoreType.DMA((n, 7)),
                              pltpu.SemaphoreType.DMA((n,))),
               compiler_params=pltpu.CompilerParams(collective_id=collective_id))
    def launch(send_sems, recv_sems, local_sems):
        x, y, c, chips = _place()
        me, sibling = (x, y, c), (x, y, 1 - c)
        barrier = pltpu.get_barrier_semaphore()
        for peer in [sibling] + [(*chip, pc) for chip in chips for pc in (c, 1 - c)]:
            pl.semaphore_signal(barrier, inc=1, device_id=peer, device_id_type=MESH)
        pl.semaphore_wait(barrier, N_DEV - 1)

        def idx(px, py, pc):
            return 4 * px + 2 * py + pc

        def copy(t, k, block, to, own=False):
            rows = dst[t].at[idx(*block)]
            return pltpu.make_async_remote_copy(
                src_ref=src[t] if own else rows, dst_ref=rows,
                send_sem=send_sems.at[t, k], recv_sem=recv_sems.at[t, k],
                device_id=to, device_id_type=MESH)

        local = [pltpu.make_async_copy(src[t], dst[t].at[idx(*me)], local_sems.at[t]) for t in range(n)]
        for cp in local:
            cp.start()
        first = []
        for t in range(n):
            first.append(copy(t, 0, me, sibling, own=True))
            first += [copy(t, 1 + j, me, (*chip, c), own=True) for j, chip in enumerate(chips)]
        for cp in first:
            cp.start()
        passed = []
        for j, chip in enumerate(chips):
            for t in range(n):
                copy(t, 1 + j, (*chip, c), me).wait_recv()
                cp = copy(t, 4 + j, (*chip, c), sibling)
                cp.start()
                passed.append(cp)
        for t in range(n):
            copy(t, 0, sibling, me).wait_recv()
        for j, chip in enumerate(chips):
            for t in range(n):
                copy(t, 4 + j, (*chip, 1 - c), me).wait_recv()
        for cp in first + passed:
            cp.wait_send()
        for cp in local:
            cp.wait()

    launch()
    return [d[...] for d in dst]


def _pair_exchange(grads, name):
    n = len(grads)

    def body(*refs):
        ins, theirs = refs[:n], refs[n:2 * n]
        send_sems, recv_sems = refs[2 * n:]
        x, y, c, _ = _place()
        copies = []
        for t in range(n):
            for q in range(4):
                give = pltpu.make_async_remote_copy(
                    src_ref=ins[t].at[2 * q + 1 - c], dst_ref=theirs[t].at[q],
                    send_sem=send_sems.at[t, q], recv_sem=recv_sems.at[t, q],
                    device_id=(x, y, 1 - c), device_id_type=MESH)
                give.start()
                copies.append(give)
        for cp in copies:
            cp.wait()

    return pl.pallas_call(
        body, name=name, out_shape=[jax.ShapeDtypeStruct((4,) + g.shape[1:], g.dtype) for g in grads],
        in_specs=[ANY] * n, out_specs=[ANY] * n,
        scratch_shapes=[pltpu.SemaphoreType.DMA((n, 4)), pltpu.SemaphoreType.DMA((n, 4))],
    )(*grads)


def _chip_exchange_beside(parts, name, collective_id):
    n = len(parts)
    src = [jax.new_ref(p, memory_space=pltpu.MemorySpace.HBM) for p in parts]
    dst = [jax.empty_ref(jax.ShapeDtypeStruct(p.shape, p.dtype), memory_space=pltpu.MemorySpace.HBM) for p in parts]

    @pl.kernel(mesh=plsc.ScalarSubcoreMesh(axis_name="sequencer", num_cores=1), name=name,
               scratch_types=(pltpu.SemaphoreType.DMA((n, 3)), pltpu.SemaphoreType.DMA((n, 3)),
                              pltpu.SemaphoreType.DMA((n,))),
               compiler_params=pltpu.CompilerParams(collective_id=collective_id))
    def launch(send_sems, recv_sems, local_sems):
        x, y, c, chips = _place()
        barrier = pltpu.get_barrier_semaphore()
        for px, py in chips:
            pl.semaphore_signal(barrier, inc=1, device_id=(px, py, c), device_id_type=MESH)
        pl.semaphore_wait(barrier, len(chips))
        copies = []
        for t in range(n):
            own = pltpu.make_async_copy(src[t].at[2 * x + y], dst[t].at[0], local_sems.at[t])
            own.start()
            copies.append(own)
            for j, (px, py) in enumerate(chips):
                cp = pltpu.make_async_remote_copy(
                    src_ref=src[t].at[2 * px + py], dst_ref=dst[t].at[1 + j],
                    send_sem=send_sems.at[t, j], recv_sem=recv_sems.at[t, j],
                    device_id=(px, py, c), device_id_type=MESH)
                cp.start()
                copies.append(cp)
        for cp in copies:
            cp.wait()

    launch()
    return [d[...] for d in dst]


_DOT_DIMS = {"nn": (((1,), (0,)), ((), ())), "nt": (((1,), (1,)), ((), ())), "tn": (((0,), (0,)), ((), ()))}


def _mm(a, b, mode, out_dtype, name, residual=None, tm=1024, tn=1024, tk=2048, out_blocks=None):
    if mode == "tn":
        (k_dim, m_dim), n_dim = a.shape, b.shape[1]
    else:
        (m_dim, k_dim), n_dim = a.shape, (b.shape[1] if mode == "nn" else b.shape[0])
    tm, tn, tk = _tile(m_dim, tm), _tile(n_dim // (out_blocks or 1), tn), _tile(k_dim, tk)
    nk = k_dim // tk

    def body(*refs):
        a_ref, b_ref = refs[:2]
        r_ref = refs[2] if residual is not None else None
        o_ref = refs[2 if residual is None else 3]

        def finish(r):
            if r_ref is not None:
                r = r + r_ref[...].astype(F32)
            o_ref[...] = r.astype(o_ref.dtype)

        part = lax.dot_general(a_ref[...].astype(BF16), b_ref[...].astype(BF16), _DOT_DIMS[mode],
                               preferred_element_type=F32)
        if nk == 1:
            finish(part)
            return
        acc, k = refs[-1], pl.program_id(2)

        @pl.when(k == 0)
        def _():
            acc[...] = part

        @pl.when((k > 0) & (k < nk - 1))
        def _():
            acc[...] += part

        @pl.when(k == nk - 1)
        def _():
            finish(acc[...] + part)

    a_spec = pl.BlockSpec((tk, tm), lambda i, j, k: (k, i)) if mode == "tn" else pl.BlockSpec((tm, tk), lambda i, j, k: (i, k))
    b_spec = pl.BlockSpec((tn, tk), lambda i, j, k: (j, k)) if mode == "nt" else pl.BlockSpec((tk, tn), lambda i, j, k: (k, j))
    o_spec, o_shape = pl.BlockSpec((tm, tn), lambda i, j, k: (i, j)), (m_dim, n_dim)
    if out_blocks:
        per = n_dim // out_blocks // tn
        o_spec = pl.BlockSpec((None, tm, tn), lambda i, j, k: (j // per, i, j % per))
        o_shape = (out_blocks, m_dim, n_dim // out_blocks)
    ins, specs = [a, b], [a_spec, b_spec]
    if residual is not None:
        ins.append(residual)
        specs.append(o_spec)
    return pl.pallas_call(
        body, name=name, grid=(m_dim // tm, n_dim // tn, nk),
        out_shape=jax.ShapeDtypeStruct(o_shape, out_dtype),
        in_specs=specs, out_specs=o_spec,
        scratch_shapes=[pltpu.VMEM((tm, tn), F32)] if nk > 1 else [],
        compiler_params=_params("parallel", "parallel", "arbitrary"),
    )(*ins)


def _rows(body, name, row_ins, full_ins, row_outs, acc_outs, ts=ROW_TILE):
    s = row_ins[0].shape[0]
    ts = _tile(s, ts, 16)
    n_ri, n_fi, n_ro = len(row_ins), len(full_ins), len(row_outs)

    def kernel(*refs):
        i = pl.program_id(0)
        ri, fi = refs[:n_ri], refs[n_ri:n_ri + n_fi]
        ro, ao = refs[n_ri + n_fi:n_ri + n_fi + n_ro], refs[n_ri + n_fi + n_ro:]

        if ao:
            @pl.when(i == 0)
            def _():
                for r in ao:
                    r[...] = jnp.zeros_like(r)

        body(i, ri, fi, ro, ao)

    def row_spec(cols):
        return pl.BlockSpec((ts, cols), lambda i: (i, 0))

    def full_spec(shape):
        return pl.BlockSpec(shape, lambda i: (0,) * len(shape))

    return pl.pallas_call(
        kernel, name=name, grid=(s // ts,),
        out_shape=[jax.ShapeDtypeStruct((s, c), d) for c, d in row_outs]
        + [jax.ShapeDtypeStruct(sh, d) for sh, d in acc_outs],
        in_specs=[row_spec(a.shape[1]) for a in row_ins] + [full_spec(a.shape) for a in full_ins],
        out_specs=[row_spec(c) for c, _ in row_outs] + [full_spec(sh) for sh, _ in acc_outs],
        compiler_params=_params("arbitrary"),
    )(*row_ins, *full_ins)


def _rms_stats(x):
    return lax.rsqrt(jnp.mean(x * x, axis=-1, keepdims=True) + EPS)


def _rms_fwd(x, g, name):
    def body(i, ri, fi, ro, ao):
        xv = ri[0][...]
        ro[0][...] = (xv * _rms_stats(xv) * fi[0][...]).astype(BF16)

    return _rows(body, name, [x], [g], [(x.shape[1], BF16)], [])[0]


def _rms_bwd_rows(xv, gv, dh):
    r = _rms_stats(xv)
    xhat = xv * r
    dxhat = dh * gv
    dx = r * (dxhat - xhat * jnp.mean(dxhat * xhat, axis=-1, keepdims=True))
    return dx, jnp.sum(dh * xhat, axis=0, keepdims=True)


def _rms_bwd(x, g, dh, dres, name):
    d = x.shape[1]

    def body(i, ri, fi, ro, ao):
        dx, dg = _rms_bwd_rows(ri[0][...], fi[0][...], ri[1][...].astype(F32))
        dx = dx + ri[2][...]
        ro[0][...] = dx
        ro[1][...] = dx.astype(BF16)
        ao[0][...] += dg

    return _rows(body, name, [x, dh, dres], [g], [(d, F32), (d, BF16)], [((1, d), F32)], ts=NORM_BWD_TILE)


def _loss_head(x, g, target, name):
    d = x.shape[1]

    def body(i, ri, fi, ro, ao):
        xv, gv = ri[0][...], fi[0][...]
        y = xv * _rms_stats(xv) * gv
        err = y - ri[1][...]
        ao[1][...] += 0.5 * jnp.sum(jnp.mean(err * err, axis=-1, keepdims=True))
        dx, dg = _rms_bwd_rows(xv, gv, err * (1.0 / d))
        ro[0][...] = dx
        ao[0][...] += dg

    return _rows(body, name, [x, target], [g], [(d, F32)], [((1, d), F32), ((8, LANES), F32)], ts=NORM_BWD_TILE)


def _merge_fwd(gg, y_conv, y_attn, name):
    d = y_conv.shape[1]

    def body(i, ri, fi, ro, ao):
        gv, yc, ya = (r[...].astype(F32) for r in ri)
        m = _sigmoid(gv[:, :d]) * yc + _sigmoid(gv[:, d:]) * ya
        ro[0][...] = m.astype(BF16)

    return _rows(body, name, [gg, y_conv, y_attn], [], [(d, BF16)], [])[0]


def _merge_bwd(gg, y_conv, y_attn, dm, name):
    d = y_conv.shape[1]

    def body(i, ri, fi, ro, ao):
        gv, yc, ya, dmv = (r[...].astype(F32) for r in ri)
        sc, sa = _sigmoid(gv[:, :d]), _sigmoid(gv[:, d:])
        ro[0][...] = (dmv * sc).astype(BF16)
        ro[1][...] = (dmv * sa).astype(BF16)
        ro[2][:, :d] = (dmv * yc * sc * (1.0 - sc)).astype(BF16)
        ro[2][:, d:] = (dmv * ya * sa * (1.0 - sa)).astype(BF16)

    return _rows(body, name, [gg, y_conv, y_attn, dm], [], [(d, BF16), (d, BF16), (2 * d, BF16)], [])


def _swiglu_fwd(gu, name):
    f = gu.shape[1] // 2

    def body(i, ri, fi, ro, ao):
        v = ri[0][...].astype(F32)
        gate, up = v[:, :f], v[:, f:]
        ro[0][...] = (gate * _sigmoid(gate) * up).astype(BF16)

    return _rows(body, name, [gu], [], [(f, BF16)], [], ts=256)[0]


def _swiglu_bwd(gu, dact, name):
    f = gu.shape[1] // 2

    def body(i, ri, fi, ro, ao):
        v, dv = ri[0][...].astype(F32), ri[1][...].astype(F32)
        gate, up = v[:, :f], v[:, f:]
        sg = _sigmoid(gate)
        ro[0][:, :f] = (dv * up * sg * (1.0 + gate * (1.0 - sg))).astype(BF16)
        ro[0][:, f:] = (dv * gate * sg).astype(BF16)

    return _rows(body, name, [gu, dact], [], [(2 * f, BF16)], [], ts=256)[0]


def _ple_fwd(x, z, pp, name):
    d = x.shape[1]

    def body(i, ri, fi, ro, ao):
        ro[0][...] = ri[0][...] + _sigmoid(ri[1][...].astype(F32)) * ri[2][...].astype(F32)

    return _rows(body, name, [x, z, pp], [], [(d, F32)], [])[0]


def _ple_bwd(dx, z, pp, name):
    d = dx.shape[1]

    def body(i, ri, fi, ro, ao):
        dv, sz = ri[0][...], _sigmoid(ri[1][...].astype(F32))
        ro[0][...] = (dv * ri[2][...].astype(F32) * sz * (1.0 - sz)).astype(BF16)
        ro[1][...] = (dv * sz).astype(BF16)

    return _rows(body, name, [dx, z, pp], [], [(d, BF16), (d, BF16)], [])


def _triangle(n, lower):
    r = lax.broadcasted_iota(jnp.int32, (n, n), 0)
    c = lax.broadcasted_iota(jnp.int32, (n, n), 1)
    return jnp.where((c <= r) if lower else (c >= r), 1.0, 0.0).astype(F32)


def _forget_fwd(f, bias, name):
    s = f.shape[0]
    tb = _tile(s, CUM_TILE, 8)

    def body(f_ref, b_ref, c_ref, carry):
        @pl.when(pl.program_id(0) == 0)
        def _():
            carry[...] = jnp.zeros_like(carry)

        xv = f_ref[...] + b_ref[...]
        logf = jnp.minimum(xv, 0.0) - jnp.log(1.0 + jnp.exp(-jnp.abs(xv)))
        cv = jnp.dot(_triangle(tb, True), logf, precision=lax.Precision.HIGHEST, preferred_element_type=F32) + carry[...]
        c_ref[...] = cv
        carry[...] = cv[tb - 1:tb, :]

    return pl.pallas_call(
        body, name=name, grid=(s // tb,), out_shape=jax.ShapeDtypeStruct((s, LANES), F32),
        in_specs=[pl.BlockSpec((tb, LANES), lambda i: (i, 0)), pl.BlockSpec((1, LANES), lambda i: (0, 0))],
        out_specs=pl.BlockSpec((tb, LANES), lambda i: (i, 0)),
        scratch_shapes=[pltpu.VMEM((1, LANES), F32)],
        compiler_params=_params("arbitrary"),
    )(f, bias)


def _forget_bwd(f, bias, dc, name):
    s = f.shape[0]
    tb = _tile(s, CUM_TILE, 16)
    nb = s // tb

    def body(f_ref, b_ref, dc_ref, df_ref, db_ref, carry):
        @pl.when(pl.program_id(0) == 0)
        def _():
            carry[...] = jnp.zeros_like(carry)
            db_ref[...] = jnp.zeros_like(db_ref)

        dlog = jnp.dot(_triangle(tb, False), dc_ref[...], precision=lax.Precision.HIGHEST,
                       preferred_element_type=F32) + carry[...]
        carry[...] = dlog[0:1, :]
        dlogit = dlog * (1.0 - _sigmoid(f_ref[...] + b_ref[...]))
        df_ref[...] = dlogit.astype(BF16)
        db_ref[...] += jnp.sum(dlogit, axis=0, keepdims=True)

    back = lambda i: (nb - 1 - i, 0)
    return pl.pallas_call(
        body, name=name, grid=(nb,),
        out_shape=[jax.ShapeDtypeStruct((s, LANES), BF16), jax.ShapeDtypeStruct((1, LANES), F32)],
        in_specs=[pl.BlockSpec((tb, LANES), back), pl.BlockSpec((1, LANES), lambda i: (0, 0)),
                  pl.BlockSpec((tb, LANES), back)],
        out_specs=[pl.BlockSpec((tb, LANES), back), pl.BlockSpec((1, LANES), lambda i: (0, 0))],
        scratch_shapes=[pltpu.VMEM((1, LANES), F32)],
        compiler_params=_params("arbitrary"),
    )(f, bias, dc)


def _glu(block, ch):
    return block[:, :ch] * _sigmoid(block[:, ch:])


def _layernorm_rows(u1, g, b):
    mu = jnp.mean(u1, axis=-1, keepdims=True)
    cen = u1 - mu
    rstd = lax.rsqrt(jnp.mean(cen * cen, axis=-1, keepdims=True) + EPS)
    xhat = cen * rstd
    return xhat, rstd, xhat * g + b


def _conv_fwd(ag, conv_w, conv_b, ln_g, ln_b, name):
    s, ch = ag.shape[0], ag.shape[1] // 2
    ts = _tile(s, CONV_TILE, HALO)
    per = ts // HALO

    def body(cur_ref, halo_ref, w_ref, cb_ref, g_ref, b_ref, u1_ref, u3_ref, cat):
        i = pl.program_id(0)
        cat[HALO:, :] = _glu(cur_ref[...], ch)
        cat[:HALO, :] = jnp.where(i == 0, 0.0, _glu(halo_ref[...], ch))
        for c0 in range(0, ch, LANES):
            cols = slice(c0, c0 + LANES)
            full = cat[:, cols]
            acc = jnp.zeros((ts, LANES), F32) + cb_ref[:, cols]
            for j in range(CONV_K):
                shifted = full if j == CONV_K - 1 else pltpu.roll(full, CONV_K - 1 - j, 0)
                acc = acc + w_ref[j:j + 1, cols] * shifted[HALO:, :]
            u1_ref[:, cols] = acc
        _, _, u2 = _layernorm_rows(u1_ref[...], g_ref[...], b_ref[...])
        u3_ref[...] = (u2 * _sigmoid(u2)).astype(BF16)

    whole = lambda shape: pl.BlockSpec(shape, lambda i: (0, 0))
    return pl.pallas_call(
        body, name=name, grid=(s // ts,),
        out_shape=[jax.ShapeDtypeStruct((s, ch), F32), jax.ShapeDtypeStruct((s, ch), BF16)],
        in_specs=[pl.BlockSpec((ts, 2 * ch), lambda i: (i, 0)),
                  pl.BlockSpec((HALO, 2 * ch), lambda i: (jnp.maximum(i * per - 1, 0), 0)),
                  whole(conv_w.shape), whole((1, ch)), whole((1, ch)), whole((1, ch))],
        out_specs=[pl.BlockSpec((ts, ch), lambda i: (i, 0)), pl.BlockSpec((ts, ch), lambda i: (i, 0))],
        scratch_shapes=[pltpu.VMEM((HALO + ts, ch), F32)],
        compiler_params=_params("arbitrary"),
    )(ag, ag, conv_w, conv_b, ln_g, ln_b)


def _conv_norm_bwd(u1, du3, ln_g, ln_b, name):
    ch = u1.shape[1]

    def body(i, ri, fi, ro, ao):
        g = fi[0][...]
        xhat, rstd, u2 = _layernorm_rows(ri[0][...], g, fi[1][...])
        sg = _sigmoid(u2)
        du2 = ri[1][...].astype(F32) * sg * (1.0 + u2 * (1.0 - sg))
        dxhat = du2 * g
        du1 = rstd * (dxhat - jnp.mean(dxhat, axis=-1, keepdims=True)
                      - xhat * jnp.mean(dxhat * xhat, axis=-1, keepdims=True))
        ro[0][...] = du1
        ao[0][...] += jnp.sum(du2 * xhat, axis=0, keepdims=True)
        ao[1][...] += jnp.sum(du2, axis=0, keepdims=True)
        ao[2][...] += jnp.sum(du1, axis=0, keepdims=True)

    return _rows(body, name, [u1, du3], [ln_g, ln_b], [(ch, F32)], [((1, ch), F32)] * 3)


def _conv_bwd(ag, du1, conv_w, name):
    s, ch = du1.shape
    ts = _tile(s, CONV_TILE, HALO)
    per = ts // HALO
    last = s // HALO - 1
    nb = s // ts

    def body(cur_ref, halo_ref, d_ref, dnext_ref, w_ref, dag_ref, dw_ref, cat, dcat):
        i = pl.program_id(0)

        @pl.when(i == 0)
        def _():
            dw_ref[...] = jnp.zeros_like(dw_ref)

        cat[HALO:, :] = _glu(cur_ref[...], ch)
        cat[:HALO, :] = jnp.where(i == 0, 0.0, _glu(halo_ref[...], ch))
        dcat[:ts, :] = d_ref[...]
        dcat[ts:, :] = jnp.where(i == nb - 1, 0.0, dnext_ref[...])
        for c0 in range(0, ch, LANES):
            cols, gate_cols = slice(c0, c0 + LANES), slice(ch + c0, ch + c0 + LANES)
            full, dfull = cat[:, cols], dcat[:, cols]
            dcur = dfull[:ts, :]
            du0 = jnp.zeros((ts, LANES), F32)
            for j in range(CONV_K):
                back = CONV_K - 1 - j
                dshift = dfull if back == 0 else pltpu.roll(dfull, ts + HALO - back, 0)
                du0 = du0 + w_ref[j:j + 1, cols] * dshift[:ts, :]
                ushift = full if back == 0 else pltpu.roll(full, back, 0)
                dw_ref[j:j + 1, cols] += jnp.sum(dcur * ushift[HALO:, :], axis=0, keepdims=True)
            a, sg = cur_ref[:, cols], _sigmoid(cur_ref[:, gate_cols])
            dag_ref[:, cols] = (du0 * sg).astype(BF16)
            dag_ref[:, gate_cols] = (du0 * a * sg * (1.0 - sg)).astype(BF16)

    return pl.pallas_call(
        body, name=name, grid=(nb,),
        out_shape=[jax.ShapeDtypeStruct((s, 2 * ch), BF16), jax.ShapeDtypeStruct(conv_w.shape, F32)],
        in_specs=[pl.BlockSpec((ts, 2 * ch), lambda i: (i, 0)),
                  pl.BlockSpec((HALO, 2 * ch), lambda i: (jnp.maximum(i * per - 1, 0), 0)),
                  pl.BlockSpec((ts, ch), lambda i: (i, 0)),
                  pl.BlockSpec((HALO, ch), lambda i: (jnp.minimum((i + 1) * per, last), 0)),
                  pl.BlockSpec(conv_w.shape, lambda i: (0, 0))],
        out_specs=[pl.BlockSpec((ts, 2 * ch), lambda i: (i, 0)), pl.BlockSpec(conv_w.shape, lambda i: (0, 0))],
        scratch_shapes=[pltpu.VMEM((HALO + ts, ch), F32), pltpu.VMEM((ts + HALO, ch), F32)],
        compiler_params=_params("arbitrary"),
    )(ag, ag, du1, du1, conv_w)


_SCALE = 1.0 / math.sqrt(HEAD_DIM)
_LOG2E = math.log2(math.e)


STAT_LANES = 8


def _causal(rows, cols, row0):
    r = row0 + lax.broadcasted_iota(jnp.int32, (rows, cols), 0)
    return lax.broadcasted_iota(jnp.int32, (rows, cols), 1) <= r


def _attn_fwd(qkv, c_row, name):
    s = qkv.shape[0]
    t = _tile(s, ATTN_FWD_TILE)
    nt, part_rows = s // t, min(ATTN_ROWS, t)
    n_parts = t // part_rows
    h_, d = N_HEADS, HEAD_DIM

    def body(q_ref, k_ref, v_ref, cr_ref, o_ref, lse_ref, *scratch):
        m_sc, l_sc, acc = scratch[:n_parts], scratch[n_parts:2 * n_parts], scratch[2 * n_parts:]
        i = pl.program_id(1)
        for part in range(n_parts):
            m_sc[part][...] = jnp.full_like(m_sc[part], NEG_INF)
            l_sc[part][...] = jnp.zeros_like(l_sc[part])
            acc[part][...] = jnp.zeros_like(acc[part])

        def block(j, diagonal):
            at = pl.multiple_of(j * t, t)
            k, v = k_ref[pl.ds(at, t), :], v_ref[pl.ds(at, t), :]
            cr = cr_ref[0, pl.ds(j, 1), :] * _LOG2E
            for part in range(n_parts):
                rows = pl.ds(part * part_rows, part_rows)
                sc = lax.dot_general(q_ref[rows, :], k, _DOT_DIMS["nt"], preferred_element_type=F32)
                sc = sc * (_SCALE * _LOG2E) - cr
                if diagonal:
                    sc = jnp.where(_causal(part_rows, t, part * part_rows), sc, NEG_INF)
                m_old = m_sc[part][...]
                m_new = jnp.maximum(m_old, jnp.max(sc, axis=-1, keepdims=True))
                alpha = jnp.exp2(m_old - m_new)
                p = jnp.exp2(sc - m_new)
                l_sc[part][...] = alpha * l_sc[part][...] + jnp.sum(p, axis=-1, keepdims=True)
                acc[part][...] = alpha * acc[part][...] + jnp.dot(p.astype(BF16), v, preferred_element_type=F32)
                m_sc[part][...] = m_new

        def below(j, carry):
            block(j, False)
            return carry

        lax.fori_loop(0, i, below, 0)
        block(i, True)
        for part in range(n_parts):
            rows = pl.ds(part * part_rows, part_rows)
            o_ref[rows, :] = (acc[part][...] / l_sc[part][...]).astype(BF16)
            lse_ref[0, rows, :] = m_sc[part][...] * (1.0 / _LOG2E) + jnp.log(l_sc[part][...])

    return pl.pallas_call(
        body, name=name, grid=(h_, nt),
        out_shape=[jax.ShapeDtypeStruct((s, h_ * d), BF16), jax.ShapeDtypeStruct((h_, s, 1), F32)],
        in_specs=[pl.BlockSpec((t, d), lambda h, i: (i, h)),
                  pl.BlockSpec((s, d), lambda h, i: (0, h_ + h)),
                  pl.BlockSpec((s, d), lambda h, i: (0, 2 * h_ + h)),
                  pl.BlockSpec((1, nt, t), lambda h, i: (h, 0, 0))],
        out_specs=[pl.BlockSpec((t, d), lambda h, i: (i, h)), pl.BlockSpec((1, t, 1), lambda h, i: (h, i, 0))],
        scratch_shapes=[pltpu.VMEM((part_rows, 1), F32)] * (2 * n_parts) + [pltpu.VMEM((part_rows, d), F32)] * n_parts,
        compiler_params=_params("parallel", "arbitrary"),
    )(qkv, qkv, qkv, c_row.reshape(h_, nt, t))


def _attn_stats(o, do, lse, name):
    s = o.shape[0]
    t = _tile(s, ATTN_TILE)
    h_, d = N_HEADS, HEAD_DIM

    def body(o_ref, do_ref, lse_ref, st_ref, dob_ref):
        dov = do_ref[...]
        st_ref[...] = jnp.zeros_like(st_ref)
        st_ref[0, :, 0:1] = -lse_ref[0]
        st_ref[0, :, 1:2] = jnp.sum(o_ref[...].astype(F32) * dov, axis=-1, keepdims=True)
        dob_ref[...] = dov.astype(BF16)

    col = pl.BlockSpec((1, t, 1), lambda h, i: (h, i, 0))
    blk = pl.BlockSpec((t, d), lambda h, i: (i, h))
    return pl.pallas_call(
        body, name=name, grid=(h_, s // t),
        out_shape=[jax.ShapeDtypeStruct((h_, s, STAT_LANES), F32), jax.ShapeDtypeStruct(o.shape, BF16)],
        in_specs=[blk, blk, col],
        out_specs=[pl.BlockSpec((1, t, STAT_LANES), lambda h, i: (h, i, 0)), blk],
        compiler_params=_params("parallel", "parallel"),
    )(o, do, lse)


def _attn_bwd(qkv, do, stats, c_row, name):
    s = qkv.shape[0]
    tq = _tile(s, ATTN_TILE)
    tk = _tile(s, ATTN_BWD_KEYS, tq)
    nq, nk, per = s // tq, s // tk, tk // tq
    h_, d = N_HEADS, HEAD_DIM

    def body(q_ref, do_ref, k_ref, v_ref, st_ref, cr_ref, dq_ref, dk_ref, dv_ref, dcr_ref, dcc_ref,
             dq_acc, dk_acc, dv_acc, dcr_acc):
        j = pl.program_id(1)

        @pl.when(j == 0)
        def _():
            dq_acc[...] = jnp.zeros_like(dq_acc)
            dcc_ref[...] = jnp.zeros_like(dcc_ref)

        dk_acc[...] = jnp.zeros_like(dk_acc)
        dv_acc[...] = jnp.zeros_like(dv_acc)
        dcr_acc[...] = jnp.zeros_like(dcr_acc)
        k, v, cr = k_ref[...], v_ref[...], cr_ref[0]

        def block(i, row0):
            rows = pl.ds(pl.multiple_of(i * tq, tq), tq)
            q, dov, st = q_ref[rows, :], do_ref[rows, :], st_ref[0, rows, :]
            sc = lax.dot_general(q, k, _DOT_DIMS["nt"], preferred_element_type=F32) * _SCALE + (st[:, 0:1] - cr)
            p = jnp.exp(sc)
            if row0 is not None:
                p = jnp.where(_causal(tq, tk, row0), p, 0.0)
            dp = lax.dot_general(dov, v, _DOT_DIMS["nt"], preferred_element_type=F32)
            ds = p * (dp - st[:, 1:2])
            ds_b = ds.astype(BF16)
            dv_acc[...] += lax.dot_general(p.astype(BF16), dov, _DOT_DIMS["tn"], preferred_element_type=F32)
            dk_acc[...] += lax.dot_general(ds_b, q, _DOT_DIMS["tn"], preferred_element_type=F32)
            dq_acc[rows, :] += jnp.dot(ds_b, k, preferred_element_type=F32)
            dcr_acc[...] -= jnp.sum(ds, axis=0, keepdims=True)
            dcc_ref[0, rows, :] += jnp.sum(ds, axis=-1, keepdims=True)

        def below(i, carry):
            block(i, None)
            return carry

        for part in range(per):
            block(j * per + part, part * tq)
        lax.fori_loop((j + 1) * per, nq, below, 0)
        dk_ref[...] = (dk_acc[...] * _SCALE).astype(BF16)
        dv_ref[...] = dv_acc[...].astype(BF16)
        dcr_ref[0] = dcr_acc[...]

        @pl.when(j == nk - 1)
        def _():
            dq_ref[...] = (dq_acc[...] * _SCALE).astype(BF16)

    head = pl.BlockSpec((s, d), lambda h, j: (0, h))
    return pl.pallas_call(
        body, name=name, grid=(h_, nk),
        out_shape=[jax.ShapeDtypeStruct((s, h_ * d), BF16)] * 3
        + [jax.ShapeDtypeStruct((h_, 1, s), F32), jax.ShapeDtypeStruct((h_, s, 1), F32)],
        in_specs=[head, head,
                  pl.BlockSpec((tk, d), lambda h, j: (j, h_ + h)),
                  pl.BlockSpec((tk, d), lambda h, j: (j, 2 * h_ + h)),
                  pl.BlockSpec((1, s, STAT_LANES), lambda h, j: (h, 0, 0)),
                  pl.BlockSpec((1, 1, tk), lambda h, j: (h, 0, j))],
        out_specs=[head, pl.BlockSpec((tk, d), lambda h, j: (j, h)), pl.BlockSpec((tk, d), lambda h, j: (j, h)),
                   pl.BlockSpec((1, 1, tk), lambda h, j: (h, 0, j)), pl.BlockSpec((1, s, 1), lambda h, j: (h, 0, 0))],
        scratch_shapes=[pltpu.VMEM((s, d), F32), pltpu.VMEM((tk, d), F32), pltpu.VMEM((tk, d), F32),
                        pltpu.VMEM((1, tk), F32)],
        compiler_params=_params("parallel", "arbitrary"),
    )(qkv, do, qkv, qkv, stats, c_row)


def _pair_sum(grads, theirs, core, name):
    _, l, r, c = theirs.shape
    tr = _tile(r, 256, 16)

    def body(core_ref, a_ref, b_ref, o_ref):
        o_ref[...] = (a_ref[...].astype(F32) + b_ref[...].astype(F32)).astype(BF16)

    spec = pl.BlockSpec((1, 1, tr, c), lambda q, k, i, core_ref: (q, k, i, 0))
    mine = pl.BlockSpec((1, 1, tr, c), lambda q, k, i, core_ref: (2 * q + core_ref[0], k, i, 0))
    return pl.pallas_call(
        body, name=name, out_shape=jax.ShapeDtypeStruct(theirs.shape, BF16),
        grid_spec=pltpu.PrefetchScalarGridSpec(num_scalar_prefetch=1, grid=(4, l, r // tr),
                                               in_specs=[mine, spec], out_specs=spec),
        compiler_params=_params("parallel", "parallel", "parallel"),
    )(core, grads, theirs)


def _adamw(g, w, m, v):
    m = ADAM_B1 * m + (1.0 - ADAM_B1) * g
    v = ADAM_B2 * v + (1.0 - ADAM_B2) * (g * g)
    m_hat = m / (1.0 - ADAM_B1 ** ADAM_STEP)
    v_hat = v / (1.0 - ADAM_B2 ** ADAM_STEP)
    return -ADAM_LR * (m_hat / (jnp.sqrt(v_hat) + ADAM_EPS) + ADAM_WD * w), m, v


def _sum_adamw(parts, w, m, v, name):
    n_parts, l, r, c = parts.shape
    tr = _tile(r, 256, 16)

    def body(p_ref, w_ref, m_ref, v_ref, g_ref, d_ref, nm_ref, nv_ref):
        g = p_ref[0].astype(F32)
        for k in range(1, n_parts):
            g = g + p_ref[k].astype(F32)
        g_ref[...] = g
        d_ref[...], nm_ref[...], nv_ref[...] = _adamw(g, w_ref[...], m_ref[...], v_ref[...])

    spec = pl.BlockSpec((1, tr, c), lambda k, i: (k, i, 0))
    return pl.pallas_call(
        body, name=name, grid=(l, r // tr), out_shape=[jax.ShapeDtypeStruct(w.shape, F32)] * 4,
        in_specs=[pl.BlockSpec((n_parts, 1, tr, c), lambda k, i: (0, k, i, 0)), spec, spec, spec],
        out_specs=[spec] * 4, compiler_params=_params("parallel", "parallel"),
    )(parts, w, m, v)


def _unshard_cols(gathered):
    return jnp.transpose(gathered, (1, 0, 2)).reshape(gathered.shape[1], -1)


def _shard_cols(full):
    k = full.shape[0]
    return jnp.transpose(full.reshape(k, N_DEV, -1), (1, 0, 2))


def _in_weights(w, d_model):
    hd = N_HEADS * HEAD_DIM
    w_in = _unshard_cols(w["w_in"])
    f0 = d_model + 3 * hd
    return {
        "w_ag": w_in[:, :d_model], "w_qkv": w_in[:, d_model:f0],
        "w_f": jnp.pad(w_in[:, f0:f0 + N_HEADS], ((0, 0), (0, LANES - N_HEADS))),
        "w_gg": w_in[:, f0 + N_HEADS:],
        "conv_w": jnp.transpose(w["conv_w"], (1, 0, 2)).reshape(CONV_K, -1),
    }


def _rest_weights(w, d_model):
    return {
        "w_conv_out": _unshard_cols(w["w_conv_out"]), "w_attn_out": _unshard_cols(w["w_attn_out"]),
        "w_out": w["w_out"].reshape(-1, d_model), "w_gate_up": _unshard_cols(w["w_gate_up"]),
        "w_down": w["w_down"].reshape(-1, d_model), "w_ple_gate": w["w_ple_gate"].reshape(-1, d_model),
        "w_ple_proj": _unshard_cols(w["w_ple_proj"]),
    }


def _pad_lanes(vec):
    return jnp.pad(vec, (0, LANES - vec.shape[0]))[None, :]


def _layer_fwd(x, p, shards, small, tag, after_mixers=None):
    d_model = x.shape[1]
    w = _in_weights(shards, d_model)
    h = _rms_fwd(x, small["norm_mix_g"], f"rms_mix_{tag}")
    ag = _mm(h, w["w_ag"], "nn", F32, f"proj_glu_{tag}")
    qkv = _mm(h, w["w_qkv"], "nn", BF16, f"proj_qkv_{tag}")
    gg = _mm(h, w["w_gg"], "nn", BF16, f"proj_gates_{tag}")
    f = _mm(h, w["w_f"], "nn", F32, f"proj_forget_{tag}")
    c = _forget_fwd(f, small["b_forget"], f"forget_{tag}")
    c_heads = jnp.transpose(c[:, :N_HEADS])
    c_row = c_heads[:, None, :]
    o, lse = _attn_fwd(qkv, c_row, f"attn_{tag}")
    u1, u3 = _conv_fwd(ag, w["conv_w"], small["conv_b"], small["conv_ln_g"], small["conv_ln_b"], f"conv_{tag}")
    rest, (u3, o) = lax.optimization_barrier(({k: shards[k] for k in _BIG if k != "w_in"}, (u3, o)))
    w.update(_rest_weights(rest, d_model))
    y_conv = _mm(u3, w["w_conv_out"], "nn", BF16, f"conv_out_{tag}")
    y_attn = _mm(o, w["w_attn_out"], "nn", BF16, f"attn_out_{tag}")
    if after_mixers is not None:
        after_mixers(y_attn)
    merged = _merge_fwd(gg, y_conv, y_attn, f"merge_{tag}")
    x1 = _mm(merged, w["w_out"], "nn", F32, f"mix_out_{tag}", residual=x)
    hf = _rms_fwd(x1, small["norm_ffn_g"], f"rms_ffn_{tag}")
    gu = _mm(hf, w["w_gate_up"], "nn", BF16, f"ffn_up_{tag}")
    act = _swiglu_fwd(gu, f"swiglu_{tag}")
    x2 = _mm(act, w["w_down"], "nn", F32, f"ffn_down_{tag}", residual=x1)
    hp = _rms_fwd(x2, small["norm_ple_g"], f"rms_ple_{tag}")
    z = _mm(hp, w["w_ple_gate"], "nn", BF16, f"ple_gate_{tag}")
    pp = _mm(p, w["w_ple_proj"], "nn", BF16, f"ple_proj_{tag}")
    x3 = _ple_fwd(x2, z, pp, f"ple_{tag}")
    saved = dict(x=x, p=p, h=h, ag=ag, qkv=qkv, gg=gg, f=f, u1=u1, u3=u3, y_conv=y_conv, c_row=c_row,
                 o=o, lse=lse, y_attn=y_attn, merged=merged, x1=x1, hf=hf, gu=gu, act=act, x2=x2, hp=hp, z=z, pp=pp)
    return x3, saved, w


def _layer_bwd_ffn(dx3, sv, w, small, tag):
    gw, gs = {}, {}
    dz, dpp = _ple_bwd(dx3, sv["z"], sv["pp"], f"ple_bwd_{tag}")
    gw["w_ple_proj"] = _mm(sv["p"], dpp, "tn", BF16, f"ple_proj_dw_{tag}")
    gw["w_ple_gate"] = _mm(sv["hp"], dz, "tn", BF16, f"ple_gate_dw_{tag}")
    dhp = _mm(dz, w["w_ple_gate"], "nt", BF16, f"ple_gate_dx_{tag}")
    dx2, dx2_b, gs["norm_ple_g"] = _rms_bwd(sv["x2"], small["norm_ple_g"], dhp, dx3, f"rms_ple_bwd_{tag}")
    gw["w_down"] = _mm(sv["act"], dx2_b, "tn", BF16, f"ffn_down_dw_{tag}", tm=FFN_N_TILE)
    dact = _mm(dx2_b, w["w_down"], "nt", BF16, f"ffn_down_dx_{tag}", tn=FFN_N_TILE)
    dgu = _swiglu_bwd(sv["gu"], dact, f"swiglu_bwd_{tag}")
    gw["w_gate_up"] = _mm(sv["hf"], dgu, "tn", BF16, f"ffn_up_dw_{tag}", tn=FFN_N_TILE, out_blocks=N_DEV)
    dhf = _mm(dgu, w["w_gate_up"], "nt", BF16, f"ffn_up_dx_{tag}", tk=FFN_K_TILE)
    dx1, dx1_b, gs["norm_ffn_g"] = _rms_bwd(sv["x1"], small["norm_ffn_g"], dhf, dx2, f"rms_ffn_bwd_{tag}")
    for k in ("norm_ple_g", "norm_ffn_g"):
        gs[k] = gs[k][0]
    return dx1, dx1_b, gw, gs


def _layer_bwd_mix(dx1, dx1_b, sv, w, small, tag):
    gw, gs = {}, {}
    gw["w_out"] = _mm(sv["merged"], dx1_b, "tn", BF16, f"mix_out_dw_{tag}")
    dm = _mm(dx1_b, w["w_out"], "nt", BF16, f"mix_out_dx_{tag}")
    dyc, dya, dgg = _merge_bwd(sv["gg"], sv["y_conv"], sv["y_attn"], dm, f"merge_bwd_{tag}")
    gw["w_attn_out"] = _mm(sv["o"], dya, "tn", BF16, f"attn_out_dw_{tag}")
    do = _mm(dya, w["w_attn_out"], "nt", F32, f"attn_out_dx_{tag}")
    stats, do_b = _attn_stats(sv["o"], do, sv["lse"], f"attn_stats_{tag}")
    dq, dk, dv, dc_row, dc_col = _attn_bwd(sv["qkv"], do_b, stats, sv["c_row"], f"attn_bwd_{tag}")
    dc = jnp.pad(jnp.transpose(dc_row[:, 0, :] + dc_col[:, :, 0]), ((0, 0), (0, LANES - N_HEADS)))
    df, db = _forget_bwd(sv["f"], small["b_forget"], dc, f"forget_bwd_{tag}")
    gs["b_forget"] = db[0, :N_HEADS]
    gw["w_conv_out"] = _mm(sv["u3"], dyc, "tn", BF16, f"conv_out_dw_{tag}")
    du3 = _mm(dyc, w["w_conv_out"], "nt", BF16, f"conv_out_dx_{tag}")
    du1, gs["conv_ln_g"], gs["conv_ln_b"], gs["conv_b"] = _conv_norm_bwd(
        sv["u1"], du3, small["conv_ln_g"], small["conv_ln_b"], f"conv_norm_bwd_{tag}")
    dag, gs["conv_w"] = _conv_bwd(sv["ag"], du1, w["conv_w"], f"conv_bwd_{tag}")
    dqkv = jnp.concatenate([dq, dk, dv], axis=1)
    g_ag = _mm(sv["h"], dag, "tn", BF16, f"proj_glu_dw_{tag}")
    g_qkv = _mm(sv["h"], dqkv, "tn", BF16, f"proj_qkv_dw_{tag}")
    g_f = _mm(sv["h"], df, "tn", BF16, f"proj_forget_dw_{tag}")
    g_gg = _mm(sv["h"], dgg, "tn", BF16, f"proj_gates_dw_{tag}")
    gw["w_in"] = jnp.concatenate([g_ag, g_qkv, g_f[:, :N_HEADS], g_gg], axis=1)
    dh = _mm(df, w["w_f"], "nt", F32, f"proj_forget_dx_{tag}")
    dh = _mm(dag, w["w_ag"], "nt", F32, f"proj_glu_dx_{tag}", residual=dh)
    dh = _mm(dqkv, w["w_qkv"], "nt", F32, f"proj_qkv_dx_{tag}", residual=dh)
    dh = _mm(dgg, w["w_gg"], "nt", F32, f"proj_gates_dx_{tag}", residual=dh)
    dx, _, gs["norm_mix_g"] = _rms_bwd(sv["x"], small["norm_mix_g"], dh, dx1, f"rms_mix_bwd_{tag}")
    for k in ("norm_mix_g", "conv_ln_g", "conv_ln_b", "conv_b"):
        gs[k] = gs[k][0]
    return dx, gw, gs


_COL_SHARDED = ("w_in", "w_conv_out", "w_attn_out", "w_gate_up", "w_ple_proj")
_ROW_SHARDED = ("w_out", "w_down", "w_ple_gate")
_BIG = _COL_SHARDED + _ROW_SHARDED
_SMALL = ("norm_mix_g", "b_forget", "conv_b", "conv_ln_g", "conv_ln_b", "norm_ffn_g", "norm_ple_g")
_ORDER = ("norm_mix_g", "w_in", "b_forget", "conv_w", "conv_b", "conv_ln_g", "conv_ln_b", "w_conv_out", "w_attn_out",
          "w_out", "norm_ffn_g", "w_gate_up", "w_down", "norm_ple_g", "w_ple_gate", "w_ple_proj", "final_g")


def _pack(vectors, rows):
    flat = jnp.concatenate([v.reshape(-1) for v in vectors])
    return jnp.pad(flat, (0, rows * LANES - flat.shape[0])).reshape(1, rows, LANES)


def _unpack(packed, like):
    flat, out, at = packed.reshape(-1), [], 0
    for v in like:
        out.append(flat[at:at + v.size].reshape(v.shape))
        at += v.size
    return out


def kernel(x, p, norm_mix_g, w_in, b_forget, conv_w, conv_b, conv_ln_g, conv_ln_b, w_conv_out, w_attn_out, w_out, norm_ffn_g, w_gate_up, w_down, norm_ple_g, w_ple_gate, w_ple_proj, final_g, loss_target, m_norm_mix_g, m_w_in, m_b_forget, m_conv_w, m_conv_b, m_conv_ln_g, m_conv_ln_b, m_w_conv_out, m_w_attn_out, m_w_out, m_norm_ffn_g, m_w_gate_up, m_w_down, m_norm_ple_g, m_w_ple_gate, m_w_ple_proj, m_final_g, v_norm_mix_g, v_w_in, v_b_forget, v_conv_w, v_conv_b, v_conv_ln_g, v_conv_ln_b, v_w_conv_out, v_w_attn_out, v_w_out, v_norm_ffn_g, v_w_gate_up, v_w_down, v_norm_ple_g, v_w_ple_gate, v_w_ple_proj, v_final_g):
    weights = dict(norm_mix_g=norm_mix_g, w_in=w_in, b_forget=b_forget, conv_w=conv_w, conv_b=conv_b, conv_ln_g=conv_ln_g, conv_ln_b=conv_ln_b, w_conv_out=w_conv_out, w_attn_out=w_attn_out, w_out=w_out, norm_ffn_g=norm_ffn_g, w_gate_up=w_gate_up, w_down=w_down, norm_ple_g=norm_ple_g, w_ple_gate=w_ple_gate, w_ple_proj=w_ple_proj, final_g=final_g)
    mom1 = dict(norm_mix_g=m_norm_mix_g, w_in=m_w_in, b_forget=m_b_forget, conv_w=m_conv_w, conv_b=m_conv_b, conv_ln_g=m_conv_ln_g, conv_ln_b=m_conv_ln_b, w_conv_out=m_w_conv_out, w_attn_out=m_w_attn_out, w_out=m_w_out, norm_ffn_g=m_norm_ffn_g, w_gate_up=m_w_gate_up, w_down=m_w_down, norm_ple_g=m_norm_ple_g, w_ple_gate=m_w_ple_gate, w_ple_proj=m_w_ple_proj, final_g=m_final_g)
    mom2 = dict(norm_mix_g=v_norm_mix_g, w_in=v_w_in, b_forget=v_b_forget, conv_w=v_conv_w, conv_b=v_conv_b, conv_ln_g=v_conv_ln_g, conv_ln_b=v_conv_ln_b, w_conv_out=v_w_conv_out, w_attn_out=v_w_attn_out, w_out=v_w_out, norm_ffn_g=v_norm_ffn_g, w_gate_up=v_w_gate_up, w_down=v_w_down, norm_ple_g=v_norm_ple_g, w_ple_gate=v_w_ple_gate, w_ple_proj=v_w_ple_proj, final_g=v_final_g)
    depth = w_in.shape[0]
    xi, yi, ci = lax.axis_index("x"), lax.axis_index("y"), lax.axis_index("c")
    me = 4 * xi + 2 * yi + ci

    rest_names = [k for k in _BIG if k != "w_in"]
    first = _all_gather([weights["w_in"][0].astype(BF16), conv_w], "gather_weights_in_l0")
    gathered_conv_w = first[1]
    shards, _ = lax.optimization_barrier(([weights[k][0].astype(BF16) for k in rest_names], first[0]))
    rest = _all_gather_beside(shards, "gather_weights_rest_l0", collective_id=0)
    gathered = [dict(zip(rest_names, rest), w_in=first[0], conv_w=gathered_conv_w[:, 0])] + [None] * (depth - 1)

    def gather_layer(l, tie):
        shards, _ = lax.optimization_barrier(([weights[k][l].astype(BF16) for k in _BIG], tie))
        later = _all_gather_beside(shards, f"gather_weights_l{l}", collective_id=l)
        gathered[l] = dict(zip(_BIG, later), conv_w=gathered_conv_w[:, l])

    act = x[0]
    layers = []
    for l in range(depth):
        if l > 0:
            gathered[l], act = lax.optimization_barrier((gathered[l], act))
        small = {k: weights[k][l][None, :] for k in _SMALL if k != "b_forget"}
        small["b_forget"] = _pad_lanes(b_forget[l])
        gather_next = functools.partial(gather_layer, l + 1) if l + 1 < depth else None
        act, saved, w = _layer_fwd(act, p[l, 0], gathered[l], small, f"l{l}", after_mixers=gather_next)
        layers.append((w, small, saved))
    dact, g_final, loss_part = _loss_head(act, final_g[None, :], loss_target[0], "loss_head")
    loss = lax.psum(loss_part[0, 0], ("x", "y", "c"))

    core = ci.reshape(1).astype(jnp.int32)
    ffn_group = ("w_ple_gate", "w_ple_proj", "w_down", "w_gate_up")
    mix_group = ("w_out", "w_attn_out", "w_conv_out", "w_in")

    def reduce_scatter(gw, names, tie, tag, collective_id):
        blocks = [(gw[k] if gw[k].ndim == 3 else _shard_cols(gw[k]) if k in _COL_SHARDED
                   else gw[k].reshape((N_DEV, -1) + gw[k].shape[1:]))[:, None] for k in names]
        theirs = _pair_exchange(blocks, f"grads_pair_exchange_{tag}")
        chip_parts = [_pair_sum(a, b, core, f"grads_pair_sum_{k}_{tag}") for k, a, b in zip(names, blocks, theirs)]
        chip_parts, tie = lax.optimization_barrier((chip_parts, tie))
        return _chip_exchange_beside(chip_parts, f"grads_chip_exchange_{tag}", collective_id), tie

    arrived = {}
    small_grads = {k: [None] * depth for k in _SMALL + ("conv_w",)}
    waiting = None
    for l in reversed(range(depth)):
        w, small, saved = layers[l]
        dx1, dx1_b, gw, gs = _layer_bwd_ffn(dact, saved, w, small, f"l{l}")
        if waiting is not None:
            arrived[waiting], dx1 = lax.optimization_barrier((arrived[waiting], dx1))
        arrived[l, 0], dx1 = reduce_scatter(gw, ffn_group, dx1, f"ffn_l{l}", depth + 2 * l)
        dact, gw, gs_mix = _layer_bwd_mix(dx1, dx1_b, saved, w, small, f"l{l}")
        arrived[l, 0], dact = lax.optimization_barrier((arrived[l, 0], dact))
        arrived[l, 1], dact = reduce_scatter(gw, mix_group, dact, f"mix_l{l}", depth + 2 * l + 1)
        waiting = (l, 1)
        for k, g in {**gs, **gs_mix}.items():
            small_grads[k][l] = g
    grad_x = dact[None]

    out = {}
    for group, names in enumerate((ffn_group, mix_group)):
        for i, k in enumerate(names):
            parts = jnp.concatenate([arrived[l, group][i] for l in range(depth)], axis=1)
            out[k] = _sum_adamw(parts, weights[k], mom1[k], mom2[k], f"adamw_{k}")

    small_names = list(_SMALL) + ["conv_w", "final_g"]
    small_list = [jnp.stack(small_grads[k]) for k in _SMALL + ("conv_w",)] + [g_final[0]]
    n_small = sum(v.size for v in small_list)
    rows = -(-n_small // (8 * LANES)) * 8
    all_parts = _all_gather([_pack(small_list, rows)], "gather_small_grads")[0]
    full_conv_w = jnp.transpose(gathered_conv_w, (1, 2, 0, 3)).reshape(depth, CONV_K, -1)
    full_conv_m = jnp.zeros_like(full_conv_w)
    ch_shard = conv_w.shape[-1]
    at = (0, 0, me * ch_shard)
    small_w = [weights[k] for k in _SMALL] + [full_conv_w, final_g]
    small_m = [mom1[k] for k in _SMALL] + [lax.dynamic_update_slice(full_conv_m, mom1["conv_w"], at), mom1["final_g"]]
    small_v = [mom2[k] for k in _SMALL] + [lax.dynamic_update_slice(full_conv_m + 1.0, mom2["conv_w"], at), mom2["final_g"]]
    res = _sum_adamw(all_parts, _pack(small_w, rows), _pack(small_m, rows), _pack(small_v, rows), "adamw_small")
    unpacked = [_unpack(r, small_list) for r in res]
    for idx, k in enumerate(small_names):
        vals = [u[idx] for u in unpacked]
        if k == "conv_w":
            vals = [lax.dynamic_slice(v, at, conv_w.shape) for v in vals]
        out[k] = vals

    return (loss, grad_x, *[out[k][0] for k in _ORDER], *[out[k][1] for k in _ORDER],
            *[out[k][2] for k in _ORDER], *[out[k][3] for k in _ORDER])
```

```python
import functools
import math

import jax
import jax.numpy as jnp
from jax import lax
from jax.experimental import pallas as pl
from jax.experimental.pallas import tpu as pltpu
from jax.experimental.pallas import tpu_sc as plsc

F32 = jnp.float32
BF16 = jnp.bfloat16
MESH = pl.DeviceIdType.MESH
ANY = pl.BlockSpec(memory_space=pl.ANY)

N_DEV = 8
N_HEADS = 8
HEAD_DIM = 128
CONV_K = 31
HALO = 32
EPS = 1e-6
NEG_INF = -1e30
ADAM_LR = 0.001
ADAM_B1 = 0.9
ADAM_B2 = 0.999
ADAM_EPS = 1e-08
ADAM_WD = 0.01
ADAM_STEP = 10

V7X_VMEM_BYTES = 64 * 1024 * 1024
VMEM_LIMIT = V7X_VMEM_BYTES * 3 // 4
LANES = 128
ROW_TILE = 512
NORM_BWD_TILE = 256
CONV_TILE = 256
ATTN_TILE = 512
ATTN_BWD_KEYS = 1024
ATTN_FWD_TILE = 1024
ATTN_ROWS = 256
CUM_TILE = 256
FFN_K_TILE = 2816
FFN_N_TILE = 1408


def _params(*sem):
    return pltpu.CompilerParams(dimension_semantics=sem, vmem_limit_bytes=VMEM_LIMIT)


def _tile(dim, pref, mult=LANES):
    t = min(pref, dim)
    t -= t % mult
    while t >= mult:
        if dim % t == 0:
            return t
        t -= mult
    return dim


def _sigmoid(x):
    return 1.0 / (1.0 + jnp.exp(-x))


def _place():
    x, y, c = lax.axis_index("x"), lax.axis_index("y"), lax.axis_index("c")
    chips = [(1 - x, y), (x, 1 - y), (1 - x, 1 - y)]
    return x, y, c, chips


def _all_gather(shards, name):
    n = len(shards)

    def body(*refs):
        ins, outs = refs[:n], refs[n:2 * n]
        send_sems, recv_sems, local_sems = refs[2 * n:]
        x, y, c, chips = _place()
        me, sibling = (x, y, c), (x, y, 1 - c)

        def idx(px, py, pc):
            return 4 * px + 2 * py + pc

        def copy(t, k, block, to, src=None):
            rows = outs[t].at[idx(*block)]
            return pltpu.make_async_remote_copy(
                src_ref=rows if src is None else src, dst_ref=rows,
                send_sem=send_sems.at[t, k], recv_sem=recv_sems.at[t, k],
                device_id=to, device_id_type=MESH)

        local = [pltpu.make_async_copy(ins[t], outs[t].at[idx(*me)], local_sems.at[t]) for t in range(n)]
        for cp in local:
            cp.start()
        first = []
        for t in range(n):
            first.append(copy(t, 0, me, sibling, src=ins[t]))
            first += [copy(t, 1 + j, me, (*chip, c), src=ins[t]) for j, chip in enumerate(chips)]
        for cp in first:
            cp.start()
        passed = []
        for j, chip in enumerate(chips):
            for t in range(n):
                copy(t, 1 + j, (*chip, c), me).wait_recv()
                cp = copy(t, 4 + j, (*chip, c), sibling)
                cp.start()
                passed.append(cp)
        for t in range(n):
            copy(t, 0, sibling, me).wait_recv()
        for j, chip in enumerate(chips):
            for t in range(n):
                copy(t, 4 + j, (*chip, 1 - c), me).wait_recv()
        for cp in first + passed:
            cp.wait_send()
        for cp in local:
            cp.wait()

    return pl.pallas_call(
        body, name=name,
        out_shape=[jax.ShapeDtypeStruct((N_DEV,) + s.shape, s.dtype) for s in shards],
        in_specs=[ANY] * n, out_specs=[ANY] * n,
        scratch_shapes=[pltpu.SemaphoreType.DMA((n, 7)), pltpu.SemaphoreType.DMA((n, 7)),
                        pltpu.SemaphoreType.DMA((n,))],
    )(*shards)


def _all_gather_beside(shards, name, collective_id):
    n = len(shards)
    src = [jax.new_ref(s, memory_space=pltpu.MemorySpace.HBM) for s in shards]
    dst = [jax.empty_ref(jax.ShapeDtypeStruct((N_DEV,) + s.shape, s.dtype), memory_space=pltpu.MemorySpace.HBM)
           for s in shards]

    @pl.kernel(mesh=plsc.ScalarSubcoreMesh(axis_name="sequencer", num_cores=1), name=name,
               scratch_types=(pltpu.SemaphoreType.DMA((n, 7)), pltpu.SemaphoreType.DMA((n, 7)),
                              pltpu.SemaphoreType.DMA((n,))),
               compiler_params=pltpu.CompilerParams(collective_id=collective_id))
    def launch(send_sems, recv_sems, local_sems):
        x, y, c, chips = _place()
        me, sibling = (x, y, c), (x, y, 1 - c)
        barrier = pltpu.get_barrier_semaphore()
        for peer in [sibling] + [(*chip, pc) for chip in chips for pc in (c, 1 - c)]:
            pl.semaphore_signal(barrier, inc=1, device_id=peer, device_id_type=MESH)
        pl.semaphore_wait(barrier, N_DEV - 1)

        def idx(px, py, pc):
            return 4 * px + 2 * py + pc

        def copy(t, k, block, to, own=False):
            rows = dst[t].at[idx(*block)]
            return pltpu.make_async_remote_copy(
                src_ref=src[t] if own else rows, dst_ref=rows,
                send_sem=send_sems.at[t, k], recv_sem=recv_sems.at[t, k],
                device_id=to, device_id_type=MESH)

        local = [pltpu.make_async_copy(src[t], dst[t].at[idx(*me)], local_sems.at[t]) for t in range(n)]
        for cp in local:
            cp.start()
        first = []
        for t in range(n):
            first.append(copy(t, 0, me, sibling, own=True))
            first += [copy(t, 1 + j, me, (*chip, c), own=True) for j, chip in enumerate(chips)]
        for cp in first:
            cp.start()
        passed = []
        for j, chip in enumerate(chips):
            for t in range(n):
                copy(t, 1 + j, (*chip, c), me).wait_recv()
                cp = copy(t, 4 + j, (*chip, c), sibling)
                cp.start()
                passed.append(cp)
        for t in range(n):
            copy(t, 0, sibling, me).wait_recv()
        for j, chip in enumerate(chips):
            for t in range(n):
                copy(t, 4 + j, (*chip, 1 - c), me).wait_recv()
        for cp in first + passed:
            cp.wait_send()
        for cp in local:
            cp.wait()

    launch()
    return [d[...] for d in dst]


def _pair_exchange(grads, name):
    n = len(grads)

    def body(*refs):
        ins, theirs = refs[:n], refs[n:2 * n]
        send_sems, recv_sems = refs[2 * n:]
        x, y, c, _ = _place()
        copies = []
        for t in range(n):
            for q in range(4):
                give = pltpu.make_async_remote_copy(
                    src_ref=ins[t].at[2 * q + 1 - c], dst_ref=theirs[t].at[q],
                    send_sem=send_sems.at[t, q], recv_sem=recv_sems.at[t, q],
                    device_id=(x, y, 1 - c), device_id_type=MESH)
                give.start()
                copies.append(give)
        for cp in copies:
            cp.wait()

    return pl.pallas_call(
        body, name=name, out_shape=[jax.ShapeDtypeStruct((4,) + g.shape[1:], g.dtype) for g in grads],
        in_specs=[ANY] * n, out_specs=[ANY] * n,
        scratch_shapes=[pltpu.SemaphoreType.DMA((n, 4)), pltpu.SemaphoreType.DMA((n, 4))],
    )(*grads)


def _chip_exchange_beside(parts, name, collective_id):
    n = len(parts)
    src = [jax.new_ref(p, memory_space=pltpu.MemorySpace.HBM) for p in parts]
    dst = [jax.empty_ref(jax.ShapeDtypeStruct(p.shape, p.dtype), memory_space=pltpu.MemorySpace.HBM) for p in parts]

    @pl.kernel(mesh=plsc.ScalarSubcoreMesh(axis_name="sequencer", num_cores=1), name=name,
               scratch_types=(pltpu.SemaphoreType.DMA((n, 3)), pltpu.SemaphoreType.DMA((n, 3)),
                              pltpu.SemaphoreType.DMA((n,))),
               compiler_params=pltpu.CompilerParams(collective_id=collective_id))
    def launch(send_sems, recv_sems, local_sems):
        x, y, c, chips = _place()
        barrier = pltpu.get_barrier_semaphore()
        for px, py in chips:
            pl.semaphore_signal(barrier, inc=1, device_id=(px, py, c), device_id_type=MESH)
        pl.semaphore_wait(barrier, len(chips))
        copies = []
        for t in range(n):
            own = pltpu.make_async_copy(src[t].at[2 * x + y], dst[t].at[0], local_sems.at[t])
            own.start()
            copies.append(own)
            for j, (px, py) in enumerate(chips):
                cp = pltpu.make_async_remote_copy(
                    src_ref=src[t].at[2 * px + py], dst_ref=dst[t].at[1 + j],
                    send_sem=send_sems.at[t, j], recv_sem=recv_sems.at[t, j],
                    device_id=(px, py, c), device_id_type=MESH)
                cp.start()
                copies.append(cp)
        for cp in copies:
            cp.wait()

    launch()
    return [d[...] for d in dst]


_DOT_DIMS = {"nn": (((1,), (0,)), ((), ())), "nt": (((1,), (1,)), ((), ())), "tn": (((0,), (0,)), ((), ()))}


def _mm(a, b, mode, out_dtype, name, residual=None, tm=1024, tn=1024, tk=2048, out_blocks=None, scaled_cols=None):
    if mode == "tn":
        (k_dim, m_dim), n_dim = a.shape, b.shape[1]
    else:
        (m_dim, k_dim), n_dim = a.shape, (b.shape[1] if mode == "nn" else b.shape[0])
    tm, tn, tk = _tile(m_dim, tm), _tile(n_dim // (out_blocks or 1), tn), _tile(k_dim, tk)
    nk = k_dim // tk

    def body(*refs):
        a_ref, b_ref = refs[:2]
        r_ref = refs[2] if residual is not None else None
        o_ref = refs[2 if residual is None else 3]

        def finish(r):
            if r_ref is not None:
                r = r + r_ref[...].astype(F32)
            if scaled_cols:
                r = r * jnp.where(pl.program_id(1) < scaled_cols[0] // tn, scaled_cols[1], 1.0)
            o_ref[...] = r.astype(o_ref.dtype)

        part = lax.dot_general(a_ref[...].astype(BF16), b_ref[...].astype(BF16), _DOT_DIMS[mode],
                               preferred_element_type=F32)
        if nk == 1:
            finish(part)
            return
        acc, k = refs[-1], pl.program_id(2)

        @pl.when(k == 0)
        def _():
            acc[...] = part

        @pl.when((k > 0) & (k < nk - 1))
        def _():
            acc[...] += part

        @pl.when(k == nk - 1)
        def _():
            finish(acc[...] + part)

    a_spec = pl.BlockSpec((tk, tm), lambda i, j, k: (k, i)) if mode == "tn" else pl.BlockSpec((tm, tk), lambda i, j, k: (i, k))
    b_spec = pl.BlockSpec((tn, tk), lambda i, j, k: (j, k)) if mode == "nt" else pl.BlockSpec((tk, tn), lambda i, j, k: (k, j))
    o_spec, o_shape = pl.BlockSpec((tm, tn), lambda i, j, k: (i, j)), (m_dim, n_dim)
    if out_blocks:
        per = n_dim // out_blocks // tn
        o_spec = pl.BlockSpec((None, tm, tn), lambda i, j, k: (j // per, i, j % per))
        o_shape = (out_blocks, m_dim, n_dim // out_blocks)
    ins, specs = [a, b], [a_spec, b_spec]
    if residual is not None:
        ins.append(residual)
        specs.append(o_spec)
    return pl.pallas_call(
        body, name=name, grid=(m_dim // tm, n_dim // tn, nk),
        out_shape=jax.ShapeDtypeStruct(o_shape, out_dtype),
        in_specs=specs, out_specs=o_spec,
        scratch_shapes=[pltpu.VMEM((tm, tn), F32)] if nk > 1 else [],
        compiler_params=_params("parallel", "parallel", "arbitrary"),
    )(*ins)


def _rows(body, name, row_ins, full_ins, row_outs, acc_outs, ts=ROW_TILE):
    s = row_ins[0].shape[0]
    ts = _tile(s, ts, 16)
    n_ri, n_fi, n_ro = len(row_ins), len(full_ins), len(row_outs)

    def kernel(*refs):
        i = pl.program_id(0)
        ri, fi = refs[:n_ri], refs[n_ri:n_ri + n_fi]
        ro, ao = refs[n_ri + n_fi:n_ri + n_fi + n_ro], refs[n_ri + n_fi + n_ro:]

        if ao:
            @pl.when(i == 0)
            def _():
                for r in ao:
                    r[...] = jnp.zeros_like(r)

        body(i, ri, fi, ro, ao)

    def row_spec(cols):
        return pl.BlockSpec((ts, cols), lambda i: (i, 0))

    def full_spec(shape):
        return pl.BlockSpec(shape, lambda i: (0,) * len(shape))

    return pl.pallas_call(
        kernel, name=name, grid=(s // ts,),
        out_shape=[jax.ShapeDtypeStruct((s, c), d) for c, d in row_outs]
        + [jax.ShapeDtypeStruct(sh, d) for sh, d in acc_outs],
        in_specs=[row_spec(a.shape[1]) for a in row_ins] + [full_spec(a.shape) for a in full_ins],
        out_specs=[row_spec(c) for c, _ in row_outs] + [full_spec(sh) for sh, _ in acc_outs],
        compiler_params=_params("arbitrary"),
    )(*row_ins, *full_ins)


def _rms_stats(x):
    return lax.rsqrt(jnp.mean(x * x, axis=-1, keepdims=True) + EPS)


def _rms_fwd(x, g, name):
    def body(i, ri, fi, ro, ao):
        xv = ri[0][...]
        ro[0][...] = (xv * _rms_stats(xv) * fi[0][...]).astype(BF16)

    return _rows(body, name, [x], [g], [(x.shape[1], BF16)], [])[0]


def _rms_bwd_rows(xv, gv, dh):
    r = _rms_stats(xv)
    xhat = xv * r
    dxhat = dh * gv
    dx = r * (dxhat - xhat * jnp.mean(dxhat * xhat, axis=-1, keepdims=True))
    return dx, jnp.sum(dh * xhat, axis=0, keepdims=True)


def _rms_bwd(x, g, dh, dres, name):
    d = x.shape[1]

    def body(i, ri, fi, ro, ao):
        dx, dg = _rms_bwd_rows(ri[0][...], fi[0][...], ri[1][...].astype(F32))
        dx = dx + ri[2][...]
        ro[0][...] = dx
        ro[1][...] = dx.astype(BF16)
        ao[0][...] += dg

    return _rows(body, name, [x, dh, dres], [g], [(d, F32), (d, BF16)], [((1, d), F32)], ts=NORM_BWD_TILE)


def _loss_head(x, g, target, name):
    d = x.shape[1]

    def body(i, ri, fi, ro, ao):
        xv, gv = ri[0][...], fi[0][...]
        y = xv * _rms_stats(xv) * gv
        err = y - ri[1][...]
        ao[1][...] += 0.5 * jnp.sum(jnp.mean(err * err, axis=-1, keepdims=True))
        dx, dg = _rms_bwd_rows(xv, gv, err * (1.0 / d))
        ro[0][...] = dx
        ao[0][...] += dg

    return _rows(body, name, [x, target], [g], [(d, F32)], [((1, d), F32), ((8, LANES), F32)], ts=NORM_BWD_TILE)


def _merge_fwd(gg, y_conv, y_attn, name):
    d = y_conv.shape[1]

    def body(i, ri, fi, ro, ao):
        gv, yc, ya = (r[...].astype(F32) for r in ri)
        m = _sigmoid(gv[:, :d]) * yc + _sigmoid(gv[:, d:]) * ya
        ro[0][...] = m.astype(BF16)

    return _rows(body, name, [gg, y_conv, y_attn], [], [(d, BF16)], [])[0]


def _merge_bwd(gg, y_conv, y_attn, dm, name):
    d = y_conv.shape[1]

    def body(i, ri, fi, ro, ao):
        gv, yc, ya, dmv = (r[...].astype(F32) for r in ri)
        sc, sa = _sigmoid(gv[:, :d]), _sigmoid(gv[:, d:])
        ro[0][...] = (dmv * sc).astype(BF16)
        ro[1][...] = (dmv * sa).astype(BF16)
        ro[2][:, :d] = (dmv * yc * sc * (1.0 - sc)).astype(BF16)
        ro[2][:, d:] = (dmv * ya * sa * (1.0 - sa)).astype(BF16)

    return _rows(body, name, [gg, y_conv, y_attn, dm], [], [(d, BF16), (d, BF16), (2 * d, BF16)], [])


def _swiglu_fwd(gu, name):
    f = gu.shape[1] // 2

    def body(i, ri, fi, ro, ao):
        v = ri[0][...].astype(F32)
        gate, up = v[:, :f], v[:, f:]
        ro[0][...] = (gate * _sigmoid(gate) * up).astype(BF16)

    return _rows(body, name, [gu], [], [(f, BF16)], [], ts=256)[0]


def _swiglu_bwd(gu, dact, name):
    f = gu.shape[1] // 2

    def body(i, ri, fi, ro, ao):
        v, dv = ri[0][...].astype(F32), ri[1][...].astype(F32)
        gate, up = v[:, :f], v[:, f:]
        sg = _sigmoid(gate)
        ro[0][:, :f] = (dv * up * sg * (1.0 + gate * (1.0 - sg))).astype(BF16)
        ro[0][:, f:] = (dv * gate * sg).astype(BF16)

    return _rows(body, name, [gu, dact], [], [(2 * f, BF16)], [], ts=256)[0]


def _ple_fwd(x, z, pp, name):
    d = x.shape[1]

    def body(i, ri, fi, ro, ao):
        ro[0][...] = ri[0][...] + _sigmoid(ri[1][...].astype(F32)) * ri[2][...].astype(F32)

    return _rows(body, name, [x, z, pp], [], [(d, F32)], [])[0]


def _ple_bwd(dx, z, pp, name):
    d = dx.shape[1]

    def body(i, ri, fi, ro, ao):
        dv, sz = ri[0][...], _sigmoid(ri[1][...].astype(F32))
        ro[0][...] = (dv * ri[2][...].astype(F32) * sz * (1.0 - sz)).astype(BF16)
        ro[1][...] = (dv * sz).astype(BF16)

    return _rows(body, name, [dx, z, pp], [], [(d, BF16), (d, BF16)], [])


def _triangle(n, lower):
    r = lax.broadcasted_iota(jnp.int32, (n, n), 0)
    c = lax.broadcasted_iota(jnp.int32, (n, n), 1)
    return jnp.where((c <= r) if lower else (c >= r), 1.0, 0.0).astype(F32)


def _forget_fwd(f, bias, name):
    s = f.shape[0]
    tb = _tile(s, CUM_TILE, 8)

    def body(f_ref, b_ref, c_ref, carry):
        @pl.when(pl.program_id(0) == 0)
        def _():
            carry[...] = jnp.zeros_like(carry)

        xv = f_ref[...] + b_ref[...]
        logf = jnp.minimum(xv, 0.0) - jnp.log(1.0 + jnp.exp(-jnp.abs(xv)))
        cv = jnp.dot(_triangle(tb, True), logf, precision=lax.Precision.HIGHEST, preferred_element_type=F32) + carry[...]
        c_ref[...] = cv
        carry[...] = cv[tb - 1:tb, :]

    return pl.pallas_call(
        body, name=name, grid=(s // tb,), out_shape=jax.ShapeDtypeStruct((s, LANES), F32),
        in_specs=[pl.BlockSpec((tb, LANES), lambda i: (i, 0)), pl.BlockSpec((1, LANES), lambda i: (0, 0))],
        out_specs=pl.BlockSpec((tb, LANES), lambda i: (i, 0)),
        scratch_shapes=[pltpu.VMEM((1, LANES), F32)],
        compiler_params=_params("arbitrary"),
    )(f, bias)


def _forget_bwd(f, bias, dc, name):
    s = f.shape[0]
    tb = _tile(s, CUM_TILE, 16)
    nb = s // tb

    def body(f_ref, b_ref, dc_ref, df_ref, db_ref, carry):
        @pl.when(pl.program_id(0) == 0)
        def _():
            carry[...] = jnp.zeros_like(carry)
            db_ref[...] = jnp.zeros_like(db_ref)

        dlog = jnp.dot(_triangle(tb, False), dc_ref[...], precision=lax.Precision.HIGHEST,
                       preferred_element_type=F32) + carry[...]
        carry[...] = dlog[0:1, :]
        dlogit = dlog * (1.0 - _sigmoid(f_ref[...] + b_ref[...]))
        df_ref[...] = dlogit.astype(BF16)
        db_ref[...] += jnp.sum(dlogit, axis=0, keepdims=True)

    back = lambda i: (nb - 1 - i, 0)
    return pl.pallas_call(
        body, name=name, grid=(nb,),
        out_shape=[jax.ShapeDtypeStruct((s, LANES), BF16), jax.ShapeDtypeStruct((1, LANES), F32)],
        in_specs=[pl.BlockSpec((tb, LANES), back), pl.BlockSpec((1, LANES), lambda i: (0, 0)),
                  pl.BlockSpec((tb, LANES), back)],
        out_specs=[pl.BlockSpec((tb, LANES), back), pl.BlockSpec((1, LANES), lambda i: (0, 0))],
        scratch_shapes=[pltpu.VMEM((1, LANES), F32)],
        compiler_params=_params("arbitrary"),
    )(f, bias, dc)


def _glu(block, ch):
    return block[:, :ch] * _sigmoid(block[:, ch:])


def _layernorm_rows(u1, g, b):
    mu = jnp.mean(u1, axis=-1, keepdims=True)
    cen = u1 - mu
    rstd = lax.rsqrt(jnp.mean(cen * cen, axis=-1, keepdims=True) + EPS)
    xhat = cen * rstd
    return xhat, rstd, xhat * g + b


def _conv_fwd(ag, conv_w, conv_b, ln_g, ln_b, name):
    s, ch = ag.shape[0], ag.shape[1] // 2
    ts = _tile(s, CONV_TILE, HALO)
    per = ts // HALO

    def body(cur_ref, halo_ref, w_ref, cb_ref, g_ref, b_ref, u1_ref, u3_ref, cat):
        i = pl.program_id(0)
        cat[HALO:, :] = _glu(cur_ref[...], ch)
        cat[:HALO, :] = jnp.where(i == 0, 0.0, _glu(halo_ref[...], ch))
        for c0 in range(0, ch, LANES):
            cols = slice(c0, c0 + LANES)
            full = cat[:, cols]
            acc = jnp.zeros((ts, LANES), F32) + cb_ref[:, cols]
            for j in range(CONV_K):
                shifted = full if j == CONV_K - 1 else pltpu.roll(full, CONV_K - 1 - j, 0)
                acc = acc + w_ref[j:j + 1, cols] * shifted[HALO:, :]
            u1_ref[:, cols] = acc
        _, _, u2 = _layernorm_rows(u1_ref[...], g_ref[...], b_ref[...])
        u3_ref[...] = (u2 * _sigmoid(u2)).astype(BF16)

    whole = lambda shape: pl.BlockSpec(shape, lambda i: (0, 0))
    return pl.pallas_call(
        body, name=name, grid=(s // ts,),
        out_shape=[jax.ShapeDtypeStruct((s, ch), F32), jax.ShapeDtypeStruct((s, ch), BF16)],
        in_specs=[pl.BlockSpec((ts, 2 * ch), lambda i: (i, 0)),
                  pl.BlockSpec((HALO, 2 * ch), lambda i: (jnp.maximum(i * per - 1, 0), 0)),
                  whole(conv_w.shape), whole((1, ch)), whole((1, ch)), whole((1, ch))],
        out_specs=[pl.BlockSpec((ts, ch), lambda i: (i, 0)), pl.BlockSpec((ts, ch), lambda i: (i, 0))],
        scratch_shapes=[pltpu.VMEM((HALO + ts, ch), F32)],
        compiler_params=_params("arbitrary"),
    )(ag, ag, conv_w, conv_b, ln_g, ln_b)


def _conv_norm_bwd(u1, du3, ln_g, ln_b, name):
    ch = u1.shape[1]

    def body(i, ri, fi, ro, ao):
        g = fi[0][...]
        xhat, rstd, u2 = _layernorm_rows(ri[0][...], g, fi[1][...])
        sg = _sigmoid(u2)
        du2 = ri[1][...].astype(F32) * sg * (1.0 + u2 * (1.0 - sg))
        dxhat = du2 * g
        du1 = rstd * (dxhat - jnp.mean(dxhat, axis=-1, keepdims=True)
                      - xhat * jnp.mean(dxhat * xhat, axis=-1, keepdims=True))
        ro[0][...] = du1
        ao[0][...] += jnp.sum(du2 * xhat, axis=0, keepdims=True)
        ao[1][...] += jnp.sum(du2, axis=0, keepdims=True)
        ao[2][...] += jnp.sum(du1, axis=0, keepdims=True)

    return _rows(body, name, [u1, du3], [ln_g, ln_b], [(ch, F32)], [((1, ch), F32)] * 3)


def _conv_bwd(ag, du1, conv_w, name):
    s, ch = du1.shape
    ts = _tile(s, CONV_TILE, HALO)
    per = ts // HALO
    last = s // HALO - 1
    nb = s // ts

    def body(cur_ref, halo_ref, d_ref, dnext_ref, w_ref, dag_ref, dw_ref, cat, dcat):
        i = pl.program_id(0)

        @pl.when(i == 0)
        def _():
            dw_ref[...] = jnp.zeros_like(dw_ref)

        cat[HALO:, :] = _glu(cur_ref[...], ch)
        cat[:HALO, :] = jnp.where(i == 0, 0.0, _glu(halo_ref[...], ch))
        dcat[:ts, :] = d_ref[...]
        dcat[ts:, :] = jnp.where(i == nb - 1, 0.0, dnext_ref[...])
        for c0 in range(0, ch, LANES):
            cols, gate_cols = slice(c0, c0 + LANES), slice(ch + c0, ch + c0 + LANES)
            full, dfull = cat[:, cols], dcat[:, cols]
            dcur = dfull[:ts, :]
            du0 = jnp.zeros((ts, LANES), F32)
            for j in range(CONV_K):
                back = CONV_K - 1 - j
                dshift = dfull if back == 0 else pltpu.roll(dfull, ts + HALO - back, 0)
                du0 = du0 + w_ref[j:j + 1, cols] * dshift[:ts, :]
                ushift = full if back == 0 else pltpu.roll(full, back, 0)
                dw_ref[j:j + 1, cols] += jnp.sum(dcur * ushift[HALO:, :], axis=0, keepdims=True)
            a, sg = cur_ref[:, cols], _sigmoid(cur_ref[:, gate_cols])
            dag_ref[:, cols] = (du0 * sg).astype(BF16)
            dag_ref[:, gate_cols] = (du0 * a * sg * (1.0 - sg)).astype(BF16)

    return pl.pallas_call(
        body, name=name, grid=(nb,),
        out_shape=[jax.ShapeDtypeStruct((s, 2 * ch), BF16), jax.ShapeDtypeStruct(conv_w.shape, F32)],
        in_specs=[pl.BlockSpec((ts, 2 * ch), lambda i: (i, 0)),
                  pl.BlockSpec((HALO, 2 * ch), lambda i: (jnp.maximum(i * per - 1, 0), 0)),
                  pl.BlockSpec((ts, ch), lambda i: (i, 0)),
                  pl.BlockSpec((HALO, ch), lambda i: (jnp.minimum((i + 1) * per, last), 0)),
                  pl.BlockSpec(conv_w.shape, lambda i: (0, 0))],
        out_specs=[pl.BlockSpec((ts, 2 * ch), lambda i: (i, 0)), pl.BlockSpec(conv_w.shape, lambda i: (0, 0))],
        scratch_shapes=[pltpu.VMEM((HALO + ts, ch), F32), pltpu.VMEM((ts + HALO, ch), F32)],
        compiler_params=_params("arbitrary"),
    )(ag, ag, du1, du1, conv_w)


_SCALE = 1.0 / math.sqrt(HEAD_DIM)
_LOG2E = math.log2(math.e)


STAT_LANES = 8


def _causal(rows, cols, row0):
    r = row0 + lax.broadcasted_iota(jnp.int32, (rows, cols), 0)
    return lax.broadcasted_iota(jnp.int32, (rows, cols), 1) <= r


def _attn_fwd(qkv, c_row, name):
    s = qkv.shape[0]
    t = _tile(s, ATTN_FWD_TILE)
    nt, part_rows = s // t, min(ATTN_ROWS, t)
    n_parts = t // part_rows
    h_, d = N_HEADS, HEAD_DIM

    def body(q_ref, k_ref, v_ref, cr_ref, o_ref, lse_ref, *scratch):
        m_sc, l_sc, acc = scratch[:n_parts], scratch[n_parts:2 * n_parts], scratch[2 * n_parts:]
        i = pl.program_id(1)
        for part in range(n_parts):
            m_sc[part][...] = jnp.full_like(m_sc[part], NEG_INF)
            l_sc[part][...] = jnp.zeros_like(l_sc[part])
            acc[part][...] = jnp.zeros_like(acc[part])

        def block(j, diagonal):
            at = pl.multiple_of(j * t, t)
            k, v = k_ref[pl.ds(at, t), :], v_ref[pl.ds(at, t), :]
            cr = cr_ref[0, pl.ds(j, 1), :] * _LOG2E
            for part in range(n_parts):
                rows = pl.ds(part * part_rows, part_rows)
                sc = lax.dot_general(q_ref[rows, :], k, _DOT_DIMS["nt"], preferred_element_type=F32) - cr
                if diagonal:
                    sc = jnp.where(_causal(part_rows, t, part * part_rows), sc, NEG_INF)
                m_old = m_sc[part][...]
                m_new = jnp.maximum(m_old, jnp.max(sc, axis=-1, keepdims=True))
                alpha = jnp.exp2(m_old - m_new)
                p = jnp.exp2(sc - m_new)
                l_sc[part][...] = alpha * l_sc[part][...] + jnp.sum(p, axis=-1, keepdims=True)
                acc[part][...] = alpha * acc[part][...] + jnp.dot(p.astype(BF16), v, preferred_element_type=F32)
                m_sc[part][...] = m_new

        def below(j, carry):
            block(j, False)
            return carry

        lax.fori_loop(0, i, below, 0)
        block(i, True)
        for part in range(n_parts):
            rows = pl.ds(part * part_rows, part_rows)
            o_ref[rows, :] = (acc[part][...] / l_sc[part][...]).astype(BF16)
            lse_ref[0, rows, :] = m_sc[part][...] * (1.0 / _LOG2E) + jnp.log(l_sc[part][...])

    return pl.pallas_call(
        body, name=name, grid=(h_, nt),
        out_shape=[jax.ShapeDtypeStruct((s, h_ * d), BF16), jax.ShapeDtypeStruct((h_, s, 1), F32)],
        in_specs=[pl.BlockSpec((t, d), lambda h, i: (i, h)),
                  pl.BlockSpec((s, d), lambda h, i: (0, h_ + h)),
                  pl.BlockSpec((s, d), lambda h, i: (0, 2 * h_ + h)),
                  pl.BlockSpec((1, nt, t), lambda h, i: (h, 0, 0))],
        out_specs=[pl.BlockSpec((t, d), lambda h, i: (i, h)), pl.BlockSpec((1, t, 1), lambda h, i: (h, i, 0))],
        scratch_shapes=[pltpu.VMEM((part_rows, 1), F32)] * (2 * n_parts) + [pltpu.VMEM((part_rows, d), F32)] * n_parts,
        compiler_params=_params("parallel", "arbitrary"),
    )(qkv, qkv, qkv, c_row.reshape(h_, nt, t))


def _attn_stats(o, do, lse, name):
    s = o.shape[0]
    t = _tile(s, ATTN_TILE)
    h_, d = N_HEADS, HEAD_DIM

    def body(o_ref, do_ref, lse_ref, st_ref, dob_ref):
        dov = do_ref[...]
        st_ref[...] = jnp.zeros_like(st_ref)
        st_ref[0, :, 0:1] = -lse_ref[0]
        st_ref[0, :, 1:2] = jnp.sum(o_ref[...].astype(F32) * dov, axis=-1, keepdims=True)
        dob_ref[...] = dov.astype(BF16)

    col = pl.BlockSpec((1, t, 1), lambda h, i: (h, i, 0))
    blk = pl.BlockSpec((t, d), lambda h, i: (i, h))
    return pl.pallas_call(
        body, name=name, grid=(h_, s // t),
        out_shape=[jax.ShapeDtypeStruct((h_, s, STAT_LANES), F32), jax.ShapeDtypeStruct(o.shape, BF16)],
        in_specs=[blk, blk, col],
        out_specs=[pl.BlockSpec((1, t, STAT_LANES), lambda h, i: (h, i, 0)), blk],
        compiler_params=_params("parallel", "parallel"),
    )(o, do, lse)


def _attn_bwd(qkv, do, stats, c_row, name):
    s = qkv.shape[0]
    tq = _tile(s, ATTN_TILE)
    tk = _tile(s, ATTN_BWD_KEYS, tq)
    nq, nk, per = s // tq, s // tk, tk // tq
    h_, d = N_HEADS, HEAD_DIM

    def body(q_ref, do_ref, k_ref, v_ref, st_ref, cr_ref, dq_ref, dk_ref, dv_ref, dcr_ref, dcc_ref,
             dq_acc, dk_acc, dv_acc, dcr_acc):
        j = pl.program_id(1)

        @pl.when(j == 0)
        def _():
            dq_acc[...] = jnp.zeros_like(dq_acc)
            dcc_ref[...] = jnp.zeros_like(dcc_ref)

        dk_acc[...] = jnp.zeros_like(dk_acc)
        dv_acc[...] = jnp.zeros_like(dv_acc)
        dcr_acc[...] = jnp.zeros_like(dcr_acc)
        k, v, cr = k_ref[...], v_ref[...], cr_ref[0] * _LOG2E

        def block(i, row0):
            rows = pl.ds(pl.multiple_of(i * tq, tq), tq)
            q, dov, st = q_ref[rows, :], do_ref[rows, :], st_ref[0, rows, :]
            sc = lax.dot_general(q, k, _DOT_DIMS["nt"], preferred_element_type=F32) + (st[:, 0:1] * _LOG2E - cr)
            p = jnp.exp2(sc)
            if row0 is not None:
                p = jnp.where(_causal(tq, tk, row0), p, 0.0)
            dp = lax.dot_general(dov, v, _DOT_DIMS["nt"], preferred_element_type=F32)
            ds = p * (dp - st[:, 1:2])
            ds_b = ds.astype(BF16)
            dv_acc[...] += lax.dot_general(p.astype(BF16), dov, _DOT_DIMS["tn"], preferred_element_type=F32)
            dk_acc[...] += lax.dot_general(ds_b, q, _DOT_DIMS["tn"], preferred_element_type=F32)
            dq_acc[rows, :] += jnp.dot(ds_b, k, preferred_element_type=F32)
            dcr_acc[...] -= jnp.sum(ds, axis=0, keepdims=True)
            dcc_ref[0, rows, :] += jnp.sum(ds, axis=-1, keepdims=True)

        def below(i, carry):
            block(i, None)
            return carry

        for part in range(per):
            block(j * per + part, part * tq)
        lax.fori_loop((j + 1) * per, nq, below, 0)
        dk_ref[...] = (dk_acc[...] * (1.0 / _LOG2E)).astype(BF16)
        dv_ref[...] = dv_acc[...].astype(BF16)
        dcr_ref[0] = dcr_acc[...]

        @pl.when(j == nk - 1)
        def _():
            dq_ref[...] = (dq_acc[...] * _SCALE).astype(BF16)

    head = pl.BlockSpec((s, d), lambda h, j: (0, h))
    return pl.pallas_call(
        body, name=name, grid=(h_, nk),
        out_shape=[jax.ShapeDtypeStruct((s, h_ * d), BF16)] * 3
        + [jax.ShapeDtypeStruct((h_, 1, s), F32), jax.ShapeDtypeStruct((h_, s, 1), F32)],
        in_specs=[head, head,
                  pl.BlockSpec((tk, d), lambda h, j: (j, h_ + h)),
                  pl.BlockSpec((tk, d), lambda h, j: (j, 2 * h_ + h)),
                  pl.BlockSpec((1, s, STAT_LANES), lambda h, j: (h, 0, 0)),
                  pl.BlockSpec((1, 1, tk), lambda h, j: (h, 0, j))],
        out_specs=[head, pl.BlockSpec((tk, d), lambda h, j: (j, h)), pl.BlockSpec((tk, d), lambda h, j: (j, h)),
                   pl.BlockSpec((1, 1, tk), lambda h, j: (h, 0, j)), pl.BlockSpec((1, s, 1), lambda h, j: (h, 0, 0))],
        scratch_shapes=[pltpu.VMEM((s, d), F32), pltpu.VMEM((tk, d), F32), pltpu.VMEM((tk, d), F32),
                        pltpu.VMEM((1, tk), F32)],
        compiler_params=_params("parallel", "arbitrary"),
    )(qkv, do, qkv, qkv, stats, c_row)


def _pair_sum(grads, theirs, core, name):
    _, l, r, c = theirs.shape
    tr = _tile(r, 256, 16)

    def body(core_ref, a_ref, b_ref, o_ref):
        o_ref[...] = (a_ref[...].astype(F32) + b_ref[...].astype(F32)).astype(BF16)

    spec = pl.BlockSpec((1, 1, tr, c), lambda q, k, i, core_ref: (q, k, i, 0))
    mine = pl.BlockSpec((1, 1, tr, c), lambda q, k, i, core_ref: (2 * q + core_ref[0], k, i, 0))
    return pl.pallas_call(
        body, name=name, out_shape=jax.ShapeDtypeStruct(theirs.shape, BF16),
        grid_spec=pltpu.PrefetchScalarGridSpec(num_scalar_prefetch=1, grid=(4, l, r // tr),
                                               in_specs=[mine, spec], out_specs=spec),
        compiler_params=_params("parallel", "parallel", "parallel"),
    )(core, grads, theirs)


def _adamw(g, w, m, v):
    m = ADAM_B1 * m + (1.0 - ADAM_B1) * g
    v = ADAM_B2 * v + (1.0 - ADAM_B2) * (g * g)
    m_hat = m / (1.0 - ADAM_B1 ** ADAM_STEP)
    v_hat = v / (1.0 - ADAM_B2 ** ADAM_STEP)
    return -ADAM_LR * (m_hat / (jnp.sqrt(v_hat) + ADAM_EPS) + ADAM_WD * w), m, v


def _sum_adamw(parts, w, m, v, name):
    n_parts, l, r, c = parts.shape
    tr = _tile(r, 256, 16)

    def body(p_ref, w_ref, m_ref, v_ref, g_ref, d_ref, nm_ref, nv_ref):
        g = p_ref[0].astype(F32)
        for k in range(1, n_parts):
            g = g + p_ref[k].astype(F32)
        g_ref[...] = g
        d_ref[...], nm_ref[...], nv_ref[...] = _adamw(g, w_ref[...], m_ref[...], v_ref[...])

    spec = pl.BlockSpec((1, tr, c), lambda k, i: (k, i, 0))
    return pl.pallas_call(
        body, name=name, grid=(l, r // tr), out_shape=[jax.ShapeDtypeStruct(w.shape, F32)] * 4,
        in_specs=[pl.BlockSpec((n_parts, 1, tr, c), lambda k, i: (0, k, i, 0)), spec, spec, spec],
        out_specs=[spec] * 4, compiler_params=_params("parallel", "parallel"),
    )(parts, w, m, v)


def _unshard_cols(gathered):
    return jnp.transpose(gathered, (1, 0, 2)).reshape(gathered.shape[1], -1)


def _shard_cols(full):
    k = full.shape[0]
    return jnp.transpose(full.reshape(k, N_DEV, -1), (1, 0, 2))


def _in_weights(w, d_model):
    hd = N_HEADS * HEAD_DIM
    w_in = _unshard_cols(w["w_in"])
    f0 = d_model + 3 * hd
    return {
        "w_ag": w_in[:, :d_model], "w_qkv": w_in[:, d_model:f0],
        "w_f": jnp.pad(w_in[:, f0:f0 + N_HEADS], ((0, 0), (0, LANES - N_HEADS))),
        "w_gg": w_in[:, f0 + N_HEADS:],
        "conv_w": jnp.transpose(w["conv_w"], (1, 0, 2)).reshape(CONV_K, -1),
    }


def _rest_weights(w, d_model):
    return {
        "w_conv_out": _unshard_cols(w["w_conv_out"]), "w_attn_out": _unshard_cols(w["w_attn_out"]),
        "w_out": w["w_out"].reshape(-1, d_model), "w_gate_up": _unshard_cols(w["w_gate_up"]),
        "w_down": w["w_down"].reshape(-1, d_model), "w_ple_gate": w["w_ple_gate"].reshape(-1, d_model),
        "w_ple_proj": _unshard_cols(w["w_ple_proj"]),
    }


def _pad_lanes(vec):
    return jnp.pad(vec, (0, LANES - vec.shape[0]))[None, :]


def _layer_fwd(x, p, shards, small, tag, after_mixers=None):
    d_model = x.shape[1]
    w = _in_weights(shards, d_model)
    h = _rms_fwd(x, small["norm_mix_g"], f"rms_mix_{tag}")
    ag = _mm(h, w["w_ag"], "nn", F32, f"proj_glu_{tag}")
    qkv = _mm(h, w["w_qkv"], "nn", BF16, f"proj_qkv_{tag}", scaled_cols=(N_HEADS * HEAD_DIM, _SCALE * _LOG2E))
    gg = _mm(h, w["w_gg"], "nn", BF16, f"proj_gates_{tag}")
    f = _mm(h, w["w_f"], "nn", F32, f"proj_forget_{tag}")
    c = _forget_fwd(f, small["b_forget"], f"forget_{tag}")
    c_heads = jnp.transpose(c[:, :N_HEADS])
    c_row = c_heads[:, None, :]
    o, lse = _attn_fwd(qkv, c_row, f"attn_{tag}")
    u1, u3 = _conv_fwd(ag, w["conv_w"], small["conv_b"], small["conv_ln_g"], small["conv_ln_b"], f"conv_{tag}")
    rest, (u3, o) = lax.optimization_barrier(({k: shards[k] for k in _BIG if k != "w_in"}, (u3, o)))
    w.update(_rest_weights(rest, d_model))
    y_conv = _mm(u3, w["w_conv_out"], "nn", BF16, f"conv_out_{tag}")
    y_attn = _mm(o, w["w_attn_out"], "nn", BF16, f"attn_out_{tag}")
    if after_mixers is not None:
        after_mixers(y_attn)
    merged = _merge_fwd(gg, y_conv, y_attn, f"merge_{tag}")
    x1 = _mm(merged, w["w_out"], "nn", F32, f"mix_out_{tag}", residual=x)
    hf = _rms_fwd(x1, small["norm_ffn_g"], f"rms_ffn_{tag}")
    gu = _mm(hf, w["w_gate_up"], "nn", BF16, f"ffn_up_{tag}")
    act = _swiglu_fwd(gu, f"swiglu_{tag}")
    x2 = _mm(act, w["w_down"], "nn", F32, f"ffn_down_{tag}", residual=x1)
    hp = _rms_fwd(x2, small["norm_ple_g"], f"rms_ple_{tag}")
    z = _mm(hp, w["w_ple_gate"], "nn", BF16, f"ple_gate_{tag}")
    pp = _mm(p, w["w_ple_proj"], "nn", BF16, f"ple_proj_{tag}")
    x3 = _ple_fwd(x2, z, pp, f"ple_{tag}")
    saved = dict(x=x, p=p, h=h, ag=ag, qkv=qkv, gg=gg, f=f, u1=u1, u3=u3, y_conv=y_conv, c_row=c_row,
                 o=o, lse=lse, y_attn=y_attn, merged=merged, x1=x1, hf=hf, gu=gu, act=act, x2=x2, hp=hp, z=z, pp=pp)
    return x3, saved, w


def _layer_bwd_ffn(dx3, sv, w, small, tag):
    gw, gs = {}, {}
    dz, dpp = _ple_bwd(dx3, sv["z"], sv["pp"], f"ple_bwd_{tag}")
    gw["w_ple_proj"] = _mm(sv["p"], dpp, "tn", BF16, f"ple_proj_dw_{tag}")
    gw["w_ple_gate"] = _mm(sv["hp"], dz, "tn", BF16, f"ple_gate_dw_{tag}")
    dhp = _mm(dz, w["w_ple_gate"], "nt", BF16, f"ple_gate_dx_{tag}")
    dx2, dx2_b, gs["norm_ple_g"] = _rms_bwd(sv["x2"], small["norm_ple_g"], dhp, dx3, f"rms_ple_bwd_{tag}")
    gw["w_down"] = _mm(sv["act"], dx2_b, "tn", BF16, f"ffn_down_dw_{tag}", tm=FFN_N_TILE)
    dact = _mm(dx2_b, w["w_down"], "nt", BF16, f"ffn_down_dx_{tag}", tn=FFN_N_TILE)
    dgu = _swiglu_bwd(sv["gu"], dact, f"swiglu_bwd_{tag}")
    gw["w_gate_up"] = _mm(sv["hf"], dgu, "tn", BF16, f"ffn_up_dw_{tag}", tn=FFN_N_TILE, out_blocks=N_DEV)
    dhf = _mm(dgu, w["w_gate_up"], "nt", BF16, f"ffn_up_dx_{tag}", tk=FFN_K_TILE)
    dx1, dx1_b, gs["norm_ffn_g"] = _rms_bwd(sv["x1"], small["norm_ffn_g"], dhf, dx2, f"rms_ffn_bwd_{tag}")
    for k in ("norm_ple_g", "norm_ffn_g"):
        gs[k] = gs[k][0]
    return dx1, dx1_b, gw, gs


def _layer_bwd_mix(dx1, dx1_b, sv, w, small, tag):
    gw, gs = {}, {}
    gw["w_out"] = _mm(sv["merged"], dx1_b, "tn", BF16, f"mix_out_dw_{tag}")
    dm = _mm(dx1_b, w["w_out"], "nt", BF16, f"mix_out_dx_{tag}")
    dyc, dya, dgg = _merge_bwd(sv["gg"], sv["y_conv"], sv["y_attn"], dm, f"merge_bwd_{tag}")
    gw["w_attn_out"] = _mm(sv["o"], dya, "tn", BF16, f"attn_out_dw_{tag}")
    do = _mm(dya, w["w_attn_out"], "nt", F32, f"attn_out_dx_{tag}")
    stats, do_b = _attn_stats(sv["o"], do, sv["lse"], f"attn_stats_{tag}")
    dq, dk, dv, dc_row, dc_col = _attn_bwd(sv["qkv"], do_b, stats, sv["c_row"], f"attn_bwd_{tag}")
    dc = jnp.pad(jnp.transpose(dc_row[:, 0, :] + dc_col[:, :, 0]), ((0, 0), (0, LANES - N_HEADS)))
    df, db = _forget_bwd(sv["f"], small["b_forget"], dc, f"forget_bwd_{tag}")
    gs["b_forget"] = db[0, :N_HEADS]
    gw["w_conv_out"] = _mm(sv["u3"], dyc, "tn", BF16, f"conv_out_dw_{tag}")
    du3 = _mm(dyc, w["w_conv_out"], "nt", BF16, f"conv_out_dx_{tag}")
    du1, gs["conv_ln_g"], gs["conv_ln_b"], gs["conv_b"] = _conv_norm_bwd(
        sv["u1"], du3, small["conv_ln_g"], small["conv_ln_b"], f"conv_norm_bwd_{tag}")
    dag, gs["conv_w"] = _conv_bwd(sv["ag"], du1, w["conv_w"], f"conv_bwd_{tag}")
    dqkv = jnp.concatenate([dq, dk, dv], axis=1)
    g_ag = _mm(sv["h"], dag, "tn", BF16, f"proj_glu_dw_{tag}")
    g_qkv = _mm(sv["h"], dqkv, "tn", BF16, f"proj_qkv_dw_{tag}")
    g_f = _mm(sv["h"], df, "tn", BF16, f"proj_forget_dw_{tag}")
    g_gg = _mm(sv["h"], dgg, "tn", BF16, f"proj_gates_dw_{tag}")
    gw["w_in"] = jnp.concatenate([g_ag, g_qkv, g_f[:, :N_HEADS], g_gg], axis=1)
    dh = _mm(df, w["w_f"], "nt", F32, f"proj_forget_dx_{tag}")
    dh = _mm(dag, w["w_ag"], "nt", F32, f"proj_glu_dx_{tag}", residual=dh)
    dh = _mm(dqkv, w["w_qkv"], "nt", F32, f"proj_qkv_dx_{tag}", residual=dh)
    dh = _mm(dgg, w["w_gg"], "nt", F32, f"proj_gates_dx_{tag}", residual=dh)
    dx, _, gs["norm_mix_g"] = _rms_bwd(sv["x"], small["norm_mix_g"], dh, dx1, f"rms_mix_bwd_{tag}")
    for k in ("norm_mix_g", "conv_ln_g", "conv_ln_b", "conv_b"):
        gs[k] = gs[k][0]
    return dx, gw, gs


_COL_SHARDED = ("w_in", "w_conv_out", "w_attn_out", "w_gate_up", "w_ple_proj")
_ROW_SHARDED = ("w_out", "w_down", "w_ple_gate")
_BIG = _COL_SHARDED + _ROW_SHARDED
_SMALL = ("norm_mix_g", "b_forget", "conv_b", "conv_ln_g", "conv_ln_b", "norm_ffn_g", "norm_ple_g")
_ORDER = ("norm_mix_g", "w_in", "b_forget", "conv_w", "conv_b", "conv_ln_g", "conv_ln_b", "w_conv_out", "w_attn_out",
          "w_out", "norm_ffn_g", "w_gate_up", "w_down", "norm_ple_g", "w_ple_gate", "w_ple_proj", "final_g")


def _pack(vectors, rows):
    flat = jnp.concatenate([v.reshape(-1) for v in vectors])
    return jnp.pad(flat, (0, rows * LANES - flat.shape[0])).reshape(1, rows, LANES)


def _unpack(packed, like):
    flat, out, at = packed.reshape(-1), [], 0
    for v in like:
        out.append(flat[at:at + v.size].reshape(v.shape))
        at += v.size
    return out


def kernel(x, p, norm_mix_g, w_in, b_forget, conv_w, conv_b, conv_ln_g, conv_ln_b, w_conv_out, w_attn_out, w_out, norm_ffn_g, w_gate_up, w_down, norm_ple_g, w_ple_gate, w_ple_proj, final_g, loss_target, m_norm_mix_g, m_w_in, m_b_forget, m_conv_w, m_conv_b, m_conv_ln_g, m_conv_ln_b, m_w_conv_out, m_w_attn_out, m_w_out, m_norm_ffn_g, m_w_gate_up, m_w_down, m_norm_ple_g, m_w_ple_gate, m_w_ple_proj, m_final_g, v_norm_mix_g, v_w_in, v_b_forget, v_conv_w, v_conv_b, v_conv_ln_g, v_conv_ln_b, v_w_conv_out, v_w_attn_out, v_w_out, v_norm_ffn_g, v_w_gate_up, v_w_down, v_norm_ple_g, v_w_ple_gate, v_w_ple_proj, v_final_g):
    weights = dict(norm_mix_g=norm_mix_g, w_in=w_in, b_forget=b_forget, conv_w=conv_w, conv_b=conv_b, conv_ln_g=conv_ln_g, conv_ln_b=conv_ln_b, w_conv_out=w_conv_out, w_attn_out=w_attn_out, w_out=w_out, norm_ffn_g=norm_ffn_g, w_gate_up=w_gate_up, w_down=w_down, norm_ple_g=norm_ple_g, w_ple_gate=w_ple_gate, w_ple_proj=w_ple_proj, final_g=final_g)
    mom1 = dict(norm_mix_g=m_norm_mix_g, w_in=m_w_in, b_forget=m_b_forget, conv_w=m_conv_w, conv_b=m_conv_b, conv_ln_g=m_conv_ln_g, conv_ln_b=m_conv_ln_b, w_conv_out=m_w_conv_out, w_attn_out=m_w_attn_out, w_out=m_w_out, norm_ffn_g=m_norm_ffn_g, w_gate_up=m_w_gate_up, w_down=m_w_down, norm_ple_g=m_norm_ple_g, w_ple_gate=m_w_ple_gate, w_ple_proj=m_w_ple_proj, final_g=m_final_g)
    mom2 = dict(norm_mix_g=v_norm_mix_g, w_in=v_w_in, b_forget=v_b_forget, conv_w=v_conv_w, conv_b=v_conv_b, conv_ln_g=v_conv_ln_g, conv_ln_b=v_conv_ln_b, w_conv_out=v_w_conv_out, w_attn_out=v_w_attn_out, w_out=v_w_out, norm_ffn_g=v_norm_ffn_g, w_gate_up=v_w_gate_up, w_down=v_w_down, norm_ple_g=v_norm_ple_g, w_ple_gate=v_w_ple_gate, w_ple_proj=v_w_ple_proj, final_g=v_final_g)
    depth = w_in.shape[0]
    xi, yi, ci = lax.axis_index("x"), lax.axis_index("y"), lax.axis_index("c")
    me = 4 * xi + 2 * yi + ci

    rest_names = [k for k in _BIG if k != "w_in"]
    first = _all_gather([weights["w_in"][0].astype(BF16), conv_w], "gather_weights_in_l0")
    gathered_conv_w = first[1]
    shards, _ = lax.optimization_barrier(([weights[k][0].astype(BF16) for k in rest_names], first[0]))
    rest = _all_gather_beside(shards, "gather_weights_rest_l0", collective_id=0)
    gathered = [dict(zip(rest_names, rest), w_in=first[0], conv_w=gathered_conv_w[:, 0])] + [None] * (depth - 1)

    def gather_layer(l, tie):
        shards, _ = lax.optimization_barrier(([weights[k][l].astype(BF16) for k in _BIG], tie))
        later = _all_gather_beside(shards, f"gather_weights_l{l}", collective_id=l)
        gathered[l] = dict(zip(_BIG, later), conv_w=gathered_conv_w[:, l])

    act = x[0]
    layers = []
    for l in range(depth):
        if l > 0:
            gathered[l], act = lax.optimization_barrier((gathered[l], act))
        small = {k: weights[k][l][None, :] for k in _SMALL if k != "b_forget"}
        small["b_forget"] = _pad_lanes(b_forget[l])
        gather_next = functools.partial(gather_layer, l + 1) if l + 1 < depth else None
        act, saved, w = _layer_fwd(act, p[l, 0], gathered[l], small, f"l{l}", after_mixers=gather_next)
        layers.append((w, small, saved))
    dact, g_final, loss_part = _loss_head(act, final_g[None, :], loss_target[0], "loss_head")
    loss = lax.psum(loss_part[0, 0], ("x", "y", "c"))

    core = ci.reshape(1).astype(jnp.int32)
    ffn_group = ("w_ple_gate", "w_ple_proj", "w_down", "w_gate_up")
    mix_group = ("w_out", "w_attn_out", "w_conv_out", "w_in")

    def reduce_scatter(gw, names, tie, tag, collective_id):
        blocks = [(gw[k] if gw[k].ndim == 3 else _shard_cols(gw[k]) if k in _COL_SHARDED
                   else gw[k].reshape((N_DEV, -1) + gw[k].shape[1:]))[:, None] for k in names]
        theirs = _pair_exchange(blocks, f"grads_pair_exchange_{tag}")
        chip_parts = [_pair_sum(a, b, core, f"grads_pair_sum_{k}_{tag}") for k, a, b in zip(names, blocks, theirs)]
        chip_parts, tie = lax.optimization_barrier((chip_parts, tie))
        return _chip_exchange_beside(chip_parts, f"grads_chip_exchange_{tag}", collective_id), tie

    arrived = {}
    small_grads = {k: [None] * depth for k in _SMALL + ("conv_w",)}
    waiting = None
    for l in reversed(range(depth)):
        w, small, saved = layers[l]
        dx1, dx1_b, gw, gs = _layer_bwd_ffn(dact, saved, w, small, f"l{l}")
        if waiting is not None:
            arrived[waiting], dx1 = lax.optimization_barrier((arrived[waiting], dx1))
        arrived[l, 0], dx1 = reduce_scatter(gw, ffn_group, dx1, f"ffn_l{l}", depth + 2 * l)
        dact, gw, gs_mix = _layer_bwd_mix(dx1, dx1_b, saved, w, small, f"l{l}")
        arrived[l, 0], dact = lax.optimization_barrier((arrived[l, 0], dact))
        arrived[l, 1], dact = reduce_scatter(gw, mix_group, dact, f"mix_l{l}", depth + 2 * l + 1)
        waiting = (l, 1)
        for k, g in {**gs, **gs_mix}.items():
            small_grads[k][l] = g
    grad_x = dact[None]

    out = {}
    for group, names in enumerate((ffn_group, mix_group)):
        for i, k in enumerate(names):
            parts = jnp.concatenate([arrived[l, group][i] for l in range(depth)], axis=1)
            out[k] = _sum_adamw(parts, weights[k], mom1[k], mom2[k], f"adamw_{k}")

    small_names = list(_SMALL) + ["conv_w", "final_g"]
    small_list = [jnp.stack(small_grads[k]) for k in _SMALL + ("conv_w",)] + [g_final[0]]
    n_small = sum(v.size for v in small_list)
    rows = -(-n_small // (8 * LANES)) * 8
    all_parts = _all_gather([_pack(small_list, rows)], "gather_small_grads")[0]
    full_conv_w = jnp.transpose(gathered_conv_w, (1, 2, 0, 3)).reshape(depth, CONV_K, -1)
    full_conv_m = jnp.zeros_like(full_conv_w)
    ch_shard = conv_w.shape[-1]
    at = (0, 0, me * ch_shard)
    small_w = [weights[k] for k in _SMALL] + [full_conv_w, final_g]
    small_m = [mom1[k] for k in _SMALL] + [lax.dynamic_update_slice(full_conv_m, mom1["conv_w"], at), mom1["final_g"]]
    small_v = [mom2[k] for k in _SMALL] + [lax.dynamic_update_slice(full_conv_m + 1.0, mom2["conv_w"], at), mom2["final_g"]]
    res = _sum_adamw(all_parts, _pack(small_w, rows), _pack(small_m, rows), _pack(small_v, rows), "adamw_small")
    unpacked = [_unpack(r, small_list) for r in res]
    for idx, k in enumerate(small_names):
        vals = [u[idx] for u in unpacked]
        if k == "conv_w":
            vals = [lax.dynamic_slice(v, at, conv_w.shape) for v in vals]
        out[k] = vals

    return (loss, grad_x, *[out[k][0] for k in _ORDER], *[out[k][1] for k in _ORDER],
            *[out[k][2] for k in _ORDER], *[out[k][3] for k in _ORDER])
```
